```python
import math
import jax, jax.numpy as jnp
from jax import lax
import numpy as np

D_MODEL = 1024
BATCH = 8
SEQ = 4096
DEPTH = 4

GRID_W = 64
CTX_LEN = 256
N_MIXERS = 4
BLOCK = 128
EPS = 1e-6
ROPE_BASE = 10000.0
ADALN_CHUNKS = 6

ATT_HEADS = 16
ATT_KV_HEADS = 4
ATT_GROUP = ATT_HEADS // ATT_KV_HEADS
ATT_HEAD_DIM = D_MODEL // ATT_HEADS
ATT_IN_DIM = (ATT_HEADS + 2 * ATT_KV_HEADS) * ATT_HEAD_DIM
WINDOW = 128

SSM_D_INNER = 2 * D_MODEL
SSM_HEAD_DIM = 64
SSM_HEADS = SSM_D_INNER // SSM_HEAD_DIM
SSM_GROUPS = 4
SSM_HPG = SSM_HEADS // SSM_GROUPS
SSM_STATE = 128
SSM_CONV = 5
SSM_CONV_DIM = SSM_D_INNER + 2 * SSM_GROUPS * SSM_STATE
SSM_IN_DIM = SSM_D_INNER + SSM_CONV_DIM + 2 * SSM_HEADS

ML_HEADS = 8
ML_QK_DIM = D_MODEL // (2 * ML_HEADS)
ML_V_DIM = D_MODEL // ML_HEADS
ML_IN_DIM = 2 * ML_HEADS * ML_QK_DIM + 2 * ML_HEADS * ML_V_DIM + 4 * ML_HEADS

MLA_HEADS = 16
MLA_Q_RANK = D_MODEL // 4
MLA_KV_RANK = D_MODEL // 4
MLA_NOPE = 64
MLA_ROPE = 32
MLA_V = 64
MLA_IN_DIM = MLA_Q_RANK + MLA_KV_RANK + MLA_ROPE

MOE_GROUPS = 4
MOE_PER_GROUP = 4
MOE_EXPERTS = MOE_GROUPS * MOE_PER_GROUP
MOE_FF = D_MODEL // 4
MOE_TOPK = 2

N_ATT = (DEPTH + 3) // 4
N_SSM = (DEPTH + 2) // 4
N_ML = (DEPTH + 1) // 4
N_MLA = DEPTH // 4

kernel_name = "hybrid_interleaved_diffusion_trunk"


def rmsnorm(x, g):
    xf = x.astype(jnp.float32)
    y = xf * lax.rsqrt(jnp.mean(xf * xf, axis=-1, keepdims=True) + EPS)
    return (y * g.astype(jnp.float32)).astype(x.dtype)


def group_rmsnorm(x, g, n_groups):
    shp = x.shape
    xf = x.astype(jnp.float32).reshape(*shp[:-1], n_groups, shp[-1] // n_groups)
    y = xf * lax.rsqrt(jnp.mean(xf * xf, axis=-1, keepdims=True) + EPS)
    return (y.reshape(shp) * g.astype(jnp.float32)).astype(x.dtype)


def axial_rope(n_tokens, rot_dim):
    rows = n_tokens // GRID_W
    row = jnp.repeat(jnp.arange(rows), GRID_W).astype(jnp.float32)
    col = jnp.tile(jnp.arange(GRID_W), rows).astype(jnp.float32)
    quarter = rot_dim // 4
    inv = ROPE_BASE ** (-jnp.arange(quarter, dtype=jnp.float32) / quarter)
    ang = jnp.concatenate([row[:, None] * inv, col[:, None] * inv], axis=-1)
    return jnp.cos(ang), jnp.sin(ang)


def apply_rope(x, cos, sin):
    shape = (1, cos.shape[0]) + (1,) * (x.ndim - 3) + (cos.shape[1],)
    cos = cos.reshape(shape).astype(x.dtype)
    sin = sin.reshape(shape).astype(x.dtype)
    x1, x2 = jnp.split(x, 2, axis=-1)
    return jnp.concatenate([x1 * cos - x2 * sin, x2 * cos + x1 * sin], axis=-1)


def softmax_with_sink(logits, sink):
    s = jnp.broadcast_to(sink[:, :, None, None], logits.shape[:-1] + (1,))
    p = jax.nn.softmax(jnp.concatenate([logits, s], axis=-1), axis=-1)
    return p[..., :-1]


def centred_dwconv(x, w, b):
    k_w = w.shape[0]
    pad = k_w // 2
    n = x.shape[1]
    xp = jnp.pad(x, ((0, 0), (pad, pad), (0, 0)))
    return b + sum(xp[:, t:t + n] * w[t] for t in range(k_w))


def to_chunks(t):
    bsz, n = t.shape[:2]
    return jnp.moveaxis(t.reshape(bsz, n // BLOCK, BLOCK, *t.shape[2:]), 1, 0)


def from_chunks(t):
    t = jnp.moveaxis(t, 0, 1)
    return t.reshape(t.shape[0], t.shape[1] * t.shape[2], *t.shape[3:])


def windowed_gqa(hc, hx, w_in, sink, w_out, need_ctx):
    bsz, n_lat, _ = hx.shape
    n_blk = n_lat // BLOCK
    scale = ATT_HEAD_DIM ** -0.5
    sink = sink.astype(jnp.float32).reshape(ATT_KV_HEADS, ATT_GROUP)

    def project(h):
        qkv = h @ w_in
        q, k, v = jnp.split(qkv, [ATT_HEADS * ATT_HEAD_DIM, (ATT_HEADS + ATT_KV_HEADS) * ATT_HEAD_DIM], axis=-1)
        lead = h.shape[:2]
        return (q.reshape(*lead, ATT_KV_HEADS, ATT_GROUP, ATT_HEAD_DIM),
                k.reshape(*lead, ATT_KV_HEADS, ATT_HEAD_DIM),
                v.reshape(*lead, ATT_KV_HEADS, ATT_HEAD_DIM))

    qc, kc, vc = project(hc)
    qx, kx, vx = project(hx)
    cos, sin = axial_rope(n_lat, ATT_HEAD_DIM)
    qx = apply_rope(qx, cos, sin)
    kx = apply_rope(kx, cos, sin)

    yc = None
    if need_ctx:
        lg = jnp.einsum('bqhgd,bkhd->bhgqk', qc, kc).astype(jnp.float32) * scale
        p = softmax_with_sink(lg, sink).astype(vc.dtype)
        oc = jnp.einsum('bhgqk,bkhd->bqhgd', p, vc)
        yc = oc.reshape(bsz, hc.shape[1], ATT_HEADS * ATT_HEAD_DIM) @ w_out

    def band(t):
        tp = jnp.pad(t, ((0, 0), (BLOCK, BLOCK), (0, 0), (0, 0)))
        tb = tp.reshape(bsz, n_blk + 2, BLOCK, ATT_KV_HEADS, ATT_HEAD_DIM)
        tb = jnp.concatenate([tb[:, :-2], tb[:, 1:-1], tb[:, 2:]], axis=2)
        return jnp.moveaxis(tb, 1, 0)

    kb, vb = band(kx), band(vx)
    qb = to_chunks(qx)
    qi = jnp.arange(BLOCK)[:, None]
    kj = jnp.arange(3 * BLOCK)[None, :]
    in_win = (kj >= qi + BLOCK - WINDOW) & (kj <= qi + BLOCK + WINDOW)
    kpos = (jnp.arange(n_blk)[:, None] - 1) * BLOCK + jnp.arange(3 * BLOCK)[None, :]
    valid = (kpos >= 0) & (kpos < n_lat)
    mask = in_win[None] & valid[:, None, :]

    def block_attn(args):
        q, k, v, m = args
        ll = jnp.einsum('bqhgd,bkhd->bhgqk', q, k).astype(jnp.float32) * scale
        ll = jnp.where(m, ll, -jnp.inf)
        lcx = jnp.einsum('bqhgd,bkhd->bhgqk', q, kc).astype(jnp.float32) * scale
        p = softmax_with_sink(jnp.concatenate([ll, lcx], axis=-1), sink).astype(v.dtype)
        return (jnp.einsum('bhgqk,bkhd->bqhgd', p[..., :3 * BLOCK], v)
                + jnp.einsum('bhgqk,bkhd->bqhgd', p[..., 3 * BLOCK:], vc))

    ob = lax.map(block_attn, (qb, kb, vb, mask))
    yx = from_chunks(ob).reshape(bsz, n_lat, ATT_HEADS * ATT_HEAD_DIM) @ w_out
    return yc, yx


def ssd_scan(u, a, bm, cm, state0):
    tri = jnp.tril(jnp.ones((BLOCK, BLOCK), bool))

    def step(state, inp):
        uq, aq, bq, cq = inp
        acum = jnp.cumsum(aq, axis=1)
        seg = acum[:, :, None] - acum[:, None, :]
        lmat = jnp.exp(jnp.where(tri[None, :, :, None, None], seg, -jnp.inf))
        cb = jnp.einsum('blgn,bsgn->blsg', cq, bq)
        y = jnp.einsum('blsg,blsge,bsgep->blgep', cb, lmat, uq)
        y = y + jnp.einsum('blgn,bgepn->blgep', cq, state) * jnp.exp(acum)[..., None]
        decay_end = jnp.exp(acum[:, -1:] - acum)
        state = (state * jnp.exp(acum[:, -1])[..., None, None]
                 + jnp.einsum('bsgn,bsge,bsgep->bgepn', bq, decay_end, uq))
        return state, y

    state, ys = lax.scan(step, state0, (to_chunks(u), to_chunks(a), to_chunks(bm), to_chunks(cm)))
    return from_chunks(ys), state


def mamba2_bidir(hc, hx, w_in, conv_w, conv_b, dt_bias, a_log, d_skip, norm_g, w_out, need_ctx):
    a_neg = -jnp.exp(a_log.astype(jnp.float32))

    def project(h):
        p = h @ w_in
        z, xbc, dt = jnp.split(p, [SSM_D_INNER, SSM_D_INNER + SSM_CONV_DIM], axis=-1)
        xbc = jax.nn.silu(centred_dwconv(xbc, conv_w, conv_b))
        xs, bm, cm = jnp.split(xbc, [SSM_D_INNER, SSM_D_INNER + SSM_GROUPS * SSM_STATE], axis=-1)
        return z, xs, bm, cm, dt.reshape(*h.shape[:2], 2, SSM_HEADS)

    def run(xs, bm, cm, dt, d, state0):
        bsz, n = xs.shape[:2]
        dtd = jax.nn.softplus(dt[:, :, d].astype(jnp.float32) + dt_bias[d].astype(jnp.float32))
        dtg = dtd.reshape(bsz, n, SSM_GROUPS, SSM_HPG)
        xh = xs.reshape(bsz, n, SSM_GROUPS, SSM_HPG, SSM_HEAD_DIM)
        u = xh * dtg[..., None]
        a = dtg * a_neg[d].reshape(SSM_GROUPS, SSM_HPG)
        bg = bm.reshape(bsz, n, SSM_GROUPS, SSM_STATE)
        cg = cm.reshape(bsz, n, SSM_GROUPS, SSM_STATE)
        if d == 1:
            u, a, bg, cg = (jnp.flip(t, axis=1) for t in (u, a, bg, cg))
        y, st = ssd_scan(u, a, bg, cg, state0)
        if d == 1:
            y = jnp.flip(y, axis=1)
        y = y + d_skip[d].reshape(SSM_GROUPS, SSM_HPG)[:, :, None] * xh
        return y.reshape(bsz, n, SSM_D_INNER), st

    zc, xc, bc, cc, dtc = project(hc)
    zx, xx, bx, cx, dtx = project(hx)
    bsz = hx.shape[0]
    zero = jnp.zeros((bsz, SSM_GROUPS, SSM_HPG, SSM_HEAD_DIM, SSM_STATE), jnp.float32)
    yc_f, st_f = run(xc, bc, cc, dtc, 0, zero)
    yc_b, st_b = run(xc, bc, cc, dtc, 1, zero)
    yx_f, _ = run(xx, bx, cx, dtx, 0, st_f)
    yx_b, _ = run(xx, bx, cx, dtx, 1, st_b)

    def out(y, z):
        return group_rmsnorm(y * jax.nn.silu(z), norm_g, SSM_GROUPS) @ w_out

    yc = out(yc_f + yc_b, zc) if need_ctx else None
    return yc, out(yx_f + yx_b, zx)


def mlstm_scan(q, k, v, ig, lf, state0):
    tri = jnp.tril(jnp.ones((BLOCK, BLOCK), bool))

    def step(carry, inp):
        cs, ns, ms = carry
        qq, kq, vq, iq, fq = inp
        bcum = jnp.cumsum(fq, axis=1)
        logd = bcum[:, :, None] - bcum[:, None, :] + iq[:, None, :]
        logd = jnp.where(tri[None, :, :, None], logd, -jnp.inf)
        inter = bcum + ms[:, None]
        mt = jnp.maximum(inter, jnp.max(logd, axis=2))
        dmat = jnp.exp(logd - mt[:, :, None])
        sc = jnp.einsum('bthd,bshd->btsh', qq, kq) * dmat
        w_int = jnp.exp(inter - mt)
        num = (jnp.einsum('btsh,bshv->bthv', sc, vq)
               + w_int[..., None] * jnp.einsum('bthd,bhdv->bthv', qq, cs))
        den = jnp.sum(sc, axis=2) + w_int * jnp.einsum('bthd,bhd->bth', qq, ns)
        h = num / jnp.maximum(jnp.abs(den), jnp.exp(-mt))[..., None]
        tot = bcum[:, -1]
        logw = tot[:, None] - bcum + iq
        m_new = jnp.maximum(tot + ms, jnp.max(logw, axis=1))
        ws = jnp.exp(logw - m_new[:, None])
        cw = jnp.exp(tot + ms - m_new)
        c_new = cw[..., None, None] * cs + jnp.einsum('bsh,bshd,bshv->bhdv', ws, kq, vq)
        n_new = cw[..., None] * ns + jnp.einsum('bsh,bshd->bhd', ws, kq)
        return (c_new, n_new, m_new), h

    state, hs = lax.scan(step, state0, tuple(to_chunks(t) for t in (q, k, v, ig, lf)))
    return from_chunks(hs), state


def mlstm_bidir(hc, hx, w_in, gate_b, norm_g, w_out, need_ctx):
    qk = ML_HEADS * ML_QK_DIM
    vd = ML_HEADS * ML_V_DIM

    def project(h):
        p = h @ w_in
        q, k, v, o, g = jnp.split(p, [qk, 2 * qk, 2 * qk + vd, 2 * qk + 2 * vd], axis=-1)
        lead = h.shape[:2]
        q = q.reshape(*lead, ML_HEADS, ML_QK_DIM)
        k = k.reshape(*lead, ML_HEADS, ML_QK_DIM) * (ML_QK_DIM ** -0.5)
        v = v.reshape(*lead, ML_HEADS, ML_V_DIM)
        g = g.reshape(*lead, 4, ML_HEADS).astype(jnp.float32) + gate_b.astype(jnp.float32)
        return q, k, v, o, g

    def run(q, k, v, g, d, state0):
        ig = g[:, :, 2 * d]
        lf = jax.nn.log_sigmoid(g[:, :, 2 * d + 1])
        if d == 1:
            q, k, v, ig, lf = (jnp.flip(t, axis=1) for t in (q, k, v, ig, lf))
        h, st = mlstm_scan(q, k, v, ig, lf, state0)
        if d == 1:
            h = jnp.flip(h, axis=1)
        return h, st

    qc, kc, vc, oc, gc = project(hc)
    qx, kx, vx, ox, gx = project(hx)
    bsz = hx.shape[0]
    zero = (jnp.zeros((bsz, ML_HEADS, ML_QK_DIM, ML_V_DIM), jnp.float32),
            jnp.zeros((bsz, ML_HEADS, ML_QK_DIM), jnp.float32),
            jnp.zeros((bsz, ML_HEADS), jnp.float32))
    hc_f, st_f = run(qc, kc, vc, gc, 0, zero)
    hc_b, st_b = run(qc, kc, vc, gc, 1, zero)
    hx_f, _ = run(qx, kx, vx, gx, 0, st_f)
    hx_b, _ = run(qx, kx, vx, gx, 1, st_b)

    def out(h, o):
        h = h.reshape(*h.shape[:2], vd)
        return (group_rmsnorm(h, norm_g, ML_HEADS) * jax.nn.sigmoid(o)) @ w_out

    yc = out(hc_f + hc_b, oc) if need_ctx else None
    return yc, out(hx_f + hx_b, ox)


def mla(hc, hx, w_in, q_norm_g, w_q_up, kv_norm_g, w_kv_up, w_out, need_ctx):
    bsz, n_lat, _ = hx.shape
    n_blk = n_lat // BLOCK
    scale = (MLA_NOPE + MLA_ROPE) ** -0.5

    def project(h, rope):
        p = h @ w_in
        cq, ckv, kr = jnp.split(p, [MLA_Q_RANK, MLA_Q_RANK + MLA_KV_RANK], axis=-1)
        lead = h.shape[:2]
        q = (rmsnorm(cq, q_norm_g) @ w_q_up).reshape(*lead, MLA_HEADS, MLA_NOPE + MLA_ROPE)
        kv = (rmsnorm(ckv, kv_norm_g) @ w_kv_up).reshape(*lead, MLA_HEADS, MLA_NOPE + MLA_V)
        qn, qr = jnp.split(q, [MLA_NOPE], axis=-1)
        kn, v = jnp.split(kv, [MLA_NOPE], axis=-1)
        if rope is not None:
            qr = apply_rope(qr, *rope)
            kr = apply_rope(kr, *rope)
        return qn, qr, kn, kr, v

    qn_c, qr_c, kn_c, kr_c, v_c = project(hc, None)
    qn_x, qr_x, kn_x, kr_x, v_x = project(hx, axial_rope(n_lat, MLA_ROPE))

    def scores(qn, qr, kn, kr):
        return (jnp.einsum('bqhd,bkhd->bhqk', qn, kn)
                + jnp.einsum('bqhd,bkd->bhqk', qr, kr)).astype(jnp.float32) * scale

    yc = None
    if need_ctx:
        p = jax.nn.softmax(scores(qn_c, qr_c, kn_c, kr_c), axis=-1).astype(v_c.dtype)
        oc = jnp.einsum('bhqk,bkhd->bqhd', p, v_c)
        yc = oc.reshape(bsz, hc.shape[1], MLA_HEADS * MLA_V) @ w_out

    def block_attn(args):
        qn, qr = args
        s = jnp.concatenate([scores(qn, qr, kn_x, kr_x), scores(qn, qr, kn_c, kr_c)], axis=-1)
        p = jax.nn.softmax(s, axis=-1).astype(v_x.dtype)
        return (jnp.einsum('bhqk,bkhd->bqhd', p[..., :n_lat], v_x)
                + jnp.einsum('bhqk,bkhd->bqhd', p[..., n_lat:], v_c))

    ob = lax.map(block_attn, (to_chunks(qn_x), to_chunks(qr_x)))
    yx = from_chunks(ob).reshape(bsz, n_lat, MLA_HEADS * MLA_V) @ w_out
    return yc, yx


def hier_moe(h, w_rg, b_rg, w_re, b_re, w_gate, w_up, w_down):
    lead = h.shape[:-1]
    lg = (h @ w_rg + b_rg).astype(jnp.float32)
    g_sel = jnp.argmax(lg, axis=-1)
    p_g = jnp.take_along_axis(jax.nn.softmax(lg, axis=-1), g_sel[..., None], axis=-1)
    le = (h @ w_re + b_re).astype(jnp.float32).reshape(*lead, MOE_GROUPS, MOE_PER_GROUP)
    le_sel = jnp.take_along_axis(le, g_sel[..., None, None], axis=-2)[..., 0, :]
    top_v, top_i = lax.top_k(le_sel, MOE_TOPK)
    w = jax.nn.softmax(top_v, axis=-1) * p_g
    eid = g_sel[..., None] * MOE_PER_GROUP + top_i
    combine = jnp.sum(jax.nn.one_hot(eid, MOE_EXPERTS, dtype=jnp.float32) * w[..., None], axis=-2)
    combine = combine.astype(h.dtype)
    out = jnp.zeros_like(h)
    for e in range(MOE_EXPERTS):
        act = jax.nn.silu(h @ w_gate[e]) * (h @ w_up[e])
        out = out + combine[..., e:e + 1] * (act @ w_down[e])
    return out


def setup_inputs(seed: int = 0) -> dict:
    key = jax.random.key(seed)
    ks = iter(jax.random.split(key, 64))
    D = D_MODEL

    def nrm(shape, scale):
        return jax.random.normal(next(ks), shape, jnp.float32) * scale

    def gain(shape):
        return 1.0 + nrm(shape, 0.02)

    def unif(shape, lo, hi):
        return jax.random.uniform(next(ks), shape, jnp.float32, lo, hi)

    dt0 = jnp.exp(unif((N_SSM, 2, SSM_HEADS), math.log(1e-3), math.log(1e-1)))
    gate_b = jnp.stack([nrm((N_ML, ML_HEADS), 0.1), 3.0 + unif((N_ML, ML_HEADS), 0.0, 3.0),
                        nrm((N_ML, ML_HEADS), 0.1), 3.0 + unif((N_ML, ML_HEADS), 0.0, 3.0)], axis=1)
    return {
        "x": nrm((BATCH, SEQ, D), 1.0),
        "c": nrm((BATCH, D), 1.0),
        "ctx": nrm((BATCH, CTX_LEN, D), 1.0),
        "c_ctx": nrm((D,), 1.0),
        "norm1_g": gain((DEPTH, D)),
        "norm2_g": gain((DEPTH, D)),
        "w_mod": nrm((DEPTH, D, ADALN_CHUNKS * D), 0.5 * D ** -0.5),
        "b_mod": nrm((DEPTH, ADALN_CHUNKS * D), 0.02),
        "moe_w_group": nrm((DEPTH, D, MOE_GROUPS), D ** -0.5),
        "moe_b_group": nrm((DEPTH, MOE_GROUPS), 0.01),
        "moe_w_expert": nrm((DEPTH, D, MOE_EXPERTS), D ** -0.5),
        "moe_b_expert": nrm((DEPTH, MOE_EXPERTS), 0.01),
        "moe_w_gate": nrm((DEPTH, MOE_EXPERTS, D, MOE_FF), D ** -0.5),
        "moe_w_up": nrm((DEPTH, MOE_EXPERTS, D, MOE_FF), D ** -0.5),
        "moe_w_down": nrm((DEPTH, MOE_EXPERTS, MOE_FF, D), MOE_FF ** -0.5),
        "attn_w_in": nrm((N_ATT, D, ATT_IN_DIM), D ** -0.5),
        "attn_sink": nrm((N_ATT, ATT_HEADS), 0.5),
        "attn_w_out": nrm((N_ATT, ATT_HEADS * ATT_HEAD_DIM, D), (ATT_HEADS * ATT_HEAD_DIM) ** -0.5),
        "ssm_w_in": nrm((N_SSM, D, SSM_IN_DIM), D ** -0.5),
        "ssm_conv_w": nrm((N_SSM, SSM_CONV, SSM_CONV_DIM), SSM_CONV ** -0.5),
        "ssm_conv_b": nrm((N_SSM, SSM_CONV_DIM), 0.02),
        "ssm_dt_bias": dt0 + jnp.log(-jnp.expm1(-dt0)),
        "ssm_a_log": jnp.log(unif((N_SSM, 2, SSM_HEADS), 1.0, 16.0)),
        "ssm_d": 1.0 + nrm((N_SSM, 2, SSM_HEADS), 0.1),
        "ssm_norm_g": gain((N_SSM, SSM_D_INNER)),
        "ssm_w_out": nrm((N_SSM, SSM_D_INNER, D), SSM_D_INNER ** -0.5),
        "mlstm_w_in": nrm((N_ML, D, ML_IN_DIM), D ** -0.5),
        "mlstm_gate_b": gate_b,
        "mlstm_norm_g": gain((N_ML, ML_HEADS * ML_V_DIM)),
        "mlstm_w_out": nrm((N_ML, ML_HEADS * ML_V_DIM, D), (ML_HEADS * ML_V_DIM) ** -0.5),
        "mla_w_in": nrm((N_MLA, D, MLA_IN_DIM), D ** -0.5),
        "mla_q_norm_g": gain((N_MLA, MLA_Q_RANK)),
        "mla_w_q_up": nrm((N_MLA, MLA_Q_RANK, MLA_HEADS * (MLA_NOPE + MLA_ROPE)), MLA_Q_RANK ** -0.5),
        "mla_kv_norm_g": gain((N_MLA, MLA_KV_RANK)),
        "mla_w_kv_up": nrm((N_MLA, MLA_KV_RANK, MLA_HEADS * (MLA_NOPE + MLA_V)), MLA_KV_RANK ** -0.5),
        "mla_w_out": nrm((N_MLA, MLA_HEADS * MLA_V, D), (MLA_HEADS * MLA_V) ** -0.5),
        "final_norm_g": gain((D,)),
    }


def reference(x, c, ctx, c_ctx, norm1_g, norm2_g, w_mod, b_mod,
              moe_w_group, moe_b_group, moe_w_expert, moe_b_expert, moe_w_gate, moe_w_up, moe_w_down,
              attn_w_in, attn_sink, attn_w_out,
              ssm_w_in, ssm_conv_w, ssm_conv_b, ssm_dt_bias, ssm_a_log, ssm_d, ssm_norm_g, ssm_w_out,
              mlstm_w_in, mlstm_gate_b, mlstm_norm_g, mlstm_w_out,
              mla_w_in, mla_q_norm_g, mla_w_q_up, mla_kv_norm_g, mla_w_kv_up, mla_w_out,
              final_norm_g):
    xs, cs = x, ctx
    for i in range(DEPTH):
        kind, j = i % N_MIXERS, i // N_MIXERS
        need_ctx = i < DEPTH - 1
        mod_x = jax.nn.silu(c) @ w_mod[i] + b_mod[i]
        mod_c = jax.nn.silu(c_ctx) @ w_mod[i] + b_mod[i]
        sh1x, sc1x, g1x, sh2x, sc2x, g2x = jnp.split(mod_x[:, None, :], ADALN_CHUNKS, axis=-1)
        sh1c, sc1c, g1c, sh2c, sc2c, g2c = jnp.split(mod_c, ADALN_CHUNKS, axis=-1)

        hx = rmsnorm(xs, norm1_g[i]) * (1.0 + sc1x) + sh1x
        hc = rmsnorm(cs, norm1_g[i]) * (1.0 + sc1c) + sh1c
        if kind == 0:
            yc, yx = windowed_gqa(hc, hx, attn_w_in[j], attn_sink[j], attn_w_out[j], need_ctx)
        elif kind == 1:
            yc, yx = mamba2_bidir(hc, hx, ssm_w_in[j], ssm_conv_w[j], ssm_conv_b[j], ssm_dt_bias[j],
                                  ssm_a_log[j], ssm_d[j], ssm_norm_g[j], ssm_w_out[j], need_ctx)
        elif kind == 2:
            yc, yx = mlstm_bidir(hc, hx, mlstm_w_in[j], mlstm_gate_b[j], mlstm_norm_g[j],
                                 mlstm_w_out[j], need_ctx)
        else:
            yc, yx = mla(hc, hx, mla_w_in[j], mla_q_norm_g[j], mla_w_q_up[j], mla_kv_norm_g[j],
                         mla_w_kv_up[j], mla_w_out[j], need_ctx)
        xs = xs + g1x * yx

        moe_args = (moe_w_group[i], moe_b_group[i], moe_w_expert[i], moe_b_expert[i],
                    moe_w_gate[i], moe_w_up[i], moe_w_down[i])
        hx = rmsnorm(xs, norm2_g[i]) * (1.0 + sc2x) + sh2x
        xs = xs + g2x * hier_moe(hx, *moe_args)
        if need_ctx:
            cs = cs + g1c * yc
            hc = rmsnorm(cs, norm2_g[i]) * (1.0 + sc2c) + sh2c
            cs = cs + g2c * hier_moe(hc, *moe_args)
    return rmsnorm(xs, final_norm_g)
```

```python
import functools
import math

import jax
import jax.numpy as jnp
from jax import lax
from jax.experimental import pallas as pl
from jax.experimental.pallas import tpu as pltpu

F32 = jnp.float32
BF16 = jnp.bfloat16

D_MODEL = 1024
GRID_W = 64
EPS = 1e-6
ROPE_BASE = 10000.0
ADALN_CHUNKS = 6
CHUNK = 128
TM = 256
MOD_ROWS = 8
LANES = 128
V7X_VMEM_LIMIT = 48 * 1024 * 1024

ATT_HEADS, ATT_KV_HEADS, ATT_HEAD_DIM, WINDOW = 16, 4, 64, 128
ATT_GROUP = ATT_HEADS // ATT_KV_HEADS
SSM_D_INNER, SSM_HEAD_DIM, SSM_HEADS, SSM_GROUPS, SSM_STATE, SSM_CONV = 2048, 64, 32, 4, 128, 5
SSM_HPG = SSM_HEADS // SSM_GROUPS
SSM_BC = SSM_GROUPS * SSM_STATE
SSM_CONV_DIM = SSM_D_INNER + 2 * SSM_BC
ML_HEADS, ML_QK_DIM, ML_V_DIM = 8, 64, 128
MLA_HEADS, MLA_RANK, MLA_NOPE, MLA_ROPE, MLA_V = 16, 256, 64, 32, 64
MOE_GROUPS, MOE_PER_GROUP, MOE_EXPERTS, MOE_FF = 4, 4, 16, 256
MOE_ROWS = 1024


def _dot(a, b):
    return jnp.dot(a, b, preferred_element_type=F32)


def _dot_nt(a, b):
    return lax.dot_general(a, b, (((1,), (1,)), ((), ())), preferred_element_type=F32)


def _split_bf16(x):
    hi = x.astype(BF16)
    lo = (x - hi.astype(F32)).astype(BF16)
    return hi, lo


def _dot_split(a, b):
    a_hi, a_lo = _split_bf16(a)
    b_hi, b_lo = _split_bf16(b)
    return _dot(a_hi, b_hi) + _dot(a_lo, b_hi) + _dot(a_hi, b_lo)


def _sigmoid(x):
    return 1.0 / (1.0 + jnp.exp(-x))


def _silu(x):
    return x * _sigmoid(x)


def _softplus(x):
    return jnp.maximum(x, 0.0) + jnp.log1p(jnp.exp(-jnp.abs(x)))


def _params(*sem):
    return pltpu.CompilerParams(dimension_semantics=sem, vmem_limit_bytes=V7X_VMEM_LIMIT)


def _mod_row(t, nt, nb):
    return jnp.where(t % nt == 0, nb, t // nt)


def _mod_kernel(c_ref, w_ref, b_ref, o_ref):
    o_ref[0] = _dot_split(_silu(c_ref[...]), w_ref[0]) + b_ref[0]


def _modulation(cvec, w_mod, b_mod):
    depth, d, n = w_mod.shape
    tn = 1536
    rows = cvec.shape[0]
    return pl.pallas_call(
        _mod_kernel,
        grid=(depth, n // tn),
        in_specs=[pl.BlockSpec((rows, d), lambda l, j: (0, 0)),
                  pl.BlockSpec((1, d, tn), lambda l, j: (l, 0, j)),
                  pl.BlockSpec((1, 1, tn), lambda l, j: (l, 0, j))],
        out_specs=pl.BlockSpec((1, rows, tn), lambda l, j: (l, 0, j)),
        out_shape=jax.ShapeDtypeStruct((depth, rows, n), F32),
        compiler_params=_params("arbitrary", "arbitrary"),
        name="modulation",
    )(cvec, w_mod, b_mod.reshape(depth, 1, n))


def _normed(x, g_row, mod, sh_row, sc_row):
    y = x * lax.rsqrt(jnp.mean(x * x, axis=-1, keepdims=True) + EPS) * g_row
    return y * (1.0 + mod[sc_row:sc_row + 1, :]) + mod[sh_row:sh_row + 1, :]


def _proj_columns(hb, w_ref, o_ref, start, width, rot_start, cos, sin, chunk=512):
    for c in range(0, width, chunk):
        cw = min(chunk, width - c)
        acc = _dot(hb, w_ref[:, start + c:start + c + cw])
        if rot_start is not None:
            rot = _dot(hb, w_ref[:, rot_start + c:rot_start + c + cw])
            reps = cw // LANES
            acc = acc * jnp.tile(cos, (1, reps)) + rot * jnp.tile(sin, (1, reps))
        o_ref[:, c:c + cw] = acc.astype(o_ref.dtype)


def _normproj_kernel(*refs, outs, has_rope):
    x_ref, mod_ref, g_ref, w_ref = refs[:4]
    k = 4
    cos = sin = None
    if has_rope:
        cos, sin = refs[4][...], refs[5][...]
        k = 6
    hb = _normed(x_ref[...], g_ref[...], mod_ref[0], 0, 1).astype(BF16)
    for o_ref, (start, width, rot_start) in zip(refs[k:], outs):
        _proj_columns(hb, w_ref, o_ref, start, width, rot_start, cos, sin)


def _normproj(x, mods, gain, w, outs, out_dtypes, nt, nb, tables=None):
    t_tok, d = x.shape
    n = w.shape[1]
    in_specs = [pl.BlockSpec((TM, d), lambda t: (t, 0)),
                pl.BlockSpec((1, MOD_ROWS, d), lambda t: (_mod_row(t, nt, nb), 0, 0)),
                pl.BlockSpec((1, d), lambda t: (0, 0)),
                pl.BlockSpec((d, n), lambda t: (0, 0))]
    args = [x, mods, gain.reshape(1, d), w]
    if tables is not None:
        in_specs += [pl.BlockSpec((TM, LANES), lambda t: (t % nt, 0))] * 2
        args += list(tables)
    return pl.pallas_call(
        functools.partial(_normproj_kernel, outs=tuple(outs), has_rope=tables is not None),
        grid=(t_tok // TM,),
        in_specs=in_specs,
        out_specs=[pl.BlockSpec((TM, o[1]), lambda t: (t, 0)) for o in outs],
        out_shape=[jax.ShapeDtypeStruct((t_tok, o[1]), dt) for o, dt in zip(outs, out_dtypes)],
        compiler_params=_params("parallel"),
        name="normproj",
    )(*args)


def _outproj_kernel(*refs, prologue, n_in):
    ins = refs[:n_in]
    x_ref, mod_ref, w_ref, o_ref = refs[n_in:]
    a = prologue(*ins)
    y = _dot(a, w_ref[...])
    o_ref[...] = x_ref[...] + mod_ref[0][2:3, :] * y


def _outproj(ins, in_widths, x, mods, w, prologue, nt, nb, extra=()):
    t_tok, d = x.shape
    in_specs = [pl.BlockSpec((TM, wd), lambda t: (t, 0)) for wd in in_widths]
    in_specs += [pl.BlockSpec(e.shape, lambda t: (0, 0)) for e in extra]
    n_in = len(in_specs)
    in_specs += [pl.BlockSpec((TM, d), lambda t: (t, 0)),
                 pl.BlockSpec((1, MOD_ROWS, d), lambda t: (_mod_row(t, nt, nb), 0, 0)),
                 pl.BlockSpec(w.shape, lambda t: (0, 0))]
    return pl.pallas_call(
        functools.partial(_outproj_kernel, prologue=prologue, n_in=n_in),
        grid=(t_tok // TM,),
        in_specs=in_specs,
        out_specs=pl.BlockSpec((TM, d), lambda t: (t, 0)),
        out_shape=jax.ShapeDtypeStruct((t_tok, d), F32),
        input_output_aliases={n_in: 0},
        compiler_params=_params("parallel"),
        name="outproj",
    )(*ins, *extra, x, mods, w)


def _group_rms(y, n_groups):
    width = y.shape[1] // n_groups
    parts = []
    for g in range(n_groups):
        yg = y[:, g * width:(g + 1) * width]
        parts.append(yg * lax.rsqrt(jnp.mean(yg * yg, axis=-1, keepdims=True) + EPS))
    return jnp.concatenate(parts, axis=1)


def _plain_prologue(o_ref):
    return o_ref[...]


def _ssm_prologue(yf_ref, yb_ref, z_ref, g_ref):
    y = (yf_ref[...].astype(F32) + yb_ref[...].astype(F32)) * _silu(z_ref[...].astype(F32))
    return (_group_rms(y, SSM_GROUPS) * g_ref[...]).astype(BF16)


def _mlstm_prologue(hf_ref, hb_ref, o_ref, g_ref):
    h = hf_ref[...].astype(F32) + hb_ref[...].astype(F32)
    return (_group_rms(h, ML_HEADS) * g_ref[...] * _sigmoid(o_ref[...].astype(F32))).astype(BF16)


def _route(logits):
    lane = lax.broadcasted_iota(jnp.int32, logits.shape, 1).astype(F32)
    neg = -jnp.inf
    lg = jnp.where((lane >= MOE_EXPERTS) & (lane < MOE_EXPERTS + MOE_GROUPS), logits, neg)
    gmax = jnp.max(lg, axis=-1, keepdims=True)
    g_sel = jnp.min(jnp.where(lg == gmax, lane, LANES), axis=-1, keepdims=True) - MOE_EXPERTS
    p_g = 1.0 / jnp.sum(jnp.exp(lg - gmax), axis=-1, keepdims=True)
    in_group = (lane >= g_sel * MOE_PER_GROUP) & (lane < (g_sel + 1) * MOE_PER_GROUP)
    le = jnp.where(in_group, logits, neg)
    v1 = jnp.max(le, axis=-1, keepdims=True)
    i1 = jnp.min(jnp.where(le == v1, lane, LANES), axis=-1, keepdims=True)
    le2 = jnp.where(lane == i1, neg, le)
    v2 = jnp.max(le2, axis=-1, keepdims=True)
    i2 = jnp.min(jnp.where(le2 == v2, lane, LANES), axis=-1, keepdims=True)
    e2 = jnp.exp(v2 - v1)
    w1 = p_g / (1.0 + e2)
    return jnp.where(lane == i1, w1, 0.0) + jnp.where(lane == i2, w1 * e2, 0.0)


def _moe_kernel(x_ref, mods_ref, g_ref, wr_ref, br_ref, wgu_ref, wd_ref, o_ref, h_scr, comb_scr, *, nt, nb, sub):
    i = pl.program_id(0)
    e = pl.program_id(1)

    @pl.when(e == 0)
    def _():
        for s in range(sub):
            rows = slice(s * TM, (s + 1) * TM)
            mod = mods_ref[_mod_row(i * sub + s, nt, nb)]
            x = x_ref[rows, :]
            h = _normed(x, g_ref[...], mod, 3, 4)
            h_scr[rows, :] = h.astype(BF16)
            comb_scr[rows, :] = _route(_dot_split(h, wr_ref[...]) + br_ref[...])
            o_ref[rows, :] = x

    hb = h_scr[...]
    gu = _dot(hb, wgu_ref[0])
    act = (_silu(gu[:, :MOE_FF]) * gu[:, MOE_FF:]).astype(BF16)
    y = _dot(act, wd_ref[0])
    lane = lax.broadcasted_iota(jnp.int32, comb_scr.shape, 1)
    cw = jnp.sum(jnp.where(lane == e, comb_scr[...], 0.0), axis=-1, keepdims=True)
    for s in range(sub):
        rows = slice(s * TM, (s + 1) * TM)
        gate = mods_ref[_mod_row(i * sub + s, nt, nb)][5:6, :]
        o_ref[rows, :] += gate * (cw[rows, :] * y[rows, :])


def _moe(x, mods, gain, w_router, b_router, w_gu, w_down, nt, nb):
    t_tok, d = x.shape
    rows = MOE_ROWS if t_tok % MOE_ROWS == 0 else TM
    sub = rows // TM
    return pl.pallas_call(
        functools.partial(_moe_kernel, nt=nt, nb=nb, sub=sub),
        grid=(t_tok // rows, MOE_EXPERTS),
        in_specs=[pl.BlockSpec((rows, d), lambda i, e: (i, 0)),
                  pl.BlockSpec(mods.shape, lambda i, e: (0, 0, 0)),
                  pl.BlockSpec((1, d), lambda i, e: (0, 0)),
                  pl.BlockSpec((d, LANES), lambda i, e: (0, 0)),
                  pl.BlockSpec((1, LANES), lambda i, e: (0, 0)),
                  pl.BlockSpec((1, d, 2 * MOE_FF), lambda i, e: (e, 0, 0)),
                  pl.BlockSpec((1, MOE_FF, d), lambda i, e: (e, 0, 0))],
        out_specs=pl.BlockSpec((rows, d), lambda i, e: (i, 0)),
        out_shape=jax.ShapeDtypeStruct((t_tok, d), F32),
        scratch_shapes=[pltpu.VMEM((rows, d), BF16), pltpu.VMEM((rows, LANES), F32)],
        input_output_aliases={0: 0},
        compiler_params=_params("parallel", "arbitrary"),
        name="moe",
    )(x, mods, gain.reshape(1, d), w_router, b_router, w_gu, w_down)


def _attn_kernel(sink_ref, q_ref, kp_ref, kc_ref, kn_ref, vp_ref, vc_ref, vn_ref, kx_ref, vx_ref, o_ref, *, n_lat):
    j = pl.program_id(1)
    jl = j - TM // CHUNK
    k_loc = jnp.concatenate([kp_ref[...], kc_ref[...], kn_ref[...]], axis=0)
    v_loc = jnp.concatenate([vp_ref[...], vc_ref[...], vn_ref[...]], axis=0)
    kx, vx = kx_ref[...], vx_ref[...]
    qi = lax.broadcasted_iota(jnp.int32, (CHUNK, 3 * CHUNK), 0)
    kj = lax.broadcasted_iota(jnp.int32, (CHUNK, 3 * CHUNK), 1)
    kpos = (jl - 1) * CHUNK + kj
    ok = (kj >= qi + CHUNK - WINDOW) & (kj <= qi + CHUNK + WINDOW) & (kpos >= 0) & (kpos < n_lat) & (jl >= 0)
    bias = jnp.concatenate([jnp.where(ok, 0.0, -jnp.inf)] * ATT_GROUP, axis=0)
    row_head = lax.broadcasted_iota(jnp.int32, (ATT_GROUP * CHUNK, 1), 0) // CHUNK
    dh = ATT_HEAD_DIM
    for g in range(ATT_KV_HEADS):
        cols = slice(g * dh, (g + 1) * dh)
        qg = jnp.concatenate([q_ref[:, (g * ATT_GROUP + a) * dh:(g * ATT_GROUP + a + 1) * dh]
                              for a in range(ATT_GROUP)], axis=0) * (dh ** -0.5)
        s_loc = _dot_nt(qg, k_loc[:, cols]) + bias
        s_ctx = _dot_nt(qg, kx[:, cols])
        sink = jnp.zeros((ATT_GROUP * CHUNK, 1), F32)
        for a in range(ATT_GROUP):
            sink = jnp.where(row_head == a, sink_ref[g * ATT_GROUP + a], sink)
        m = jnp.maximum(jnp.maximum(jnp.max(s_loc, axis=-1, keepdims=True),
                                    jnp.max(s_ctx, axis=-1, keepdims=True)), sink)
        p_loc = jnp.exp(s_loc - m)
        p_ctx = jnp.exp(s_ctx - m)
        den = (jnp.sum(p_loc, axis=-1, keepdims=True) + jnp.sum(p_ctx, axis=-1, keepdims=True)
               + jnp.exp(sink - m))
        og = (_dot(p_loc.astype(BF16), v_loc[:, cols]) + _dot(p_ctx.astype(BF16), vx[:, cols])) / den
        for a in range(ATT_GROUP):
            h = g * ATT_GROUP + a
            o_ref[:, h * dh:(h + 1) * dh] = og[a * CHUNK:(a + 1) * CHUNK, :].astype(o_ref.dtype)


def _windowed_attention(q, k, v, sink, nb, seg):
    t_tok = q.shape[0]
    nblk = seg // CHUNK
    ctx_blk = TM // CHUNK
    kvw = ATT_KV_HEADS * ATT_HEAD_DIM

    def blk(off):
        return lambda b, j, *_: (b * nblk + jnp.clip(j + off, ctx_blk, nblk - 1), 0)

    kv_spec = [pl.BlockSpec((CHUNK, kvw), blk(o)) for o in (-1, 0, 1)]
    ctx_spec = pl.BlockSpec((TM, kvw), lambda b, j, *_: (b * (seg // TM), 0))
    return pl.pallas_call(
        functools.partial(_attn_kernel, n_lat=seg - TM),
        grid_spec=pltpu.PrefetchScalarGridSpec(
            num_scalar_prefetch=1,
            grid=(nb, nblk),
            in_specs=[pl.BlockSpec((CHUNK, q.shape[1]), lambda b, j, *_: (b * nblk + j, 0))]
            + kv_spec + kv_spec + [ctx_spec, ctx_spec],
            out_specs=pl.BlockSpec((CHUNK, q.shape[1]), lambda b, j, *_: (b * nblk + j, 0)),
        ),
        out_shape=jax.ShapeDtypeStruct((t_tok, q.shape[1]), BF16),
        compiler_params=_params("parallel", "parallel"),
        name="windowed_attention",
    )(sink, q, k, k, k, v, v, v, k, v)


def _conv_kernel(cur_ref, prev_ref, next_ref, dt_ref, w_ref, b_ref, dtb_ref, aneg_ref,
                 xs_ref, bm_ref, cm_ref, dtv_ref, a_ref, ext, *, nt):
    t = pl.program_id(0)
    tl = t % nt
    halo = 8
    first = (tl == 0) | (tl == 1)
    last = (tl == 0) | (tl == nt - 1)
    ext[0:halo, :] = jnp.where(first, 0.0, prev_ref[...].astype(F32)[halo:2 * halo, :])
    ext[halo:halo + TM, :] = cur_ref[...].astype(F32)
    ext[halo + TM:2 * halo + TM, :] = jnp.where(last, 0.0, next_ref[...].astype(F32)[0:halo, :])
    pad = SSM_CONV // 2
    chunk = 512
    for c in range(0, SSM_CONV_DIM, chunk):
        acc = jnp.broadcast_to(b_ref[:, c:c + chunk], (TM, chunk))
        for k in range(SSM_CONV):
            acc = acc + w_ref[k:k + 1, c:c + chunk] * ext[halo - pad + k:halo - pad + k + TM, c:c + chunk]
        y = _silu(acc)
        if c < SSM_D_INNER:
            xs_ref[:, c:c + chunk] = y.astype(xs_ref.dtype)
        elif c < SSM_D_INNER + SSM_BC:
            bm_ref[:, c - SSM_D_INNER:c - SSM_D_INNER + chunk] = y.astype(bm_ref.dtype)
        else:
            off = c - SSM_D_INNER - SSM_BC
            cm_ref[:, off:off + chunk] = y.astype(cm_ref.dtype)
    dtv = _softplus(dt_ref[...] + dtb_ref[...])
    dtv_ref[...] = dtv
    a_ref[...] = dtv * aneg_ref[...]


def _ssm_conv(xbc, dt, conv_w, conv_b, dt_bias, a_neg, nt):
    t_tok = xbc.shape[0]
    n16 = t_tok // 16
    row = lambda t: (t, 0)
    fixed = lambda t: (0, 0)
    return pl.pallas_call(
        functools.partial(_conv_kernel, nt=nt),
        grid=(t_tok // TM,),
        in_specs=[pl.BlockSpec((TM, SSM_CONV_DIM), row),
                  pl.BlockSpec((16, SSM_CONV_DIM), lambda t: (jnp.maximum(t * (TM // 16) - 1, 0), 0)),
                  pl.BlockSpec((16, SSM_CONV_DIM), lambda t: (jnp.minimum((t + 1) * (TM // 16), n16 - 1), 0)),
                  pl.BlockSpec((TM, LANES), row),
                  pl.BlockSpec((8, SSM_CONV_DIM), fixed),
                  pl.BlockSpec((1, SSM_CONV_DIM), fixed),
                  pl.BlockSpec((1, LANES), fixed),
                  pl.BlockSpec((1, LANES), fixed)],
        out_specs=[pl.BlockSpec((TM, SSM_D_INNER), row), pl.BlockSpec((TM, SSM_BC), row),
                   pl.BlockSpec((TM, SSM_BC), row), pl.BlockSpec((TM, LANES), row), pl.BlockSpec((TM, LANES), row)],
        out_shape=[jax.ShapeDtypeStruct((t_tok, SSM_D_INNER), BF16), jax.ShapeDtypeStruct((t_tok, SSM_BC), BF16),
                   jax.ShapeDtypeStruct((t_tok, SSM_BC), BF16), jax.ShapeDtypeStruct((t_tok, LANES), F32),
                   jax.ShapeDtypeStruct((t_tok, LANES), F32)],
        scratch_shapes=[pltpu.VMEM((TM + 16, SSM_CONV_DIM), F32)],
        compiler_params=_params("parallel"),
        name="ssm_conv",
    )(xbc, xbc, xbc, dt, conv_w, conv_b, dt_bias, a_neg)


def _tri(lower):
    r = lax.broadcasted_iota(jnp.int32, (CHUNK, CHUNK), 0)
    c = lax.broadcasted_iota(jnp.int32, (CHUNK, CHUNK), 1)
    return (c <= r) if lower else (c >= r)


def _cumsum_rows(x):
    tri = jnp.where(_tri(True), 1.0, 0.0).astype(BF16)
    hi, lo = _split_bf16(x)
    return _dot(tri, hi) + _dot(tri, lo)


def _ssd_direction(xs_ref, bm_ref, cm_ref, dtv_ref, a_ref, dsk_ref, st_ref, y_ref, d):
    a = a_ref[...]
    dtv = dtv_ref[...]
    cum = _cumsum_rows(a)
    cum_end = cum[CHUNK - 1:CHUNK, :]
    if d == 0:
        row_v, col_v = cum, -cum
        inter = jnp.exp(cum)
        w_upd = jnp.exp(cum_end - cum) * dtv
    else:
        ecum = cum - a
        row_v, col_v = -ecum, ecum
        inter = jnp.exp(cum_end - ecum)
        w_upd = jnp.exp(ecum) * dtv
    col_t = jnp.transpose(col_v)
    dt_t = jnp.transpose(dtv)
    decay_end = jnp.exp(cum_end)
    mask = _tri(d == 0)
    p = SSM_HEAD_DIM
    for g in range(SSM_GROUPS):
        ncols = slice(g * SSM_STATE, (g + 1) * SSM_STATE)
        bg = bm_ref[:, ncols]
        cg = cm_ref[:, ncols]
        cb = _dot_nt(cg, bg)
        state = st_ref[g]
        y_in = _dot(cg, state.astype(BF16))
        upd, dec = [], []
        for e in range(SSM_HPG):
            h = g * SSM_HPG + e
            ln = d * SSM_HEADS + h
            seg = jnp.where(mask, row_v[:, ln:ln + 1] + col_t[ln:ln + 1, :], -jnp.inf)
            m_h = (cb * jnp.exp(seg) * dt_t[ln:ln + 1, :]).astype(BF16)
            xh = xs_ref[:, h * p:(h + 1) * p]
            xf = xh.astype(F32)
            yh = (_dot(m_h, xh) + y_in[:, e * p:(e + 1) * p] * inter[:, ln:ln + 1]
                  + dsk_ref[:, h * p:(h + 1) * p] * xf)
            y_ref[:, h * p:(h + 1) * p] = yh.astype(y_ref.dtype)
            upd.append((xf * w_upd[:, ln:ln + 1]).astype(BF16))
            dec.append(jnp.broadcast_to(decay_end[:, ln:ln + 1], (1, p)))
        bg_t = jnp.transpose(bg.astype(F32)).astype(BF16)
        st_ref[g] = state * jnp.concatenate(dec, axis=1) + _dot(bg_t, jnp.concatenate(upd, axis=1))


def _ssd_kernel(xsf, bmf, cmf, dtf, af, xsb, bmb, cmb, dtb, ab, dskf, dskb, yf_ref, yb_ref, stf, stb):
    @pl.when(pl.program_id(1) == 0)
    def _():
        stf[...] = jnp.zeros_like(stf)
        stb[...] = jnp.zeros_like(stb)

    _ssd_direction(xsf, bmf, cmf, dtf, af, dskf, stf, yf_ref, 0)
    _ssd_direction(xsb, bmb, cmb, dtb, ab, dskb, stb, yb_ref, 1)


def _scan_maps(nchunk):
    ctx_chunks = TM // CHUNK
    fwd = lambda b, t: (b * nchunk + t, 0)
    bwd = lambda b, t: (b * nchunk + jnp.where(t < ctx_chunks, ctx_chunks - 1 - t, nchunk + ctx_chunks - 1 - t), 0)
    return fwd, bwd


def _ssd_scan(xs, bm, cm, dtv, a, dsk_f, dsk_b, nb, seg):
    t_tok = xs.shape[0]
    nchunk = seg // CHUNK
    fwd, bwd = _scan_maps(nchunk)
    widths = (SSM_D_INNER, SSM_BC, SSM_BC, LANES, LANES)
    fixed = lambda b, t: (0, 0)
    in_specs = ([pl.BlockSpec((CHUNK, w), fwd) for w in widths] + [pl.BlockSpec((CHUNK, w), bwd) for w in widths]
                + [pl.BlockSpec((1, SSM_D_INNER), fixed)] * 2)
    arrs = (xs, bm, cm, dtv, a)
    return pl.pallas_call(
        _ssd_kernel,
        grid=(nb, nchunk),
        in_specs=in_specs,
        out_specs=[pl.BlockSpec((CHUNK, SSM_D_INNER), fwd), pl.BlockSpec((CHUNK, SSM_D_INNER), bwd)],
        out_shape=[jax.ShapeDtypeStruct((t_tok, SSM_D_INNER), BF16)] * 2,
        scratch_shapes=[pltpu.VMEM((SSM_GROUPS, SSM_STATE, SSM_HPG * SSM_HEAD_DIM), F32)] * 2,
        compiler_params=_params("parallel", "arbitrary"),
        name="ssd_scan",
    )(*arrs, *arrs, dsk_f, dsk_b)


def _log_sigmoid(x):
    return jnp.minimum(x, 0.0) - jnp.log1p(jnp.exp(-jnp.abs(x)))


def _mlstm_direction(qk_ref, v_ref, g_ref, gb_ref, c_st, m_st, h_ref, d):
    gates = g_ref[...] + gb_ref[...]
    lf_all = _log_sigmoid(gates)
    cum = _cumsum_rows(lf_all)
    nh = ML_HEADS
    ig = gates[:, 16 * d:16 * d + nh]
    lf = lf_all[:, 16 * d + nh:16 * d + 2 * nh]
    fc = cum[:, 16 * d + nh:16 * d + 2 * nh]
    tot = fc[CHUNK - 1:CHUNK, :]
    m_prev = m_st[0:1, 0:nh]
    if d == 0:
        row_v, col_v = fc, ig - fc
        inter = fc + m_prev
        logw = tot - fc + ig
    else:
        ec = fc - lf
        row_v, col_v = -ec, ec + ig
        inter = tot - ec + m_prev
        logw = ec + ig
    col_t = jnp.transpose(jnp.concatenate([col_v, jnp.zeros((CHUNK, LANES - nh), F32)], axis=1))
    m_new = jnp.maximum(tot + m_prev, jnp.max(logw, axis=0, keepdims=True))
    ws = jnp.exp(logw - m_new)
    cw = jnp.exp(tot + m_prev - m_new)
    mask = _tri(d == 0)
    dk, dv = ML_QK_DIM, ML_V_DIM
    ones_col = jnp.where(lax.broadcasted_iota(jnp.int32, (CHUNK, LANES), 1) == 0, 1.0, 0.0).astype(BF16)
    for h in range(nh):
        q = qk_ref[:, h * dk:(h + 1) * dk]
        k = qk_ref[:, nh * dk + h * dk:nh * dk + (h + 1) * dk] * (dk ** -0.5)
        v_ext = jnp.concatenate([v_ref[:, h * dv:(h + 1) * dv], ones_col], axis=1)
        logd = jnp.where(mask, row_v[:, h:h + 1] + col_t[h:h + 1, :], -jnp.inf)
        mt = jnp.maximum(inter[:, h:h + 1], jnp.max(logd, axis=-1, keepdims=True))
        sc = (_dot_nt(q, k) * jnp.exp(logd - mt)).astype(BF16)
        state = c_st[h]
        nd = _dot(sc, v_ext) + jnp.exp(inter[:, h:h + 1] - mt) * _dot(q, state.astype(BF16))
        den = jnp.maximum(jnp.abs(nd[:, dv:dv + 1]), jnp.exp(-mt))
        h_ref[:, h * dv:(h + 1) * dv] = (nd[:, :dv] / den).astype(h_ref.dtype)
        k_w = jnp.transpose(k.astype(F32) * ws[:, h:h + 1]).astype(BF16)
        c_st[h] = cw[:, h:h + 1] * state + _dot(k_w, v_ext)
    m_st[0:1, 0:nh] = m_new


def _mlstm_kernel(qkf, vf, gf, qkb, vb, gb, gbias, hf_ref, hb_ref, cf, mf, cb, mb):
    @pl.when(pl.program_id(1) == 0)
    def _():
        for r in (cf, mf, cb, mb):
            r[...] = jnp.zeros_like(r)

    _mlstm_direction(qkf, vf, gf, gbias, cf, mf, hf_ref, 0)
    _mlstm_direction(qkb, vb, gb, gbias, cb, mb, hb_ref, 1)


def _mlstm_scan(qk, v, g, gate_b, nb, seg):
    t_tok = qk.shape[0]
    nchunk = seg // CHUNK
    fwd, bwd = _scan_maps(nchunk)
    widths = (qk.shape[1], v.shape[1], LANES)
    in_specs = ([pl.BlockSpec((CHUNK, w), fwd) for w in widths] + [pl.BlockSpec((CHUNK, w), bwd) for w in widths]
                + [pl.BlockSpec((1, LANES), lambda b, t: (0, 0))])
    state = [pltpu.VMEM((ML_HEADS, ML_QK_DIM, ML_V_DIM + LANES), F32), pltpu.VMEM((8, LANES), F32)]
    return pl.pallas_call(
        _mlstm_kernel,
        grid=(nb, nchunk),
        in_specs=in_specs,
        out_specs=[pl.BlockSpec((CHUNK, v.shape[1]), fwd), pl.BlockSpec((CHUNK, v.shape[1]), bwd)],
        out_shape=[jax.ShapeDtypeStruct((t_tok, v.shape[1]), BF16)] * 2,
        scratch_shapes=state + state,
        compiler_params=_params("parallel", "arbitrary"),
        name="mlstm_scan",
    )(qk, v, g, qk, v, g, gate_b)


def _rms_rows(x, g_row):
    return x * lax.rsqrt(jnp.mean(x * x, axis=-1, keepdims=True) + EPS) * g_row


def _mla_q_kernel(cq_ref, g_ref, w_ref, cos_ref, sin_ref, q_ref):
    hb = _rms_rows(cq_ref[...], g_ref[...]).astype(BF16)
    n = q_ref.shape[1]
    _proj_columns(hb, w_ref, q_ref, 0, n, n, cos_ref[...], sin_ref[...])


def _mla_q(p3, gain, w_q, tables, nt):
    t_tok = p3.shape[0]
    n = w_q.shape[1] // 2
    return pl.pallas_call(
        _mla_q_kernel,
        grid=(t_tok // TM,),
        in_specs=[pl.BlockSpec((TM, MLA_RANK), lambda t: (t, 0)),
                  pl.BlockSpec((1, MLA_RANK), lambda t: (0, 0)),
                  pl.BlockSpec(w_q.shape, lambda t: (0, 0)),
                  pl.BlockSpec((TM, LANES), lambda t: (t % nt, 0)),
                  pl.BlockSpec((TM, LANES), lambda t: (t % nt, 0))],
        out_specs=pl.BlockSpec((TM, n), lambda t: (t, 0)),
        out_shape=jax.ShapeDtypeStruct((t_tok, n), BF16),
        compiler_params=_params("parallel"),
        name="mla_q",
    )(p3, gain.reshape(1, MLA_RANK), w_q, *tables)


def _mla_kv_kernel(ckv_ref, kr_ref, g_ref, wkt_ref, cos_ref, sin_ref, kt_ref, cn_ref):
    cn = _rms_rows(ckv_ref[...], g_ref[...])
    cn_ref[...] = cn.astype(cn_ref.dtype)
    kn_t = _dot(wkt_ref[...], jnp.transpose(cn).astype(BF16))
    kr_t = jnp.transpose(kr_ref[...])
    half = MLA_ROPE // 2
    x1, x2 = kr_t[0:half, :], kr_t[half:MLA_ROPE, :]
    cos, sin = cos_ref[...], sin_ref[...]
    roped = jnp.concatenate([x1 * cos - x2 * sin, x2 * cos + x1 * sin,
                             jnp.zeros((LANES - MLA_NOPE - MLA_ROPE, TM), F32)], axis=0)
    for h in range(MLA_HEADS):
        kt_ref[0, h * LANES:h * LANES + MLA_NOPE, :] = kn_t[h * MLA_NOPE:(h + 1) * MLA_NOPE, :].astype(kt_ref.dtype)
        kt_ref[0, h * LANES + MLA_NOPE:(h + 1) * LANES, :] = roped.astype(kt_ref.dtype)


def _mla_kv(p3, gain, wk_t, tables_t, nb, seg):
    t_tok = p3.shape[0]
    nt = seg // TM
    half = MLA_ROPE // 2
    return pl.pallas_call(
        _mla_kv_kernel,
        grid=(nb, nt),
        in_specs=[pl.BlockSpec((TM, MLA_RANK), lambda b, j: (b * nt + j, 1)),
                  pl.BlockSpec((TM, LANES), lambda b, j: (b * nt + j, 2 * MLA_RANK // LANES)),
                  pl.BlockSpec((1, MLA_RANK), lambda b, j: (0, 0)),
                  pl.BlockSpec(wk_t.shape, lambda b, j: (0, 0)),
                  pl.BlockSpec((half, TM), lambda b, j: (0, j)),
                  pl.BlockSpec((half, TM), lambda b, j: (0, j))],
        out_specs=[pl.BlockSpec((1, MLA_HEADS * LANES, TM), lambda b, j: (b, 0, j)),
                   pl.BlockSpec((TM, MLA_RANK), lambda b, j: (b * nt + j, 0))],
        out_shape=[jax.ShapeDtypeStruct((nb, MLA_HEADS * LANES, seg), BF16),
                   jax.ShapeDtypeStruct((t_tok, MLA_RANK), BF16)],
        compiler_params=_params("parallel", "parallel"),
        name="mla_kv",
    )(p3, p3, gain.reshape(1, MLA_RANK), wk_t, *tables_t)


def _mla_attn_kernel(q_ref, kt_ref, cn_ref, wv_ref, o_ref):
    @pl.when(pl.program_id(2) == 0)
    def _():
        o_ref[...] = jnp.zeros_like(o_ref)

    @pl.when(pl.program_id(2) > 0)
    def _():
        cn = cn_ref[...]
        for hh in range(2):
            s = _dot(q_ref[:, hh * LANES:(hh + 1) * LANES], kt_ref[0, hh * LANES:(hh + 1) * LANES, :])
            p = jnp.exp(s - jnp.max(s, axis=-1, keepdims=True))
            pc = _dot(p.astype(BF16), cn) / jnp.sum(p, axis=-1, keepdims=True)
            o = _dot(pc.astype(BF16), wv_ref[:, hh * MLA_V:(hh + 1) * MLA_V])
            o_ref[:, hh * MLA_V:(hh + 1) * MLA_V] = o.astype(o_ref.dtype)


def _mla_attention(q, k_t, cn, w_v, nb, seg):
    t_tok = q.shape[0]
    nt = seg // TM
    return pl.pallas_call(
        _mla_attn_kernel,
        grid=(nb, MLA_HEADS // 2, nt),
        in_specs=[pl.BlockSpec((TM, 2 * LANES), lambda b, hp, j: (b * nt + j, hp)),
                  pl.BlockSpec((1, 2 * LANES, seg), lambda b, hp, j: (b, hp, 0)),
                  pl.BlockSpec((seg, MLA_RANK), lambda b, hp, j: (b, 0)),
                  pl.BlockSpec((MLA_RANK, 2 * MLA_V), lambda b, hp, j: (0, hp))],
        out_specs=pl.BlockSpec((TM, 2 * MLA_V), lambda b, hp, j: (b * nt + j, hp)),
        out_shape=jax.ShapeDtypeStruct((t_tok, MLA_HEADS * MLA_V), BF16),
        compiler_params=_params("parallel", "parallel", "arbitrary"),
        name="mla_attention",
    )(q, k_t, cn, w_v)


def _final_kernel(x_ref, g_ref, o_ref):
    o_ref[0] = _rms_rows(x_ref[...], g_ref[...])


def _final_norm(x, gain, nb, seg):
    d = x.shape[1]
    nt = seg // TM
    return pl.pallas_call(
        _final_kernel,
        grid=(nb, nt - 1),
        in_specs=[pl.BlockSpec((TM, d), lambda b, j: (b * nt + 1 + j, 0)),
                  pl.BlockSpec((1, d), lambda b, j: (0, 0))],
        out_specs=pl.BlockSpec((1, TM, d), lambda b, j: (b, j, 0)),
        out_shape=jax.ShapeDtypeStruct((nb, seg - TM, d), F32),
        compiler_params=_params("parallel", "parallel"),
        name="final_norm",
    )(x, gain.reshape(1, d))


def _rope_angles(n_lat, rot_dim):
    rows = n_lat // GRID_W
    row = jnp.repeat(jnp.arange(rows), GRID_W).astype(F32)
    col = jnp.tile(jnp.arange(GRID_W), rows).astype(F32)
    quarter = rot_dim // 4
    inv = ROPE_BASE ** (-jnp.arange(quarter, dtype=F32) / quarter)
    ang = jnp.concatenate([row[:, None] * inv, col[:, None] * inv], axis=-1)
    return jnp.cos(ang), jnp.sin(ang)


def _with_ctx_rows(tab, fill):
    return jnp.concatenate([jnp.full((TM, tab.shape[1]), fill, F32), tab], axis=0)


def _attn_rope_tables(n_lat):
    cos, sin = _rope_angles(n_lat, ATT_HEAD_DIM)
    cos_h = jnp.concatenate([cos, cos], axis=1)
    sin_h = jnp.concatenate([-sin, sin], axis=1)
    reps = LANES // ATT_HEAD_DIM
    return (_with_ctx_rows(jnp.tile(cos_h, (1, reps)), 1.0), _with_ctx_rows(jnp.tile(sin_h, (1, reps)), 0.0))


def _swap_halves(w, n_heads, dim):
    w3 = w.reshape(w.shape[0], n_heads, 2, dim // 2)
    return w3[:, :, ::-1, :].reshape(w.shape[0], n_heads * dim)


def _mla_q_tables(n_lat):
    scale = (MLA_NOPE + MLA_ROPE) ** -0.5
    cos, sin = _rope_angles(n_lat, MLA_ROPE)
    zeros = jnp.zeros((n_lat, LANES - MLA_NOPE - MLA_ROPE), F32)
    cos_c = jnp.concatenate([jnp.ones((n_lat, MLA_NOPE), F32), cos, cos, zeros], axis=1) * scale
    sin_c = jnp.concatenate([jnp.zeros((n_lat, MLA_NOPE), F32), -sin, sin, zeros], axis=1) * scale
    ctx_cos = jnp.concatenate([jnp.full((TM, MLA_NOPE + MLA_ROPE), scale, F32),
                               jnp.zeros((TM, LANES - MLA_NOPE - MLA_ROPE), F32)], axis=1)
    return (jnp.concatenate([ctx_cos, cos_c], axis=0), _with_ctx_rows(sin_c, 0.0))


def _mla_q_weights(w_q_up):
    r = w_q_up.shape[0]
    w3 = w_q_up.reshape(r, MLA_HEADS, MLA_NOPE + MLA_ROPE)
    nope, rope = w3[:, :, :MLA_NOPE], w3[:, :, MLA_NOPE:]
    pad = jnp.zeros((r, MLA_HEADS, LANES - MLA_NOPE - MLA_ROPE), F32)
    half = MLA_ROPE // 2
    rope_sw = jnp.concatenate([rope[:, :, half:], rope[:, :, :half]], axis=-1)
    ext = jnp.concatenate([nope, rope, pad], axis=-1).reshape(r, MLA_HEADS * LANES)
    rot = jnp.concatenate([jnp.zeros_like(nope), rope_sw, pad], axis=-1).reshape(r, MLA_HEADS * LANES)
    return jnp.concatenate([ext, rot], axis=1).astype(BF16)


def kernel(x, c, ctx, c_ctx, norm1_g, norm2_g, w_mod, b_mod, moe_w_group, moe_b_group, moe_w_expert, moe_b_expert, moe_w_gate, moe_w_up, moe_w_down, attn_w_in, attn_sink, attn_w_out, ssm_w_in, ssm_conv_w, ssm_conv_b, ssm_dt_bias, ssm_a_log, ssm_d, ssm_norm_g, ssm_w_out, mlstm_w_in, mlstm_gate_b, mlstm_norm_g, mlstm_w_out, mla_w_in, mla_q_norm_g, mla_w_q_up, mla_kv_norm_g, mla_w_kv_up, mla_w_out, final_norm_g):
    nb, n_lat, d = x.shape
    assert ctx.shape[1] == TM and d == D_MODEL and n_lat % TM == 0
    depth = w_mod.shape[0]
    seg = TM + n_lat
    nt = seg // TM
    t_tok = nb * seg

    xs = jnp.concatenate([ctx, x], axis=1).reshape(t_tok, d)

    rows = -(-(nb + 1) // 8) * 8
    cvec = jnp.concatenate([c, c_ctx[None, :], jnp.zeros((rows - nb - 1, d), F32)], axis=0)
    mods = _modulation(cvec, w_mod, b_mod).reshape(depth, rows, ADALN_CHUNKS, d)
    mods = jnp.pad(mods, ((0, 0), (0, 0), (0, MOD_ROWS - ADALN_CHUNKS), (0, 0)))

    w_router = jnp.concatenate([moe_w_expert, moe_w_group,
                                jnp.zeros((depth, d, LANES - MOE_EXPERTS - MOE_GROUPS), F32)], axis=-1)
    b_router = jnp.concatenate([moe_b_expert, moe_b_group,
                                jnp.zeros((depth, LANES - MOE_EXPERTS - MOE_GROUPS), F32)], axis=-1)
    w_gu = jnp.concatenate([moe_w_gate, moe_w_up], axis=-1).astype(BF16)
    w_down = moe_w_down.astype(BF16)

    for i in range(depth):
        kind = i % 4
        mod_i = mods[i]
        if kind == 0:
            nq, nk = ATT_HEADS * ATT_HEAD_DIM, ATT_KV_HEADS * ATT_HEAD_DIM
            w_in = attn_w_in[i // 4]
            w_all = jnp.concatenate([w_in, _swap_halves(w_in[:, :nq], ATT_HEADS, ATT_HEAD_DIM),
                                     _swap_halves(w_in[:, nq:nq + nk], ATT_KV_HEADS, ATT_HEAD_DIM)],
                                    axis=1).astype(BF16)
            n_in = w_in.shape[1]
            q, k, v = _normproj(xs, mod_i, norm1_g[i], w_all,
                                [(0, nq, n_in), (nq, nk, n_in + nq), (nq + nk, nk, None)],
                                [BF16, BF16, BF16], nt, nb, tables=_attn_rope_tables(n_lat))
            o = _windowed_attention(q, k, v, attn_sink[i // 4], nb, seg)
            xs = _outproj([o], [nq], xs, mod_i, attn_w_out[i // 4].astype(BF16), _plain_prologue, nt, nb)
        elif kind == 1:
            j = i // 4
            w_in = jnp.pad(ssm_w_in[j], ((0, 0), (0, LANES - 2 * SSM_HEADS))).astype(BF16)
            z, xbc, dt = _normproj(xs, mod_i, norm1_g[i], w_in,
                                   [(0, SSM_D_INNER, None), (SSM_D_INNER, SSM_CONV_DIM, None),
                                    (SSM_D_INNER + SSM_CONV_DIM, LANES, None)],
                                   [BF16, BF16, F32], nt, nb)
            lane_pad = LANES - 2 * SSM_HEADS
            dt_bias = jnp.pad(ssm_dt_bias[j].reshape(1, -1), ((0, 0), (0, lane_pad)))
            a_neg = jnp.pad(-jnp.exp(ssm_a_log[j].astype(F32)).reshape(1, -1), ((0, 0), (0, lane_pad)))
            conv_w = jnp.pad(ssm_conv_w[j], ((0, 8 - SSM_CONV), (0, 0)))
            xc, bm, cm, dtv, a = _ssm_conv(xbc, dt, conv_w, ssm_conv_b[j].reshape(1, -1), dt_bias, a_neg, nt)
            dsk = jnp.repeat(ssm_d[j], SSM_HEAD_DIM, axis=1)
            yf, yb = _ssd_scan(xc, bm, cm, dtv, a, dsk[0:1], dsk[1:2], nb, seg)
            xs = _outproj([yf, yb, z], [SSM_D_INNER] * 3, xs, mod_i, ssm_w_out[j].astype(BF16), _ssm_prologue,
                          nt, nb, extra=(ssm_norm_g[j].reshape(1, -1),))
        elif kind == 2:
            j = i // 4
            nqk, nv = 2 * ML_HEADS * ML_QK_DIM, ML_HEADS * ML_V_DIM
            w_in = jnp.pad(mlstm_w_in[j], ((0, 0), (0, LANES - 4 * ML_HEADS))).astype(BF16)
            qk, v, o, g = _normproj(xs, mod_i, norm1_g[i], w_in,
                                    [(0, nqk, None), (nqk, nv, None), (nqk + nv, nv, None), (nqk + 2 * nv, LANES, None)],
                                    [BF16, BF16, BF16, F32], nt, nb)
            gate_b = jnp.pad(mlstm_gate_b[j].reshape(1, -1), ((0, 0), (0, LANES - 4 * ML_HEADS)))
            hf, hb = _mlstm_scan(qk, v, g, gate_b, nb, seg)
            xs = _outproj([hf, hb, o], [nv] * 3, xs, mod_i, mlstm_w_out[j].astype(BF16), _mlstm_prologue,
                          nt, nb, extra=(mlstm_norm_g[j].reshape(1, -1),))
        else:
            j = i // 4
            w_in = jnp.pad(mla_w_in[j], ((0, 0), (0, LANES - MLA_ROPE))).astype(BF16)
            (p3,) = _normproj(xs, mod_i, norm1_g[i], w_in, [(0, w_in.shape[1], None)], [F32], nt, nb)
            q = _mla_q(p3, mla_q_norm_g[j], _mla_q_weights(mla_w_q_up[j]), _mla_q_tables(n_lat), nt)
            w_kv = mla_w_kv_up[j].reshape(MLA_RANK, MLA_HEADS, MLA_NOPE + MLA_V)
            wk_t = jnp.transpose(w_kv[:, :, :MLA_NOPE].reshape(MLA_RANK, MLA_HEADS * MLA_NOPE)).astype(BF16)
            w_v = w_kv[:, :, MLA_NOPE:].reshape(MLA_RANK, MLA_HEADS * MLA_V).astype(BF16)
            cos, sin = _rope_angles(n_lat, MLA_ROPE)
            tables_t = (jnp.transpose(_with_ctx_rows(cos, 1.0)), jnp.transpose(_with_ctx_rows(sin, 0.0)))
            k_t, cn = _mla_kv(p3, mla_kv_norm_g[j], wk_t, tables_t, nb, seg)
            o = _mla_attention(q, k_t, cn, w_v, nb, seg)
            xs = _outproj([o], [MLA_HEADS * MLA_V], xs, mod_i, mla_w_out[j].astype(BF16), _plain_prologue, nt, nb)

        xs = _moe(xs, mod_i, norm2_g[i], w_router[i], b_router[i].reshape(1, -1), w_gu[i], w_down[i], nt, nb)

    return _final_norm(xs, final_norm_g, nb, seg)
```

```python
import functools
import math

import jax
import jax.numpy as jnp
from jax import lax
from jax.experimental import pallas as pl
from jax.experimental.pallas import tpu as pltpu

F32 = jnp.float32
BF16 = jnp.bfloat16

D_MODEL = 1024
GRID_W = 64
EPS = 1e-6
ROPE_BASE = 10000.0
ADALN_CHUNKS = 6
CHUNK = 128
TM = 256
MOD_ROWS = 8
LANES = 128
V7X_VMEM_LIMIT = 48 * 1024 * 1024

ATT_HEADS, ATT_KV_HEADS, ATT_HEAD_DIM, WINDOW = 16, 4, 64, 128
ATT_GROUP = ATT_HEADS // ATT_KV_HEADS
SSM_D_INNER, SSM_HEAD_DIM, SSM_HEADS, SSM_GROUPS, SSM_STATE, SSM_CONV = 2048, 64, 32, 4, 128, 5
SSM_HPG = SSM_HEADS // SSM_GROUPS
SSM_BC = SSM_GROUPS * SSM_STATE
SSM_CONV_DIM = SSM_D_INNER + 2 * SSM_BC
ML_HEADS, ML_QK_DIM, ML_V_DIM = 8, 64, 128
MLA_HEADS, MLA_RANK, MLA_NOPE, MLA_ROPE, MLA_V = 16, 256, 64, 32, 64
MOE_GROUPS, MOE_PER_GROUP, MOE_EXPERTS, MOE_FF = 4, 4, 16, 256
MOE_ROWS = 1024
MLA_KEY_TILE = 512
LOG2_E = 1.4426950408889634


def _dot(a, b):
    return jnp.dot(a, b, preferred_element_type=F32)


def _dot_nt(a, b):
    return lax.dot_general(a, b, (((1,), (1,)), ((), ())), preferred_element_type=F32)


def _split_bf16(x):
    hi = x.astype(BF16)
    lo = (x - hi.astype(F32)).astype(BF16)
    return hi, lo


def _dot_split(a, b):
    a_hi, a_lo = _split_bf16(a)
    b_hi, b_lo = _split_bf16(b)
    return _dot(a_hi, b_hi) + _dot(a_lo, b_hi) + _dot(a_hi, b_lo)


def _sigmoid(x):
    return 1.0 / (1.0 + jnp.exp(-x))


def _silu(x):
    return x * _sigmoid(x)


def _softplus(x):
    return jnp.maximum(x, 0.0) + jnp.log1p(jnp.exp(-jnp.abs(x)))


def _params(*sem):
    return pltpu.CompilerParams(dimension_semantics=sem, vmem_limit_bytes=V7X_VMEM_LIMIT)


def _mod_row(t, nt, nb):
    return jnp.where(t % nt == 0, nb, t // nt)


def _mod_kernel(c_ref, w_ref, b_ref, o_ref):
    o_ref[0] = _dot_split(_silu(c_ref[...]), w_ref[0]) + b_ref[0]


def _modulation(cvec, w_mod, b_mod):
    depth, d, n = w_mod.shape
    tn = 1536
    rows = cvec.shape[0]
    return pl.pallas_call(
        _mod_kernel,
        grid=(depth, n // tn),
        in_specs=[pl.BlockSpec((rows, d), lambda l, j: (0, 0)),
                  pl.BlockSpec((1, d, tn), lambda l, j: (l, 0, j)),
                  pl.BlockSpec((1, 1, tn), lambda l, j: (l, 0, j))],
        out_specs=pl.BlockSpec((1, rows, tn), lambda l, j: (l, 0, j)),
        out_shape=jax.ShapeDtypeStruct((depth, rows, n), F32),
        compiler_params=_params("arbitrary", "arbitrary"),
        name="modulation",
    )(cvec, w_mod, b_mod.reshape(depth, 1, n))


def _normed(x, g_row, mod, sh_row, sc_row):
    y = x * lax.rsqrt(jnp.mean(x * x, axis=-1, keepdims=True) + EPS) * g_row
    return y * (1.0 + mod[sc_row:sc_row + 1, :]) + mod[sh_row:sh_row + 1, :]


def _proj_columns(hb, w_ref, o_ref, start, width, rot_start, cos, sin, chunk=512):
    for c in range(0, width, chunk):
        cw = min(chunk, width - c)
        acc = _dot(hb, w_ref[:, start + c:start + c + cw])
        if rot_start is not None:
            rot = _dot(hb, w_ref[:, rot_start + c:rot_start + c + cw])
            reps = cw // LANES
            acc = acc * jnp.tile(cos, (1, reps)) + rot * jnp.tile(sin, (1, reps))
        o_ref[:, c:c + cw] = acc.astype(o_ref.dtype)


def _normproj_kernel(*refs, outs, has_rope):
    x_ref, mod_ref, g_ref, w_ref = refs[:4]
    k = 4
    cos = sin = None
    if has_rope:
        cos, sin = refs[4][...], refs[5][...]
        k = 6
    hb = _normed(x_ref[...], g_ref[...], mod_ref[0], 0, 1).astype(BF16)
    for o_ref, (start, width, rot_start) in zip(refs[k:], outs):
        _proj_columns(hb, w_ref, o_ref, start, width, rot_start, cos, sin)


def _normproj(x, mods, gain, w, outs, out_dtypes, nt, nb, tables=None):
    t_tok, d = x.shape
    n = w.shape[1]
    in_specs = [pl.BlockSpec((TM, d), lambda t: (t, 0)),
                pl.BlockSpec((1, MOD_ROWS, d), lambda t: (_mod_row(t, nt, nb), 0, 0)),
                pl.BlockSpec((1, d), lambda t: (0, 0)),
                pl.BlockSpec((d, n), lambda t: (0, 0))]
    args = [x, mods, gain.reshape(1, d), w]
    if tables is not None:
        in_specs += [pl.BlockSpec((TM, LANES), lambda t: (t % nt, 0))] * 2
        args += list(tables)
    return pl.pallas_call(
        functools.partial(_normproj_kernel, outs=tuple(outs), has_rope=tables is not None),
        grid=(t_tok // TM,),
        in_specs=in_specs,
        out_specs=[pl.BlockSpec((TM, o[1]), lambda t: (t, 0)) for o in outs],
        out_shape=[jax.ShapeDtypeStruct((t_tok, o[1]), dt) for o, dt in zip(outs, out_dtypes)],
        compiler_params=_params("parallel"),
        name="normproj",
    )(*args)


def _outproj_kernel(*refs, prologue, n_in):
    ins = refs[:n_in]
    x_ref, mod_ref, w_ref, o_ref = refs[n_in:]
    a = prologue(*ins)
    y = _dot(a, w_ref[...])
    o_ref[...] = x_ref[...] + mod_ref[0][2:3, :] * y


def _outproj(ins, in_widths, x, mods, w, prologue, nt, nb, extra=(), ins_t=()):
    t_tok, d = x.shape
    in_specs = [pl.BlockSpec((1, a.shape[1], TM), lambda t: (t // nt, 0, t % nt)) for a in ins_t]
    in_specs += [pl.BlockSpec((TM, wd), lambda t: (t, 0)) for wd in in_widths]
    in_specs += [pl.BlockSpec(e.shape, lambda t: (0, 0)) for e in extra]
    n_in = len(in_specs)
    in_specs += [pl.BlockSpec((TM, d), lambda t: (t, 0)),
                 pl.BlockSpec((1, MOD_ROWS, d), lambda t: (_mod_row(t, nt, nb), 0, 0)),
                 pl.BlockSpec(w.shape, lambda t: (0, 0))]
    return pl.pallas_call(
        functools.partial(_outproj_kernel, prologue=prologue, n_in=n_in),
        grid=(t_tok // TM,),
        in_specs=in_specs,
        out_specs=pl.BlockSpec((TM, d), lambda t: (t, 0)),
        out_shape=jax.ShapeDtypeStruct((t_tok, d), F32),
        input_output_aliases={n_in: 0},
        compiler_params=_params("parallel"),
        name="outproj",
    )(*ins_t, *ins, *extra, x, mods, w)


def _group_rms(y, n_groups):
    width = y.shape[1] // n_groups
    parts = []
    for g in range(n_groups):
        yg = y[:, g * width:(g + 1) * width]
        parts.append(yg * lax.rsqrt(jnp.mean(yg * yg, axis=-1, keepdims=True) + EPS))
    return jnp.concatenate(parts, axis=1)


def _plain_prologue(o_ref):
    return o_ref[...]


def _ssm_prologue(yf_ref, yb_ref, z_ref, g_ref):
    y = jnp.transpose(yf_ref[0].astype(F32) + yb_ref[0].astype(F32)) * _silu(z_ref[...].astype(F32))
    return (_group_rms(y, SSM_GROUPS) * g_ref[...]).astype(BF16)


def _mlstm_prologue(hf_ref, hb_ref, o_ref, g_ref):
    h = jnp.transpose(hf_ref[0].astype(F32) + hb_ref[0].astype(F32))
    return (_group_rms(h, ML_HEADS) * g_ref[...] * _sigmoid(o_ref[...].astype(F32))).astype(BF16)


def _route(logits):
    lane = lax.broadcasted_iota(jnp.int32, logits.shape, 1).astype(F32)
    neg = -jnp.inf
    lg = jnp.where((lane >= MOE_EXPERTS) & (lane < MOE_EXPERTS + MOE_GROUPS), logits, neg)
    gmax = jnp.max(lg, axis=-1, keepdims=True)
    g_sel = jnp.min(jnp.where(lg == gmax, lane, LANES), axis=-1, keepdims=True) - MOE_EXPERTS
    p_g = 1.0 / jnp.sum(jnp.exp(lg - gmax), axis=-1, keepdims=True)
    in_group = (lane >= g_sel * MOE_PER_GROUP) & (lane < (g_sel + 1) * MOE_PER_GROUP)
    le = jnp.where(in_group, logits, neg)
    v1 = jnp.max(le, axis=-1, keepdims=True)
    i1 = jnp.min(jnp.where(le == v1, lane, LANES), axis=-1, keepdims=True)
    le2 = jnp.where(lane == i1, neg, le)
    v2 = jnp.max(le2, axis=-1, keepdims=True)
    i2 = jnp.min(jnp.where(le2 == v2, lane, LANES), axis=-1, keepdims=True)
    e2 = jnp.exp(v2 - v1)
    w1 = p_g / (1.0 + e2)
    return jnp.where(lane == i1, w1, 0.0) + jnp.where(lane == i2, w1 * e2, 0.0)


def _moe_kernel(x_ref, mods_ref, g_ref, wr_ref, br_ref, wgu_ref, wd_ref, o_ref, h_scr, comb_scr, *, nt, nb, sub):
    i = pl.program_id(0)
    e = pl.program_id(1)

    @pl.when(e == 0)
    def _():
        for s in range(sub):
            rows = slice(s * TM, (s + 1) * TM)
            mod = mods_ref[_mod_row(i * sub + s, nt, nb)]
            x = x_ref[rows, :]
            h = _normed(x, g_ref[...], mod, 3, 4)
            h_scr[rows, :] = h.astype(BF16)
            comb_scr[rows, :] = _route(_dot_split(h, wr_ref[...]) + br_ref[...])
            o_ref[rows, :] = x

    hb = h_scr[...]
    gu = _dot(hb, wgu_ref[0])
    act = (_silu(gu[:, :MOE_FF]) * gu[:, MOE_FF:]).astype(BF16)
    y = _dot(act, wd_ref[0])
    lane = lax.broadcasted_iota(jnp.int32, comb_scr.shape, 1)
    cw = jnp.sum(jnp.where(lane == e, comb_scr[...], 0.0), axis=-1, keepdims=True)
    for s in range(sub):
        rows = slice(s * TM, (s + 1) * TM)
        gate = mods_ref[_mod_row(i * sub + s, nt, nb)][5:6, :]
        o_ref[rows, :] += gate * (cw[rows, :] * y[rows, :])


def _moe(x, mods, gain, w_router, b_router, w_gu, w_down, nt, nb):
    t_tok, d = x.shape
    rows = MOE_ROWS if t_tok % MOE_ROWS == 0 else TM
    sub = rows // TM
    return pl.pallas_call(
        functools.partial(_moe_kernel, nt=nt, nb=nb, sub=sub),
        grid=(t_tok // rows, MOE_EXPERTS),
        in_specs=[pl.BlockSpec((rows, d), lambda i, e: (i, 0)),
                  pl.BlockSpec(mods.shape, lambda i, e: (0, 0, 0)),
                  pl.BlockSpec((1, d), lambda i, e: (0, 0)),
                  pl.BlockSpec((d, LANES), lambda i, e: (0, 0)),
                  pl.BlockSpec((1, LANES), lambda i, e: (0, 0)),
                  pl.BlockSpec((1, d, 2 * MOE_FF), lambda i, e: (e, 0, 0)),
                  pl.BlockSpec((1, MOE_FF, d), lambda i, e: (e, 0, 0))],
        out_specs=pl.BlockSpec((rows, d), lambda i, e: (i, 0)),
        out_shape=jax.ShapeDtypeStruct((t_tok, d), F32),
        scratch_shapes=[pltpu.VMEM((rows, d), BF16), pltpu.VMEM((rows, LANES), F32)],
        input_output_aliases={0: 0},
        compiler_params=_params("parallel", "arbitrary"),
        name="moe",
    )(x, mods, gain.reshape(1, d), w_router, b_router, w_gu, w_down)


def _attn_kernel(sink_ref, q_ref, kp_ref, kc_ref, kn_ref, vp_ref, vc_ref, vn_ref, kx_ref, vx_ref, o_ref, *, n_lat):
    j = pl.program_id(1)
    jl = j - TM // CHUNK
    k_loc = jnp.concatenate([kp_ref[...], kc_ref[...], kn_ref[...]], axis=0)
    v_loc = jnp.concatenate([vp_ref[...], vc_ref[...], vn_ref[...]], axis=0)
    kx, vx = kx_ref[...], vx_ref[...]
    qi = lax.broadcasted_iota(jnp.int32, (CHUNK, 3 * CHUNK), 0)
    kj = lax.broadcasted_iota(jnp.int32, (CHUNK, 3 * CHUNK), 1)
    kpos = (jl - 1) * CHUNK + kj
    ok = (kj >= qi + CHUNK - WINDOW) & (kj <= qi + CHUNK + WINDOW) & (kpos >= 0) & (kpos < n_lat) & (jl >= 0)
    bias = jnp.concatenate([jnp.where(ok, 0.0, -jnp.inf)] * ATT_GROUP, axis=0)
    row_head = lax.broadcasted_iota(jnp.int32, (ATT_GROUP * CHUNK, 1), 0) // CHUNK
    dh = ATT_HEAD_DIM
    for g in range(ATT_KV_HEADS):
        cols = slice(g * dh, (g + 1) * dh)
        qg = jnp.concatenate([q_ref[:, (g * ATT_GROUP + a) * dh:(g * ATT_GROUP + a + 1) * dh]
                              for a in range(ATT_GROUP)], axis=0) * (dh ** -0.5)
        s_loc = _dot_nt(qg, k_loc[:, cols]) + bias
        s_ctx = _dot_nt(qg, kx[:, cols])
        sink = jnp.zeros((ATT_GROUP * CHUNK, 1), F32)
        for a in range(ATT_GROUP):
            sink = jnp.where(row_head == a, sink_ref[g * ATT_GROUP + a], sink)
        m = jnp.maximum(jnp.maximum(jnp.max(s_loc, axis=-1, keepdims=True),
                                    jnp.max(s_ctx, axis=-1, keepdims=True)), sink)
        p_loc = jnp.exp(s_loc - m)
        p_ctx = jnp.exp(s_ctx - m)
        den = (jnp.sum(p_loc, axis=-1, keepdims=True) + jnp.sum(p_ctx, axis=-1, keepdims=True)
               + jnp.exp(sink - m))
        og = (_dot(p_loc.astype(BF16), v_loc[:, cols]) + _dot(p_ctx.astype(BF16), vx[:, cols])) / den
        for a in range(ATT_GROUP):
            h = g * ATT_GROUP + a
            o_ref[:, h * dh:(h + 1) * dh] = og[a * CHUNK:(a + 1) * CHUNK, :].astype(o_ref.dtype)


def _windowed_attention(q, k, v, sink, nb, seg):
    t_tok = q.shape[0]
    nblk = seg // CHUNK
    ctx_blk = TM // CHUNK
    kvw = ATT_KV_HEADS * ATT_HEAD_DIM

    def blk(off):
        return lambda b, j, *_: (b * nblk + jnp.clip(j + off, ctx_blk, nblk - 1), 0)

    kv_spec = [pl.BlockSpec((CHUNK, kvw), blk(o)) for o in (-1, 0, 1)]
    ctx_spec = pl.BlockSpec((TM, kvw), lambda b, j, *_: (b * (seg // TM), 0))
    return pl.pallas_call(
        functools.partial(_attn_kernel, n_lat=seg - TM),
        grid_spec=pltpu.PrefetchScalarGridSpec(
            num_scalar_prefetch=1,
            grid=(nb, nblk),
            in_specs=[pl.BlockSpec((CHUNK, q.shape[1]), lambda b, j, *_: (b * nblk + j, 0))]
            + kv_spec + kv_spec + [ctx_spec, ctx_spec],
            out_specs=pl.BlockSpec((CHUNK, q.shape[1]), lambda b, j, *_: (b * nblk + j, 0)),
        ),
        out_shape=jax.ShapeDtypeStruct((t_tok, q.shape[1]), BF16),
        compiler_params=_params("parallel", "parallel"),
        name="windowed_attention",
    )(sink, q, k, k, k, v, v, v, k, v)


def _conv_kernel(cur_ref, prev_ref, next_ref, dt_ref, w_ref, b_ref, dtb_ref, aneg_ref,
                 xs_ref, bm_ref, cm_ref, dtv_ref, a_ref, ext, *, nt):
    t = pl.program_id(0)
    tl = t % nt
    halo = 8
    first = (tl == 0) | (tl == 1)
    last = (tl == 0) | (tl == nt - 1)
    ext[0:halo, :] = jnp.where(first, 0.0, prev_ref[...].astype(F32)[halo:2 * halo, :])
    ext[halo:halo + TM, :] = cur_ref[...].astype(F32)
    ext[halo + TM:2 * halo + TM, :] = jnp.where(last, 0.0, next_ref[...].astype(F32)[0:halo, :])
    pad = SSM_CONV // 2
    chunk = 512
    for c in range(0, SSM_CONV_DIM, chunk):
        acc = jnp.broadcast_to(b_ref[:, c:c + chunk], (TM, chunk))
        for k in range(SSM_CONV):
            acc = acc + w_ref[k:k + 1, c:c + chunk] * ext[halo - pad + k:halo - pad + k + TM, c:c + chunk]
        y = _silu(acc)
        if c < SSM_D_INNER:
            xs_ref[0, c:c + chunk, :] = jnp.transpose(y).astype(xs_ref.dtype)
        elif c < SSM_D_INNER + SSM_BC:
            bm_ref[:, c - SSM_D_INNER:c - SSM_D_INNER + chunk] = y.astype(bm_ref.dtype)
        else:
            off = c - SSM_D_INNER - SSM_BC
            cm_ref[0, off:off + chunk, :] = jnp.transpose(y).astype(cm_ref.dtype)
    dtv = _softplus(dt_ref[...] + dtb_ref[...])
    dtv_ref[0] = jnp.transpose(dtv)
    a_ref[0] = jnp.transpose(dtv * aneg_ref[...])


def _ssm_conv(xbc, dt, conv_w, conv_b, dt_bias, a_neg, nt):
    t_tok = xbc.shape[0]
    n16 = t_tok // 16
    seg = nt * TM
    nb = t_tok // seg
    row = lambda t: (t, 0)
    col = lambda t: (t // nt, 0, t % nt)
    fixed = lambda t: (0, 0)
    return pl.pallas_call(
        functools.partial(_conv_kernel, nt=nt),
        grid=(t_tok // TM,),
        in_specs=[pl.BlockSpec((TM, SSM_CONV_DIM), row),
                  pl.BlockSpec((16, SSM_CONV_DIM), lambda t: (jnp.maximum(t * (TM // 16) - 1, 0), 0)),
                  pl.BlockSpec((16, SSM_CONV_DIM), lambda t: (jnp.minimum((t + 1) * (TM // 16), n16 - 1), 0)),
                  pl.BlockSpec((TM, LANES), row),
                  pl.BlockSpec((8, SSM_CONV_DIM), fixed),
                  pl.BlockSpec((1, SSM_CONV_DIM), fixed),
                  pl.BlockSpec((1, LANES), fixed),
                  pl.BlockSpec((1, LANES), fixed)],
        out_specs=[pl.BlockSpec((1, SSM_D_INNER, TM), col), pl.BlockSpec((TM, SSM_BC), row),
                   pl.BlockSpec((1, SSM_BC, TM), col), pl.BlockSpec((1, LANES, TM), col),
                   pl.BlockSpec((1, LANES, TM), col)],
        out_shape=[jax.ShapeDtypeStruct((nb, SSM_D_INNER, seg), BF16), jax.ShapeDtypeStruct((t_tok, SSM_BC), BF16),
                   jax.ShapeDtypeStruct((nb, SSM_BC, seg), BF16), jax.ShapeDtypeStruct((nb, LANES, seg), F32),
                   jax.ShapeDtypeStruct((nb, LANES, seg), F32)],
        scratch_shapes=[pltpu.VMEM((TM + 16, SSM_CONV_DIM), F32)],
        compiler_params=_params("parallel"),
        name="ssm_conv",
    )(xbc, xbc, xbc, dt, conv_w, conv_b, dt_bias, a_neg)


def _tri(lower):
    r = lax.broadcasted_iota(jnp.int32, (CHUNK, CHUNK), 0)
    c = lax.broadcasted_iota(jnp.int32, (CHUNK, CHUNK), 1)
    return (c <= r) if lower else (c >= r)


def _cumsum_rows(x):
    tri = jnp.where(_tri(True), 1.0, 0.0).astype(BF16)
    hi, lo = _split_bf16(x)
    return _dot(tri, hi) + _dot(tri, lo)


def _ssd_direction(xs_ref, bm_ref, ct_ref, dt_ref, a_ref, dsk_ref, st_ref, y_ref, d):
    a_t = a_ref[0]
    dt_t = dt_ref[0]
    upper = jnp.where(_tri(False), 1.0, 0.0).astype(BF16)
    a_hi, a_lo = _split_bf16(a_t)
    cum = _dot(a_hi, upper) + _dot(a_lo, upper)
    cum_end = cum[:, CHUNK - 1:CHUNK]
    if d == 0:
        lane_v, sub_v = cum, -cum
        inter = jnp.exp(cum)
        w_upd = jnp.exp(cum_end - cum) * dt_t
    else:
        ecum = cum - a_t
        lane_v, sub_v = -ecum, ecum
        inter = jnp.exp(cum_end - ecum)
        w_upd = jnp.exp(ecum) * dt_t
    sub_c = jnp.transpose(sub_v)
    decay_end = jnp.exp(cum_end)
    mask = _tri(d == 1)
    p = SSM_HEAD_DIM
    for g in range(SSM_GROUPS):
        bg = bm_ref[:, g * SSM_STATE:(g + 1) * SSM_STATE]
        cg_t = ct_ref[0, g * SSM_STATE:(g + 1) * SSM_STATE, :]
        cb_t = _dot(bg, cg_t)
        state = st_ref[g]
        y_in = _dot(state.astype(BF16), cg_t)
        upd = []
        for e in range(SSM_HPG):
            h = g * SSM_HPG + e
            ln = d * SSM_HEADS + h
            rows = slice(e * p, (e + 1) * p)
            seg = jnp.where(mask, sub_c[:, ln:ln + 1] + lane_v[ln:ln + 1, :], -jnp.inf)
            m_t = (cb_t * jnp.exp(seg)).astype(BF16)
            xf = xs_ref[0, h * p:(h + 1) * p, :].astype(F32)
            u = (xf * dt_t[ln:ln + 1, :]).astype(BF16)
            yh = _dot(u, m_t) + y_in[rows, :] * inter[ln:ln + 1, :] + dsk_ref[d, h] * xf
            y_ref[0, h * p:(h + 1) * p, :] = yh.astype(y_ref.dtype)
            upd.append((xf * w_upd[ln:ln + 1, :]).astype(BF16))
        new = _dot(jnp.concatenate(upd, axis=0), bg)
        for e in range(SSM_HPG):
            ln = d * SSM_HEADS + g * SSM_HPG + e
            rows = slice(e * p, (e + 1) * p)
            st_ref[g, rows, :] = state[rows, :] * decay_end[ln:ln + 1, :] + new[rows, :]


def _ssd_kernel(dsk_ref, xsf, bmf, ctf, dtf, af, xsb, bmb, ctb, dtb, ab, yf_ref, yb_ref, stf, stb):
    @pl.when(pl.program_id(1) == 0)
    def _():
        stf[...] = jnp.zeros_like(stf)
        stb[...] = jnp.zeros_like(stb)

    _ssd_direction(xsf, bmf, ctf, dtf, af, dsk_ref, stf, yf_ref, 0)
    _ssd_direction(xsb, bmb, ctb, dtb, ab, dsk_ref, stb, yb_ref, 1)


def _scan_maps(nchunk):
    ctx_chunks = TM // CHUNK
    fwd = lambda b, t: (b * nchunk + t, 0)
    bwd = lambda b, t: (b * nchunk + jnp.where(t < ctx_chunks, ctx_chunks - 1 - t, nchunk + ctx_chunks - 1 - t), 0)
    return fwd, bwd


def _ssd_scan(xs_t, bm, c_t, dt_t, a_t, d_skip, nb, seg):
    nchunk = seg // CHUNK
    fwd_rows, bwd_rows = _scan_maps(nchunk)

    def specs(rows_map):
        cols_map = lambda b, t, *_: (b, 0, rows_map(b, t)[0] - b * nchunk)
        return [pl.BlockSpec((1, SSM_D_INNER, CHUNK), cols_map),
                pl.BlockSpec((CHUNK, SSM_BC), lambda b, t, *_: rows_map(b, t)),
                pl.BlockSpec((1, SSM_BC, CHUNK), cols_map),
                pl.BlockSpec((1, LANES, CHUNK), cols_map),
                pl.BlockSpec((1, LANES, CHUNK), cols_map)]

    arrs = (xs_t, bm, c_t, dt_t, a_t)
    return pl.pallas_call(
        _ssd_kernel,
        grid_spec=pltpu.PrefetchScalarGridSpec(
            num_scalar_prefetch=1,
            grid=(nb, nchunk),
            in_specs=specs(fwd_rows) + specs(bwd_rows),
            out_specs=[specs(fwd_rows)[0], specs(bwd_rows)[0]],
            scratch_shapes=[pltpu.VMEM((SSM_GROUPS, SSM_HPG * SSM_HEAD_DIM, SSM_STATE), F32)] * 2,
        ),
        out_shape=[jax.ShapeDtypeStruct(xs_t.shape, BF16)] * 2,
        compiler_params=_params("parallel", "arbitrary"),
        name="ssd_scan",
    )(d_skip, *arrs, *arrs)


def _log_sigmoid(x):
    return jnp.minimum(x, 0.0) - jnp.log1p(jnp.exp(-jnp.abs(x)))


ML_STATE_ROWS = ML_V_DIM + 16


def _mlstm_direction(qk_ref, v_ref, g_ref, gb_ref, c_st, m_st, h_ref, d):
    nh, dk, dv = ML_HEADS, ML_QK_DIM, ML_V_DIM
    g_t = jnp.transpose(g_ref[...] + gb_ref[...])
    ig = g_t[16 * d:16 * d + nh, :]
    lf = _log_sigmoid(g_t[16 * d + nh:16 * d + 2 * nh, :])
    upper = jnp.where(_tri(False), 1.0, 0.0).astype(BF16)
    lf_hi, lf_lo = _split_bf16(lf)
    fc = _dot(lf_hi, upper) + _dot(lf_lo, upper)
    tot = fc[:, CHUNK - 1:CHUNK]
    m_prev = m_st[...]
    if d == 0:
        lane_v, sub_v = fc, ig - fc
        inter = fc + m_prev
        logw = tot - fc + ig
    else:
        ec = fc - lf
        lane_v, sub_v = -ec, ec + ig
        inter = tot - ec + m_prev
        logw = ec + ig
    sub_c = jnp.transpose(jnp.concatenate([sub_v, jnp.zeros((LANES - nh, CHUNK), F32)], axis=0))
    m_new = jnp.maximum(tot + m_prev, jnp.max(logw, axis=-1, keepdims=True))
    ws = jnp.exp(logw - m_new)
    cw = jnp.exp(tot + m_prev - m_new)
    mask = _tri(d == 1)
    q_t = jnp.transpose(qk_ref[:, :nh * dk].astype(F32))
    v_t = jnp.transpose(v_ref[...].astype(F32))
    lane = lax.broadcasted_iota(jnp.int32, (1, LANES), 1)
    zeros_q = jnp.zeros((dk, CHUNK), F32)
    for h in range(nh):
        half = h % 2
        k_pair = qk_ref[:, nh * dk + (h - half) * dk:nh * dk + (h - half + 2) * dk] * (dk ** -0.5)
        qh = q_t[h * dk:(h + 1) * dk, :]
        q_m = jnp.concatenate([zeros_q, qh] if half else [qh, zeros_q], axis=0).astype(BF16)
        logd = jnp.where(mask, sub_c[:, h:h + 1] + lane_v[h:h + 1, :], -jnp.inf)
        mt = jnp.maximum(inter[h:h + 1, :], jnp.max(logd, axis=0, keepdims=True))
        sc = _dot(k_pair, q_m) * jnp.exp(logd - mt)
        vh = v_t[h * dv:(h + 1) * dv, :]
        state = c_st[h]
        cq = _dot(state.astype(BF16), q_m)
        w_int = jnp.exp(inter[h:h + 1, :] - mt)
        num = _dot(vh.astype(BF16), sc.astype(BF16)) + w_int * cq[:dv, :]
        den = jnp.sum(sc, axis=0, keepdims=True) + w_int * cq[dv:dv + 1, :]
        h_ref[0, h * dv:(h + 1) * dv, :] = (num / jnp.maximum(jnp.abs(den), jnp.exp(-mt))).astype(h_ref.dtype)
        ws_h = ws[h:h + 1, :]
        lhs = jnp.concatenate([vh * ws_h, jnp.broadcast_to(ws_h, (ML_STATE_ROWS - dv, CHUNK))], axis=0)
        own = (lane >= half * dk) & (lane < (half + 1) * dk)
        row = lax.broadcasted_iota(jnp.int32, (ML_STATE_ROWS, 1), 0)
        new = cw[h:h + 1, :] * state + _dot(lhs.astype(BF16), k_pair)
        c_st[h] = jnp.where(own & (row <= dv), new, 0.0)
    m_st[...] = jnp.broadcast_to(m_new, m_st.shape)


def _mlstm_kernel(qkf, vf, gf, qkb, vb, gb, gbias, hf_ref, hb_ref, cf, mf, cb, mb):
    @pl.when(pl.program_id(1) == 0)
    def _():
        for r in (cf, mf, cb, mb):
            r[...] = jnp.zeros_like(r)

    _mlstm_direction(qkf, vf, gf, gbias, cf, mf, hf_ref, 0)
    _mlstm_direction(qkb, vb, gb, gbias, cb, mb, hb_ref, 1)


def _mlstm_scan(qk, v, g, gate_b, nb, seg):
    t_tok = qk.shape[0]
    nchunk = seg // CHUNK
    fwd, bwd = _scan_maps(nchunk)
    widths = (qk.shape[1], v.shape[1], LANES)
    in_specs = ([pl.BlockSpec((CHUNK, w), fwd) for w in widths] + [pl.BlockSpec((CHUNK, w), bwd) for w in widths]
                + [pl.BlockSpec((1, LANES), lambda b, t: (0, 0))])
    state = [pltpu.VMEM((ML_HEADS, ML_STATE_ROWS, LANES), F32), pltpu.VMEM((ML_HEADS, LANES), F32)]
    out_map = lambda rows_map: (lambda b, t: (b, 0, rows_map(b, t)[0] - b * nchunk))
    return pl.pallas_call(
        _mlstm_kernel,
        grid=(nb, nchunk),
        in_specs=in_specs,
        out_specs=[pl.BlockSpec((1, v.shape[1], CHUNK), out_map(fwd)),
                   pl.BlockSpec((1, v.shape[1], CHUNK), out_map(bwd))],
        out_shape=[jax.ShapeDtypeStruct((nb, v.shape[1], seg), BF16)] * 2,
        scratch_shapes=state + state,
        compiler_params=_params("parallel", "arbitrary"),
        name="mlstm_scan",
    )(qk, v, g, qk, v, g, gate_b)


def _rms_rows(x, g_row):
    return x * lax.rsqrt(jnp.mean(x * x, axis=-1, keepdims=True) + EPS) * g_row


def _mla_q_kernel(cq_ref, g_ref, w_ref, cos_ref, sin_ref, q_ref):
    hb = _rms_rows(cq_ref[...], g_ref[...]).astype(BF16)
    n = q_ref.shape[1]
    _proj_columns(hb, w_ref, q_ref, 0, n, n, cos_ref[...], sin_ref[...])


def _mla_q(p3, gain, w_q, tables, nt):
    t_tok = p3.shape[0]
    n = w_q.shape[1] // 2
    return pl.pallas_call(
        _mla_q_kernel,
        grid=(t_tok // TM,),
        in_specs=[pl.BlockSpec((TM, MLA_RANK), lambda t: (t, 0)),
                  pl.BlockSpec((1, MLA_RANK), lambda t: (0, 0)),
                  pl.BlockSpec(w_q.shape, lambda t: (0, 0)),
                  pl.BlockSpec((TM, LANES), lambda t: (t % nt, 0)),
                  pl.BlockSpec((TM, LANES), lambda t: (t % nt, 0))],
        out_specs=pl.BlockSpec((TM, n), lambda t: (t, 0)),
        out_shape=jax.ShapeDtypeStruct((t_tok, n), BF16),
        compiler_params=_params("parallel"),
        name="mla_q",
    )(p3, gain.reshape(1, MLA_RANK), w_q, *tables)


def _mla_kv_kernel(ckv_ref, kr_ref, g_ref, wkt_ref, cos_ref, sin_ref, kt_ref, cn_ref):
    cn = _rms_rows(ckv_ref[...], g_ref[...])
    cn_ref[...] = cn.astype(cn_ref.dtype)
    kn_t = _dot(wkt_ref[...], jnp.transpose(cn).astype(BF16))
    kr_t = jnp.transpose(kr_ref[...])
    half = MLA_ROPE // 2
    x1, x2 = kr_t[0:half, :], kr_t[half:MLA_ROPE, :]
    cos, sin = cos_ref[...], sin_ref[...]
    roped = jnp.concatenate([x1 * cos - x2 * sin, x2 * cos + x1 * sin,
                             jnp.zeros((LANES - MLA_NOPE - MLA_ROPE, TM), F32)], axis=0)
    for h in range(MLA_HEADS):
        kt_ref[0, h * LANES:h * LANES + MLA_NOPE, :] = kn_t[h * MLA_NOPE:(h + 1) * MLA_NOPE, :].astype(kt_ref.dtype)
        kt_ref[0, h * LANES + MLA_NOPE:(h + 1) * LANES, :] = roped.astype(kt_ref.dtype)


def _mla_kv(p3, gain, wk_t, tables_t, nb, seg):
    t_tok = p3.shape[0]
    nt = seg // TM
    half = MLA_ROPE // 2
    return pl.pallas_call(
        _mla_kv_kernel,
        grid=(nb, nt),
        in_specs=[pl.BlockSpec((TM, MLA_RANK), lambda b, j: (b * nt + j, 1)),
                  pl.BlockSpec((TM, LANES), lambda b, j: (b * nt + j, 2 * MLA_RANK // LANES)),
                  pl.BlockSpec((1, MLA_RANK), lambda b, j: (0, 0)),
                  pl.BlockSpec(wk_t.shape, lambda b, j: (0, 0)),
                  pl.BlockSpec((half, TM), lambda b, j: (0, j)),
                  pl.BlockSpec((half, TM), lambda b, j: (0, j))],
        out_specs=[pl.BlockSpec((1, MLA_HEADS * LANES, TM), lambda b, j: (b, 0, j)),
                   pl.BlockSpec((TM, MLA_RANK), lambda b, j: (b * nt + j, 0))],
        out_shape=[jax.ShapeDtypeStruct((nb, MLA_HEADS * LANES, seg), BF16),
                   jax.ShapeDtypeStruct((t_tok, MLA_RANK), BF16)],
        compiler_params=_params("parallel", "parallel"),
        name="mla_kv",
    )(p3, p3, gain.reshape(1, MLA_RANK), wk_t, *tables_t)


def _mla_attn_kernel(q_ref, kt_ref, cn_ref, wv_ref, o_ref):
    @pl.when(pl.program_id(2) == 0)
    def _():
        o_ref[...] = jnp.zeros_like(o_ref)

    @pl.when(pl.program_id(2) > 0)
    def _():
        heads = range(2)
        seg = kt_ref.shape[2]
        tk = MLA_KEY_TILE if (seg - TM) % MLA_KEY_TILE == 0 else TM
        tiles = [(0, TM)] + [(a, a + tk) for a in range(TM, seg, tk)]

        def scores(hh, j):
            return _dot(q_ref[:, hh * LANES:(hh + 1) * LANES],
                        kt_ref[0, hh * LANES:(hh + 1) * LANES, tiles[j][0]:tiles[j][1]])

        m = [jnp.full((TM, 1), -jnp.inf, F32) for _ in heads]
        l = [jnp.zeros((TM, 1), F32) for _ in heads]
        acc = [jnp.zeros((TM, MLA_RANK), F32) for _ in heads]
        s_next = [scores(hh, 0) for hh in heads]
        for j in range(len(tiles)):
            s_cur = s_next
            if j + 1 < len(tiles):
                s_next = [scores(hh, j + 1) for hh in heads]
            cn = cn_ref[tiles[j][0]:tiles[j][1], :]
            for hh in heads:
                m_new = jnp.maximum(m[hh], jnp.max(s_cur[hh], axis=-1, keepdims=True))
                alpha = jnp.exp2(m[hh] - m_new)
                p = jnp.exp2(s_cur[hh] - m_new)
                l[hh] = alpha * l[hh] + jnp.sum(p, axis=-1, keepdims=True)
                acc[hh] = alpha * acc[hh] + _dot(p.astype(BF16), cn)
                m[hh] = m_new
        for hh in heads:
            pc = (acc[hh] / l[hh]).astype(BF16)
            o = _dot(pc, wv_ref[:, hh * MLA_V:(hh + 1) * MLA_V])
            o_ref[:, hh * MLA_V:(hh + 1) * MLA_V] = o.astype(o_ref.dtype)


def _mla_attention(q, k_t, cn, w_v, nb, seg):
    t_tok = q.shape[0]
    nt = seg // TM
    return pl.pallas_call(
        _mla_attn_kernel,
        grid=(nb, MLA_HEADS // 2, nt),
        in_specs=[pl.BlockSpec((TM, 2 * LANES), lambda b, hp, j: (b * nt + j, hp)),
                  pl.BlockSpec((1, 2 * LANES, seg), lambda b, hp, j: (b, hp, 0)),
                  pl.BlockSpec((seg, MLA_RANK), lambda b, hp, j: (b, 0)),
                  pl.BlockSpec((MLA_RANK, 2 * MLA_V), lambda b, hp, j: (0, hp))],
        out_specs=pl.BlockSpec((TM, 2 * MLA_V), lambda b, hp, j: (b * nt + j, hp)),
        out_shape=jax.ShapeDtypeStruct((t_tok, MLA_HEADS * MLA_V), BF16),
        compiler_params=_params("parallel", "parallel", "arbitrary"),
        name="mla_attention",
    )(q, k_t, cn, w_v)


def _final_kernel(x_ref, g_ref, o_ref):
    o_ref[0] = _rms_rows(x_ref[...], g_ref[...])


def _final_norm(x, gain, nb, seg):
    d = x.shape[1]
    nt = seg // TM
    return pl.pallas_call(
        _final_kernel,
        grid=(nb, nt - 1),
        in_specs=[pl.BlockSpec((TM, d), lambda b, j: (b * nt + 1 + j, 0)),
                  pl.BlockSpec((1, d), lambda b, j: (0, 0))],
        out_specs=pl.BlockSpec((1, TM, d), lambda b, j: (b, j, 0)),
        out_shape=jax.ShapeDtypeStruct((nb, seg - TM, d), F32),
        compiler_params=_params("parallel", "parallel"),
        name="final_norm",
    )(x, gain.reshape(1, d))


def _rope_angles(n_lat, rot_dim):
    rows = n_lat // GRID_W
    row = jnp.repeat(jnp.arange(rows), GRID_W).astype(F32)
    col = jnp.tile(jnp.arange(GRID_W), rows).astype(F32)
    quarter = rot_dim // 4
    inv = ROPE_BASE ** (-jnp.arange(quarter, dtype=F32) / quarter)
    ang = jnp.concatenate([row[:, None] * inv, col[:, None] * inv], axis=-1)
    return jnp.cos(ang), jnp.sin(ang)


def _with_ctx_rows(tab, fill):
    return jnp.concatenate([jnp.full((TM, tab.shape[1]), fill, F32), tab], axis=0)


def _attn_rope_tables(n_lat):
    cos, sin = _rope_angles(n_lat, ATT_HEAD_DIM)
    cos_h = jnp.concatenate([cos, cos], axis=1)
    sin_h = jnp.concatenate([-sin, sin], axis=1)
    reps = LANES // ATT_HEAD_DIM
    return (_with_ctx_rows(jnp.tile(cos_h, (1, reps)), 1.0), _with_ctx_rows(jnp.tile(sin_h, (1, reps)), 0.0))


def _swap_halves(w, n_heads, dim):
    w3 = w.reshape(w.shape[0], n_heads, 2, dim // 2)
    return w3[:, :, ::-1, :].reshape(w.shape[0], n_heads * dim)


def _mla_q_tables(n_lat):
    scale = (MLA_NOPE + MLA_ROPE) ** -0.5 * LOG2_E
    cos, sin = _rope_angles(n_lat, MLA_ROPE)
    zeros = jnp.zeros((n_lat, LANES - MLA_NOPE - MLA_ROPE), F32)
    cos_c = jnp.concatenate([jnp.ones((n_lat, MLA_NOPE), F32), cos, cos, zeros], axis=1) * scale
    sin_c = jnp.concatenate([jnp.zeros((n_lat, MLA_NOPE), F32), -sin, sin, zeros], axis=1) * scale
    ctx_cos = jnp.concatenate([jnp.full((TM, MLA_NOPE + MLA_ROPE), scale, F32),
                               jnp.zeros((TM, LANES - MLA_NOPE - MLA_ROPE), F32)], axis=1)
    return (jnp.concatenate([ctx_cos, cos_c], axis=0), _with_ctx_rows(sin_c, 0.0))


def _mla_q_weights(w_q_up):
    r = w_q_up.shape[0]
    w3 = w_q_up.reshape(r, MLA_HEADS, MLA_NOPE + MLA_ROPE)
    nope, rope = w3[:, :, :MLA_NOPE], w3[:, :, MLA_NOPE:]
    pad = jnp.zeros((r, MLA_HEADS, LANES - MLA_NOPE - MLA_ROPE), F32)
    half = MLA_ROPE // 2
    rope_sw = jnp.concatenate([rope[:, :, half:], rope[:, :, :half]], axis=-1)
    ext = jnp.concatenate([nope, rope, pad], axis=-1).reshape(r, MLA_HEADS * LANES)
    rot = jnp.concatenate([jnp.zeros_like(nope), rope_sw, pad], axis=-1).reshape(r, MLA_HEADS * LANES)
    return jnp.concatenate([ext, rot], axis=1).astype(BF16)


def kernel(x, c, ctx, c_ctx, norm1_g, norm2_g, w_mod, b_mod, moe_w_group, moe_b_group, moe_w_expert, moe_b_expert, moe_w_gate, moe_w_up, moe_w_down, attn_w_in, attn_sink, attn_w_out, ssm_w_in, ssm_conv_w, ssm_conv_b, ssm_dt_bias, ssm_a_log, ssm_d, ssm_norm_g, ssm_w_out, mlstm_w_in, mlstm_gate_b, mlstm_norm_g, mlstm_w_out, mla_w_in, mla_q_norm_g, mla_w_q_up, mla_kv_norm_g, mla_w_kv_up, mla_w_out, final_norm_g):
    nb, n_lat, d = x.shape
    assert ctx.shape[1] == TM and d == D_MODEL and n_lat % TM == 0
    depth = w_mod.shape[0]
    seg = TM + n_lat
    nt = seg // TM
    t_tok = nb * seg

    xs = jnp.concatenate([ctx, x], axis=1).reshape(t_tok, d)

    rows = -(-(nb + 1) // 8) * 8
    cvec = jnp.concatenate([c, c_ctx[None, :], jnp.zeros((rows - nb - 1, d), F32)], axis=0)
    mods = _modulation(cvec, w_mod, b_mod).reshape(depth, rows, ADALN_CHUNKS, d)
    mods = jnp.pad(mods, ((0, 0), (0, 0), (0, MOD_ROWS - ADALN_CHUNKS), (0, 0)))

    w_router = jnp.concatenate([moe_w_expert, moe_w_group,
                                jnp.zeros((depth, d, LANES - MOE_EXPERTS - MOE_GROUPS), F32)], axis=-1)
    b_router = jnp.concatenate([moe_b_expert, moe_b_group,
                                jnp.zeros((depth, LANES - MOE_EXPERTS - MOE_GROUPS), F32)], axis=-1)
    w_gu = jnp.concatenate([moe_w_gate, moe_w_up], axis=-1).astype(BF16)
    w_down = moe_w_down.astype(BF16)

    for i in range(depth):
        kind = i % 4
        mod_i = mods[i]
        if kind == 0:
            nq, nk = ATT_HEADS * ATT_HEAD_DIM, ATT_KV_HEADS * ATT_HEAD_DIM
            w_in = attn_w_in[i // 4]
            w_all = jnp.concatenate([w_in, _swap_halves(w_in[:, :nq], ATT_HEADS, ATT_HEAD_DIM),
                                     _swap_halves(w_in[:, nq:nq + nk], ATT_KV_HEADS, ATT_HEAD_DIM)],
                                    axis=1).astype(BF16)
            n_in = w_in.shape[1]
            q, k, v = _normproj(xs, mod_i, norm1_g[i], w_all,
                                [(0, nq, n_in), (nq, nk, n_in + nq), (nq + nk, nk, None)],
                                [BF16, BF16, BF16], nt, nb, tables=_attn_rope_tables(n_lat))
            o = _windowed_attention(q, k, v, attn_sink[i // 4], nb, seg)
            xs = _outproj([o], [nq], xs, mod_i, attn_w_out[i // 4].astype(BF16), _plain_prologue, nt, nb)
        elif kind == 1:
            j = i // 4
            w_in = jnp.pad(ssm_w_in[j], ((0, 0), (0, LANES - 2 * SSM_HEADS))).astype(BF16)
            z, xbc, dt = _normproj(xs, mod_i, norm1_g[i], w_in,
                                   [(0, SSM_D_INNER, None), (SSM_D_INNER, SSM_CONV_DIM, None),
                                    (SSM_D_INNER + SSM_CONV_DIM, LANES, None)],
                                   [BF16, BF16, F32], nt, nb)
            lane_pad = LANES - 2 * SSM_HEADS
            dt_bias = jnp.pad(ssm_dt_bias[j].reshape(1, -1), ((0, 0), (0, lane_pad)))
            a_neg = jnp.pad(-jnp.exp(ssm_a_log[j].astype(F32)).reshape(1, -1), ((0, 0), (0, lane_pad)))
            conv_w = jnp.pad(ssm_conv_w[j], ((0, 8 - SSM_CONV), (0, 0)))
            xc, bm, cm, dtv, a = _ssm_conv(xbc, dt, conv_w, ssm_conv_b[j].reshape(1, -1), dt_bias, a_neg, nt)
            yf, yb = _ssd_scan(xc, bm, cm, dtv, a, ssm_d[j].astype(F32), nb, seg)
            xs = _outproj([z], [SSM_D_INNER], xs, mod_i, ssm_w_out[j].astype(BF16), _ssm_prologue,
                          nt, nb, extra=(ssm_norm_g[j].reshape(1, -1),), ins_t=(yf, yb))
        elif kind == 2:
            j = i // 4
            nqk, nv = 2 * ML_HEADS * ML_QK_DIM, ML_HEADS * ML_V_DIM
            w_in = jnp.pad(mlstm_w_in[j], ((0, 0), (0, LANES - 4 * ML_HEADS))).astype(BF16)
            qk, v, o, g = _normproj(xs, mod_i, norm1_g[i], w_in,
                                    [(0, nqk, None), (nqk, nv, None), (nqk + nv, nv, None), (nqk + 2 * nv, LANES, None)],
                                    [BF16, BF16, BF16, F32], nt, nb)
            gate_b = jnp.pad(mlstm_gate_b[j].reshape(1, -1), ((0, 0), (0, LANES - 4 * ML_HEADS)))
            hf, hb = _mlstm_scan(qk, v, g, gate_b, nb, seg)
            xs = _outproj([o], [nv], xs, mod_i, mlstm_w_out[j].astype(BF16), _mlstm_prologue,
                          nt, nb, extra=(mlstm_norm_g[j].reshape(1, -1),), ins_t=(hf, hb))
        else:
            j = i // 4
            w_in = jnp.pad(mla_w_in[j], ((0, 0), (0, LANES - MLA_ROPE))).astype(BF16)
            (p3,) = _normproj(xs, mod_i, norm1_g[i], w_in, [(0, w_in.shape[1], None)], [F32], nt, nb)
            q = _mla_q(p3, mla_q_norm_g[j], _mla_q_weights(mla_w_q_up[j]), _mla_q_tables(n_lat), nt)
            w_kv = mla_w_kv_up[j].reshape(MLA_RANK, MLA_HEADS, MLA_NOPE + MLA_V)
            wk_t = jnp.transpose(w_kv[:, :, :MLA_NOPE].reshape(MLA_RANK, MLA_HEADS * MLA_NOPE)).astype(BF16)
            w_v = w_kv[:, :, MLA_NOPE:].reshape(MLA_RANK, MLA_HEADS * MLA_V).astype(BF16)
            cos, sin = _rope_angles(n_lat, MLA_ROPE)
            tables_t = (jnp.transpose(_with_ctx_rows(cos, 1.0)), jnp.transpose(_with_ctx_rows(sin, 0.0)))
            k_t, cn = _mla_kv(p3, mla_kv_norm_g[j], wk_t, tables_t, nb, seg)
            o = _mla_attention(q, k_t, cn, w_v, nb, seg)
            xs = _outproj([o], [MLA_HEADS * MLA_V], xs, mod_i, mla_w_out[j].astype(BF16), _plain_prologue, nt, nb)

        xs = _moe(xs, mod_i, norm2_g[i], w_router[i], b_router[i].reshape(1, -1), w_gu[i], w_down[i], nt, nb)

    return _final_norm(xs, final_norm_g, nb, seg)
```

```python
import functools
import math

import jax
import jax.numpy as jnp
from jax import lax
from jax.experimental import pallas as pl
from jax.experimental.pallas import tpu as pltpu

F32 = jnp.float32
BF16 = jnp.bfloat16

D_MODEL = 1024
GRID_W = 64
EPS = 1e-6
ROPE_BASE = 10000.0
ADALN_CHUNKS = 6
CHUNK = 128
TM = 256
MOD_ROWS = 8
LANES = 128
V7X_VMEM_LIMIT = 48 * 1024 * 1024

ATT_HEADS, ATT_KV_HEADS, ATT_HEAD_DIM, WINDOW = 16, 4, 64, 128
ATT_GROUP = ATT_HEADS // ATT_KV_HEADS
SSM_D_INNER, SSM_HEAD_DIM, SSM_HEADS, SSM_GROUPS, SSM_STATE, SSM_CONV = 2048, 64, 32, 4, 128, 5
SSM_HPG = SSM_HEADS // SSM_GROUPS
SSM_BC = SSM_GROUPS * SSM_STATE
SSM_CONV_DIM = SSM_D_INNER + 2 * SSM_BC
ML_HEADS, ML_QK_DIM, ML_V_DIM = 8, 64, 128
MLA_HEADS, MLA_RANK, MLA_NOPE, MLA_ROPE, MLA_V = 16, 256, 64, 32, 64
MOE_GROUPS, MOE_PER_GROUP, MOE_EXPERTS, MOE_FF = 4, 4, 16, 256
MOE_ROWS = 1024
MLA_KEY_TILE = 512
MLA_HEADS_PER_STEP = 4
LOG2_E = 1.4426950408889634


def _dot(a, b):
    return jnp.dot(a, b, preferred_element_type=F32)


def _dot_nt(a, b):
    return lax.dot_general(a, b, (((1,), (1,)), ((), ())), preferred_element_type=F32)


def _split_bf16(x):
    hi = x.astype(BF16)
    lo = (x - hi.astype(F32)).astype(BF16)
    return hi, lo


def _dot_split(a, b):
    a_hi, a_lo = _split_bf16(a)
    b_hi, b_lo = _split_bf16(b)
    return _dot(a_hi, b_hi) + _dot(a_lo, b_hi) + _dot(a_hi, b_lo)


def _sigmoid(x):
    return 1.0 / (1.0 + jnp.exp(-x))


def _silu(x):
    return x * _sigmoid(x)


def _softplus(x):
    return jnp.maximum(x, 0.0) + jnp.log1p(jnp.exp(-jnp.abs(x)))


def _params(*sem):
    return pltpu.CompilerParams(dimension_semantics=sem, vmem_limit_bytes=V7X_VMEM_LIMIT)


def _mod_row(t, nt, nb):
    return jnp.where(t % nt == 0, nb, t // nt)


def _mod_kernel(c_ref, w_ref, b_ref, o_ref):
    o_ref[0] = _dot_split(_silu(c_ref[...]), w_ref[0]) + b_ref[0]


def _modulation(cvec, w_mod, b_mod):
    depth, d, n = w_mod.shape
    tn = 1536
    rows = cvec.shape[0]
    return pl.pallas_call(
        _mod_kernel,
        grid=(depth, n // tn),
        in_specs=[pl.BlockSpec((rows, d), lambda l, j: (0, 0)),
                  pl.BlockSpec((1, d, tn), lambda l, j: (l, 0, j)),
                  pl.BlockSpec((1, 1, tn), lambda l, j: (l, 0, j))],
        out_specs=pl.BlockSpec((1, rows, tn), lambda l, j: (l, 0, j)),
        out_shape=jax.ShapeDtypeStruct((depth, rows, n), F32),
        compiler_params=_params("arbitrary", "arbitrary"),
        name="modulation",
    )(cvec, w_mod, b_mod.reshape(depth, 1, n))


def _normed(x, g_row, mod, sh_row, sc_row):
    y = x * lax.rsqrt(jnp.mean(x * x, axis=-1, keepdims=True) + EPS) * g_row
    return y * (1.0 + mod[sc_row:sc_row + 1, :]) + mod[sh_row:sh_row + 1, :]


def _proj_columns(hb, w_ref, o_ref, start, width, rot_start, cos, sin, chunk=512):
    for c in range(0, width, chunk):
        cw = min(chunk, width - c)
        acc = _dot(hb, w_ref[:, start + c:start + c + cw])
        if rot_start is not None:
            rot = _dot(hb, w_ref[:, rot_start + c:rot_start + c + cw])
            reps = cw // LANES
            acc = acc * jnp.tile(cos, (1, reps)) + rot * jnp.tile(sin, (1, reps))
        o_ref[:, c:c + cw] = acc.astype(o_ref.dtype)


def _normproj_kernel(*refs, outs, has_rope):
    x_ref, mod_ref, g_ref, w_ref = refs[:4]
    k = 4
    cos = sin = None
    if has_rope:
        cos, sin = refs[4][...], refs[5][...]
        k = 6
    hb = _normed(x_ref[...], g_ref[...], mod_ref[0], 0, 1).astype(BF16)
    for o_ref, (start, width, rot_start) in zip(refs[k:], outs):
        _proj_columns(hb, w_ref, o_ref, start, width, rot_start, cos, sin)


def _normproj(x, mods, gain, w, outs, out_dtypes, nt, nb, tables=None):
    t_tok, d = x.shape
    n = w.shape[1]
    in_specs = [pl.BlockSpec((TM, d), lambda t: (t, 0)),
                pl.BlockSpec((1, MOD_ROWS, d), lambda t: (_mod_row(t, nt, nb), 0, 0)),
                pl.BlockSpec((1, d), lambda t: (0, 0)),
                pl.BlockSpec((d, n), lambda t: (0, 0))]
    args = [x, mods, gain.reshape(1, d), w]
    if tables is not None:
        in_specs += [pl.BlockSpec((TM, LANES), lambda t: (t % nt, 0))] * 2
        args += list(tables)
    return pl.pallas_call(
        functools.partial(_normproj_kernel, outs=tuple(outs), has_rope=tables is not None),
        grid=(t_tok // TM,),
        in_specs=in_specs,
        out_specs=[pl.BlockSpec((TM, o[1]), lambda t: (t, 0)) for o in outs],
        out_shape=[jax.ShapeDtypeStruct((t_tok, o[1]), dt) for o, dt in zip(outs, out_dtypes)],
        compiler_params=_params("parallel"),
        name="normproj",
    )(*args)


def _outproj_kernel(*refs, prologue, n_in):
    ins = refs[:n_in]
    x_ref, mod_ref, w_ref, o_ref = refs[n_in:]
    a = prologue(*ins)
    y = _dot(a, w_ref[...])
    o_ref[...] = x_ref[...] + mod_ref[0][2:3, :] * y


def _outproj(ins, in_widths, x, mods, w, prologue, nt, nb, extra=(), ins_t=()):
    t_tok, d = x.shape
    in_specs = [pl.BlockSpec((1, a.shape[1], TM), lambda t: (t // nt, 0, t % nt)) for a in ins_t]
    in_specs += [pl.BlockSpec((TM, wd), lambda t: (t, 0)) for wd in in_widths]
    in_specs += [pl.BlockSpec(e.shape, lambda t: (0, 0)) for e in extra]
    n_in = len(in_specs)
    in_specs += [pl.BlockSpec((TM, d), lambda t: (t, 0)),
                 pl.BlockSpec((1, MOD_ROWS, d), lambda t: (_mod_row(t, nt, nb), 0, 0)),
                 pl.BlockSpec(w.shape, lambda t: (0, 0))]
    return pl.pallas_call(
        functools.partial(_outproj_kernel, prologue=prologue, n_in=n_in),
        grid=(t_tok // TM,),
        in_specs=in_specs,
        out_specs=pl.BlockSpec((TM, d), lambda t: (t, 0)),
        out_shape=jax.ShapeDtypeStruct((t_tok, d), F32),
        input_output_aliases={n_in: 0},
        compiler_params=_params("parallel"),
        name="outproj",
    )(*ins_t, *ins, *extra, x, mods, w)


def _group_rms(y, n_groups):
    width = y.shape[1] // n_groups
    parts = []
    for g in range(n_groups):
        yg = y[:, g * width:(g + 1) * width]
        parts.append(yg * lax.rsqrt(jnp.mean(yg * yg, axis=-1, keepdims=True) + EPS))
    return jnp.concatenate(parts, axis=1)


def _plain_prologue(o_ref):
    return o_ref[...]


def _ssm_prologue(yf_ref, yb_ref, z_ref, g_ref):
    y = jnp.transpose(yf_ref[0].astype(F32) + yb_ref[0].astype(F32)) * _silu(z_ref[...].astype(F32))
    return (_group_rms(y, SSM_GROUPS) * g_ref[...]).astype(BF16)


def _mlstm_prologue(hf_ref, hb_ref, o_ref, g_ref):
    h = jnp.transpose(hf_ref[0].astype(F32) + hb_ref[0].astype(F32))
    return (_group_rms(h, ML_HEADS) * g_ref[...] * _sigmoid(o_ref[...].astype(F32))).astype(BF16)


def _route(logits):
    lane = lax.broadcasted_iota(jnp.int32, logits.shape, 1).astype(F32)
    neg = -jnp.inf
    lg = jnp.where((lane >= MOE_EXPERTS) & (lane < MOE_EXPERTS + MOE_GROUPS), logits, neg)
    gmax = jnp.max(lg, axis=-1, keepdims=True)
    g_sel = jnp.min(jnp.where(lg == gmax, lane, LANES), axis=-1, keepdims=True) - MOE_EXPERTS
    p_g = 1.0 / jnp.sum(jnp.exp(lg - gmax), axis=-1, keepdims=True)
    in_group = (lane >= g_sel * MOE_PER_GROUP) & (lane < (g_sel + 1) * MOE_PER_GROUP)
    le = jnp.where(in_group, logits, neg)
    v1 = jnp.max(le, axis=-1, keepdims=True)
    i1 = jnp.min(jnp.where(le == v1, lane, LANES), axis=-1, keepdims=True)
    le2 = jnp.where(lane == i1, neg, le)
    v2 = jnp.max(le2, axis=-1, keepdims=True)
    i2 = jnp.min(jnp.where(le2 == v2, lane, LANES), axis=-1, keepdims=True)
    e2 = jnp.exp(v2 - v1)
    w1 = p_g / (1.0 + e2)
    return jnp.where(lane == i1, w1, 0.0) + jnp.where(lane == i2, w1 * e2, 0.0)


def _moe_kernel(x_ref, mods_ref, g_ref, wr_ref, br_ref, wgu_ref, wd_ref, o_ref, h_scr, comb_scr, *, nt, nb, sub):
    i = pl.program_id(0)
    e = pl.program_id(1)

    @pl.when(e == 0)
    def _():
        for s in range(sub):
            rows = slice(s * TM, (s + 1) * TM)
            mod = mods_ref[_mod_row(i * sub + s, nt, nb)]
            x = x_ref[rows, :]
            h = _normed(x, g_ref[...], mod, 3, 4)
            h_scr[rows, :] = h.astype(BF16)
            comb_scr[rows, :] = _route(_dot_split(h, wr_ref[...]) + br_ref[...])
            o_ref[rows, :] = x

    gw = MOE_PER_GROUP * MOE_FF
    gu = _dot(h_scr[...], wgu_ref[0])
    act = _silu(gu[:, :gw]) * gu[:, gw:]
    lane = lax.broadcasted_iota(jnp.int32, comb_scr.shape, 1)
    comb = comb_scr[...]
    scaled = []
    for k in range(MOE_PER_GROUP):
        cw = jnp.sum(jnp.where(lane == e * MOE_PER_GROUP + k, comb, 0.0), axis=-1, keepdims=True)
        scaled.append((act[:, k * MOE_FF:(k + 1) * MOE_FF] * cw).astype(BF16))
    y = _dot(jnp.concatenate(scaled, axis=1), wd_ref[0])
    for s in range(sub):
        rows = slice(s * TM, (s + 1) * TM)
        gate = mods_ref[_mod_row(i * sub + s, nt, nb)][5:6, :]
        o_ref[rows, :] += gate * y[rows, :]


def _moe(x, mods, gain, w_router, b_router, w_gu, w_down, nt, nb):
    t_tok, d = x.shape
    rows = MOE_ROWS if t_tok % MOE_ROWS == 0 else TM
    sub = rows // TM
    return pl.pallas_call(
        functools.partial(_moe_kernel, nt=nt, nb=nb, sub=sub),
        grid=(t_tok // rows, MOE_GROUPS),
        in_specs=[pl.BlockSpec((rows, d), lambda i, e: (i, 0)),
                  pl.BlockSpec(mods.shape, lambda i, e: (0, 0, 0)),
                  pl.BlockSpec((1, d), lambda i, e: (0, 0)),
                  pl.BlockSpec((d, LANES), lambda i, e: (0, 0)),
                  pl.BlockSpec((1, LANES), lambda i, e: (0, 0)),
                  pl.BlockSpec((1, d, 2 * MOE_PER_GROUP * MOE_FF), lambda i, e: (e, 0, 0)),
                  pl.BlockSpec((1, MOE_PER_GROUP * MOE_FF, d), lambda i, e: (e, 0, 0))],
        out_specs=pl.BlockSpec((rows, d), lambda i, e: (i, 0)),
        out_shape=jax.ShapeDtypeStruct((t_tok, d), F32),
        scratch_shapes=[pltpu.VMEM((rows, d), BF16), pltpu.VMEM((rows, LANES), F32)],
        input_output_aliases={0: 0},
        compiler_params=_params("parallel", "arbitrary"),
        name="moe",
    )(x, mods, gain.reshape(1, d), w_router, b_router, w_gu, w_down)


def _attn_kernel(sink_ref, q_ref, kp_ref, kc_ref, kn_ref, vp_ref, vc_ref, vn_ref, kx_ref, vx_ref, o_ref, *, n_lat):
    j = pl.program_id(1)
    jl = j - TM // CHUNK
    k_loc = jnp.concatenate([kp_ref[...], kc_ref[...], kn_ref[...]], axis=0)
    v_loc = jnp.concatenate([vp_ref[...], vc_ref[...], vn_ref[...]], axis=0)
    kx, vx = kx_ref[...], vx_ref[...]
    qi = lax.broadcasted_iota(jnp.int32, (CHUNK, 3 * CHUNK), 0)
    kj = lax.broadcasted_iota(jnp.int32, (CHUNK, 3 * CHUNK), 1)
    kpos = (jl - 1) * CHUNK + kj
    ok = (kj >= qi + CHUNK - WINDOW) & (kj <= qi + CHUNK + WINDOW) & (kpos >= 0) & (kpos < n_lat) & (jl >= 0)
    bias = jnp.concatenate([jnp.where(ok, 0.0, -jnp.inf)] * ATT_GROUP, axis=0)
    row_head = lax.broadcasted_iota(jnp.int32, (ATT_GROUP * CHUNK, 1), 0) // CHUNK
    dh = ATT_HEAD_DIM
    for g in range(ATT_KV_HEADS):
        cols = slice(g * dh, (g + 1) * dh)
        qg = jnp.concatenate([q_ref[:, (g * ATT_GROUP + a) * dh:(g * ATT_GROUP + a + 1) * dh]
                              for a in range(ATT_GROUP)], axis=0) * (dh ** -0.5)
        s_loc = _dot_nt(qg, k_loc[:, cols]) + bias
        s_ctx = _dot_nt(qg, kx[:, cols])
        sink = jnp.zeros((ATT_GROUP * CHUNK, 1), F32)
        for a in range(ATT_GROUP):
            sink = jnp.where(row_head == a, sink_ref[g * ATT_GROUP + a], sink)
        m = jnp.maximum(jnp.maximum(jnp.max(s_loc, axis=-1, keepdims=True),
                                    jnp.max(s_ctx, axis=-1, keepdims=True)), sink)
        p_loc = jnp.exp(s_loc - m)
        p_ctx = jnp.exp(s_ctx - m)
        den = (jnp.sum(p_loc, axis=-1, keepdims=True) + jnp.sum(p_ctx, axis=-1, keepdims=True)
               + jnp.exp(sink - m))
        og = (_dot(p_loc.astype(BF16), v_loc[:, cols]) + _dot(p_ctx.astype(BF16), vx[:, cols])) / den
        for a in range(ATT_GROUP):
            h = g * ATT_GROUP + a
            o_ref[:, h * dh:(h + 1) * dh] = og[a * CHUNK:(a + 1) * CHUNK, :].astype(o_ref.dtype)


def _windowed_attention(q, k, v, sink, nb, seg):
    t_tok = q.shape[0]
    nblk = seg // CHUNK
    ctx_blk = TM // CHUNK
    kvw = ATT_KV_HEADS * ATT_HEAD_DIM

    def blk(off):
        return lambda b, j, *_: (b * nblk + jnp.clip(j + off, ctx_blk, nblk - 1), 0)

    kv_spec = [pl.BlockSpec((CHUNK, kvw), blk(o)) for o in (-1, 0, 1)]
    ctx_spec = pl.BlockSpec((TM, kvw), lambda b, j, *_: (b * (seg // TM), 0))
    return pl.pallas_call(
        functools.partial(_attn_kernel, n_lat=seg - TM),
        grid_spec=pltpu.PrefetchScalarGridSpec(
            num_scalar_prefetch=1,
            grid=(nb, nblk),
            in_specs=[pl.BlockSpec((CHUNK, q.shape[1]), lambda b, j, *_: (b * nblk + j, 0))]
            + kv_spec + kv_spec + [ctx_spec, ctx_spec],
            out_specs=pl.BlockSpec((CHUNK, q.shape[1]), lambda b, j, *_: (b * nblk + j, 0)),
        ),
        out_shape=jax.ShapeDtypeStruct((t_tok, q.shape[1]), BF16),
        compiler_params=_params("parallel", "parallel"),
        name="windowed_attention",
    )(sink, q, k, k, k, v, v, v, k, v)


def _conv_kernel(cur_ref, prev_ref, next_ref, dt_ref, w_ref, b_ref, dtb_ref, aneg_ref,
                 xs_ref, bm_ref, cm_ref, dtv_ref, a_ref, ext, *, nt):
    t = pl.program_id(0)
    tl = t % nt
    halo = 8
    first = (tl == 0) | (tl == 1)
    last = (tl == 0) | (tl == nt - 1)
    ext[0:halo, :] = jnp.where(first, 0.0, prev_ref[...].astype(F32)[halo:2 * halo, :])
    ext[halo:halo + TM, :] = cur_ref[...].astype(F32)
    ext[halo + TM:2 * halo + TM, :] = jnp.where(last, 0.0, next_ref[...].astype(F32)[0:halo, :])
    pad = SSM_CONV // 2
    chunk = 512
    for c in range(0, SSM_CONV_DIM, chunk):
        acc = jnp.broadcast_to(b_ref[:, c:c + chunk], (TM, chunk))
        for k in range(SSM_CONV):
            acc = acc + w_ref[k:k + 1, c:c + chunk] * ext[halo - pad + k:halo - pad + k + TM, c:c + chunk]
        y = _silu(acc)
        if c < SSM_D_INNER:
            xs_ref[0, c:c + chunk, :] = jnp.transpose(y).astype(xs_ref.dtype)
        elif c < SSM_D_INNER + SSM_BC:
            bm_ref[:, c - SSM_D_INNER:c - SSM_D_INNER + chunk] = y.astype(bm_ref.dtype)
        else:
            off = c - SSM_D_INNER - SSM_BC
            cm_ref[0, off:off + chunk, :] = jnp.transpose(y).astype(cm_ref.dtype)
    dtv = _softplus(dt_ref[...] + dtb_ref[...])
    dtv_ref[0] = jnp.transpose(dtv)
    a_ref[0] = jnp.transpose(dtv * aneg_ref[...])


def _ssm_conv(xbc, dt, conv_w, conv_b, dt_bias, a_neg, nt):
    t_tok = xbc.shape[0]
    n16 = t_tok // 16
    seg = nt * TM
    nb = t_tok // seg
    row = lambda t: (t, 0)
    col = lambda t: (t // nt, 0, t % nt)
    fixed = lambda t: (0, 0)
    return pl.pallas_call(
        functools.partial(_conv_kernel, nt=nt),
        grid=(t_tok // TM,),
        in_specs=[pl.BlockSpec((TM, SSM_CONV_DIM), row),
                  pl.BlockSpec((16, SSM_CONV_DIM), lambda t: (jnp.maximum(t * (TM // 16) - 1, 0), 0)),
                  pl.BlockSpec((16, SSM_CONV_DIM), lambda t: (jnp.minimum((t + 1) * (TM // 16), n16 - 1), 0)),
                  pl.BlockSpec((TM, LANES), row),
                  pl.BlockSpec((8, SSM_CONV_DIM), fixed),
                  pl.BlockSpec((1, SSM_CONV_DIM), fixed),
                  pl.BlockSpec((1, LANES), fixed),
                  pl.BlockSpec((1, LANES), fixed)],
        out_specs=[pl.BlockSpec((1, SSM_D_INNER, TM), col), pl.BlockSpec((TM, SSM_BC), row),
                   pl.BlockSpec((1, SSM_BC, TM), col), pl.BlockSpec((1, LANES, TM), col),
                   pl.BlockSpec((1, LANES, TM), col)],
        out_shape=[jax.ShapeDtypeStruct((nb, SSM_D_INNER, seg), BF16), jax.ShapeDtypeStruct((t_tok, SSM_BC), BF16),
                   jax.ShapeDtypeStruct((nb, SSM_BC, seg), BF16), jax.ShapeDtypeStruct((nb, LANES, seg), F32),
                   jax.ShapeDtypeStruct((nb, LANES, seg), F32)],
        scratch_shapes=[pltpu.VMEM((TM + 16, SSM_CONV_DIM), F32)],
        compiler_params=_params("parallel"),
        name="ssm_conv",
    )(xbc, xbc, xbc, dt, conv_w, conv_b, dt_bias, a_neg)


def _tri(lower):
    r = lax.broadcasted_iota(jnp.int32, (CHUNK, CHUNK), 0)
    c = lax.broadcasted_iota(jnp.int32, (CHUNK, CHUNK), 1)
    return (c <= r) if lower else (c >= r)


def _cumsum_rows(x):
    tri = jnp.where(_tri(True), 1.0, 0.0).astype(BF16)
    hi, lo = _split_bf16(x)
    return _dot(tri, hi) + _dot(tri, lo)


def _ssd_direction(xs_ref, bm_ref, ct_ref, dt_ref, a_ref, dsk_ref, st_ref, y_ref, d):
    a_t = a_ref[0]
    dt_t = dt_ref[0]
    upper = jnp.where(_tri(False), 1.0, 0.0).astype(BF16)
    a_hi, a_lo = _split_bf16(a_t)
    cum = _dot(a_hi, upper) + _dot(a_lo, upper)
    cum_end = cum[:, CHUNK - 1:CHUNK]
    if d == 0:
        lane_v, sub_v = cum, -cum
        inter = jnp.exp(cum)
        w_upd = jnp.exp(cum_end - cum) * dt_t
    else:
        ecum = cum - a_t
        lane_v, sub_v = -ecum, ecum
        inter = jnp.exp(cum_end - ecum)
        w_upd = jnp.exp(ecum) * dt_t
    sub_c = jnp.transpose(sub_v)
    decay_end = jnp.exp(cum_end)
    mask = _tri(d == 1)
    p = SSM_HEAD_DIM
    for g in range(SSM_GROUPS):
        bg = bm_ref[:, g * SSM_STATE:(g + 1) * SSM_STATE]
        cg_t = ct_ref[0, g * SSM_STATE:(g + 1) * SSM_STATE, :]
        cb_t = _dot(bg, cg_t)
        state = st_ref[g]
        y_in = _dot(state.astype(BF16), cg_t)
        upd = []
        for e in range(SSM_HPG):
            h = g * SSM_HPG + e
            ln = d * SSM_HEADS + h
            rows = slice(e * p, (e + 1) * p)
            seg = jnp.where(mask, sub_c[:, ln:ln + 1] + lane_v[ln:ln + 1, :], -jnp.inf)
            m_t = (cb_t * jnp.exp(seg)).astype(BF16)
            xf = xs_ref[0, h * p:(h + 1) * p, :].astype(F32)
            u = (xf * dt_t[ln:ln + 1, :]).astype(BF16)
            yh = _dot(u, m_t) + y_in[rows, :] * inter[ln:ln + 1, :] + dsk_ref[d, h] * xf
            y_ref[0, h * p:(h + 1) * p, :] = yh.astype(y_ref.dtype)
            upd.append((xf * w_upd[ln:ln + 1, :]).astype(BF16))
        new = _dot(jnp.concatenate(upd, axis=0), bg)
        for e in range(SSM_HPG):
            ln = d * SSM_HEADS + g * SSM_HPG + e
            rows = slice(e * p, (e + 1) * p)
            st_ref[g, rows, :] = state[rows, :] * decay_end[ln:ln + 1, :] + new[rows, :]


def _ssd_kernel(dsk_ref, xsf, bmf, ctf, dtf, af, xsb, bmb, ctb, dtb, ab, yf_ref, yb_ref, stf, stb):
    @pl.when(pl.program_id(1) == 0)
    def _():
        stf[...] = jnp.zeros_like(stf)
        stb[...] = jnp.zeros_like(stb)

    _ssd_direction(xsf, bmf, ctf, dtf, af, dsk_ref, stf, yf_ref, 0)
    _ssd_direction(xsb, bmb, ctb, dtb, ab, dsk_ref, stb, yb_ref, 1)


def _scan_maps(nchunk):
    ctx_chunks = TM // CHUNK
    fwd = lambda b, t: (b * nchunk + t, 0)
    bwd = lambda b, t: (b * nchunk + jnp.where(t < ctx_chunks, ctx_chunks - 1 - t, nchunk + ctx_chunks - 1 - t), 0)
    return fwd, bwd


def _ssd_scan(xs_t, bm, c_t, dt_t, a_t, d_skip, nb, seg):
    nchunk = seg // CHUNK
    fwd_rows, bwd_rows = _scan_maps(nchunk)

    def specs(rows_map):
        cols_map = lambda b, t, *_: (b, 0, rows_map(b, t)[0] - b * nchunk)
        return [pl.BlockSpec((1, SSM_D_INNER, CHUNK), cols_map),
                pl.BlockSpec((CHUNK, SSM_BC), lambda b, t, *_: rows_map(b, t)),
                pl.BlockSpec((1, SSM_BC, CHUNK), cols_map),
                pl.BlockSpec((1, LANES, CHUNK), cols_map),
                pl.BlockSpec((1, LANES, CHUNK), cols_map)]

    arrs = (xs_t, bm, c_t, dt_t, a_t)
    return pl.pallas_call(
        _ssd_kernel,
        grid_spec=pltpu.PrefetchScalarGridSpec(
            num_scalar_prefetch=1,
            grid=(nb, nchunk),
            in_specs=specs(fwd_rows) + specs(bwd_rows),
            out_specs=[specs(fwd_rows)[0], specs(bwd_rows)[0]],
            scratch_shapes=[pltpu.VMEM((SSM_GROUPS, SSM_HPG * SSM_HEAD_DIM, SSM_STATE), F32)] * 2,
        ),
        out_shape=[jax.ShapeDtypeStruct(xs_t.shape, BF16)] * 2,
        compiler_params=_params("parallel", "arbitrary"),
        name="ssd_scan",
    )(d_skip, *arrs, *arrs)


def _log_sigmoid(x):
    return jnp.minimum(x, 0.0) - jnp.log1p(jnp.exp(-jnp.abs(x)))


ML_STATE_ROWS = ML_V_DIM + 16


def _mlstm_direction(qk_ref, v_ref, g_ref, gb_ref, c_st, m_st, h_ref, d):
    nh, dk, dv = ML_HEADS, ML_QK_DIM, ML_V_DIM
    g_t = jnp.transpose(g_ref[...] + gb_ref[...])
    ig = g_t[16 * d:16 * d + nh, :]
    lf = _log_sigmoid(g_t[16 * d + nh:16 * d + 2 * nh, :])
    upper = jnp.where(_tri(False), 1.0, 0.0).astype(BF16)
    lf_hi, lf_lo = _split_bf16(lf)
    fc = _dot(lf_hi, upper) + _dot(lf_lo, upper)
    tot = fc[:, CHUNK - 1:CHUNK]
    m_prev = m_st[...]
    if d == 0:
        lane_v, sub_v = fc, ig - fc
        inter = fc + m_prev
        logw = tot - fc + ig
    else:
        ec = fc - lf
        lane_v, sub_v = -ec, ec + ig
        inter = tot - ec + m_prev
        logw = ec + ig
    sub_c = jnp.transpose(jnp.concatenate([sub_v, jnp.zeros((LANES - nh, CHUNK), F32)], axis=0))
    m_new = jnp.maximum(tot + m_prev, jnp.max(logw, axis=-1, keepdims=True))
    ws = jnp.exp(logw - m_new)
    cw = jnp.exp(tot + m_prev - m_new)
    mask = _tri(d == 1)
    q_t = jnp.transpose(qk_ref[:, :nh * dk].astype(F32))
    v_t = jnp.transpose(v_ref[...].astype(F32))
    lane = lax.broadcasted_iota(jnp.int32, (1, LANES), 1)
    zeros_q = jnp.zeros((dk, CHUNK), F32)
    for h in range(nh):
        half = h % 2
        k_pair = qk_ref[:, nh * dk + (h - half) * dk:nh * dk + (h - half + 2) * dk] * (dk ** -0.5)
        qh = q_t[h * dk:(h + 1) * dk, :]
        q_m = jnp.concatenate([zeros_q, qh] if half else [qh, zeros_q], axis=0).astype(BF16)
        logd = jnp.where(mask, sub_c[:, h:h + 1] + lane_v[h:h + 1, :], -jnp.inf)
        mt = jnp.maximum(inter[h:h + 1, :], jnp.max(logd, axis=0, keepdims=True))
        sc = _dot(k_pair, q_m) * jnp.exp(logd - mt)
        vh = v_t[h * dv:(h + 1) * dv, :]
        state = c_st[h]
        cq = _dot(state.astype(BF16), q_m)
        w_int = jnp.exp(inter[h:h + 1, :] - mt)
        num = _dot(vh.astype(BF16), sc.astype(BF16)) + w_int * cq[:dv, :]
        den = jnp.sum(sc, axis=0, keepdims=True) + w_int * cq[dv:dv + 1, :]
        h_ref[0, h * dv:(h + 1) * dv, :] = (num / jnp.maximum(jnp.abs(den), jnp.exp(-mt))).astype(h_ref.dtype)
        ws_h = ws[h:h + 1, :]
        lhs = jnp.concatenate([vh * ws_h, jnp.broadcast_to(ws_h, (ML_STATE_ROWS - dv, CHUNK))], axis=0)
        own = (lane >= half * dk) & (lane < (half + 1) * dk)
        row = lax.broadcasted_iota(jnp.int32, (ML_STATE_ROWS, 1), 0)
        new = cw[h:h + 1, :] * state + _dot(lhs.astype(BF16), k_pair)
        c_st[h] = jnp.where(own & (row <= dv), new, 0.0)
    m_st[...] = jnp.broadcast_to(m_new, m_st.shape)


def _mlstm_kernel(qkf, vf, gf, qkb, vb, gb, gbias, hf_ref, hb_ref, cf, mf, cb, mb):
    @pl.when(pl.program_id(1) == 0)
    def _():
        for r in (cf, mf, cb, mb):
            r[...] = jnp.zeros_like(r)

    _mlstm_direction(qkf, vf, gf, gbias, cf, mf, hf_ref, 0)
    _mlstm_direction(qkb, vb, gb, gbias, cb, mb, hb_ref, 1)


def _mlstm_scan(qk, v, g, gate_b, nb, seg):
    t_tok = qk.shape[0]
    nchunk = seg // CHUNK
    fwd, bwd = _scan_maps(nchunk)
    widths = (qk.shape[1], v.shape[1], LANES)
    in_specs = ([pl.BlockSpec((CHUNK, w), fwd) for w in widths] + [pl.BlockSpec((CHUNK, w), bwd) for w in widths]
                + [pl.BlockSpec((1, LANES), lambda b, t: (0, 0))])
    state = [pltpu.VMEM((ML_HEADS, ML_STATE_ROWS, LANES), F32), pltpu.VMEM((ML_HEADS, LANES), F32)]
    out_map = lambda rows_map: (lambda b, t: (b, 0, rows_map(b, t)[0] - b * nchunk))
    return pl.pallas_call(
        _mlstm_kernel,
        grid=(nb, nchunk),
        in_specs=in_specs,
        out_specs=[pl.BlockSpec((1, v.shape[1], CHUNK), out_map(fwd)),
                   pl.BlockSpec((1, v.shape[1], CHUNK), out_map(bwd))],
        out_shape=[jax.ShapeDtypeStruct((nb, v.shape[1], seg), BF16)] * 2,
        scratch_shapes=state + state,
        compiler_params=_params("parallel", "arbitrary"),
        name="mlstm_scan",
    )(qk, v, g, qk, v, g, gate_b)


def _rms_rows(x, g_row):
    return x * lax.rsqrt(jnp.mean(x * x, axis=-1, keepdims=True) + EPS) * g_row


MLA_SCORE_SCALE = (MLA_NOPE + MLA_ROPE) ** -0.5 * LOG2_E


def _mla_q_kernel(cq_ref, g_ref, wt_ref, cos_ref, sin_ref, qt_ref):
    cq_t = jnp.transpose(_rms_rows(cq_ref[...], g_ref[...])).astype(BF16)
    q_t = _dot(wt_ref[...], cq_t) * MLA_SCORE_SCALE
    qt_ref[0] = q_t.astype(qt_ref.dtype)
    cos, sin = cos_ref[...], sin_ref[...]
    half = MLA_ROPE // 2
    for h in range(MLA_HEADS):
        r0 = h * LANES + MLA_NOPE
        x1, x2 = q_t[r0:r0 + half, :], q_t[r0 + half:r0 + MLA_ROPE, :]
        qt_ref[0, r0:r0 + half, :] = (x1 * cos - x2 * sin).astype(qt_ref.dtype)
        qt_ref[0, r0 + half:r0 + MLA_ROPE, :] = (x2 * cos + x1 * sin).astype(qt_ref.dtype)


def _mla_q(p3, gain, wq_t, tables_t, nt):
    t_tok = p3.shape[0]
    n = wq_t.shape[0]
    half = MLA_ROPE // 2
    return pl.pallas_call(
        _mla_q_kernel,
        grid=(t_tok // TM,),
        in_specs=[pl.BlockSpec((TM, MLA_RANK), lambda t: (t, 0)),
                  pl.BlockSpec((1, MLA_RANK), lambda t: (0, 0)),
                  pl.BlockSpec(wq_t.shape, lambda t: (0, 0)),
                  pl.BlockSpec((half, TM), lambda t: (0, t % nt)),
                  pl.BlockSpec((half, TM), lambda t: (0, t % nt))],
        out_specs=pl.BlockSpec((1, n, TM), lambda t: (t, 0, 0)),
        out_shape=jax.ShapeDtypeStruct((t_tok // TM, n, TM), BF16),
        compiler_params=_params("parallel"),
        name="mla_q",
    )(p3, gain.reshape(1, MLA_RANK), wq_t, *tables_t)


def _mla_kv_kernel(ckv_ref, kr_ref, g_ref, wk_ref, wvt_ref, cos_ref, sin_ref, k_ref, vt_ref):
    cn = _rms_rows(ckv_ref[...], g_ref[...])
    vt_ref[0] = _dot(wvt_ref[...], jnp.transpose(cn).astype(BF16)).astype(vt_ref.dtype)
    kn = _dot(cn.astype(BF16), wk_ref[...])
    kr = pltpu.roll(kr_ref[...], MLA_NOPE, axis=1)
    half = MLA_ROPE // 2
    lane = lax.broadcasted_iota(jnp.int32, (1, LANES), 1)
    partner = jnp.where(lane < MLA_NOPE + half, pltpu.roll(kr, LANES - half, axis=1), pltpu.roll(kr, half, axis=1))
    roped = kr * cos_ref[...] + partner * sin_ref[...]
    for h in range(MLA_HEADS):
        k_ref[:, h * LANES:(h + 1) * LANES] = (kn[:, h * LANES:(h + 1) * LANES] + roped).astype(k_ref.dtype)


def _mla_kv(p3, gain, w_k, wv_t, tables, nb, seg):
    t_tok = p3.shape[0]
    nt = seg // TM
    return pl.pallas_call(
        _mla_kv_kernel,
        grid=(nb, nt),
        in_specs=[pl.BlockSpec((TM, MLA_RANK), lambda b, j: (b * nt + j, 1)),
                  pl.BlockSpec((TM, LANES), lambda b, j: (b * nt + j, 2 * MLA_RANK // LANES)),
                  pl.BlockSpec((1, MLA_RANK), lambda b, j: (0, 0)),
                  pl.BlockSpec(w_k.shape, lambda b, j: (0, 0)),
                  pl.BlockSpec(wv_t.shape, lambda b, j: (0, 0)),
                  pl.BlockSpec((TM, LANES), lambda b, j: (j, 0)),
                  pl.BlockSpec((TM, LANES), lambda b, j: (j, 0))],
        out_specs=[pl.BlockSpec((TM, MLA_HEADS * LANES), lambda b, j: (b * nt + j, 0)),
                   pl.BlockSpec((1, MLA_HEADS * MLA_V, TM), lambda b, j: (b, 0, j))],
        out_shape=[jax.ShapeDtypeStruct((t_tok, MLA_HEADS * LANES), BF16),
                   jax.ShapeDtypeStruct((nb, MLA_HEADS * MLA_V, seg), BF16)],
        compiler_params=_params("parallel", "parallel"),
        name="mla_kv",
    )(p3, p3, gain.reshape(1, MLA_RANK), w_k, wv_t, *tables)


def _mla_attn_kernel(qt_ref, k_ref, vt_ref, o_ref):
    @pl.when(pl.program_id(2) == 0)
    def _():
        o_ref[...] = jnp.zeros_like(o_ref)

    @pl.when(pl.program_id(2) > 0)
    def _():
        heads = range(MLA_HEADS_PER_STEP)
        seg = k_ref.shape[0]
        tk = MLA_KEY_TILE if (seg - TM) % MLA_KEY_TILE == 0 else TM
        tiles = [(0, TM)] + [(a, a + tk) for a in range(TM, seg, tk)]

        def scores(hh, j):
            return _dot(k_ref[tiles[j][0]:tiles[j][1], hh * LANES:(hh + 1) * LANES],
                        qt_ref[0, hh * LANES:(hh + 1) * LANES, :])

        m = [jnp.full((1, TM), -jnp.inf, F32) for _ in heads]
        l = [jnp.zeros((1, TM), F32) for _ in heads]
        acc = [jnp.zeros((MLA_V, TM), F32) for _ in heads]
        s_next = [scores(hh, 0) for hh in heads]
        for j in range(len(tiles)):
            s_cur = s_next
            if j + 1 < len(tiles):
                s_next = [scores(hh, j + 1) for hh in heads]
            for hh in heads:
                m_new = jnp.maximum(m[hh], jnp.max(s_cur[hh], axis=0, keepdims=True))
                alpha = jnp.exp2(m[hh] - m_new)
                p = jnp.exp2(s_cur[hh] - m_new)
                l[hh] = alpha * l[hh] + jnp.sum(p, axis=0, keepdims=True)
                v_t = vt_ref[0, hh * MLA_V:(hh + 1) * MLA_V, tiles[j][0]:tiles[j][1]]
                acc[hh] = alpha * acc[hh] + _dot(v_t, p.astype(BF16))
                m[hh] = m_new
        o_t = jnp.concatenate([acc[hh] / l[hh] for hh in heads], axis=0)
        o_ref[...] = jnp.transpose(o_t).astype(o_ref.dtype)


def _mla_attention(q_t, k, v_t, nb, seg):
    t_tok = k.shape[0]
    nt = seg // TM
    hps = MLA_HEADS_PER_STEP
    return pl.pallas_call(
        _mla_attn_kernel,
        grid=(nb, MLA_HEADS // hps, nt),
        in_specs=[pl.BlockSpec((1, hps * LANES, TM), lambda b, hp, j: (b * nt + j, hp, 0)),
                  pl.BlockSpec((seg, hps * LANES), lambda b, hp, j: (b, hp)),
                  pl.BlockSpec((1, hps * MLA_V, seg), lambda b, hp, j: (b, hp, 0))],
        out_specs=pl.BlockSpec((TM, hps * MLA_V), lambda b, hp, j: (b * nt + j, hp)),
        out_shape=jax.ShapeDtypeStruct((t_tok, MLA_HEADS * MLA_V), BF16),
        compiler_params=_params("parallel", "parallel", "arbitrary"),
        name="mla_attention",
    )(q_t, k, v_t)


def _final_kernel(x_ref, g_ref, o_ref):
    o_ref[0] = _rms_rows(x_ref[...], g_ref[...])


def _final_norm(x, gain, nb, seg):
    d = x.shape[1]
    nt = seg // TM
    return pl.pallas_call(
        _final_kernel,
        grid=(nb, nt - 1),
        in_specs=[pl.BlockSpec((TM, d), lambda b, j: (b * nt + 1 + j, 0)),
                  pl.BlockSpec((1, d), lambda b, j: (0, 0))],
        out_specs=pl.BlockSpec((1, TM, d), lambda b, j: (b, j, 0)),
        out_shape=jax.ShapeDtypeStruct((nb, seg - TM, d), F32),
        compiler_params=_params("parallel", "parallel"),
        name="final_norm",
    )(x, gain.reshape(1, d))


def _rope_angles(n_lat, rot_dim):
    rows = n_lat // GRID_W
    row = jnp.repeat(jnp.arange(rows), GRID_W).astype(F32)
    col = jnp.tile(jnp.arange(GRID_W), rows).astype(F32)
    quarter = rot_dim // 4
    inv = ROPE_BASE ** (-jnp.arange(quarter, dtype=F32) / quarter)
    ang = jnp.concatenate([row[:, None] * inv, col[:, None] * inv], axis=-1)
    return jnp.cos(ang), jnp.sin(ang)


def _with_ctx_rows(tab, fill):
    return jnp.concatenate([jnp.full((TM, tab.shape[1]), fill, F32), tab], axis=0)


def _attn_rope_tables(n_lat):
    cos, sin = _rope_angles(n_lat, ATT_HEAD_DIM)
    cos_h = jnp.concatenate([cos, cos], axis=1)
    sin_h = jnp.concatenate([-sin, sin], axis=1)
    reps = LANES // ATT_HEAD_DIM
    return (_with_ctx_rows(jnp.tile(cos_h, (1, reps)), 1.0), _with_ctx_rows(jnp.tile(sin_h, (1, reps)), 0.0))


def _swap_halves(w, n_heads, dim):
    w3 = w.reshape(w.shape[0], n_heads, 2, dim // 2)
    return w3[:, :, ::-1, :].reshape(w.shape[0], n_heads * dim)


def _mla_k_tables(n_lat):
    cos, sin = _rope_angles(n_lat, MLA_ROPE)
    lo = jnp.zeros((n_lat, MLA_NOPE), F32)
    hi = jnp.zeros((n_lat, LANES - MLA_NOPE - MLA_ROPE), F32)
    cos_c = jnp.concatenate([lo, cos, cos, hi], axis=1)
    sin_c = jnp.concatenate([lo, -sin, sin, hi], axis=1)
    ctx_cos = jnp.concatenate([lo[:TM], jnp.ones((TM, MLA_ROPE), F32), hi[:TM]], axis=1)
    return (jnp.concatenate([ctx_cos, cos_c], axis=0), _with_ctx_rows(sin_c, 0.0))


def _pad_heads(w, real):
    r = w.shape[0]
    w3 = w.reshape(r, MLA_HEADS, real)
    return jnp.pad(w3, ((0, 0), (0, 0), (0, LANES - real))).reshape(r, MLA_HEADS * LANES)


def kernel(x, c, ctx, c_ctx, norm1_g, norm2_g, w_mod, b_mod, moe_w_group, moe_b_group, moe_w_expert, moe_b_expert, moe_w_gate, moe_w_up, moe_w_down, attn_w_in, attn_sink, attn_w_out, ssm_w_in, ssm_conv_w, ssm_conv_b, ssm_dt_bias, ssm_a_log, ssm_d, ssm_norm_g, ssm_w_out, mlstm_w_in, mlstm_gate_b, mlstm_norm_g, mlstm_w_out, mla_w_in, mla_q_norm_g, mla_w_q_up, mla_kv_norm_g, mla_w_kv_up, mla_w_out, final_norm_g):
    nb, n_lat, d = x.shape
    assert ctx.shape[1] == TM and d == D_MODEL and n_lat % TM == 0
    depth = w_mod.shape[0]
    seg = TM + n_lat
    nt = seg // TM
    t_tok = nb * seg

    xs = jnp.concatenate([ctx, x], axis=1).reshape(t_tok, d)

    rows = -(-(nb + 1) // 8) * 8
    cvec = jnp.concatenate([c, c_ctx[None, :], jnp.zeros((rows - nb - 1, d), F32)], axis=0)
    mods = _modulation(cvec, w_mod, b_mod).reshape(depth, rows, ADALN_CHUNKS, d)
    mods = jnp.pad(mods, ((0, 0), (0, 0), (0, MOD_ROWS - ADALN_CHUNKS), (0, 0)))

    w_router = jnp.concatenate([moe_w_expert, moe_w_group,
                                jnp.zeros((depth, d, LANES - MOE_EXPERTS - MOE_GROUPS), F32)], axis=-1)
    b_router = jnp.concatenate([moe_b_expert, moe_b_group,
                                jnp.zeros((depth, LANES - MOE_EXPERTS - MOE_GROUPS), F32)], axis=-1)
    def by_group(w):
        w = w.astype(BF16).reshape(depth, MOE_GROUPS, MOE_PER_GROUP, d, MOE_FF)
        return jnp.transpose(w, (0, 1, 3, 2, 4)).reshape(depth, MOE_GROUPS, d, MOE_PER_GROUP * MOE_FF)

    w_gu = jnp.concatenate([by_group(moe_w_gate), by_group(moe_w_up)], axis=-1)
    w_down = moe_w_down.astype(BF16).reshape(depth, MOE_GROUPS, MOE_PER_GROUP * MOE_FF, d)

    for i in range(depth):
        kind = i % 4
        mod_i = mods[i]
        if kind == 0:
            nq, nk = ATT_HEADS * ATT_HEAD_DIM, ATT_KV_HEADS * ATT_HEAD_DIM
            w_in = attn_w_in[i // 4]
            w_all = jnp.concatenate([w_in, _swap_halves(w_in[:, :nq], ATT_HEADS, ATT_HEAD_DIM),
                                     _swap_halves(w_in[:, nq:nq + nk], ATT_KV_HEADS, ATT_HEAD_DIM)],
                                    axis=1).astype(BF16)
            n_in = w_in.shape[1]
            q, k, v = _normproj(xs, mod_i, norm1_g[i], w_all,
                                [(0, nq, n_in), (nq, nk, n_in + nq), (nq + nk, nk, None)],
                                [BF16, BF16, BF16], nt, nb, tables=_attn_rope_tables(n_lat))
            o = _windowed_attention(q, k, v, attn_sink[i // 4], nb, seg)
            xs = _outproj([o], [nq], xs, mod_i, attn_w_out[i // 4].astype(BF16), _plain_prologue, nt, nb)
        elif kind == 1:
            j = i // 4
            w_in = jnp.pad(ssm_w_in[j], ((0, 0), (0, LANES - 2 * SSM_HEADS))).astype(BF16)
            z, xbc, dt = _normproj(xs, mod_i, norm1_g[i], w_in,
                                   [(0, SSM_D_INNER, None), (SSM_D_INNER, SSM_CONV_DIM, None),
                                    (SSM_D_INNER + SSM_CONV_DIM, LANES, None)],
                                   [BF16, BF16, F32], nt, nb)
            lane_pad = LANES - 2 * SSM_HEADS
            dt_bias = jnp.pad(ssm_dt_bias[j].reshape(1, -1), ((0, 0), (0, lane_pad)))
            a_neg = jnp.pad(-jnp.exp(ssm_a_log[j].astype(F32)).reshape(1, -1), ((0, 0), (0, lane_pad)))
            conv_w = jnp.pad(ssm_conv_w[j], ((0, 8 - SSM_CONV), (0, 0)))
            xc, bm, cm, dtv, a = _ssm_conv(xbc, dt, conv_w, ssm_conv_b[j].reshape(1, -1), dt_bias, a_neg, nt)
            yf, yb = _ssd_scan(xc, bm, cm, dtv, a, ssm_d[j].astype(F32), nb, seg)
            xs = _outproj([z], [SSM_D_INNER], xs, mod_i, ssm_w_out[j].astype(BF16), _ssm_prologue,
                          nt, nb, extra=(ssm_norm_g[j].reshape(1, -1),), ins_t=(yf, yb))
        elif kind == 2:
            j = i // 4
            nqk, nv = 2 * ML_HEADS * ML_QK_DIM, ML_HEADS * ML_V_DIM
            w_in = jnp.pad(mlstm_w_in[j], ((0, 0), (0, LANES - 4 * ML_HEADS))).astype(BF16)
            qk, v, o, g = _normproj(xs, mod_i, norm1_g[i], w_in,
                                    [(0, nqk, None), (nqk, nv, None), (nqk + nv, nv, None), (nqk + 2 * nv, LANES, None)],
                                    [BF16, BF16, BF16, F32], nt, nb)
            gate_b = jnp.pad(mlstm_gate_b[j].reshape(1, -1), ((0, 0), (0, LANES - 4 * ML_HEADS)))
            hf, hb = _mlstm_scan(qk, v, g, gate_b, nb, seg)
            xs = _outproj([o], [nv], xs, mod_i, mlstm_w_out[j].astype(BF16), _mlstm_prologue,
                          nt, nb, extra=(mlstm_norm_g[j].reshape(1, -1),), ins_t=(hf, hb))
        else:
            j = i // 4
            w_in = jnp.pad(mla_w_in[j], ((0, 0), (0, LANES - MLA_ROPE))).astype(BF16)
            (p3,) = _normproj(xs, mod_i, norm1_g[i], w_in, [(0, w_in.shape[1], None)], [F32], nt, nb)
            cos, sin = _rope_angles(n_lat, MLA_ROPE)
            tables_t = (jnp.transpose(_with_ctx_rows(cos, 1.0)), jnp.transpose(_with_ctx_rows(sin, 0.0)))
            wq_t = jnp.transpose(_pad_heads(mla_w_q_up[j], MLA_NOPE + MLA_ROPE)).astype(BF16)
            q_t = _mla_q(p3, mla_q_norm_g[j], wq_t, tables_t, nt)
            w_kv = mla_w_kv_up[j].reshape(MLA_RANK, MLA_HEADS, MLA_NOPE + MLA_V)
            w_k = _pad_heads(w_kv[:, :, :MLA_NOPE].reshape(MLA_RANK, MLA_HEADS * MLA_NOPE), MLA_NOPE).astype(BF16)
            wv_t = jnp.transpose(w_kv[:, :, MLA_NOPE:].reshape(MLA_RANK, MLA_HEADS * MLA_V)).astype(BF16)
            k, v_t = _mla_kv(p3, mla_kv_norm_g[j], w_k, wv_t, _mla_k_tables(n_lat), nb, seg)
            o = _mla_attention(q_t, k, v_t, nb, seg)
            xs = _outproj([o], [MLA_HEADS * MLA_V], xs, mod_i, mla_w_out[j].astype(BF16), _plain_prologue, nt, nb)

        xs = _moe(xs, mod_i, norm2_g[i], w_router[i], b_router[i].reshape(1, -1), w_gu[i], w_down[i], nt, nb)

    return _final_norm(xs, final_norm_g, nb, seg)
```

```python
import functools
import math

import jax
import jax.numpy as jnp
from jax import lax
from jax.experimental import pallas as pl
from jax.experimental.pallas import tpu as pltpu

F32 = jnp.float32
BF16 = jnp.bfloat16

D_MODEL = 1024
GRID_W = 64
EPS = 1e-6
ROPE_BASE = 10000.0
ADALN_CHUNKS = 6
CHUNK = 128
TM = 256
MOD_ROWS = 8
LANES = 128
V7X_VMEM_LIMIT = 48 * 1024 * 1024

ATT_HEADS, ATT_KV_HEADS, ATT_HEAD_DIM, WINDOW = 16, 4, 64, 128
ATT_GROUP = ATT_HEADS // ATT_KV_HEADS
SSM_D_INNER, SSM_HEAD_DIM, SSM_HEADS, SSM_GROUPS, SSM_STATE, SSM_CONV = 2048, 64, 32, 4, 128, 5
SSM_HPG = SSM_HEADS // SSM_GROUPS
SSM_BC = SSM_GROUPS * SSM_STATE
SSM_CONV_DIM = SSM_D_INNER + 2 * SSM_BC
ML_HEADS, ML_QK_DIM, ML_V_DIM = 8, 64, 128
MLA_HEADS, MLA_RANK, MLA_NOPE, MLA_ROPE, MLA_V = 16, 256, 64, 32, 64
MOE_GROUPS, MOE_PER_GROUP, MOE_EXPERTS, MOE_FF = 4, 4, 16, 256
MOE_ROWS = 1024
MLA_KEY_TILE = 512
MLA_HEADS_PER_STEP = 4
LOG2_E = 1.4426950408889634


def _dot(a, b):
    return jnp.dot(a, b, preferred_element_type=F32)


def _dot_nt(a, b):
    return lax.dot_general(a, b, (((1,), (1,)), ((), ())), preferred_element_type=F32)


def _split_bf16(x):
    hi = x.astype(BF16)
    lo = (x - hi.astype(F32)).astype(BF16)
    return hi, lo


def _dot_split(a, b):
    a_hi, a_lo = _split_bf16(a)
    b_hi, b_lo = _split_bf16(b)
    return _dot(a_hi, b_hi) + _dot(a_lo, b_hi) + _dot(a_hi, b_lo)


def _sigmoid(x):
    return 1.0 / (1.0 + jnp.exp(-x))


def _silu(x):
    return x * _sigmoid(x)


def _softplus(x):
    return jnp.maximum(x, 0.0) + jnp.log1p(jnp.exp(-jnp.abs(x)))


def _params(*sem):
    return pltpu.CompilerParams(dimension_semantics=sem, vmem_limit_bytes=V7X_VMEM_LIMIT)


def _mod_row(t, nt, nb):
    return jnp.where(t % nt == 0, nb, t // nt)


def _mod_kernel(c_ref, w_ref, b_ref, o_ref):
    o_ref[0] = _dot_split(_silu(c_ref[...]), w_ref[0]) + b_ref[0]


def _modulation(cvec, w_mod, b_mod):
    depth, d, n = w_mod.shape
    tn = 1536
    rows = cvec.shape[0]
    return pl.pallas_call(
        _mod_kernel,
        grid=(depth, n // tn),
        in_specs=[pl.BlockSpec((rows, d), lambda l, j: (0, 0)),
                  pl.BlockSpec((1, d, tn), lambda l, j: (l, 0, j)),
                  pl.BlockSpec((1, 1, tn), lambda l, j: (l, 0, j))],
        out_specs=pl.BlockSpec((1, rows, tn), lambda l, j: (l, 0, j)),
        out_shape=jax.ShapeDtypeStruct((depth, rows, n), F32),
        compiler_params=_params("arbitrary", "arbitrary"),
        name="modulation",
    )(cvec, w_mod, b_mod.reshape(depth, 1, n))


def _normed(x, g_row, mod, sh_row, sc_row):
    y = x * lax.rsqrt(jnp.mean(x * x, axis=-1, keepdims=True) + EPS) * g_row
    return y * (1.0 + mod[sc_row:sc_row + 1, :]) + mod[sh_row:sh_row + 1, :]


def _proj_columns(hb, w_ref, o_ref, start, width, rot_start, cos, sin, scale=None, chunk=512):
    for c in range(0, width, chunk):
        cw = min(chunk, width - c)
        acc = _dot(hb, w_ref[:, start + c:start + c + cw])
        if rot_start is not None:
            rot = _dot(hb, w_ref[:, rot_start + c:rot_start + c + cw])
            reps = cw // LANES
            acc = acc * jnp.tile(cos, (1, reps)) + rot * jnp.tile(sin, (1, reps))
        if scale is not None:
            acc = acc * scale
        o_ref[:, c:c + cw] = acc.astype(o_ref.dtype)


def _normproj_kernel(*refs, outs, has_rope):
    x_ref, mod_ref, g_ref, w_ref = refs[:4]
    k = 4
    cos = sin = None
    if has_rope:
        cos, sin = refs[4][...], refs[5][...]
        k = 6
    hb = _normed(x_ref[...], g_ref[...], mod_ref[0], 0, 1).astype(BF16)
    for o_ref, spec in zip(refs[k:], outs):
        _proj_columns(hb, w_ref, o_ref, *spec[:3], cos, sin, scale=spec[3] if len(spec) > 3 else None)


def _normproj(x, mods, gain, w, outs, out_dtypes, nt, nb, tables=None):
    t_tok, d = x.shape
    n = w.shape[1]
    in_specs = [pl.BlockSpec((TM, d), lambda t: (t, 0)),
                pl.BlockSpec((1, MOD_ROWS, d), lambda t: (_mod_row(t, nt, nb), 0, 0)),
                pl.BlockSpec((1, d), lambda t: (0, 0)),
                pl.BlockSpec((d, n), lambda t: (0, 0))]
    args = [x, mods, gain.reshape(1, d), w]
    if tables is not None:
        in_specs += [pl.BlockSpec((TM, LANES), lambda t: (t % nt, 0))] * 2
        args += list(tables)
    return pl.pallas_call(
        functools.partial(_normproj_kernel, outs=tuple(outs), has_rope=tables is not None),
        grid=(t_tok // TM,),
        in_specs=in_specs,
        out_specs=[pl.BlockSpec((TM, o[1]), lambda t: (t, 0)) for o in outs],
        out_shape=[jax.ShapeDtypeStruct((t_tok, o[1]), dt) for o, dt in zip(outs, out_dtypes)],
        compiler_params=_params("parallel"),
        name="normproj",
    )(*args)


def _outproj_kernel(*refs, prologue, n_in):
    ins = refs[:n_in]
    x_ref, mod_ref, w_ref, o_ref = refs[n_in:]
    a = prologue(*ins)
    y = _dot(a, w_ref[...])
    o_ref[...] = x_ref[...] + mod_ref[0][2:3, :] * y


def _outproj(ins, in_widths, x, mods, w, prologue, nt, nb, extra=(), ins_t=()):
    t_tok, d = x.shape
    in_specs = [pl.BlockSpec((1, a.shape[1], TM), lambda t: (t // nt, 0, t % nt)) for a in ins_t]
    in_specs += [pl.BlockSpec((TM, wd), lambda t: (t, 0)) for wd in in_widths]
    in_specs += [pl.BlockSpec(e.shape, lambda t: (0, 0)) for e in extra]
    n_in = len(in_specs)
    in_specs += [pl.BlockSpec((TM, d), lambda t: (t, 0)),
                 pl.BlockSpec((1, MOD_ROWS, d), lambda t: (_mod_row(t, nt, nb), 0, 0)),
                 pl.BlockSpec(w.shape, lambda t: (0, 0))]
    return pl.pallas_call(
        functools.partial(_outproj_kernel, prologue=prologue, n_in=n_in),
        grid=(t_tok // TM,),
        in_specs=in_specs,
        out_specs=pl.BlockSpec((TM, d), lambda t: (t, 0)),
        out_shape=jax.ShapeDtypeStruct((t_tok, d), F32),
        input_output_aliases={n_in: 0},
        compiler_params=_params("parallel"),
        name="outproj",
    )(*ins_t, *ins, *extra, x, mods, w)


def _group_rms(y, n_groups):
    width = y.shape[1] // n_groups
    parts = []
    for g in range(n_groups):
        yg = y[:, g * width:(g + 1) * width]
        parts.append(yg * lax.rsqrt(jnp.mean(yg * yg, axis=-1, keepdims=True) + EPS))
    return jnp.concatenate(parts, axis=1)


def _plain_prologue(o_ref):
    return o_ref[...]


def _ssm_prologue(yf_ref, yb_ref, z_ref, g_ref):
    y = jnp.transpose(yf_ref[0].astype(F32) + yb_ref[0].astype(F32)) * _silu(z_ref[...].astype(F32))
    return (_group_rms(y, SSM_GROUPS) * g_ref[...]).astype(BF16)


def _mlstm_prologue(hf_ref, hb_ref, o_ref, g_ref):
    h = jnp.transpose(hf_ref[0].astype(F32) + hb_ref[0].astype(F32))
    return (_group_rms(h, ML_HEADS) * g_ref[...] * _sigmoid(o_ref[...].astype(F32))).astype(BF16)


def _route(logits):
    lane = lax.broadcasted_iota(jnp.int32, logits.shape, 1).astype(F32)
    neg = -jnp.inf
    lg = jnp.where((lane >= MOE_EXPERTS) & (lane < MOE_EXPERTS + MOE_GROUPS), logits, neg)
    gmax = jnp.max(lg, axis=-1, keepdims=True)
    g_sel = jnp.min(jnp.where(lg == gmax, lane, LANES), axis=-1, keepdims=True) - MOE_EXPERTS
    p_g = 1.0 / jnp.sum(jnp.exp(lg - gmax), axis=-1, keepdims=True)
    in_group = (lane >= g_sel * MOE_PER_GROUP) & (lane < (g_sel + 1) * MOE_PER_GROUP)
    le = jnp.where(in_group, logits, neg)
    v1 = jnp.max(le, axis=-1, keepdims=True)
    i1 = jnp.min(jnp.where(le == v1, lane, LANES), axis=-1, keepdims=True)
    le2 = jnp.where(lane == i1, neg, le)
    v2 = jnp.max(le2, axis=-1, keepdims=True)
    i2 = jnp.min(jnp.where(le2 == v2, lane, LANES), axis=-1, keepdims=True)
    e2 = jnp.exp(v2 - v1)
    w1 = p_g / (1.0 + e2)
    return jnp.where(lane == i1, w1, 0.0) + jnp.where(lane == i2, w1 * e2, 0.0)


def _moe_kernel(x_ref, mods_ref, g_ref, wr_ref, br_ref, wg_ref, wu_ref, wd_ref, o_ref, h_scr, comb_scr,
                *, nt, nb, sub):
    i = pl.program_id(0)
    e = pl.program_id(1)

    @pl.when(e == 0)
    def _():
        w_hi, w_lo = _split_bf16(wr_ref[...])
        w_both = jnp.concatenate([w_hi, w_lo], axis=1)
        for s in range(sub):
            rows = slice(s * TM, (s + 1) * TM)
            mod = mods_ref[_mod_row(i * sub + s, nt, nb)]
            x = x_ref[rows, :]
            h = _normed(x, g_ref[...], mod, 3, 4)
            h_hi, h_lo = _split_bf16(h)
            h_scr[rows, :] = h_hi
            both = _dot(h_hi, w_both)
            logits = both[:, :LANES] + both[:, LANES:] + _dot(h_lo, w_hi) + br_ref[...]
            comb_scr[rows, :] = _route(logits)
            o_ref[rows, :] = x

    hb = h_scr[...]
    lane = lax.broadcasted_iota(jnp.int32, comb_scr.shape, 1)
    comb = comb_scr[...]
    scaled = []
    for k in range(MOE_PER_GROUP):
        act = _silu(_dot(hb, wg_ref[0, k])) * _dot(hb, wu_ref[0, k])
        cw = jnp.sum(jnp.where(lane == e * MOE_PER_GROUP + k, comb, 0.0), axis=-1, keepdims=True)
        scaled.append((act * cw).astype(BF16))
    y = _dot(jnp.concatenate(scaled, axis=1), wd_ref[0, 0])
    for s in range(sub):
        rows = slice(s * TM, (s + 1) * TM)
        gate = mods_ref[_mod_row(i * sub + s, nt, nb)][5:6, :]
        o_ref[rows, :] += gate * y[rows, :]


def _moe(x, mods, gain, w_router, b_router, w_gate, w_up, w_down, layer, nt, nb):
    t_tok, d = x.shape
    pg = MOE_PER_GROUP
    rows = MOE_ROWS if t_tok % MOE_ROWS == 0 else TM
    sub = rows // TM
    return pl.pallas_call(
        functools.partial(_moe_kernel, nt=nt, nb=nb, sub=sub),
        grid=(t_tok // rows, MOE_GROUPS),
        in_specs=[pl.BlockSpec((rows, d), lambda i, e: (i, 0)),
                  pl.BlockSpec(mods.shape, lambda i, e: (0, 0, 0)),
                  pl.BlockSpec((1, d), lambda i, e: (0, 0)),
                  pl.BlockSpec((d, LANES), lambda i, e: (0, 0)),
                  pl.BlockSpec((1, LANES), lambda i, e: (0, 0)),
                  pl.BlockSpec((1, pg, d, MOE_FF), lambda i, e: (layer, e, 0, 0)),
                  pl.BlockSpec((1, pg, d, MOE_FF), lambda i, e: (layer, e, 0, 0)),
                  pl.BlockSpec((1, 1, pg * MOE_FF, d), lambda i, e: (layer, e, 0, 0))],
        out_specs=pl.BlockSpec((rows, d), lambda i, e: (i, 0)),
        out_shape=jax.ShapeDtypeStruct((t_tok, d), F32),
        scratch_shapes=[pltpu.VMEM((rows, d), BF16), pltpu.VMEM((rows, LANES), F32)],
        input_output_aliases={0: 0},
        compiler_params=_params("parallel", "arbitrary"),
        name="moe",
    )(x, mods, gain.reshape(1, d), w_router, b_router, w_gate, w_up, w_down)


def _attn_kernel(sink_ref, q_ref, kp_ref, kc_ref, kn_ref, vp_ref, vc_ref, vn_ref, kx_ref, vx_ref, o_ref, *, n_lat):
    j = pl.program_id(1)
    jl = j - 1
    dh = ATT_HEAD_DIM
    n_loc = TM + 2 * CHUNK
    k_all = jnp.concatenate([kp_ref[...], kc_ref[...], kn_ref[...], kx_ref[...]], axis=0)
    v_t = jnp.transpose(jnp.concatenate([vp_ref[...], vc_ref[...], vn_ref[...], vx_ref[...]],
                                        axis=0).astype(F32)).astype(BF16)
    q_t = jnp.transpose(q_ref[...].astype(F32)).astype(BF16)
    n_keys = k_all.shape[0]
    kj = lax.broadcasted_iota(jnp.int32, (n_keys, TM), 0)
    qi = lax.broadcasted_iota(jnp.int32, (n_keys, TM), 1)
    kpos = jl * TM - CHUNK + kj
    local_ok = (kj >= qi + CHUNK - WINDOW) & (kj <= qi + CHUNK + WINDOW) & (kpos >= 0) & (kpos < n_lat) & (jl >= 0)
    bias = jnp.where((kj >= n_loc) | local_ok, 0.0, -jnp.inf)
    bias = jnp.concatenate([bias] * ATT_GROUP, axis=1)
    lane_head = lax.broadcasted_iota(jnp.int32, (1, ATT_GROUP * TM), 1) // TM
    zeros_q = jnp.zeros((dh, ATT_GROUP * TM), BF16)
    outs = []
    for g in range(ATT_KV_HEADS):
        half = g % 2
        k_pair = k_all[:, (g - half) * dh:(g - half + 2) * dh]
        qg = jnp.concatenate([q_t[(g * ATT_GROUP + a) * dh:(g * ATT_GROUP + a + 1) * dh, :]
                              for a in range(ATT_GROUP)], axis=1)
        q_m = jnp.concatenate([zeros_q, qg] if half else [qg, zeros_q], axis=0)
        s = _dot(k_pair, q_m) + bias
        sink = jnp.zeros((1, ATT_GROUP * TM), F32)
        for a in range(ATT_GROUP):
            sink = jnp.where(lane_head == a, sink_ref[g * ATT_GROUP + a] * LOG2_E, sink)
        m = jnp.maximum(jnp.max(s, axis=0, keepdims=True), sink)
        p = jnp.exp2(s - m)
        den = jnp.sum(p, axis=0, keepdims=True) + jnp.exp2(sink - m)
        og = _dot(v_t[g * dh:(g + 1) * dh, :], p.astype(BF16)) / den
        outs += [og[:, a * TM:(a + 1) * TM] for a in range(ATT_GROUP)]
    o_ref[...] = jnp.transpose(jnp.concatenate(outs, axis=0)).astype(o_ref.dtype)


def _windowed_attention(q, k, v, sink, nb, seg):
    t_tok = q.shape[0]
    nchunk = seg // CHUNK
    nt = seg // TM
    per_tile = TM // CHUNK
    kvw = ATT_KV_HEADS * ATT_HEAD_DIM

    def halo(off):
        return lambda b, j, *_: (b * nchunk + jnp.clip(j * per_tile + off, per_tile, nchunk - 1), 0)

    tile = lambda b, j, *_: (b * nt + j, 0)
    kv_spec = [pl.BlockSpec((CHUNK, kvw), halo(-1)), pl.BlockSpec((TM, kvw), tile),
               pl.BlockSpec((CHUNK, kvw), halo(per_tile))]
    ctx_spec = pl.BlockSpec((TM, kvw), lambda b, j, *_: (b * nt, 0))
    return pl.pallas_call(
        functools.partial(_attn_kernel, n_lat=seg - TM),
        grid_spec=pltpu.PrefetchScalarGridSpec(
            num_scalar_prefetch=1,
            grid=(nb, nt),
            in_specs=[pl.BlockSpec((TM, q.shape[1]), tile)] + kv_spec + kv_spec + [ctx_spec, ctx_spec],
            out_specs=pl.BlockSpec((TM, q.shape[1]), tile),
        ),
        out_shape=jax.ShapeDtypeStruct((t_tok, q.shape[1]), BF16),
        compiler_params=_params("parallel", "parallel"),
        name="windowed_attention",
    )(sink, q, k, k, k, v, v, v, k, v)


def _conv_kernel(cur_ref, prev_ref, next_ref, dt_ref, w_ref, b_ref, dtb_ref, aneg_ref,
                 xs_ref, bm_ref, cm_ref, dtv_ref, a_ref, ext, *, nt):
    t = pl.program_id(0)
    tl = t % nt
    halo = 8
    first = (tl == 0) | (tl == 1)
    last = (tl == 0) | (tl == nt - 1)
    ext[0:halo, :] = jnp.where(first, 0.0, prev_ref[...].astype(F32)[halo:2 * halo, :])
    ext[halo:halo + TM, :] = cur_ref[...].astype(F32)
    ext[halo + TM:2 * halo + TM, :] = jnp.where(last, 0.0, next_ref[...].astype(F32)[0:halo, :])
    pad = SSM_CONV // 2
    chunk = 512
    for c in range(0, SSM_CONV_DIM, chunk):
        acc = jnp.broadcast_to(b_ref[:, c:c + chunk], (TM, chunk))
        for k in range(SSM_CONV):
            acc = acc + w_ref[k:k + 1, c:c + chunk] * ext[halo - pad + k:halo - pad + k + TM, c:c + chunk]
        y = _silu(acc)
        if c < SSM_D_INNER:
            xs_ref[0, c:c + chunk, :] = jnp.transpose(y).astype(xs_ref.dtype)
        elif c < SSM_D_INNER + SSM_BC:
            bm_ref[:, c - SSM_D_INNER:c - SSM_D_INNER + chunk] = y.astype(bm_ref.dtype)
        else:
            off = c - SSM_D_INNER - SSM_BC
            cm_ref[0, off:off + chunk, :] = jnp.transpose(y).astype(cm_ref.dtype)
    dtv = _softplus(dt_ref[...] + dtb_ref[...])
    dtv_ref[0] = jnp.transpose(dtv)
    a_ref[0] = jnp.transpose(dtv * aneg_ref[...])


def _ssm_conv(xbc, dt, conv_w, conv_b, dt_bias, a_neg, nt):
    t_tok = xbc.shape[0]
    n16 = t_tok // 16
    seg = nt * TM
    nb = t_tok // seg
    row = lambda t: (t, 0)
    col = lambda t: (t // nt, 0, t % nt)
    fixed = lambda t: (0, 0)
    return pl.pallas_call(
        functools.partial(_conv_kernel, nt=nt),
        grid=(t_tok // TM,),
        in_specs=[pl.BlockSpec((TM, SSM_CONV_DIM), row),
                  pl.BlockSpec((16, SSM_CONV_DIM), lambda t: (jnp.maximum(t * (TM // 16) - 1, 0), 0)),
                  pl.BlockSpec((16, SSM_CONV_DIM), lambda t: (jnp.minimum((t + 1) * (TM // 16), n16 - 1), 0)),
                  pl.BlockSpec((TM, LANES), row),
                  pl.BlockSpec((8, SSM_CONV_DIM), fixed),
                  pl.BlockSpec((1, SSM_CONV_DIM), fixed),
                  pl.BlockSpec((1, LANES), fixed),
                  pl.BlockSpec((1, LANES), fixed)],
        out_specs=[pl.BlockSpec((1, SSM_D_INNER, TM), col), pl.BlockSpec((TM, SSM_BC), row),
                   pl.BlockSpec((1, SSM_BC, TM), col), pl.BlockSpec((1, LANES, TM), col),
                   pl.BlockSpec((1, LANES, TM), col)],
        out_shape=[jax.ShapeDtypeStruct((nb, SSM_D_INNER, seg), BF16), jax.ShapeDtypeStruct((t_tok, SSM_BC), BF16),
                   jax.ShapeDtypeStruct((nb, SSM_BC, seg), BF16), jax.ShapeDtypeStruct((nb, LANES, seg), F32),
                   jax.ShapeDtypeStruct((nb, LANES, seg), F32)],
        scratch_shapes=[pltpu.VMEM((TM + 16, SSM_CONV_DIM), F32)],
        compiler_params=_params("parallel"),
        name="ssm_conv",
    )(xbc, xbc, xbc, dt, conv_w, conv_b, dt_bias, a_neg)


def _tri(lower):
    r = lax.broadcasted_iota(jnp.int32, (CHUNK, CHUNK), 0)
    c = lax.broadcasted_iota(jnp.int32, (CHUNK, CHUNK), 1)
    return (c <= r) if lower else (c >= r)


def _cumsum_rows(x):
    tri = jnp.where(_tri(True), 1.0, 0.0).astype(BF16)
    hi, lo = _split_bf16(x)
    return _dot(tri, hi) + _dot(tri, lo)


def _ssd_direction(xs_ref, bm_ref, ct_ref, dt_ref, a_ref, dsk_ref, st_ref, y_ref, d):
    a_t = a_ref[0]
    dt_t = dt_ref[0]
    upper = jnp.where(_tri(False), 1.0, 0.0).astype(BF16)
    a_hi, a_lo = _split_bf16(a_t)
    cum = _dot(a_hi, upper) + _dot(a_lo, upper)
    cum_end = cum[:, CHUNK - 1:CHUNK]
    if d == 0:
        lane_v, sub_v = cum, -cum
        inter = jnp.exp(cum)
        w_upd = jnp.exp(cum_end - cum) * dt_t
    else:
        ecum = cum - a_t
        lane_v, sub_v = -ecum, ecum
        inter = jnp.exp(cum_end - ecum)
        w_upd = jnp.exp(ecum) * dt_t
    sub_c = jnp.transpose(sub_v)
    decay_end = jnp.exp(cum_end)
    mask = _tri(d == 1)
    p = SSM_HEAD_DIM
    for g in range(SSM_GROUPS):
        bg = bm_ref[:, g * SSM_STATE:(g + 1) * SSM_STATE]
        cg_t = ct_ref[0, g * SSM_STATE:(g + 1) * SSM_STATE, :]
        cb_t = _dot(bg, cg_t)
        state = st_ref[g]
        y_in = _dot(state.astype(BF16), cg_t)
        upd = []
        for e in range(SSM_HPG):
            h = g * SSM_HPG + e
            ln = d * SSM_HEADS + h
            rows = slice(e * p, (e + 1) * p)
            seg = jnp.where(mask, sub_c[:, ln:ln + 1] + lane_v[ln:ln + 1, :], -jnp.inf)
            m_t = (cb_t * jnp.exp(seg)).astype(BF16)
            xf = xs_ref[0, h * p:(h + 1) * p, :].astype(F32)
            u = (xf * dt_t[ln:ln + 1, :]).astype(BF16)
            yh = _dot(u, m_t) + y_in[rows, :] * inter[ln:ln + 1, :] + dsk_ref[d, h] * xf
            y_ref[0, h * p:(h + 1) * p, :] = yh.astype(y_ref.dtype)
            upd.append((xf * w_upd[ln:ln + 1, :]).astype(BF16))
        new = _dot(jnp.concatenate(upd, axis=0), bg)
        for e in range(SSM_HPG):
            ln = d * SSM_HEADS + g * SSM_HPG + e
            rows = slice(e * p, (e + 1) * p)
            st_ref[g, rows, :] = state[rows, :] * decay_end[ln:ln + 1, :] + new[rows, :]


def _ssd_kernel(dsk_ref, xsf, bmf, ctf, dtf, af, xsb, bmb, ctb, dtb, ab, yf_ref, yb_ref, stf, stb):
    @pl.when(pl.program_id(1) == 0)
    def _():
        stf[...] = jnp.zeros_like(stf)
        stb[...] = jnp.zeros_like(stb)

    _ssd_direction(xsf, bmf, ctf, dtf, af, dsk_ref, stf, yf_ref, 0)
    _ssd_direction(xsb, bmb, ctb, dtb, ab, dsk_ref, stb, yb_ref, 1)


def _scan_maps(nchunk):
    ctx_chunks = TM // CHUNK
    fwd = lambda b, t: (b * nchunk + t, 0)
    bwd = lambda b, t: (b * nchunk + jnp.where(t < ctx_chunks, ctx_chunks - 1 - t, nchunk + ctx_chunks - 1 - t), 0)
    return fwd, bwd


def _ssd_scan(xs_t, bm, c_t, dt_t, a_t, d_skip, nb, seg):
    nchunk = seg // CHUNK
    fwd_rows, bwd_rows = _scan_maps(nchunk)

    def specs(rows_map):
        cols_map = lambda b, t, *_: (b, 0, rows_map(b, t)[0] - b * nchunk)
        return [pl.BlockSpec((1, SSM_D_INNER, CHUNK), cols_map),
                pl.BlockSpec((CHUNK, SSM_BC), lambda b, t, *_: rows_map(b, t)),
                pl.BlockSpec((1, SSM_BC, CHUNK), cols_map),
                pl.BlockSpec((1, LANES, CHUNK), cols_map),
                pl.BlockSpec((1, LANES, CHUNK), cols_map)]

    arrs = (xs_t, bm, c_t, dt_t, a_t)
    return pl.pallas_call(
        _ssd_kernel,
        grid_spec=pltpu.PrefetchScalarGridSpec(
            num_scalar_prefetch=1,
            grid=(nb, nchunk),
            in_specs=specs(fwd_rows) + specs(bwd_rows),
            out_specs=[specs(fwd_rows)[0], specs(bwd_rows)[0]],
            scratch_shapes=[pltpu.VMEM((SSM_GROUPS, SSM_HPG * SSM_HEAD_DIM, SSM_STATE), F32)] * 2,
        ),
        out_shape=[jax.ShapeDtypeStruct(xs_t.shape, BF16)] * 2,
        compiler_params=_params("parallel", "arbitrary"),
        name="ssd_scan",
    )(d_skip, *arrs, *arrs)


def _log_sigmoid(x):
    return jnp.minimum(x, 0.0) - jnp.log1p(jnp.exp(-jnp.abs(x)))


ML_STATE_ROWS = ML_V_DIM + 16


def _mlstm_direction(qk_ref, v_ref, g_ref, gb_ref, c_st, m_st, h_ref, d):
    nh, dk, dv = ML_HEADS, ML_QK_DIM, ML_V_DIM
    g_t = jnp.transpose(g_ref[...] + gb_ref[...])
    ig = g_t[16 * d:16 * d + nh, :]
    lf = _log_sigmoid(g_t[16 * d + nh:16 * d + 2 * nh, :])
    upper = jnp.where(_tri(False), 1.0, 0.0).astype(BF16)
    lf_hi, lf_lo = _split_bf16(lf)
    fc = _dot(lf_hi, upper) + _dot(lf_lo, upper)
    tot = fc[:, CHUNK - 1:CHUNK]
    m_prev = m_st[...]
    if d == 0:
        lane_v, sub_v = fc, ig - fc
        inter = fc + m_prev
        logw = tot - fc + ig
    else:
        ec = fc - lf
        lane_v, sub_v = -ec, ec + ig
        inter = tot - ec + m_prev
        logw = ec + ig
    sub_c = jnp.transpose(jnp.concatenate([sub_v, jnp.zeros((LANES - nh, CHUNK), F32)], axis=0))
    m_new = jnp.maximum(tot + m_prev, jnp.max(logw, axis=-1, keepdims=True))
    ws = jnp.exp(logw - m_new)
    cw = jnp.exp(tot + m_prev - m_new)
    mask = _tri(d == 1)
    q_t = jnp.transpose(qk_ref[:, :nh * dk].astype(F32))
    v_t = jnp.transpose(v_ref[...].astype(F32))
    lane = lax.broadcasted_iota(jnp.int32, (1, LANES), 1)
    zeros_q = jnp.zeros((dk, CHUNK), F32)
    for h in range(nh):
        half = h % 2
        k_pair = qk_ref[:, nh * dk + (h - half) * dk:nh * dk + (h - half + 2) * dk] * (dk ** -0.5)
        qh = q_t[h * dk:(h + 1) * dk, :]
        q_m = jnp.concatenate([zeros_q, qh] if half else [qh, zeros_q], axis=0).astype(BF16)
        logd = jnp.where(mask, sub_c[:, h:h + 1] + lane_v[h:h + 1, :], -jnp.inf)
        mt = jnp.maximum(inter[h:h + 1, :], jnp.max(logd, axis=0, keepdims=True))
        sc = _dot(k_pair, q_m) * jnp.exp(logd - mt)
        vh = v_t[h * dv:(h + 1) * dv, :]
        state = c_st[h]
        cq = _dot(state.astype(BF16), q_m)
        w_int = jnp.exp(inter[h:h + 1, :] - mt)
        num = _dot(vh.astype(BF16), sc.astype(BF16)) + w_int * cq[:dv, :]
        den = jnp.sum(sc, axis=0, keepdims=True) + w_int * cq[dv:dv + 1, :]
        h_ref[0, h * dv:(h + 1) * dv, :] = (num / jnp.maximum(jnp.abs(den), jnp.exp(-mt))).astype(h_ref.dtype)
        ws_h = ws[h:h + 1, :]
        lhs = jnp.concatenate([vh * ws_h, jnp.broadcast_to(ws_h, (ML_STATE_ROWS - dv, CHUNK))], axis=0)
        own = (lane >= half * dk) & (lane < (half + 1) * dk)
        row = lax.broadcasted_iota(jnp.int32, (ML_STATE_ROWS, 1), 0)
        new = cw[h:h + 1, :] * state + _dot(lhs.astype(BF16), k_pair)
        c_st[h] = jnp.where(own & (row <= dv), new, 0.0)
    m_st[...] = jnp.broadcast_to(m_new, m_st.shape)


def _mlstm_kernel(qkf, vf, gf, qkb, vb, gb, gbias, hf_ref, hb_ref, cf, mf, cb, mb):
    @pl.when(pl.program_id(1) == 0)
    def _():
        for r in (cf, mf, cb, mb):
            r[...] = jnp.zeros_like(r)

    _mlstm_direction(qkf, vf, gf, gbias, cf, mf, hf_ref, 0)
    _mlstm_direction(qkb, vb, gb, gbias, cb, mb, hb_ref, 1)


def _mlstm_scan(qk, v, g, gate_b, nb, seg):
    t_tok = qk.shape[0]
    nchunk = seg // CHUNK
    fwd, bwd = _scan_maps(nchunk)
    widths = (qk.shape[1], v.shape[1], LANES)
    in_specs = ([pl.BlockSpec((CHUNK, w), fwd) for w in widths] + [pl.BlockSpec((CHUNK, w), bwd) for w in widths]
                + [pl.BlockSpec((1, LANES), lambda b, t: (0, 0))])
    state = [pltpu.VMEM((ML_HEADS, ML_STATE_ROWS, LANES), F32), pltpu.VMEM((ML_HEADS, LANES), F32)]
    out_map = lambda rows_map: (lambda b, t: (b, 0, rows_map(b, t)[0] - b * nchunk))
    return pl.pallas_call(
        _mlstm_kernel,
        grid=(nb, nchunk),
        in_specs=in_specs,
        out_specs=[pl.BlockSpec((1, v.shape[1], CHUNK), out_map(fwd)),
                   pl.BlockSpec((1, v.shape[1], CHUNK), out_map(bwd))],
        out_shape=[jax.ShapeDtypeStruct((nb, v.shape[1], seg), BF16)] * 2,
        scratch_shapes=state + state,
        compiler_params=_params("parallel", "arbitrary"),
        name="mlstm_scan",
    )(qk, v, g, qk, v, g, gate_b)


def _rms_rows(x, g_row):
    return x * lax.rsqrt(jnp.mean(x * x, axis=-1, keepdims=True) + EPS) * g_row


MLA_SCORE_SCALE = (MLA_NOPE + MLA_ROPE) ** -0.5 * LOG2_E


def _mla_q_kernel(cq_ref, g_ref, wt_ref, cos_ref, sin_ref, qt_ref):
    cq_t = jnp.transpose(_rms_rows(cq_ref[...], g_ref[...])).astype(BF16)
    q_t = _dot(wt_ref[...], cq_t) * MLA_SCORE_SCALE
    qt_ref[0] = q_t.astype(qt_ref.dtype)
    cos, sin = cos_ref[...], sin_ref[...]
    half = MLA_ROPE // 2
    for h in range(MLA_HEADS):
        r0 = h * LANES + MLA_NOPE
        x1, x2 = q_t[r0:r0 + half, :], q_t[r0 + half:r0 + MLA_ROPE, :]
        qt_ref[0, r0:r0 + half, :] = (x1 * cos - x2 * sin).astype(qt_ref.dtype)
        qt_ref[0, r0 + half:r0 + MLA_ROPE, :] = (x2 * cos + x1 * sin).astype(qt_ref.dtype)


def _mla_q(p3, gain, wq_t, tables_t, nt):
    t_tok = p3.shape[0]
    n = wq_t.shape[0]
    half = MLA_ROPE // 2
    return pl.pallas_call(
        _mla_q_kernel,
        grid=(t_tok // TM,),
        in_specs=[pl.BlockSpec((TM, MLA_RANK), lambda t: (t, 0)),
                  pl.BlockSpec((1, MLA_RANK), lambda t: (0, 0)),
                  pl.BlockSpec(wq_t.shape, lambda t: (0, 0)),
                  pl.BlockSpec((half, TM), lambda t: (0, t % nt)),
                  pl.BlockSpec((half, TM), lambda t: (0, t % nt))],
        out_specs=pl.BlockSpec((1, n, TM), lambda t: (t, 0, 0)),
        out_shape=jax.ShapeDtypeStruct((t_tok // TM, n, TM), BF16),
        compiler_params=_params("parallel"),
        name="mla_q",
    )(p3, gain.reshape(1, MLA_RANK), wq_t, *tables_t)


def _mla_kv_kernel(ckv_ref, kr_ref, g_ref, wk_ref, wvt_ref, cos_ref, sin_ref, k_ref, vt_ref):
    cn = _rms_rows(ckv_ref[...], g_ref[...])
    vt_ref[0] = _dot(wvt_ref[...], jnp.transpose(cn).astype(BF16)).astype(vt_ref.dtype)
    kn = _dot(cn.astype(BF16), wk_ref[...])
    kr = pltpu.roll(kr_ref[...], MLA_NOPE, axis=1)
    half = MLA_ROPE // 2
    lane = lax.broadcasted_iota(jnp.int32, (1, LANES), 1)
    partner = jnp.where(lane < MLA_NOPE + half, pltpu.roll(kr, LANES - half, axis=1), pltpu.roll(kr, half, axis=1))
    roped = kr * cos_ref[...] + partner * sin_ref[...]
    for h in range(MLA_HEADS):
        k_ref[:, h * LANES:(h + 1) * LANES] = (kn[:, h * LANES:(h + 1) * LANES] + roped).astype(k_ref.dtype)


def _mla_kv(p3, gain, w_k, wv_t, tables, nb, seg):
    t_tok = p3.shape[0]
    nt = seg // TM
    return pl.pallas_call(
        _mla_kv_kernel,
        grid=(nb, nt),
        in_specs=[pl.BlockSpec((TM, MLA_RANK), lambda b, j: (b * nt + j, 1)),
                  pl.BlockSpec((TM, LANES), lambda b, j: (b * nt + j, 2 * MLA_RANK // LANES)),
                  pl.BlockSpec((1, MLA_RANK), lambda b, j: (0, 0)),
                  pl.BlockSpec(w_k.shape, lambda b, j: (0, 0)),
                  pl.BlockSpec(wv_t.shape, lambda b, j: (0, 0)),
                  pl.BlockSpec((TM, LANES), lambda b, j: (j, 0)),
                  pl.BlockSpec((TM, LANES), lambda b, j: (j, 0))],
        out_specs=[pl.BlockSpec((TM, MLA_HEADS * LANES), lambda b, j: (b * nt + j, 0)),
                   pl.BlockSpec((1, MLA_HEADS * MLA_V, TM), lambda b, j: (b, 0, j))],
        out_shape=[jax.ShapeDtypeStruct((t_tok, MLA_HEADS * LANES), BF16),
                   jax.ShapeDtypeStruct((nb, MLA_HEADS * MLA_V, seg), BF16)],
        compiler_params=_params("parallel", "parallel"),
        name="mla_kv",
    )(p3, p3, gain.reshape(1, MLA_RANK), w_k, wv_t, *tables)


def _mla_attn_kernel(qt_ref, k_ref, vt_ref, o_ref):
    @pl.when(pl.program_id(2) == 0)
    def _():
        o_ref[...] = jnp.zeros_like(o_ref)

    @pl.when(pl.program_id(2) > 0)
    def _():
        heads = range(MLA_HEADS_PER_STEP)
        seg = k_ref.shape[0]
        tk = MLA_KEY_TILE if (seg - TM) % MLA_KEY_TILE == 0 else TM
        tiles = [(0, TM)] + [(a, a + tk) for a in range(TM, seg, tk)]

        def scores(hh, j):
            return _dot(k_ref[tiles[j][0]:tiles[j][1], hh * LANES:(hh + 1) * LANES],
                        qt_ref[0, hh * LANES:(hh + 1) * LANES, :])

        m = [jnp.full((1, TM), -jnp.inf, F32) for _ in heads]
        l = [jnp.zeros((1, TM), F32) for _ in heads]
        acc = [jnp.zeros((MLA_V, TM), F32) for _ in heads]
        s_next = [scores(hh, 0) for hh in heads]
        for j in range(len(tiles)):
            s_cur = s_next
            if j + 1 < len(tiles):
                s_next = [scores(hh, j + 1) for hh in heads]
            for hh in heads:
                m_new = jnp.maximum(m[hh], jnp.max(s_cur[hh], axis=0, keepdims=True))
                alpha = jnp.exp2(m[hh] - m_new)
                p = jnp.exp2(s_cur[hh] - m_new)
                l[hh] = alpha * l[hh] + jnp.sum(p, axis=0, keepdims=True)
                v_t = vt_ref[0, hh * MLA_V:(hh + 1) * MLA_V, tiles[j][0]:tiles[j][1]]
                acc[hh] = alpha * acc[hh] + _dot(v_t, p.astype(BF16))
                m[hh] = m_new
        o_t = jnp.concatenate([acc[hh] / l[hh] for hh in heads], axis=0)
        o_ref[...] = jnp.transpose(o_t).astype(o_ref.dtype)


def _mla_attention(q_t, k, v_t, nb, seg):
    t_tok = k.shape[0]
    nt = seg // TM
    hps = MLA_HEADS_PER_STEP
    return pl.pallas_call(
        _mla_attn_kernel,
        grid=(nb, MLA_HEADS // hps, nt),
        in_specs=[pl.BlockSpec((1, hps * LANES, TM), lambda b, hp, j: (b * nt + j, hp, 0)),
                  pl.BlockSpec((seg, hps * LANES), lambda b, hp, j: (b, hp)),
                  pl.BlockSpec((1, hps * MLA_V, seg), lambda b, hp, j: (b, hp, 0))],
        out_specs=pl.BlockSpec((TM, hps * MLA_V), lambda b, hp, j: (b * nt + j, hp)),
        out_shape=jax.ShapeDtypeStruct((t_tok, MLA_HEADS * MLA_V), BF16),
        compiler_params=_params("parallel", "parallel", "arbitrary"),
        name="mla_attention",
    )(q_t, k, v_t)


def _final_kernel(x_ref, g_ref, o_ref):
    o_ref[0] = _rms_rows(x_ref[...], g_ref[...])


def _final_norm(x, gain, nb, seg):
    d = x.shape[1]
    nt = seg // TM
    return pl.pallas_call(
        _final_kernel,
        grid=(nb, nt - 1),
        in_specs=[pl.BlockSpec((TM, d), lambda b, j: (b * nt + 1 + j, 0)),
                  pl.BlockSpec((1, d), lambda b, j: (0, 0))],
        out_specs=pl.BlockSpec((1, TM, d), lambda b, j: (b, j, 0)),
        out_shape=jax.ShapeDtypeStruct((nb, seg - TM, d), F32),
        compiler_params=_params("parallel", "parallel"),
        name="final_norm",
    )(x, gain.reshape(1, d))


def _rope_angles(n_lat, rot_dim):
    rows = n_lat // GRID_W
    row = jnp.repeat(jnp.arange(rows), GRID_W).astype(F32)
    col = jnp.tile(jnp.arange(GRID_W), rows).astype(F32)
    quarter = rot_dim // 4
    inv = ROPE_BASE ** (-jnp.arange(quarter, dtype=F32) / quarter)
    ang = jnp.concatenate([row[:, None] * inv, col[:, None] * inv], axis=-1)
    return jnp.cos(ang), jnp.sin(ang)


def _with_ctx_rows(tab, fill):
    return jnp.concatenate([jnp.full((TM, tab.shape[1]), fill, F32), tab], axis=0)


def _attn_rope_tables(n_lat):
    cos, sin = _rope_angles(n_lat, ATT_HEAD_DIM)
    cos_h = jnp.concatenate([cos, cos], axis=1)
    sin_h = jnp.concatenate([-sin, sin], axis=1)
    reps = LANES // ATT_HEAD_DIM
    return (_with_ctx_rows(jnp.tile(cos_h, (1, reps)), 1.0), _with_ctx_rows(jnp.tile(sin_h, (1, reps)), 0.0))


def _swap_halves(w, n_heads, dim):
    w3 = w.reshape(w.shape[0], n_heads, 2, dim // 2)
    return w3[:, :, ::-1, :].reshape(w.shape[0], n_heads * dim)


def _mla_k_tables(n_lat):
    cos, sin = _rope_angles(n_lat, MLA_ROPE)
    lo = jnp.zeros((n_lat, MLA_NOPE), F32)
    hi = jnp.zeros((n_lat, LANES - MLA_NOPE - MLA_ROPE), F32)
    cos_c = jnp.concatenate([lo, cos, cos, hi], axis=1)
    sin_c = jnp.concatenate([lo, -sin, sin, hi], axis=1)
    ctx_cos = jnp.concatenate([lo[:TM], jnp.ones((TM, MLA_ROPE), F32), hi[:TM]], axis=1)
    return (jnp.concatenate([ctx_cos, cos_c], axis=0), _with_ctx_rows(sin_c, 0.0))


def _pad_heads(w, real):
    r = w.shape[0]
    w3 = w.reshape(r, MLA_HEADS, real)
    return jnp.pad(w3, ((0, 0), (0, 0), (0, LANES - real))).reshape(r, MLA_HEADS * LANES)


def kernel(x, c, ctx, c_ctx, norm1_g, norm2_g, w_mod, b_mod, moe_w_group, moe_b_group, moe_w_expert, moe_b_expert, moe_w_gate, moe_w_up, moe_w_down, attn_w_in, attn_sink, attn_w_out, ssm_w_in, ssm_conv_w, ssm_conv_b, ssm_dt_bias, ssm_a_log, ssm_d, ssm_norm_g, ssm_w_out, mlstm_w_in, mlstm_gate_b, mlstm_norm_g, mlstm_w_out, mla_w_in, mla_q_norm_g, mla_w_q_up, mla_kv_norm_g, mla_w_kv_up, mla_w_out, final_norm_g):
    nb, n_lat, d = x.shape
    assert ctx.shape[1] == TM and d == D_MODEL and n_lat % TM == 0
    depth = w_mod.shape[0]
    seg = TM + n_lat
    nt = seg // TM
    t_tok = nb * seg

    xs = jnp.concatenate([ctx, x], axis=1).reshape(t_tok, d)

    rows = -(-(nb + 1) // 8) * 8
    cvec = jnp.concatenate([c, c_ctx[None, :], jnp.zeros((rows - nb - 1, d), F32)], axis=0)
    mods = _modulation(cvec, w_mod, b_mod).reshape(depth, rows, ADALN_CHUNKS, d)
    mods = jnp.pad(mods, ((0, 0), (0, 0), (0, MOD_ROWS - ADALN_CHUNKS), (0, 0)))

    w_router = jnp.concatenate([moe_w_expert, moe_w_group,
                                jnp.zeros((depth, d, LANES - MOE_EXPERTS - MOE_GROUPS), F32)], axis=-1)
    b_router = jnp.concatenate([moe_b_expert, moe_b_group,
                                jnp.zeros((depth, LANES - MOE_EXPERTS - MOE_GROUPS), F32)], axis=-1)
    w_gate, w_up = moe_w_gate.astype(BF16), moe_w_up.astype(BF16)
    w_down = moe_w_down.astype(BF16).reshape(depth, MOE_GROUPS, MOE_PER_GROUP * MOE_FF, d)

    for i in range(depth):
        kind = i % 4
        mod_i = mods[i]
        if kind == 0:
            nq, nk = ATT_HEADS * ATT_HEAD_DIM, ATT_KV_HEADS * ATT_HEAD_DIM
            w_in = attn_w_in[i // 4]
            w_all = jnp.concatenate([w_in, _swap_halves(w_in[:, :nq], ATT_HEADS, ATT_HEAD_DIM),
                                     _swap_halves(w_in[:, nq:nq + nk], ATT_KV_HEADS, ATT_HEAD_DIM)],
                                    axis=1).astype(BF16)
            n_in = w_in.shape[1]
            q, k, v = _normproj(xs, mod_i, norm1_g[i], w_all,
                                [(0, nq, n_in, ATT_HEAD_DIM ** -0.5 * LOG2_E), (nq, nk, n_in + nq), (nq + nk, nk, None)],
                                [BF16, BF16, BF16], nt, nb, tables=_attn_rope_tables(n_lat))
            o = _windowed_attention(q, k, v, attn_sink[i // 4], nb, seg)
            xs = _outproj([o], [nq], xs, mod_i, attn_w_out[i // 4].astype(BF16), _plain_prologue, nt, nb)
        elif kind == 1:
            j = i // 4
            w_in = jnp.pad(ssm_w_in[j], ((0, 0), (0, LANES - 2 * SSM_HEADS))).astype(BF16)
            z, xbc, dt = _normproj(xs, mod_i, norm1_g[i], w_in,
                                   [(0, SSM_D_INNER, None), (SSM_D_INNER, SSM_CONV_DIM, None),
                                    (SSM_D_INNER + SSM_CONV_DIM, LANES, None)],
                                   [BF16, BF16, F32], nt, nb)
            lane_pad = LANES - 2 * SSM_HEADS
            dt_bias = jnp.pad(ssm_dt_bias[j].reshape(1, -1), ((0, 0), (0, lane_pad)))
            a_neg = jnp.pad(-jnp.exp(ssm_a_log[j].astype(F32)).reshape(1, -1), ((0, 0), (0, lane_pad)))
            conv_w = jnp.pad(ssm_conv_w[j], ((0, 8 - SSM_CONV), (0, 0)))
            xc, bm, cm, dtv, a = _ssm_conv(xbc, dt, conv_w, ssm_conv_b[j].reshape(1, -1), dt_bias, a_neg, nt)
            yf, yb = _ssd_scan(xc, bm, cm, dtv, a, ssm_d[j].astype(F32), nb, seg)
            xs = _outproj([z], [SSM_D_INNER], xs, mod_i, ssm_w_out[j].astype(BF16), _ssm_prologue,
                          nt, nb, extra=(ssm_norm_g[j].reshape(1, -1),), ins_t=(yf, yb))
        elif kind == 2:
            j = i // 4
            nqk, nv = 2 * ML_HEADS * ML_QK_DIM, ML_HEADS * ML_V_DIM
            w_in = jnp.pad(mlstm_w_in[j], ((0, 0), (0, LANES - 4 * ML_HEADS))).astype(BF16)
            qk, v, o, g = _normproj(xs, mod_i, norm1_g[i], w_in,
                                    [(0, nqk, None), (nqk, nv, None), (nqk + nv, nv, None), (nqk + 2 * nv, LANES, None)],
                                    [BF16, BF16, BF16, F32], nt, nb)
            gate_b = jnp.pad(mlstm_gate_b[j].reshape(1, -1), ((0, 0), (0, LANES - 4 * ML_HEADS)))
            hf, hb = _mlstm_scan(qk, v, g, gate_b, nb, seg)
            xs = _outproj([o], [nv], xs, mod_i, mlstm_w_out[j].astype(BF16), _mlstm_prologue,
                          nt, nb, extra=(mlstm_norm_g[j].reshape(1, -1),), ins_t=(hf, hb))
        else:
            j = i // 4
            w_in = jnp.pad(mla_w_in[j], ((0, 0), (0, LANES - MLA_ROPE))).astype(BF16)
            (p3,) = _normproj(xs, mod_i, norm1_g[i], w_in, [(0, w_in.shape[1], None)], [F32], nt, nb)
            cos, sin = _rope_angles(n_lat, MLA_ROPE)
            tables_t = (jnp.transpose(_with_ctx_rows(cos, 1.0)), jnp.transpose(_with_ctx_rows(sin, 0.0)))
            wq_t = jnp.transpose(_pad_heads(mla_w_q_up[j], MLA_NOPE + MLA_ROPE)).astype(BF16)
            q_t = _mla_q(p3, mla_q_norm_g[j], wq_t, tables_t, nt)
            w_kv = mla_w_kv_up[j].reshape(MLA_RANK, MLA_HEADS, MLA_NOPE + MLA_V)
            w_k = _pad_heads(w_kv[:, :, :MLA_NOPE].reshape(MLA_RANK, MLA_HEADS * MLA_NOPE), MLA_NOPE).astype(BF16)
            wv_t = jnp.transpose(w_kv[:, :, MLA_NOPE:].reshape(MLA_RANK, MLA_HEADS * MLA_V)).astype(BF16)
            k, v_t = _mla_kv(p3, mla_kv_norm_g[j], w_k, wv_t, _mla_k_tables(n_lat), nb, seg)
            o = _mla_attention(q_t, k, v_t, nb, seg)
            xs = _outproj([o], [MLA_HEADS * MLA_V], xs, mod_i, mla_w_out[j].astype(BF16), _plain_prologue, nt, nb)

        xs = _moe(xs, mod_i, norm2_g[i], w_router[i], b_router[i].reshape(1, -1), w_gate, w_up, w_down, i, nt, nb)

    return _final_norm(xs, final_norm_g, nb, seg)
```

```python
import functools
import math

import jax
import jax.numpy as jnp
from jax import lax
from jax.experimental import pallas as pl
from jax.experimental.pallas import tpu as pltpu

F32 = jnp.float32
BF16 = jnp.bfloat16

D_MODEL = 1024
GRID_W = 64
EPS = 1e-6
ROPE_BASE = 10000.0
ADALN_CHUNKS = 6
CHUNK = 128
SSD_CHUNK = 256
ML_CHUNK = 128
TM = 256
MOD_ROWS = 8
LANES = 128
V7X_VMEM_LIMIT = 48 * 1024 * 1024

ATT_HEADS, ATT_KV_HEADS, ATT_HEAD_DIM, WINDOW = 16, 4, 64, 128
ATT_GROUP = ATT_HEADS // ATT_KV_HEADS
SSM_D_INNER, SSM_HEAD_DIM, SSM_HEADS, SSM_GROUPS, SSM_STATE, SSM_CONV = 2048, 64, 32, 4, 128, 5
SSM_HPG = SSM_HEADS // SSM_GROUPS
SSM_BC = SSM_GROUPS * SSM_STATE
SSM_CONV_DIM = SSM_D_INNER + 2 * SSM_BC
ML_HEADS, ML_QK_DIM, ML_V_DIM = 8, 64, 128
MLA_HEADS, MLA_RANK, MLA_NOPE, MLA_ROPE, MLA_V = 16, 256, 64, 32, 64
MOE_GROUPS, MOE_PER_GROUP, MOE_EXPERTS, MOE_FF = 4, 4, 16, 256
MOE_ROWS = 1024
MLA_KEY_TILE = 512
MLA_HEADS_PER_STEP = 4
MLA_SUM_ROWS = 16
LOG2_E = 1.4426950408889634


def _dot(a, b):
    return jnp.dot(a, b, preferred_element_type=F32)


def _dot_nt(a, b):
    return lax.dot_general(a, b, (((1,), (1,)), ((), ())), preferred_element_type=F32)


def _split_bf16(x):
    hi = x.astype(BF16)
    lo = (x - hi.astype(F32)).astype(BF16)
    return hi, lo


def _dot_split(a, b):
    a_hi, a_lo = _split_bf16(a)
    b_hi, b_lo = _split_bf16(b)
    return _dot(a_hi, b_hi) + _dot(a_lo, b_hi) + _dot(a_hi, b_lo)


def _sigmoid(x):
    return 1.0 / (1.0 + jnp.exp(-x))


def _silu(x):
    return x * _sigmoid(x)


def _softplus(x):
    return jnp.maximum(x, 0.0) + jnp.log1p(jnp.exp(-jnp.abs(x)))


def _params(*sem):
    return pltpu.CompilerParams(dimension_semantics=sem, vmem_limit_bytes=V7X_VMEM_LIMIT)


def _mod_row(t, nt, nb):
    return jnp.where(t % nt == 0, nb, t // nt)


def _mod_kernel(c_ref, w_ref, b_ref, o_ref):
    o_ref[0] = _dot_split(_silu(c_ref[...]), w_ref[0]) + b_ref[0]


def _modulation(cvec, w_mod, b_mod):
    depth, d, n = w_mod.shape
    tn = 1536
    rows = cvec.shape[0]
    return pl.pallas_call(
        _mod_kernel,
        grid=(depth, n // tn),
        in_specs=[pl.BlockSpec((rows, d), lambda l, j: (0, 0)),
                  pl.BlockSpec((1, d, tn), lambda l, j: (l, 0, j)),
                  pl.BlockSpec((1, 1, tn), lambda l, j: (l, 0, j))],
        out_specs=pl.BlockSpec((1, rows, tn), lambda l, j: (l, 0, j)),
        out_shape=jax.ShapeDtypeStruct((depth, rows, n), F32),
        compiler_params=_params("arbitrary", "arbitrary"),
        name="modulation",
    )(cvec, w_mod, b_mod.reshape(depth, 1, n))


def _normed(x, g_row, mod, sh_row, sc_row):
    y = x * lax.rsqrt(jnp.mean(x * x, axis=-1, keepdims=True) + EPS) * g_row
    return y * (1.0 + mod[sc_row:sc_row + 1, :]) + mod[sh_row:sh_row + 1, :]


def _proj_columns(hb, w_ref, o_ref, start, width, rot_start, cos, sin, scale=None, chunk=512):
    for c in range(0, width, chunk):
        cw = min(chunk, width - c)
        acc = _dot(hb, w_ref[:, start + c:start + c + cw])
        if rot_start is not None:
            rot = _dot(hb, w_ref[:, rot_start + c:rot_start + c + cw])
            reps = cw // LANES
            acc = acc * jnp.tile(cos, (1, reps)) + rot * jnp.tile(sin, (1, reps))
        if scale is not None:
            acc = acc * scale
        o_ref[:, c:c + cw] = acc.astype(o_ref.dtype)


def _stream_specs(x, nt):
    if isinstance(x, tuple):
        d = x[0].shape[-1]
        return [pl.BlockSpec((1, TM, d), lambda t: (t // nt, jnp.maximum(t % nt - 1, 0), 0)),
                pl.BlockSpec((1, TM, d), lambda t: (t // nt, 0, 0))], list(x)
    return [pl.BlockSpec((TM, x.shape[1]), lambda t: (t, 0))], [x]


def _stream_tile(x_refs, nt):
    if len(x_refs) == 1:
        return x_refs[0][...]
    return jnp.where(pl.program_id(0) % nt == 0, x_refs[1][0], x_refs[0][0])


def _normproj_kernel(*refs, outs, has_rope, n_x, nt):
    x_refs, refs = refs[:n_x], refs[n_x:]
    mod_ref, g_ref, w_ref = refs[:3]
    k = 3
    cos = sin = None
    if has_rope:
        cos, sin = refs[3][...], refs[4][...]
        k = 5
    hb = _normed(_stream_tile(x_refs, nt), g_ref[...], mod_ref[0], 0, 1).astype(BF16)
    for o_ref, spec in zip(refs[k:], outs):
        _proj_columns(hb, w_ref, o_ref, *spec[:3], cos, sin, scale=spec[3] if len(spec) > 3 else None)


def _normproj(x, mods, gain, w, outs, out_dtypes, nt, nb, tables=None):
    d, n = w.shape
    t_tok = nb * nt * TM
    x_specs, x_args = _stream_specs(x, nt)
    in_specs = x_specs + [pl.BlockSpec((1, MOD_ROWS, d), lambda t: (_mod_row(t, nt, nb), 0, 0)),
                          pl.BlockSpec((1, d), lambda t: (0, 0)),
                          pl.BlockSpec((d, n), lambda t: (0, 0))]
    args = x_args + [mods, gain.reshape(1, d), w]
    if tables is not None:
        in_specs += [pl.BlockSpec((TM, LANES), lambda t: (t % nt, 0))] * 2
        args += list(tables)
    return pl.pallas_call(
        functools.partial(_normproj_kernel, outs=tuple(outs), has_rope=tables is not None, n_x=len(x_args), nt=nt),
        grid=(t_tok // TM,),
        in_specs=in_specs,
        out_specs=[pl.BlockSpec((TM, o[1]), lambda t: (t, 0)) for o in outs],
        out_shape=[jax.ShapeDtypeStruct((t_tok, o[1]), dt) for o, dt in zip(outs, out_dtypes)],
        compiler_params=_params("parallel"),
        name="normproj",
    )(*args)


def _outproj_kernel(*refs, prologue, n_in, n_x, nt):
    ins = refs[:n_in]
    x_refs = refs[n_in:n_in + n_x]
    mod_ref, w_ref, o_ref = refs[n_in + n_x:]
    a = prologue(*ins)
    y = _dot(a, w_ref[...])
    o_ref[...] = _stream_tile(x_refs, nt) + mod_ref[0][2:3, :] * y


def _outproj(ins, in_widths, x, mods, w, prologue, nt, nb, extra=(), ins_t=()):
    d = w.shape[1]
    t_tok = nb * nt * TM
    in_specs = [pl.BlockSpec((1, a.shape[1], TM), lambda t: (t // nt, 0, t % nt)) for a in ins_t]
    in_specs += [pl.BlockSpec((TM, wd), lambda t: (t, 0)) for wd in in_widths]
    in_specs += [pl.BlockSpec(e.shape, lambda t: (0, 0)) for e in extra]
    n_in = len(in_specs)
    x_specs, x_args = _stream_specs(x, nt)
    in_specs += x_specs + [pl.BlockSpec((1, MOD_ROWS, d), lambda t: (_mod_row(t, nt, nb), 0, 0)),
                           pl.BlockSpec(w.shape, lambda t: (0, 0))]
    return pl.pallas_call(
        functools.partial(_outproj_kernel, prologue=prologue, n_in=n_in, n_x=len(x_args), nt=nt),
        grid=(t_tok // TM,),
        in_specs=in_specs,
        out_specs=pl.BlockSpec((TM, d), lambda t: (t, 0)),
        out_shape=jax.ShapeDtypeStruct((t_tok, d), F32),
        input_output_aliases={} if isinstance(x, tuple) else {n_in: 0},
        compiler_params=_params("parallel"),
        name="outproj",
    )(*ins_t, *ins, *extra, *x_args, mods, w)


def _group_rms(y, n_groups):
    width = y.shape[1] // n_groups
    parts = []
    for g in range(n_groups):
        yg = y[:, g * width:(g + 1) * width]
        parts.append(yg * lax.rsqrt(jnp.mean(yg * yg, axis=-1, keepdims=True) + EPS))
    return jnp.concatenate(parts, axis=1)


def _plain_prologue(o_ref):
    return o_ref[...]


def _ssm_prologue(yf_ref, yb_ref, z_ref, g_ref):
    y = jnp.transpose(yf_ref[0].astype(F32) + yb_ref[0].astype(F32)) * _silu(z_ref[...].astype(F32))
    return (_group_rms(y, SSM_GROUPS) * g_ref[...]).astype(BF16)


def _mlstm_prologue(hf_ref, hb_ref, o_ref, g_ref):
    h = jnp.transpose(hf_ref[0].astype(F32) + hb_ref[0].astype(F32))
    return (_group_rms(h, ML_HEADS) * g_ref[...] * _sigmoid(o_ref[...].astype(F32))).astype(BF16)


def _route(logits):
    lane = lax.broadcasted_iota(jnp.int32, logits.shape, 1).astype(F32)
    neg = -jnp.inf
    lg = jnp.where((lane >= MOE_EXPERTS) & (lane < MOE_EXPERTS + MOE_GROUPS), logits, neg)
    gmax = jnp.max(lg, axis=-1, keepdims=True)
    g_sel = jnp.min(jnp.where(lg == gmax, lane, LANES), axis=-1, keepdims=True) - MOE_EXPERTS
    p_g = 1.0 / jnp.sum(jnp.exp(lg - gmax), axis=-1, keepdims=True)
    in_group = (lane >= g_sel * MOE_PER_GROUP) & (lane < (g_sel + 1) * MOE_PER_GROUP)
    le = jnp.where(in_group, logits, neg)
    v1 = jnp.max(le, axis=-1, keepdims=True)
    i1 = jnp.min(jnp.where(le == v1, lane, LANES), axis=-1, keepdims=True)
    le2 = jnp.where(lane == i1, neg, le)
    v2 = jnp.max(le2, axis=-1, keepdims=True)
    i2 = jnp.min(jnp.where(le2 == v2, lane, LANES), axis=-1, keepdims=True)
    e2 = jnp.exp(v2 - v1)
    w1 = p_g / (1.0 + e2)
    return jnp.where(lane == i1, w1, 0.0) + jnp.where(lane == i2, w1 * e2, 0.0)


def _moe_kernel(x_ref, mods_ref, g_ref, wr_ref, br_ref, wg_ref, wu_ref, wd_ref, o_ref, h_scr, comb_scr,
                *, nt, nb, sub):
    i = pl.program_id(0)
    e = pl.program_id(1)

    @pl.when(e == 0)
    def _():
        w_hi, w_lo = _split_bf16(wr_ref[...])
        w_both = jnp.concatenate([w_hi, w_lo], axis=1)
        for s in range(sub):
            rows = slice(s * TM, (s + 1) * TM)
            mod = mods_ref[_mod_row(i * sub + s, nt, nb)]
            x = x_ref[rows, :]
            h = _normed(x, g_ref[...], mod, 3, 4)
            h_hi, h_lo = _split_bf16(h)
            h_scr[rows, :] = h_hi
            both = _dot(h_hi, w_both)
            logits = both[:, :LANES] + both[:, LANES:] + _dot(h_lo, w_hi) + br_ref[...]
            comb_scr[rows, :] = _route(logits)
            o_ref[rows, :] = x

    hb = h_scr[...]
    lane = lax.broadcasted_iota(jnp.int32, comb_scr.shape, 1)
    comb = comb_scr[...]
    scaled = []
    for k in range(MOE_PER_GROUP):
        act = _silu(_dot(hb, wg_ref[0, k])) * _dot(hb, wu_ref[0, k])
        cw = jnp.sum(jnp.where(lane == e * MOE_PER_GROUP + k, comb, 0.0), axis=-1, keepdims=True)
        scaled.append((act * cw).astype(BF16))
    y = _dot(jnp.concatenate(scaled, axis=1), wd_ref[0, 0])
    for s in range(sub):
        rows = slice(s * TM, (s + 1) * TM)
        gate = mods_ref[_mod_row(i * sub + s, nt, nb)][5:6, :]
        o_ref[rows, :] += gate * y[rows, :]


def _moe(x, mods, gain, w_router, b_router, w_gate, w_up, w_down, layer, nt, nb):
    t_tok, d = x.shape
    pg = MOE_PER_GROUP
    rows = MOE_ROWS if t_tok % MOE_ROWS == 0 else TM
    sub = rows // TM
    return pl.pallas_call(
        functools.partial(_moe_kernel, nt=nt, nb=nb, sub=sub),
        grid=(t_tok // rows, MOE_GROUPS),
        in_specs=[pl.BlockSpec((rows, d), lambda i, e: (i, 0)),
                  pl.BlockSpec(mods.shape, lambda i, e: (0, 0, 0)),
                  pl.BlockSpec((1, d), lambda i, e: (0, 0)),
                  pl.BlockSpec((d, LANES), lambda i, e: (0, 0)),
                  pl.BlockSpec((1, LANES), lambda i, e: (0, 0)),
                  pl.BlockSpec((1, pg, d, MOE_FF), lambda i, e: (layer, e, 0, 0)),
                  pl.BlockSpec((1, pg, d, MOE_FF), lambda i, e: (layer, e, 0, 0)),
                  pl.BlockSpec((1, 1, pg * MOE_FF, d), lambda i, e: (layer, e, 0, 0))],
        out_specs=pl.BlockSpec((rows, d), lambda i, e: (i, 0)),
        out_shape=jax.ShapeDtypeStruct((t_tok, d), F32),
        scratch_shapes=[pltpu.VMEM((rows, d), BF16), pltpu.VMEM((rows, LANES), F32)],
        input_output_aliases={0: 0},
        compiler_params=_params("parallel", "arbitrary"),
        name="moe",
    )(x, mods, gain.reshape(1, d), w_router, b_router, w_gate, w_up, w_down)


def _attn_kernel(sink_ref, q_ref, kp_ref, kc_ref, kn_ref, vp_ref, vc_ref, vn_ref, kx_ref, vx_ref, o_ref, *, n_lat):
    j = pl.program_id(1)
    jl = j - 1
    dh = ATT_HEAD_DIM
    n_loc = TM + 2 * CHUNK
    k_all = jnp.concatenate([kp_ref[...], kc_ref[...], kn_ref[...], kx_ref[...]], axis=0)
    v_t = jnp.transpose(jnp.concatenate([vp_ref[...], vc_ref[...], vn_ref[...], vx_ref[...]],
                                        axis=0).astype(F32)).astype(BF16)
    q_t = jnp.transpose(q_ref[...].astype(F32)).astype(BF16)
    n_keys = k_all.shape[0]
    kj = lax.broadcasted_iota(jnp.int32, (n_keys, TM), 0)
    qi = lax.broadcasted_iota(jnp.int32, (n_keys, TM), 1)
    kpos = jl * TM - CHUNK + kj
    local_ok = (kj >= qi + CHUNK - WINDOW) & (kj <= qi + CHUNK + WINDOW) & (kpos >= 0) & (kpos < n_lat) & (jl >= 0)
    bias = jnp.where((kj >= n_loc) | local_ok, 0.0, -jnp.inf)
    bias = jnp.concatenate([bias] * ATT_GROUP, axis=1)
    lane_head = lax.broadcasted_iota(jnp.int32, (1, ATT_GROUP * TM), 1) // TM
    zeros_q = jnp.zeros((dh, ATT_GROUP * TM), BF16)
    outs = []
    for g in range(ATT_KV_HEADS):
        half = g % 2
        k_pair = k_all[:, (g - half) * dh:(g - half + 2) * dh]
        qg = jnp.concatenate([q_t[(g * ATT_GROUP + a) * dh:(g * ATT_GROUP + a + 1) * dh, :]
                              for a in range(ATT_GROUP)], axis=1)
        q_m = jnp.concatenate([zeros_q, qg] if half else [qg, zeros_q], axis=0)
        s = _dot(k_pair, q_m) + bias
        sink = jnp.zeros((1, ATT_GROUP * TM), F32)
        for a in range(ATT_GROUP):
            sink = jnp.where(lane_head == a, sink_ref[g * ATT_GROUP + a] * LOG2_E, sink)
        m = jnp.maximum(jnp.max(s, axis=0, keepdims=True), sink)
        p = jnp.exp2(s - m)
        den = jnp.sum(p, axis=0, keepdims=True) + jnp.exp2(sink - m)
        og = _dot(v_t[g * dh:(g + 1) * dh, :], p.astype(BF16)) / den
        outs += [og[:, a * TM:(a + 1) * TM] for a in range(ATT_GROUP)]
    o_ref[...] = jnp.transpose(jnp.concatenate(outs, axis=0)).astype(o_ref.dtype)


def _windowed_attention(q, k, v, sink, nb, seg):
    t_tok = q.shape[0]
    nchunk = seg // CHUNK
    nt = seg // TM
    per_tile = TM // CHUNK
    kvw = ATT_KV_HEADS * ATT_HEAD_DIM

    def halo(off):
        return lambda b, j, *_: (b * nchunk + jnp.clip(j * per_tile + off, per_tile, nchunk - 1), 0)

    tile = lambda b, j, *_: (b * nt + j, 0)
    kv_spec = [pl.BlockSpec((CHUNK, kvw), halo(-1)), pl.BlockSpec((TM, kvw), tile),
               pl.BlockSpec((CHUNK, kvw), halo(per_tile))]
    ctx_spec = pl.BlockSpec((TM, kvw), lambda b, j, *_: (b * nt, 0))
    return pl.pallas_call(
        functools.partial(_attn_kernel, n_lat=seg - TM),
        grid_spec=pltpu.PrefetchScalarGridSpec(
            num_scalar_prefetch=1,
            grid=(nb, nt),
            in_specs=[pl.BlockSpec((TM, q.shape[1]), tile)] + kv_spec + kv_spec + [ctx_spec, ctx_spec],
            out_specs=pl.BlockSpec((TM, q.shape[1]), tile),
        ),
        out_shape=jax.ShapeDtypeStruct((t_tok, q.shape[1]), BF16),
        compiler_params=_params("parallel", "parallel"),
        name="windowed_attention",
    )(sink, q, k, k, k, v, v, v, k, v)


def _conv_kernel(cur_ref, prev_ref, next_ref, dt_ref, w_ref, b_ref, dtb_ref, aneg_ref,
                 xs_ref, bm_ref, cm_ref, dtv_ref, a_ref, ext, *, nt):
    t = pl.program_id(0)
    tl = t % nt
    halo = 8
    first = (tl == 0) | (tl == 1)
    last = (tl == 0) | (tl == nt - 1)
    ext[0:halo, :] = jnp.where(first, 0.0, prev_ref[...].astype(F32)[halo:2 * halo, :])
    ext[halo:halo + TM, :] = cur_ref[...].astype(F32)
    ext[halo + TM:2 * halo + TM, :] = jnp.where(last, 0.0, next_ref[...].astype(F32)[0:halo, :])
    pad = SSM_CONV // 2
    chunk = 512
    for c in range(0, SSM_CONV_DIM, chunk):
        acc = jnp.broadcast_to(b_ref[:, c:c + chunk], (TM, chunk))
        for k in range(SSM_CONV):
            acc = acc + w_ref[k:k + 1, c:c + chunk] * ext[halo - pad + k:halo - pad + k + TM, c:c + chunk]
        y = _silu(acc)
        if c < SSM_D_INNER:
            xs_ref[0, c:c + chunk, :] = jnp.transpose(y).astype(xs_ref.dtype)
        elif c < SSM_D_INNER + SSM_BC:
            bm_ref[:, c - SSM_D_INNER:c - SSM_D_INNER + chunk] = y.astype(bm_ref.dtype)
        else:
            off = c - SSM_D_INNER - SSM_BC
            cm_ref[0, off:off + chunk, :] = jnp.transpose(y).astype(cm_ref.dtype)
    dtv = _softplus(dt_ref[...] + dtb_ref[...])
    dtv_ref[0] = jnp.transpose(dtv)
    a_ref[0] = jnp.transpose(dtv * aneg_ref[...] * LOG2_E)


def _ssm_conv(xbc, dt, conv_w, conv_b, dt_bias, a_neg, nt):
    t_tok = xbc.shape[0]
    n16 = t_tok // 16
    seg = nt * TM
    nb = t_tok // seg
    row = lambda t: (t, 0)
    col = lambda t: (t // nt, 0, t % nt)
    fixed = lambda t: (0, 0)
    return pl.pallas_call(
        functools.partial(_conv_kernel, nt=nt),
        grid=(t_tok // TM,),
        in_specs=[pl.BlockSpec((TM, SSM_CONV_DIM), row),
                  pl.BlockSpec((16, SSM_CONV_DIM), lambda t: (jnp.maximum(t * (TM // 16) - 1, 0), 0)),
                  pl.BlockSpec((16, SSM_CONV_DIM), lambda t: (jnp.minimum((t + 1) * (TM // 16), n16 - 1), 0)),
                  pl.BlockSpec((TM, LANES), row),
                  pl.BlockSpec((8, SSM_CONV_DIM), fixed),
                  pl.BlockSpec((1, SSM_CONV_DIM), fixed),
                  pl.BlockSpec((1, LANES), fixed),
                  pl.BlockSpec((1, LANES), fixed)],
        out_specs=[pl.BlockSpec((1, SSM_D_INNER, TM), col), pl.BlockSpec((TM, SSM_BC), row),
                   pl.BlockSpec((1, SSM_BC, TM), col), pl.BlockSpec((1, LANES, TM), col),
                   pl.BlockSpec((1, LANES, TM), col)],
        out_shape=[jax.ShapeDtypeStruct((nb, SSM_D_INNER, seg), BF16), jax.ShapeDtypeStruct((t_tok, SSM_BC), BF16),
                   jax.ShapeDtypeStruct((nb, SSM_BC, seg), BF16), jax.ShapeDtypeStruct((nb, LANES, seg), F32),
                   jax.ShapeDtypeStruct((nb, LANES, seg), F32)],
        scratch_shapes=[pltpu.VMEM((TM + 16, SSM_CONV_DIM), F32)],
        compiler_params=_params("parallel"),
        name="ssm_conv",
    )(xbc, xbc, xbc, dt, conv_w, conv_b, dt_bias, a_neg)


def _tri(lower, n):
    r = lax.broadcasted_iota(jnp.int32, (n, n), 0)
    c = lax.broadcasted_iota(jnp.int32, (n, n), 1)
    return (c <= r) if lower else (c >= r)


def _ssd_direction(xs_ref, bm_ref, ct_ref, dt_ref, a_ref, dsk_ref, st_ref, y_ref, d):
    a_t = a_ref[0]
    dt_t = dt_ref[0]
    upper = jnp.where(_tri(False, SSD_CHUNK), 1.0, 0.0).astype(BF16)
    a_hi, a_lo = _split_bf16(a_t)
    cum = _dot(a_hi, upper) + _dot(a_lo, upper)
    cum_end = cum[:, SSD_CHUNK - 1:SSD_CHUNK]
    if d == 0:
        lane_v, sub_v = cum, -cum
        inter = jnp.exp2(cum)
        w_upd = jnp.exp2(cum_end - cum) * dt_t
    else:
        ecum = cum - a_t
        lane_v, sub_v = -ecum, ecum
        inter = jnp.exp2(cum_end - ecum)
        w_upd = jnp.exp2(ecum) * dt_t
    sub_c = jnp.transpose(sub_v)
    decay_end = jnp.exp2(cum_end)
    mask = _tri(d == 1, SSD_CHUNK)
    p = SSM_HEAD_DIM
    for g in range(SSM_GROUPS):
        bg = bm_ref[:, g * SSM_STATE:(g + 1) * SSM_STATE]
        cg_t = ct_ref[0, g * SSM_STATE:(g + 1) * SSM_STATE, :]
        cb_t = _dot(bg, cg_t)
        state = st_ref[g]
        y_in = _dot(state.astype(BF16), cg_t)
        upd = []
        for e in range(SSM_HPG):
            h = g * SSM_HPG + e
            ln = d * SSM_HEADS + h
            rows = slice(e * p, (e + 1) * p)
            seg = jnp.where(mask, sub_c[:, ln:ln + 1] + lane_v[ln:ln + 1, :], -jnp.inf)
            m_t = (cb_t * jnp.exp2(seg)).astype(BF16)
            xf = xs_ref[0, h * p:(h + 1) * p, :].astype(F32)
            u = (xf * dt_t[ln:ln + 1, :]).astype(BF16)
            yh = _dot(u, m_t) + y_in[rows, :] * inter[ln:ln + 1, :] + dsk_ref[d, h] * xf
            y_ref[0, h * p:(h + 1) * p, :] = yh.astype(y_ref.dtype)
            upd.append((xf * w_upd[ln:ln + 1, :]).astype(BF16))
        new = _dot(jnp.concatenate(upd, axis=0), bg)
        for e in range(SSM_HPG):
            ln = d * SSM_HEADS + g * SSM_HPG + e
            rows = slice(e * p, (e + 1) * p)
            st_ref[g, rows, :] = state[rows, :] * decay_end[ln:ln + 1, :] + new[rows, :]


def _ssd_kernel(dsk_ref, xsf, bmf, ctf, dtf, af, xsb, bmb, ctb, dtb, ab, yf_ref, yb_ref, stf, stb):
    @pl.when(pl.program_id(1) == 0)
    def _():
        stf[...] = jnp.zeros_like(stf)
        stb[...] = jnp.zeros_like(stb)

    _ssd_direction(xsf, bmf, ctf, dtf, af, dsk_ref, stf, yf_ref, 0)
    _ssd_direction(xsb, bmb, ctb, dtb, ab, dsk_ref, stb, yb_ref, 1)


def _scan_maps(nchunk, chunk):
    ctx_chunks = TM // chunk
    fwd = lambda b, t: (b * nchunk + t, 0)
    bwd = lambda b, t: (b * nchunk + jnp.where(t < ctx_chunks, ctx_chunks - 1 - t, nchunk + ctx_chunks - 1 - t), 0)
    return fwd, bwd


def _ssd_scan(xs_t, bm, c_t, dt_t, a_t, d_skip, nb, seg):
    nchunk = seg // SSD_CHUNK
    fwd_rows, bwd_rows = _scan_maps(nchunk, SSD_CHUNK)

    def specs(rows_map):
        cols_map = lambda b, t, *_: (b, 0, rows_map(b, t)[0] - b * nchunk)
        return [pl.BlockSpec((1, SSM_D_INNER, SSD_CHUNK), cols_map),
                pl.BlockSpec((SSD_CHUNK, SSM_BC), lambda b, t, *_: rows_map(b, t)),
                pl.BlockSpec((1, SSM_BC, SSD_CHUNK), cols_map),
                pl.BlockSpec((1, LANES, SSD_CHUNK), cols_map),
                pl.BlockSpec((1, LANES, SSD_CHUNK), cols_map)]

    arrs = (xs_t, bm, c_t, dt_t, a_t)
    return pl.pallas_call(
        _ssd_kernel,
        grid_spec=pltpu.PrefetchScalarGridSpec(
            num_scalar_prefetch=1,
            grid=(nb, nchunk),
            in_specs=specs(fwd_rows) + specs(bwd_rows),
            out_specs=[specs(fwd_rows)[0], specs(bwd_rows)[0]],
            scratch_shapes=[pltpu.VMEM((SSM_GROUPS, SSM_HPG * SSM_HEAD_DIM, SSM_STATE), F32)] * 2,
        ),
        out_shape=[jax.ShapeDtypeStruct(xs_t.shape, BF16)] * 2,
        compiler_params=_params("parallel", "arbitrary"),
        name="ssd_scan",
    )(d_skip, *arrs, *arrs)


def _log_sigmoid(x):
    return jnp.minimum(x, 0.0) - jnp.log1p(jnp.exp(-jnp.abs(x)))


ML_STATE_ROWS = ML_V_DIM + 16


def _mlstm_direction(qk_ref, v_ref, g_ref, gb_ref, c_st, m_st, h_ref, d):
    nh, dk, dv = ML_HEADS, ML_QK_DIM, ML_V_DIM
    g_t = jnp.transpose(g_ref[...] + gb_ref[...])
    ig = g_t[16 * d:16 * d + nh, :]
    lf = _log_sigmoid(g_t[16 * d + nh:16 * d + 2 * nh, :])
    upper = jnp.where(_tri(False, ML_CHUNK), 1.0, 0.0).astype(BF16)
    lf_hi, lf_lo = _split_bf16(lf)
    fc = _dot(lf_hi, upper) + _dot(lf_lo, upper)
    tot = fc[:, ML_CHUNK - 1:ML_CHUNK]
    m_prev = m_st[:, 0:1]
    if d == 0:
        lane_v, sub_v = fc, ig - fc
        inter = fc + m_prev
        logw = tot - fc + ig
    else:
        ec = fc - lf
        lane_v, sub_v = -ec, ec + ig
        inter = tot - ec + m_prev
        logw = ec + ig
    sub_c = jnp.transpose(jnp.concatenate([sub_v, jnp.zeros((LANES - nh, ML_CHUNK), F32)], axis=0))
    m_new = jnp.maximum(tot + m_prev, jnp.max(logw, axis=-1, keepdims=True))
    ws = jnp.exp(logw - m_new)
    cw = jnp.exp(tot + m_prev - m_new)
    mask = _tri(d == 1, ML_CHUNK)
    q_t = jnp.transpose(qk_ref[:, :nh * dk].astype(F32))
    v_t = jnp.transpose(v_ref[...].astype(F32))
    lane = lax.broadcasted_iota(jnp.int32, (1, LANES), 1)
    zeros_q = jnp.zeros((dk, ML_CHUNK), F32)
    for h in range(nh):
        half = h % 2
        k_pair = qk_ref[:, nh * dk + (h - half) * dk:nh * dk + (h - half + 2) * dk] * (dk ** -0.5)
        qh = q_t[h * dk:(h + 1) * dk, :]
        q_m = jnp.concatenate([zeros_q, qh] if half else [qh, zeros_q], axis=0).astype(BF16)
        logd = jnp.where(mask, sub_c[:, h:h + 1] + lane_v[h:h + 1, :], -jnp.inf)
        mt = jnp.maximum(inter[h:h + 1, :], jnp.max(logd, axis=0, keepdims=True))
        sc = _dot(k_pair, q_m) * jnp.exp(logd - mt)
        vh = v_t[h * dv:(h + 1) * dv, :]
        state = c_st[h]
        cq = _dot(state.astype(BF16), q_m)
        w_int = jnp.exp(inter[h:h + 1, :] - mt)
        num = _dot(vh.astype(BF16), sc.astype(BF16)) + w_int * cq[:dv, :]
        den = jnp.sum(sc, axis=0, keepdims=True) + w_int * cq[dv:dv + 1, :]
        h_ref[0, h * dv:(h + 1) * dv, :] = (num / jnp.maximum(jnp.abs(den), jnp.exp(-mt))).astype(h_ref.dtype)
        ws_h = ws[h:h + 1, :]
        lhs = jnp.concatenate([vh * ws_h, jnp.broadcast_to(ws_h, (ML_STATE_ROWS - dv, ML_CHUNK))], axis=0)
        own = (lane >= half * dk) & (lane < (half + 1) * dk)
        row = lax.broadcasted_iota(jnp.int32, (ML_STATE_ROWS, 1), 0)
        new = cw[h:h + 1, :] * state + _dot(lhs.astype(BF16), k_pair)
        c_st[h] = jnp.where(own & (row <= dv), new, 0.0)
    m_st[...] = jnp.broadcast_to(m_new, m_st.shape)


def _mlstm_kernel(qkf, vf, gf, qkb, vb, gb, gbias, hf_ref, hb_ref, cf, mf, cb, mb):
    @pl.when(pl.program_id(1) == 0)
    def _():
        for r in (cf, mf, cb, mb):
            r[...] = jnp.zeros_like(r)

    _mlstm_direction(qkf, vf, gf, gbias, cf, mf, hf_ref, 0)
    _mlstm_direction(qkb, vb, gb, gbias, cb, mb, hb_ref, 1)


def _mlstm_scan(qk, v, g, gate_b, nb, seg):
    t_tok = qk.shape[0]
    nchunk = seg // ML_CHUNK
    fwd, bwd = _scan_maps(nchunk, ML_CHUNK)
    widths = (qk.shape[1], v.shape[1], LANES)
    in_specs = ([pl.BlockSpec((ML_CHUNK, w), fwd) for w in widths] + [pl.BlockSpec((ML_CHUNK, w), bwd) for w in widths]
                + [pl.BlockSpec((1, LANES), lambda b, t: (0, 0))])
    state = [pltpu.VMEM((ML_HEADS, ML_STATE_ROWS, LANES), F32), pltpu.VMEM((ML_HEADS, LANES), F32)]
    out_map = lambda rows_map: (lambda b, t: (b, 0, rows_map(b, t)[0] - b * nchunk))
    return pl.pallas_call(
        _mlstm_kernel,
        grid=(nb, nchunk),
        in_specs=in_specs,
        out_specs=[pl.BlockSpec((1, v.shape[1], ML_CHUNK), out_map(fwd)),
                   pl.BlockSpec((1, v.shape[1], ML_CHUNK), out_map(bwd))],
        out_shape=[jax.ShapeDtypeStruct((nb, v.shape[1], seg), BF16)] * 2,
        scratch_shapes=state + state,
        compiler_params=_params("parallel", "arbitrary"),
        name="mlstm_scan",
    )(qk, v, g, qk, v, g, gate_b)


def _rms_rows(x, g_row):
    return x * lax.rsqrt(jnp.mean(x * x, axis=-1, keepdims=True) + EPS) * g_row


MLA_SCORE_SCALE = (MLA_NOPE + MLA_ROPE) ** -0.5 * LOG2_E


def _mla_q_kernel(cq_ref, g_ref, wt_ref, cos_ref, sin_ref, qt_ref):
    cq_t = jnp.transpose(_rms_rows(cq_ref[...], g_ref[...])).astype(BF16)
    q_t = _dot(wt_ref[...], cq_t) * MLA_SCORE_SCALE
    qt_ref[0] = q_t.astype(qt_ref.dtype)
    cos, sin = cos_ref[...], sin_ref[...]
    half = MLA_ROPE // 2
    for h in range(MLA_HEADS):
        r0 = h * LANES + MLA_NOPE
        x1, x2 = q_t[r0:r0 + half, :], q_t[r0 + half:r0 + MLA_ROPE, :]
        qt_ref[0, r0:r0 + half, :] = (x1 * cos - x2 * sin).astype(qt_ref.dtype)
        qt_ref[0, r0 + half:r0 + MLA_ROPE, :] = (x2 * cos + x1 * sin).astype(qt_ref.dtype)


def _mla_q(p3, gain, wq_t, tables_t, nt):
    t_tok = p3.shape[0]
    n = wq_t.shape[0]
    half = MLA_ROPE // 2
    return pl.pallas_call(
        _mla_q_kernel,
        grid=(t_tok // TM,),
        in_specs=[pl.BlockSpec((TM, MLA_RANK), lambda t: (t, 0)),
                  pl.BlockSpec((1, MLA_RANK), lambda t: (0, 0)),
                  pl.BlockSpec(wq_t.shape, lambda t: (0, 0)),
                  pl.BlockSpec((half, TM), lambda t: (0, t % nt)),
                  pl.BlockSpec((half, TM), lambda t: (0, t % nt))],
        out_specs=pl.BlockSpec((1, n, TM), lambda t: (t, 0, 0)),
        out_shape=jax.ShapeDtypeStruct((t_tok // TM, n, TM), BF16),
        compiler_params=_params("parallel"),
        name="mla_q",
    )(p3, gain.reshape(1, MLA_RANK), wq_t, *tables_t)


def _mla_kv_kernel(ckv_ref, kr_ref, g_ref, wk_ref, wvt_ref, cos_ref, sin_ref, k_ref, vt_ref):
    cn = _rms_rows(ckv_ref[...], g_ref[...])
    vt_ref[0] = _dot(wvt_ref[...], jnp.transpose(cn).astype(BF16)).astype(vt_ref.dtype)
    kn = _dot(cn.astype(BF16), wk_ref[...])
    kr = pltpu.roll(kr_ref[...], MLA_NOPE, axis=1)
    half = MLA_ROPE // 2
    lane = lax.broadcasted_iota(jnp.int32, (1, LANES), 1)
    partner = jnp.where(lane < MLA_NOPE + half, pltpu.roll(kr, LANES - half, axis=1), pltpu.roll(kr, half, axis=1))
    roped = kr * cos_ref[...] + partner * sin_ref[...]
    for h in range(MLA_HEADS):
        k_ref[:, h * LANES:(h + 1) * LANES] = (kn[:, h * LANES:(h + 1) * LANES] + roped).astype(k_ref.dtype)


def _mla_kv(p3, gain, w_k, wv_t, tables, nb, seg):
    t_tok = p3.shape[0]
    nt = seg // TM
    return pl.pallas_call(
        _mla_kv_kernel,
        grid=(nb, nt),
        in_specs=[pl.BlockSpec((TM, MLA_RANK), lambda b, j: (b * nt + j, 1)),
                  pl.BlockSpec((TM, LANES), lambda b, j: (b * nt + j, 2 * MLA_RANK // LANES)),
                  pl.BlockSpec((1, MLA_RANK), lambda b, j: (0, 0)),
                  pl.BlockSpec(w_k.shape, lambda b, j: (0, 0)),
                  pl.BlockSpec(wv_t.shape, lambda b, j: (0, 0)),
                  pl.BlockSpec((TM, LANES), lambda b, j: (j, 0)),
                  pl.BlockSpec((TM, LANES), lambda b, j: (j, 0))],
        out_specs=[pl.BlockSpec((TM, MLA_HEADS * LANES), lambda b, j: (b * nt + j, 0)),
                   pl.BlockSpec((1, MLA_HEADS * MLA_V, TM), lambda b, j: (b, 0, j))],
        out_shape=[jax.ShapeDtypeStruct((t_tok, MLA_HEADS * LANES), BF16),
                   jax.ShapeDtypeStruct((nb, MLA_HEADS * MLA_V, seg), BF16)],
        compiler_params=_params("parallel", "parallel"),
        name="mla_kv",
    )(p3, p3, gain.reshape(1, MLA_RANK), w_k, wv_t, *tables)


def _mla_attn_kernel(qt_ref, k_ref, vt_ref, o_ref):
    @pl.when(pl.program_id(2) == 0)
    def _():
        o_ref[...] = jnp.zeros_like(o_ref)

    @pl.when(pl.program_id(2) > 0)
    def _():
        heads = range(MLA_HEADS_PER_STEP)
        seg = k_ref.shape[0]
        tk = MLA_KEY_TILE if (seg - TM) % MLA_KEY_TILE == 0 else TM
        tiles = [(0, TM)] + [(a, a + tk) for a in range(TM, seg, tk)]

        def scores(hh, j):
            return _dot(k_ref[tiles[j][0]:tiles[j][1], hh * LANES:(hh + 1) * LANES],
                        qt_ref[0, hh * LANES:(hh + 1) * LANES, :])

        m = [jnp.full((1, TM), -jnp.inf, F32) for _ in heads]
        acc = [jnp.zeros((MLA_V + MLA_SUM_ROWS, TM), F32) for _ in heads]
        s_next = [scores(hh, 0) for hh in heads]
        for j in range(len(tiles)):
            s_cur = s_next
            if j + 1 < len(tiles):
                s_next = [scores(hh, j + 1) for hh in heads]
            for hh in heads:
                m_new = jnp.maximum(m[hh], jnp.max(s_cur[hh], axis=0, keepdims=True))
                alpha = jnp.exp2(m[hh] - m_new)
                p = jnp.exp2(s_cur[hh] - m_new).astype(BF16)
                v_t = vt_ref[0, hh * MLA_V:(hh + 1) * MLA_V, tiles[j][0]:tiles[j][1]]
                v_ext = jnp.concatenate([v_t, jnp.ones((MLA_SUM_ROWS, v_t.shape[1]), BF16)], axis=0)
                acc[hh] = alpha * acc[hh] + _dot(v_ext, p)
                m[hh] = m_new
        o_t = jnp.concatenate([acc[hh][:MLA_V, :] / acc[hh][MLA_V:MLA_V + 1, :] for hh in heads], axis=0)
        o_ref[...] = jnp.transpose(o_t).astype(o_ref.dtype)


def _mla_attention(q_t, k, v_t, nb, seg):
    t_tok = k.shape[0]
    nt = seg // TM
    hps = MLA_HEADS_PER_STEP
    return pl.pallas_call(
        _mla_attn_kernel,
        grid=(nb, MLA_HEADS // hps, nt),
        in_specs=[pl.BlockSpec((1, hps * LANES, TM), lambda b, hp, j: (b * nt + j, hp, 0)),
                  pl.BlockSpec((seg, hps * LANES), lambda b, hp, j: (b, hp)),
                  pl.BlockSpec((1, hps * MLA_V, seg), lambda b, hp, j: (b, hp, 0))],
        out_specs=pl.BlockSpec((TM, hps * MLA_V), lambda b, hp, j: (b * nt + j, hp)),
        out_shape=jax.ShapeDtypeStruct((t_tok, MLA_HEADS * MLA_V), BF16),
        compiler_params=_params("parallel", "parallel", "arbitrary"),
        name="mla_attention",
    )(q_t, k, v_t)


def _final_kernel(x_ref, g_ref, o_ref):
    o_ref[0] = _rms_rows(x_ref[...], g_ref[...])


def _final_norm(x, gain, nb, seg):
    d = x.shape[1]
    nt = seg // TM
    return pl.pallas_call(
        _final_kernel,
        grid=(nb, nt - 1),
        in_specs=[pl.BlockSpec((TM, d), lambda b, j: (b * nt + 1 + j, 0)),
                  pl.BlockSpec((1, d), lambda b, j: (0, 0))],
        out_specs=pl.BlockSpec((1, TM, d), lambda b, j: (b, j, 0)),
        out_shape=jax.ShapeDtypeStruct((nb, seg - TM, d), F32),
        compiler_params=_params("parallel", "parallel"),
        name="final_norm",
    )(x, gain.reshape(1, d))


def _rope_angles(n_lat, rot_dim):
    rows = n_lat // GRID_W
    row = jnp.repeat(jnp.arange(rows), GRID_W).astype(F32)
    col = jnp.tile(jnp.arange(GRID_W), rows).astype(F32)
    quarter = rot_dim // 4
    inv = ROPE_BASE ** (-jnp.arange(quarter, dtype=F32) / quarter)
    ang = jnp.concatenate([row[:, None] * inv, col[:, None] * inv], axis=-1)
    return jnp.cos(ang), jnp.sin(ang)


def _with_ctx_rows(tab, fill):
    return jnp.concatenate([jnp.full((TM, tab.shape[1]), fill, F32), tab], axis=0)


def _attn_rope_tables(n_lat):
    cos, sin = _rope_angles(n_lat, ATT_HEAD_DIM)
    cos_h = jnp.concatenate([cos, cos], axis=1)
    sin_h = jnp.concatenate([-sin, sin], axis=1)
    reps = LANES // ATT_HEAD_DIM
    return (_with_ctx_rows(jnp.tile(cos_h, (1, reps)), 1.0), _with_ctx_rows(jnp.tile(sin_h, (1, reps)), 0.0))


def _swap_halves(w, n_heads, dim):
    w3 = w.reshape(w.shape[0], n_heads, 2, dim // 2)
    return w3[:, :, ::-1, :].reshape(w.shape[0], n_heads * dim)


def _mla_k_tables(n_lat):
    cos, sin = _rope_angles(n_lat, MLA_ROPE)
    lo = jnp.zeros((n_lat, MLA_NOPE), F32)
    hi = jnp.zeros((n_lat, LANES - MLA_NOPE - MLA_ROPE), F32)
    cos_c = jnp.concatenate([lo, cos, cos, hi], axis=1)
    sin_c = jnp.concatenate([lo, -sin, sin, hi], axis=1)
    ctx_cos = jnp.concatenate([lo[:TM], jnp.ones((TM, MLA_ROPE), F32), hi[:TM]], axis=1)
    return (jnp.concatenate([ctx_cos, cos_c], axis=0), _with_ctx_rows(sin_c, 0.0))


def _pad_heads(w, real):
    r = w.shape[0]
    w3 = w.reshape(r, MLA_HEADS, real)
    return jnp.pad(w3, ((0, 0), (0, 0), (0, LANES - real))).reshape(r, MLA_HEADS * LANES)


def kernel(x, c, ctx, c_ctx, norm1_g, norm2_g, w_mod, b_mod, moe_w_group, moe_b_group, moe_w_expert, moe_b_expert, moe_w_gate, moe_w_up, moe_w_down, attn_w_in, attn_sink, attn_w_out, ssm_w_in, ssm_conv_w, ssm_conv_b, ssm_dt_bias, ssm_a_log, ssm_d, ssm_norm_g, ssm_w_out, mlstm_w_in, mlstm_gate_b, mlstm_norm_g, mlstm_w_out, mla_w_in, mla_q_norm_g, mla_w_q_up, mla_kv_norm_g, mla_w_kv_up, mla_w_out, final_norm_g):
    nb, n_lat, d = x.shape
    assert ctx.shape[1] == TM and d == D_MODEL and n_lat % TM == 0
    depth = w_mod.shape[0]
    seg = TM + n_lat
    nt = seg // TM
    t_tok = nb * seg

    xs = (x, ctx)

    rows = -(-(nb + 1) // 8) * 8
    cvec = jnp.concatenate([c, c_ctx[None, :], jnp.zeros((rows - nb - 1, d), F32)], axis=0)
    mods = _modulation(cvec, w_mod, b_mod).reshape(depth, rows, ADALN_CHUNKS, d)
    mods = jnp.pad(mods, ((0, 0), (0, 0), (0, MOD_ROWS - ADALN_CHUNKS), (0, 0)))

    w_router = jnp.concatenate([moe_w_expert, moe_w_group,
                                jnp.zeros((depth, d, LANES - MOE_EXPERTS - MOE_GROUPS), F32)], axis=-1)
    b_router = jnp.concatenate([moe_b_expert, moe_b_group,
                                jnp.zeros((depth, LANES - MOE_EXPERTS - MOE_GROUPS), F32)], axis=-1)
    w_gate, w_up = moe_w_gate.astype(BF16), moe_w_up.astype(BF16)
    w_down = moe_w_down.astype(BF16).reshape(depth, MOE_GROUPS, MOE_PER_GROUP * MOE_FF, d)

    for i in range(depth):
        kind = i % 4
        mod_i = mods[i]
        if kind == 0:
            nq, nk = ATT_HEADS * ATT_HEAD_DIM, ATT_KV_HEADS * ATT_HEAD_DIM
            w_in = attn_w_in[i // 4]
            w_all = jnp.concatenate([w_in, _swap_halves(w_in[:, :nq], ATT_HEADS, ATT_HEAD_DIM),
                                     _swap_halves(w_in[:, nq:nq + nk], ATT_KV_HEADS, ATT_HEAD_DIM)],
                                    axis=1).astype(BF16)
            n_in = w_in.shape[1]
            q, k, v = _normproj(xs, mod_i, norm1_g[i], w_all,
                                [(0, nq, n_in, ATT_HEAD_DIM ** -0.5 * LOG2_E), (nq, nk, n_in + nq), (nq + nk, nk, None)],
                                [BF16, BF16, BF16], nt, nb, tables=_attn_rope_tables(n_lat))
            o = _windowed_attention(q, k, v, attn_sink[i // 4], nb, seg)
            xs = _outproj([o], [nq], xs, mod_i, attn_w_out[i // 4].astype(BF16), _plain_prologue, nt, nb)
        elif kind == 1:
            j = i // 4
            w_in = jnp.pad(ssm_w_in[j], ((0, 0), (0, LANES - 2 * SSM_HEADS))).astype(BF16)
            z, xbc, dt = _normproj(xs, mod_i, norm1_g[i], w_in,
                                   [(0, SSM_D_INNER, None), (SSM_D_INNER, SSM_CONV_DIM, None),
                                    (SSM_D_INNER + SSM_CONV_DIM, LANES, None)],
                                   [BF16, BF16, F32], nt, nb)
            lane_pad = LANES - 2 * SSM_HEADS
            dt_bias = jnp.pad(ssm_dt_bias[j].reshape(1, -1), ((0, 0), (0, lane_pad)))
            a_neg = jnp.pad(-jnp.exp(ssm_a_log[j].astype(F32)).reshape(1, -1), ((0, 0), (0, lane_pad)))
            conv_w = jnp.pad(ssm_conv_w[j], ((0, 8 - SSM_CONV), (0, 0)))
            xc, bm, cm, dtv, a = _ssm_conv(xbc, dt, conv_w, ssm_conv_b[j].reshape(1, -1), dt_bias, a_neg, nt)
            yf, yb = _ssd_scan(xc, bm, cm, dtv, a, ssm_d[j].astype(F32), nb, seg)
            xs = _outproj([z], [SSM_D_INNER], xs, mod_i, ssm_w_out[j].astype(BF16), _ssm_prologue,
                          nt, nb, extra=(ssm_norm_g[j].reshape(1, -1),), ins_t=(yf, yb))
        elif kind == 2:
            j = i // 4
            nqk, nv = 2 * ML_HEADS * ML_QK_DIM, ML_HEADS * ML_V_DIM
            w_in = jnp.pad(mlstm_w_in[j], ((0, 0), (0, LANES - 4 * ML_HEADS))).astype(BF16)
            qk, v, o, g = _normproj(xs, mod_i, norm1_g[i], w_in,
                                    [(0, nqk, None), (nqk, nv, None), (nqk + nv, nv, None), (nqk + 2 * nv, LANES, None)],
                                    [BF16, BF16, BF16, F32], nt, nb)
            gate_b = jnp.pad(mlstm_gate_b[j].reshape(1, -1), ((0, 0), (0, LANES - 4 * ML_HEADS)))
            hf, hb = _mlstm_scan(qk, v, g, gate_b, nb, seg)
            xs = _outproj([o], [nv], xs, mod_i, mlstm_w_out[j].astype(BF16), _mlstm_prologue,
                          nt, nb, extra=(mlstm_norm_g[j].reshape(1, -1),), ins_t=(hf, hb))
        else:
            j = i // 4
            w_in = jnp.pad(mla_w_in[j], ((0, 0), (0, LANES - MLA_ROPE))).astype(BF16)
            (p3,) = _normproj(xs, mod_i, norm1_g[i], w_in, [(0, w_in.shape[1], None)], [F32], nt, nb)
            cos, sin = _rope_angles(n_lat, MLA_ROPE)
            tables_t = (jnp.transpose(_with_ctx_rows(cos, 1.0)), jnp.transpose(_with_ctx_rows(sin, 0.0)))
            wq_t = jnp.transpose(_pad_heads(mla_w_q_up[j], MLA_NOPE + MLA_ROPE)).astype(BF16)
            q_t = _mla_q(p3, mla_q_norm_g[j], wq_t, tables_t, nt)
            w_kv = mla_w_kv_up[j].reshape(MLA_RANK, MLA_HEADS, MLA_NOPE + MLA_V)
            w_k = _pad_heads(w_kv[:, :, :MLA_NOPE].reshape(MLA_RANK, MLA_HEADS * MLA_NOPE), MLA_NOPE).astype(BF16)
            wv_t = jnp.transpose(w_kv[:, :, MLA_NOPE:].reshape(MLA_RANK, MLA_HEADS * MLA_V)).astype(BF16)
            k, v_t = _mla_kv(p3, mla_kv_norm_g[j], w_k, wv_t, _mla_k_tables(n_lat), nb, seg)
            o = _mla_attention(q_t, k, v_t, nb, seg)
            xs = _outproj([o], [MLA_HEADS * MLA_V], xs, mod_i, mla_w_out[j].astype(BF16), _plain_prologue, nt, nb)

        xs = _moe(xs, mod_i, norm2_g[i], w_router[i], b_router[i].reshape(1, -1), w_gate, w_up, w_down, i, nt, nb)

    return _final_norm(xs, final_norm_g, nb, seg)
```

```python
import functools
import math

import jax
import jax.numpy as jnp
from jax import lax
from jax.experimental import pallas as pl
from jax.experimental.pallas import tpu as pltpu

F32 = jnp.float32
BF16 = jnp.bfloat16

D_MODEL = 1024
GRID_W = 64
EPS = 1e-6
ROPE_BASE = 10000.0
ADALN_CHUNKS = 6
CHUNK = 128
SSD_CHUNK = 256
ML_CHUNK = 128
TM = 256
MOD_ROWS = 8
LANES = 128
V7X_VMEM_LIMIT = 48 * 1024 * 1024

ATT_HEADS, ATT_KV_HEADS, ATT_HEAD_DIM, WINDOW = 16, 4, 64, 128
ATT_GROUP = ATT_HEADS // ATT_KV_HEADS
SSM_D_INNER, SSM_HEAD_DIM, SSM_HEADS, SSM_GROUPS, SSM_STATE, SSM_CONV = 2048, 64, 32, 4, 128, 5
SSM_HPG = SSM_HEADS // SSM_GROUPS
SSM_BC = SSM_GROUPS * SSM_STATE
SSM_CONV_DIM = SSM_D_INNER + 2 * SSM_BC
ML_HEADS, ML_QK_DIM, ML_V_DIM = 8, 64, 128
MLA_HEADS, MLA_RANK, MLA_NOPE, MLA_ROPE, MLA_V = 16, 256, 64, 32, 64
MOE_GROUPS, MOE_PER_GROUP, MOE_EXPERTS, MOE_FF = 4, 4, 16, 256
MOE_ROWS = 1024
MLA_KEY_TILE = 512
MLA_HEADS_PER_STEP = 4
LOG2_E = 1.4426950408889634


def _dot(a, b):
    return jnp.dot(a, b, preferred_element_type=F32)


def _dot_nt(a, b):
    return lax.dot_general(a, b, (((1,), (1,)), ((), ())), preferred_element_type=F32)


def _split_bf16(x):
    hi = x.astype(BF16)
    lo = (x - hi.astype(F32)).astype(BF16)
    return hi, lo


def _dot_split(a, b):
    a_hi, a_lo = _split_bf16(a)
    b_hi, b_lo = _split_bf16(b)
    return _dot(a_hi, b_hi) + _dot(a_lo, b_hi) + _dot(a_hi, b_lo)


def _sigmoid(x):
    return 1.0 / (1.0 + jnp.exp(-x))


def _silu(x):
    return x * _sigmoid(x)


def _softplus(x):
    return jnp.maximum(x, 0.0) + jnp.log1p(jnp.exp(-jnp.abs(x)))


def _params(*sem):
    return pltpu.CompilerParams(dimension_semantics=sem, vmem_limit_bytes=V7X_VMEM_LIMIT)


def _mod_row(t, nt, nb):
    return jnp.where(t % nt == 0, nb, t // nt)


def _mod_kernel(c_ref, w_ref, b_ref, o_ref):
    o_ref[0] = _dot_split(_silu(c_ref[...]), w_ref[0]) + b_ref[0]


def _modulation(cvec, w_mod, b_mod):
    depth, d, n = w_mod.shape
    tn = 1536
    rows = cvec.shape[0]
    return pl.pallas_call(
        _mod_kernel,
        grid=(depth, n // tn),
        in_specs=[pl.BlockSpec((rows, d), lambda l, j: (0, 0)),
                  pl.BlockSpec((1, d, tn), lambda l, j: (l, 0, j)),
                  pl.BlockSpec((1, 1, tn), lambda l, j: (l, 0, j))],
        out_specs=pl.BlockSpec((1, rows, tn), lambda l, j: (l, 0, j)),
        out_shape=jax.ShapeDtypeStruct((depth, rows, n), F32),
        compiler_params=_params("arbitrary", "arbitrary"),
        name="modulation",
    )(cvec, w_mod, b_mod.reshape(depth, 1, n))


def _normed(x, g_row, mod, sh_row, sc_row):
    y = x * lax.rsqrt(jnp.mean(x * x, axis=-1, keepdims=True) + EPS) * g_row
    return y * (1.0 + mod[sc_row:sc_row + 1, :]) + mod[sh_row:sh_row + 1, :]


def _proj_columns(hb, w_ref, o_ref, start, width, rot_start, cos, sin, scale=None, chunk=512):
    for c in range(0, width, chunk):
        cw = min(chunk, width - c)
        acc = _dot(hb, w_ref[:, start + c:start + c + cw])
        if rot_start is not None:
            rot = _dot(hb, w_ref[:, rot_start + c:rot_start + c + cw])
            reps = cw // LANES
            acc = acc * jnp.tile(cos, (1, reps)) + rot * jnp.tile(sin, (1, reps))
        if scale is not None:
            acc = acc * scale
        o_ref[:, c:c + cw] = acc.astype(o_ref.dtype)


def _stream_specs(x, nt):
    if isinstance(x, tuple):
        d = x[0].shape[-1]
        return [pl.BlockSpec((1, TM, d), lambda t: (t // nt, jnp.maximum(t % nt - 1, 0), 0)),
                pl.BlockSpec((1, TM, d), lambda t: (t // nt, 0, 0))], list(x)
    sub = _tiles_per_step(x, nt)
    return [pl.BlockSpec((sub * TM, x.shape[1]), lambda t: (t, 0))], [x]


def _tiles_per_step(x, nt):
    if isinstance(x, tuple):
        return 1
    return 2 if (x.shape[0] // TM) % 2 == 0 else 1


def _stream_tile(x_refs, nt, s):
    if len(x_refs) == 1:
        return x_refs[0][s * TM:(s + 1) * TM, :]
    return jnp.where(pl.program_id(0) % nt == 0, x_refs[1][0], x_refs[0][0])


def _normproj_kernel(*refs, outs, has_rope, n_x, nt, nb, sub):
    x_refs, refs = refs[:n_x], refs[n_x:]
    mods_ref, g_ref, w_ref = refs[:3]
    k = 3
    cos = sin = None
    if has_rope:
        cos, sin = refs[3][...], refs[4][...]
        k = 5
    hs = []
    for s in range(sub):
        mod = mods_ref[_mod_row(pl.program_id(0) * sub + s, nt, nb)]
        hs.append(_normed(_stream_tile(x_refs, nt, s), g_ref[...], mod, 0, 1).astype(BF16))
    hb = hs[0] if sub == 1 else jnp.concatenate(hs, axis=0)
    for o_ref, spec in zip(refs[k:], outs):
        _proj_columns(hb, w_ref, o_ref, *spec[:3], cos, sin, scale=spec[3] if len(spec) > 3 else None)


def _normproj(x, mods, gain, w, outs, out_dtypes, nt, nb, tables=None):
    d, n = w.shape
    t_tok = nb * nt * TM
    sub = _tiles_per_step(x, nt)
    assert tables is None or sub == 1
    rows = sub * TM
    x_specs, x_args = _stream_specs(x, nt)
    in_specs = x_specs + [pl.BlockSpec(mods.shape, lambda t: (0, 0, 0)),
                          pl.BlockSpec((1, d), lambda t: (0, 0)),
                          pl.BlockSpec((d, n), lambda t: (0, 0))]
    args = x_args + [mods, gain.reshape(1, d), w]
    if tables is not None:
        in_specs += [pl.BlockSpec((TM, LANES), lambda t: (t % nt, 0))] * 2
        args += list(tables)
    return pl.pallas_call(
        functools.partial(_normproj_kernel, outs=tuple(outs), has_rope=tables is not None, n_x=len(x_args),
                          nt=nt, nb=nb, sub=sub),
        grid=(t_tok // rows,),
        in_specs=in_specs,
        out_specs=[pl.BlockSpec((rows, o[1]), lambda t: (t, 0)) for o in outs],
        out_shape=[jax.ShapeDtypeStruct((t_tok, o[1]), dt) for o, dt in zip(outs, out_dtypes)],
        compiler_params=_params("parallel"),
        name="normproj",
    )(*args)


def _outproj_kernel(*refs, prologue, n_t, n_tok, n_extra, n_x, nt, nb, sub):
    ins_t = refs[:n_t * sub]
    ins = refs[n_t * sub:n_t * sub + n_tok]
    extra = refs[n_t * sub + n_tok:n_t * sub + n_tok + n_extra]
    x_refs = refs[n_t * sub + n_tok + n_extra:n_t * sub + n_tok + n_extra + n_x]
    mods_ref, w_ref, o_ref = refs[n_t * sub + n_tok + n_extra + n_x:]
    parts = []
    for s in range(sub):
        tok = [r if sub == 1 else r.at[pl.ds(s * TM, TM), :] for r in ins]
        parts.append(prologue(*ins_t[s * n_t:(s + 1) * n_t], *tok, *extra))
    a = parts[0] if sub == 1 else jnp.concatenate(parts, axis=0)
    y = _dot(a, w_ref[...])
    for s in range(sub):
        rows = slice(s * TM, (s + 1) * TM)
        gate = mods_ref[_mod_row(pl.program_id(0) * sub + s, nt, nb)][2:3, :]
        o_ref[rows, :] = _stream_tile(x_refs, nt, s) + gate * y[rows, :]


def _outproj(ins, in_widths, x, mods, w, prologue, nt, nb, extra=(), ins_t=()):
    d = w.shape[1]
    t_tok = nb * nt * TM
    sub = _tiles_per_step(x, nt)
    rows = sub * TM

    def tile_map(s):
        return lambda t: ((t * sub + s) // nt, 0, (t * sub + s) % nt)

    in_specs, args = [], []
    for s in range(sub):
        in_specs += [pl.BlockSpec((1, a.shape[1], TM), tile_map(s)) for a in ins_t]
        args += list(ins_t)
    in_specs += [pl.BlockSpec((rows, wd), lambda t: (t, 0)) for wd in in_widths]
    in_specs += [pl.BlockSpec(e.shape, lambda t: (0, 0)) for e in extra]
    x_specs, x_args = _stream_specs(x, nt)
    n_before_x = len(in_specs)
    in_specs += x_specs + [pl.BlockSpec(mods.shape, lambda t: (0, 0, 0)), pl.BlockSpec(w.shape, lambda t: (0, 0))]
    return pl.pallas_call(
        functools.partial(_outproj_kernel, prologue=prologue, n_t=len(ins_t), n_tok=len(ins), n_extra=len(extra),
                          n_x=len(x_args), nt=nt, nb=nb, sub=sub),
        grid=(t_tok // rows,),
        in_specs=in_specs,
        out_specs=pl.BlockSpec((rows, d), lambda t: (t, 0)),
        out_shape=jax.ShapeDtypeStruct((t_tok, d), F32),
        input_output_aliases={} if isinstance(x, tuple) else {n_before_x: 0},
        compiler_params=_params("parallel"),
        name="outproj",
    )(*args, *ins, *extra, *x_args, mods, w)


def _group_rms(y, n_groups):
    width = y.shape[1] // n_groups
    parts = []
    for g in range(n_groups):
        yg = y[:, g * width:(g + 1) * width]
        parts.append(yg * lax.rsqrt(jnp.mean(yg * yg, axis=-1, keepdims=True) + EPS))
    return jnp.concatenate(parts, axis=1)


def _plain_prologue(o_ref):
    return o_ref[...]


def _ssm_prologue(yf_ref, yb_ref, z_ref, g_ref):
    y = jnp.transpose(yf_ref[0].astype(F32) + yb_ref[0].astype(F32)) * _silu(z_ref[...].astype(F32))
    return (_group_rms(y, SSM_GROUPS) * g_ref[...]).astype(BF16)


def _mlstm_prologue(hf_ref, hb_ref, o_ref, g_ref):
    h = jnp.transpose(hf_ref[0].astype(F32) + hb_ref[0].astype(F32))
    return (_group_rms(h, ML_HEADS) * g_ref[...] * _sigmoid(o_ref[...].astype(F32))).astype(BF16)


MOE_ROUTE_ROWS = 24


def _route(logits):
    lt = jnp.transpose(logits)[:MOE_ROUTE_ROWS, :]
    row = lax.broadcasted_iota(jnp.int32, lt.shape, 0).astype(F32)
    neg = -jnp.inf
    lg = jnp.where((row >= MOE_EXPERTS) & (row < MOE_EXPERTS + MOE_GROUPS), lt, neg)
    gmax = jnp.max(lg, axis=0, keepdims=True)
    g_sel = jnp.min(jnp.where(lg == gmax, row, LANES), axis=0, keepdims=True) - MOE_EXPERTS
    p_g = 1.0 / jnp.sum(jnp.exp(lg - gmax), axis=0, keepdims=True)
    in_group = (row >= g_sel * MOE_PER_GROUP) & (row < (g_sel + 1) * MOE_PER_GROUP)
    le = jnp.where(in_group, lt, neg)
    v1 = jnp.max(le, axis=0, keepdims=True)
    i1 = jnp.min(jnp.where(le == v1, row, LANES), axis=0, keepdims=True)
    le2 = jnp.where(row == i1, neg, le)
    v2 = jnp.max(le2, axis=0, keepdims=True)
    i2 = jnp.min(jnp.where(le2 == v2, row, LANES), axis=0, keepdims=True)
    e2 = jnp.exp(v2 - v1)
    w1 = p_g / (1.0 + e2)
    comb_t = jnp.where(row == i1, w1, 0.0) + jnp.where(row == i2, w1 * e2, 0.0)
    pad = jnp.zeros((LANES - MOE_ROUTE_ROWS, lt.shape[1]), F32)
    return jnp.transpose(jnp.concatenate([comb_t, pad], axis=0))


def _moe_kernel(x_ref, mods_ref, g_ref, wr_ref, br_ref, wg_ref, wu_ref, wd_ref, o_ref, h_scr, comb_scr,
                *, nt, nb, sub):
    i = pl.program_id(0)
    e = pl.program_id(1)

    @pl.when(e == 0)
    def _():
        w_hi, w_lo = _split_bf16(wr_ref[...])
        w_both = jnp.concatenate([w_hi, w_lo], axis=1)
        for s in range(sub):
            rows = slice(s * TM, (s + 1) * TM)
            mod = mods_ref[_mod_row(i * sub + s, nt, nb)]
            x = x_ref[rows, :]
            h = _normed(x, g_ref[...], mod, 3, 4)
            h_hi, h_lo = _split_bf16(h)
            h_scr[rows, :] = h_hi
            both = _dot(h_hi, w_both)
            logits = both[:, :LANES] + both[:, LANES:] + _dot(h_lo, w_hi) + br_ref[...]
            comb_scr[rows, :] = _route(logits)
            o_ref[rows, :] = x

    hb = h_scr[...]
    lane = lax.broadcasted_iota(jnp.int32, comb_scr.shape, 1)
    comb = comb_scr[...]
    scaled = []
    for k in range(MOE_PER_GROUP):
        act = _silu(_dot(hb, wg_ref[0, k])) * _dot(hb, wu_ref[0, k])
        cw = jnp.sum(jnp.where(lane == e * MOE_PER_GROUP + k, comb, 0.0), axis=-1, keepdims=True)
        scaled.append((act * cw).astype(BF16))
    y = _dot(jnp.concatenate(scaled, axis=1), wd_ref[0, 0])
    for s in range(sub):
        rows = slice(s * TM, (s + 1) * TM)
        gate = mods_ref[_mod_row(i * sub + s, nt, nb)][5:6, :]
        o_ref[rows, :] += gate * y[rows, :]


def _moe(x, mods, gain, w_router, b_router, w_gate, w_up, w_down, layer, nt, nb):
    t_tok, d = x.shape
    pg = MOE_PER_GROUP
    rows = MOE_ROWS if t_tok % MOE_ROWS == 0 else TM
    sub = rows // TM
    return pl.pallas_call(
        functools.partial(_moe_kernel, nt=nt, nb=nb, sub=sub),
        grid=(t_tok // rows, MOE_GROUPS),
        in_specs=[pl.BlockSpec((rows, d), lambda i, e: (i, 0)),
                  pl.BlockSpec(mods.shape, lambda i, e: (0, 0, 0)),
                  pl.BlockSpec((1, d), lambda i, e: (0, 0)),
                  pl.BlockSpec((d, LANES), lambda i, e: (0, 0)),
                  pl.BlockSpec((1, LANES), lambda i, e: (0, 0)),
                  pl.BlockSpec((1, pg, d, MOE_FF), lambda i, e: (layer, e, 0, 0)),
                  pl.BlockSpec((1, pg, d, MOE_FF), lambda i, e: (layer, e, 0, 0)),
                  pl.BlockSpec((1, 1, pg * MOE_FF, d), lambda i, e: (layer, e, 0, 0))],
        out_specs=pl.BlockSpec((rows, d), lambda i, e: (i, 0)),
        out_shape=jax.ShapeDtypeStruct((t_tok, d), F32),
        scratch_shapes=[pltpu.VMEM((rows, d), BF16), pltpu.VMEM((rows, LANES), F32)],
        input_output_aliases={0: 0},
        compiler_params=_params("parallel", "arbitrary"),
        name="moe",
    )(x, mods, gain.reshape(1, d), w_router, b_router, w_gate, w_up, w_down)


def _attn_kernel(sink_ref, q_ref, kp_ref, kc_ref, kn_ref, vp_ref, vc_ref, vn_ref, kx_ref, vx_ref, o_ref, *, n_lat):
    j = pl.program_id(1)
    jl = j - 1
    dh = ATT_HEAD_DIM
    n_loc = TM + 2 * CHUNK
    k_all = jnp.concatenate([kp_ref[...], kc_ref[...], kn_ref[...], kx_ref[...]], axis=0)
    v_t = jnp.transpose(jnp.concatenate([vp_ref[...], vc_ref[...], vn_ref[...], vx_ref[...]],
                                        axis=0).astype(F32)).astype(BF16)
    q_t = jnp.transpose(q_ref[...].astype(F32)).astype(BF16)
    n_keys = k_all.shape[0]
    kj = lax.broadcasted_iota(jnp.int32, (n_keys, TM), 0)
    qi = lax.broadcasted_iota(jnp.int32, (n_keys, TM), 1)
    kpos = jl * TM - CHUNK + kj
    local_ok = (kj >= qi + CHUNK - WINDOW) & (kj <= qi + CHUNK + WINDOW) & (kpos >= 0) & (kpos < n_lat) & (jl >= 0)
    bias = jnp.where((kj >= n_loc) | local_ok, 0.0, -jnp.inf)
    bias = jnp.concatenate([bias] * ATT_GROUP, axis=1)
    lane_head = lax.broadcasted_iota(jnp.int32, (1, ATT_GROUP * TM), 1) // TM
    zeros_q = jnp.zeros((dh, ATT_GROUP * TM), BF16)
    outs = []
    for g in range(ATT_KV_HEADS):
        half = g % 2
        k_pair = k_all[:, (g - half) * dh:(g - half + 2) * dh]
        qg = jnp.concatenate([q_t[(g * ATT_GROUP + a) * dh:(g * ATT_GROUP + a + 1) * dh, :]
                              for a in range(ATT_GROUP)], axis=1)
        q_m = jnp.concatenate([zeros_q, qg] if half else [qg, zeros_q], axis=0)
        s = _dot(k_pair, q_m) + bias
        sink = jnp.zeros((1, ATT_GROUP * TM), F32)
        for a in range(ATT_GROUP):
            sink = jnp.where(lane_head == a, sink_ref[g * ATT_GROUP + a] * LOG2_E, sink)
        m = jnp.maximum(jnp.max(s, axis=0, keepdims=True), sink)
        p = jnp.exp2(s - m)
        den = jnp.sum(p, axis=0, keepdims=True) + jnp.exp2(sink - m)
        og = _dot(v_t[g * dh:(g + 1) * dh, :], p.astype(BF16)) / den
        outs += [og[:, a * TM:(a + 1) * TM] for a in range(ATT_GROUP)]
    o_ref[...] = jnp.transpose(jnp.concatenate(outs, axis=0)).astype(o_ref.dtype)


def _windowed_attention(q, k, v, sink, nb, seg):
    t_tok = q.shape[0]
    nchunk = seg // CHUNK
    nt = seg // TM
    per_tile = TM // CHUNK
    kvw = ATT_KV_HEADS * ATT_HEAD_DIM

    def halo(off):
        return lambda b, j, *_: (b * nchunk + jnp.clip(j * per_tile + off, per_tile, nchunk - 1), 0)

    tile = lambda b, j, *_: (b * nt + j, 0)
    kv_spec = [pl.BlockSpec((CHUNK, kvw), halo(-1)), pl.BlockSpec((TM, kvw), tile),
               pl.BlockSpec((CHUNK, kvw), halo(per_tile))]
    ctx_spec = pl.BlockSpec((TM, kvw), lambda b, j, *_: (b * nt, 0))
    return pl.pallas_call(
        functools.partial(_attn_kernel, n_lat=seg - TM),
        grid_spec=pltpu.PrefetchScalarGridSpec(
            num_scalar_prefetch=1,
            grid=(nb, nt),
            in_specs=[pl.BlockSpec((TM, q.shape[1]), tile)] + kv_spec + kv_spec + [ctx_spec, ctx_spec],
            out_specs=pl.BlockSpec((TM, q.shape[1]), tile),
        ),
        out_shape=jax.ShapeDtypeStruct((t_tok, q.shape[1]), BF16),
        compiler_params=_params("parallel", "parallel"),
        name="windowed_attention",
    )(sink, q, k, k, k, v, v, v, k, v)


def _conv_kernel(cur_ref, prev_ref, next_ref, dt_ref, w_ref, b_ref, dtb_ref, aneg_ref,
                 xs_ref, bm_ref, cm_ref, dtv_ref, a_ref, ext, *, nt):
    t = pl.program_id(0)
    tl = t % nt
    halo = 8
    first = (tl == 0) | (tl == 1)
    last = (tl == 0) | (tl == nt - 1)
    ext[0:halo, :] = jnp.where(first, 0.0, prev_ref[...].astype(F32)[halo:2 * halo, :])
    ext[halo:halo + TM, :] = cur_ref[...].astype(F32)
    ext[halo + TM:2 * halo + TM, :] = jnp.where(last, 0.0, next_ref[...].astype(F32)[0:halo, :])
    pad = SSM_CONV // 2
    chunk = 512
    for c in range(0, SSM_CONV_DIM, chunk):
        acc = jnp.broadcast_to(b_ref[:, c:c + chunk], (TM, chunk))
        for k in range(SSM_CONV):
            acc = acc + w_ref[k:k + 1, c:c + chunk] * ext[halo - pad + k:halo - pad + k + TM, c:c + chunk]
        y = _silu(acc)
        if c < SSM_D_INNER:
            xs_ref[0, c:c + chunk, :] = jnp.transpose(y).astype(xs_ref.dtype)
        elif c < SSM_D_INNER + SSM_BC:
            bm_ref[:, c - SSM_D_INNER:c - SSM_D_INNER + chunk] = y.astype(bm_ref.dtype)
        else:
            off = c - SSM_D_INNER - SSM_BC
            cm_ref[0, off:off + chunk, :] = jnp.transpose(y).astype(cm_ref.dtype)
    dtv = _softplus(dt_ref[...] + dtb_ref[...])
    dtv_ref[0] = jnp.transpose(dtv)
    a_ref[0] = jnp.transpose(dtv * aneg_ref[...] * LOG2_E)


def _ssm_conv(xbc, dt, conv_w, conv_b, dt_bias, a_neg, nt):
    t_tok = xbc.shape[0]
    n16 = t_tok // 16
    seg = nt * TM
    nb = t_tok // seg
    row = lambda t: (t, 0)
    col = lambda t: (t // nt, 0, t % nt)
    fixed = lambda t: (0, 0)
    return pl.pallas_call(
        functools.partial(_conv_kernel, nt=nt),
        grid=(t_tok // TM,),
        in_specs=[pl.BlockSpec((TM, SSM_CONV_DIM), row),
                  pl.BlockSpec((16, SSM_CONV_DIM), lambda t: (jnp.maximum(t * (TM // 16) - 1, 0), 0)),
                  pl.BlockSpec((16, SSM_CONV_DIM), lambda t: (jnp.minimum((t + 1) * (TM // 16), n16 - 1), 0)),
                  pl.BlockSpec((TM, LANES), row),
                  pl.BlockSpec((8, SSM_CONV_DIM), fixed),
                  pl.BlockSpec((1, SSM_CONV_DIM), fixed),
                  pl.BlockSpec((1, LANES), fixed),
                  pl.BlockSpec((1, LANES), fixed)],
        out_specs=[pl.BlockSpec((1, SSM_D_INNER, TM), col), pl.BlockSpec((TM, SSM_BC), row),
                   pl.BlockSpec((1, SSM_BC, TM), col), pl.BlockSpec((1, LANES, TM), col),
                   pl.BlockSpec((1, LANES, TM), col)],
        out_shape=[jax.ShapeDtypeStruct((nb, SSM_D_INNER, seg), BF16), jax.ShapeDtypeStruct((t_tok, SSM_BC), BF16),
                   jax.ShapeDtypeStruct((nb, SSM_BC, seg), BF16), jax.ShapeDtypeStruct((nb, LANES, seg), F32),
                   jax.ShapeDtypeStruct((nb, LANES, seg), F32)],
        scratch_shapes=[pltpu.VMEM((TM + 16, SSM_CONV_DIM), F32)],
        compiler_params=_params("parallel"),
        name="ssm_conv",
    )(xbc, xbc, xbc, dt, conv_w, conv_b, dt_bias, a_neg)


def _tri(lower, n):
    r = lax.broadcasted_iota(jnp.int32, (n, n), 0)
    c = lax.broadcasted_iota(jnp.int32, (n, n), 1)
    return (c <= r) if lower else (c >= r)


def _ssd_direction(xs_ref, bm_ref, ct_ref, dt_ref, a_ref, dsk_ref, st_ref, y_ref, d):
    a_t = a_ref[0]
    dt_t = dt_ref[0]
    upper = jnp.where(_tri(False, SSD_CHUNK), 1.0, 0.0).astype(BF16)
    a_hi, a_lo = _split_bf16(a_t)
    cum = _dot(a_hi, upper) + _dot(a_lo, upper)
    cum_end = cum[:, SSD_CHUNK - 1:SSD_CHUNK]
    if d == 0:
        lane_v, sub_v = cum, -cum
        inter = jnp.exp2(cum)
        w_upd = jnp.exp2(cum_end - cum) * dt_t
    else:
        ecum = cum - a_t
        lane_v, sub_v = -ecum, ecum
        inter = jnp.exp2(cum_end - ecum)
        w_upd = jnp.exp2(ecum) * dt_t
    sub_c = jnp.transpose(sub_v)
    decay_end = jnp.exp2(cum_end)
    mask = _tri(d == 1, SSD_CHUNK)
    p = SSM_HEAD_DIM
    for g in range(SSM_GROUPS):
        bg = bm_ref[:, g * SSM_STATE:(g + 1) * SSM_STATE]
        cg_t = ct_ref[0, g * SSM_STATE:(g + 1) * SSM_STATE, :]
        cb_t = _dot(bg, cg_t)
        state = st_ref[g]
        y_in = _dot(state.astype(BF16), cg_t)
        upd = []
        for e in range(SSM_HPG):
            h = g * SSM_HPG + e
            ln = d * SSM_HEADS + h
            rows = slice(e * p, (e + 1) * p)
            seg = jnp.where(mask, sub_c[:, ln:ln + 1] + lane_v[ln:ln + 1, :], -jnp.inf)
            m_t = (cb_t * jnp.exp2(seg)).astype(BF16)
            xf = xs_ref[0, h * p:(h + 1) * p, :].astype(F32)
            u = (xf * dt_t[ln:ln + 1, :]).astype(BF16)
            yh = _dot(u, m_t) + y_in[rows, :] * inter[ln:ln + 1, :] + dsk_ref[d, h] * xf
            y_ref[0, h * p:(h + 1) * p, :] = yh.astype(y_ref.dtype)
            upd.append((xf * w_upd[ln:ln + 1, :]).astype(BF16))
        new = _dot(jnp.concatenate(upd, axis=0), bg)
        for e in range(SSM_HPG):
            ln = d * SSM_HEADS + g * SSM_HPG + e
            rows = slice(e * p, (e + 1) * p)
            st_ref[g, rows, :] = state[rows, :] * decay_end[ln:ln + 1, :] + new[rows, :]


def _ssd_kernel(dsk_ref, xsf, bmf, ctf, dtf, af, xsb, bmb, ctb, dtb, ab, yf_ref, yb_ref, stf, stb):
    @pl.when(pl.program_id(1) == 0)
    def _():
        stf[...] = jnp.zeros_like(stf)
        stb[...] = jnp.zeros_like(stb)

    _ssd_direction(xsf, bmf, ctf, dtf, af, dsk_ref, stf, yf_ref, 0)
    _ssd_direction(xsb, bmb, ctb, dtb, ab, dsk_ref, stb, yb_ref, 1)


def _scan_maps(nchunk, chunk):
    ctx_chunks = TM // chunk
    fwd = lambda b, t: (b * nchunk + t, 0)
    bwd = lambda b, t: (b * nchunk + jnp.where(t < ctx_chunks, ctx_chunks - 1 - t, nchunk + ctx_chunks - 1 - t), 0)
    return fwd, bwd


def _ssd_scan(xs_t, bm, c_t, dt_t, a_t, d_skip, nb, seg):
    nchunk = seg // SSD_CHUNK
    fwd_rows, bwd_rows = _scan_maps(nchunk, SSD_CHUNK)

    def specs(rows_map):
        cols_map = lambda b, t, *_: (b, 0, rows_map(b, t)[0] - b * nchunk)
        return [pl.BlockSpec((1, SSM_D_INNER, SSD_CHUNK), cols_map),
                pl.BlockSpec((SSD_CHUNK, SSM_BC), lambda b, t, *_: rows_map(b, t)),
                pl.BlockSpec((1, SSM_BC, SSD_CHUNK), cols_map),
                pl.BlockSpec((1, LANES, SSD_CHUNK), cols_map),
                pl.BlockSpec((1, LANES, SSD_CHUNK), cols_map)]

    arrs = (xs_t, bm, c_t, dt_t, a_t)
    return pl.pallas_call(
        _ssd_kernel,
        grid_spec=pltpu.PrefetchScalarGridSpec(
            num_scalar_prefetch=1,
            grid=(nb, nchunk),
            in_specs=specs(fwd_rows) + specs(bwd_rows),
            out_specs=[specs(fwd_rows)[0], specs(bwd_rows)[0]],
            scratch_shapes=[pltpu.VMEM((SSM_GROUPS, SSM_HPG * SSM_HEAD_DIM, SSM_STATE), F32)] * 2,
        ),
        out_shape=[jax.ShapeDtypeStruct(xs_t.shape, BF16)] * 2,
        compiler_params=_params("parallel", "arbitrary"),
        name="ssd_scan",
    )(d_skip, *arrs, *arrs)


def _log_sigmoid(x):
    return jnp.minimum(x, 0.0) - jnp.log1p(jnp.exp(-jnp.abs(x)))


ML_STATE_ROWS = ML_V_DIM + 16


def _mlstm_direction(qk_ref, v_ref, g_ref, gb_ref, c_st, m_st, h_ref, d):
    nh, dk, dv = ML_HEADS, ML_QK_DIM, ML_V_DIM
    g_t = jnp.transpose(g_ref[...] + gb_ref[...])
    ig = g_t[16 * d:16 * d + nh, :]
    lf = _log_sigmoid(g_t[16 * d + nh:16 * d + 2 * nh, :])
    upper = jnp.where(_tri(False, ML_CHUNK), 1.0, 0.0).astype(BF16)
    lf_hi, lf_lo = _split_bf16(lf)
    fc = _dot(lf_hi, upper) + _dot(lf_lo, upper)
    tot = fc[:, ML_CHUNK - 1:ML_CHUNK]
    m_prev = m_st[...]
    if d == 0:
        lane_v, sub_v = fc, ig - fc
        inter = fc + m_prev
        logw = tot - fc + ig
    else:
        ec = fc - lf
        lane_v, sub_v = -ec, ec + ig
        inter = tot - ec + m_prev
        logw = ec + ig
    sub_c = jnp.transpose(jnp.concatenate([sub_v, jnp.zeros((LANES - nh, ML_CHUNK), F32)], axis=0))
    m_new = jnp.maximum(tot + m_prev, jnp.max(logw, axis=-1, keepdims=True))
    ws = jnp.exp(logw - m_new)
    cw = jnp.exp(tot + m_prev - m_new)
    mask = _tri(d == 1, ML_CHUNK)
    q_t = jnp.transpose(qk_ref[:, :nh * dk].astype(F32))
    v_t = jnp.transpose(v_ref[...].astype(F32))
    lane = lax.broadcasted_iota(jnp.int32, (1, LANES), 1)
    zeros_q = jnp.zeros((dk, ML_CHUNK), F32)
    for h in range(nh):
        half = h % 2
        k_pair = qk_ref[:, nh * dk + (h - half) * dk:nh * dk + (h - half + 2) * dk] * (dk ** -0.5)
        qh = q_t[h * dk:(h + 1) * dk, :]
        q_m = jnp.concatenate([zeros_q, qh] if half else [qh, zeros_q], axis=0).astype(BF16)
        logd = jnp.where(mask, sub_c[:, h:h + 1] + lane_v[h:h + 1, :], -jnp.inf)
        mt = jnp.maximum(inter[h:h + 1, :], jnp.max(logd, axis=0, keepdims=True))
        sc = _dot(k_pair, q_m) * jnp.exp(logd - mt)
        vh = v_t[h * dv:(h + 1) * dv, :]
        state = c_st[h]
        cq = _dot(state.astype(BF16), q_m)
        w_int = jnp.exp(inter[h:h + 1, :] - mt)
        num = _dot(vh.astype(BF16), sc.astype(BF16)) + w_int * cq[:dv, :]
        den = jnp.sum(sc, axis=0, keepdims=True) + w_int * cq[dv:dv + 1, :]
        h_ref[0, h * dv:(h + 1) * dv, :] = (num / jnp.maximum(jnp.abs(den), jnp.exp(-mt))).astype(h_ref.dtype)
        ws_h = ws[h:h + 1, :]
        lhs = jnp.concatenate([vh * ws_h, jnp.broadcast_to(ws_h, (ML_STATE_ROWS - dv, ML_CHUNK))], axis=0)
        own = (lane >= half * dk) & (lane < (half + 1) * dk)
        row = lax.broadcasted_iota(jnp.int32, (ML_STATE_ROWS, 1), 0)
        new = cw[h:h + 1, :] * state + _dot(lhs.astype(BF16), k_pair)
        c_st[h] = jnp.where(own & (row <= dv), new, 0.0)
    m_st[...] = jnp.broadcast_to(m_new, m_st.shape)


def _mlstm_kernel(qkf, vf, gf, qkb, vb, gb, gbias, hf_ref, hb_ref, cf, mf, cb, mb):
    @pl.when(pl.program_id(1) == 0)
    def _():
        for r in (cf, mf, cb, mb):
            r[...] = jnp.zeros_like(r)

    _mlstm_direction(qkf, vf, gf, gbias, cf, mf, hf_ref, 0)
    _mlstm_direction(qkb, vb, gb, gbias, cb, mb, hb_ref, 1)


def _mlstm_scan(qk, v, g, gate_b, nb, seg):
    t_tok = qk.shape[0]
    nchunk = seg // ML_CHUNK
    fwd, bwd = _scan_maps(nchunk, ML_CHUNK)
    widths = (qk.shape[1], v.shape[1], LANES)
    in_specs = ([pl.BlockSpec((ML_CHUNK, w), fwd) for w in widths] + [pl.BlockSpec((ML_CHUNK, w), bwd) for w in widths]
                + [pl.BlockSpec((1, LANES), lambda b, t: (0, 0))])
    state = [pltpu.VMEM((ML_HEADS, ML_STATE_ROWS, LANES), F32), pltpu.VMEM((ML_HEADS, LANES), F32)]
    out_map = lambda rows_map: (lambda b, t: (b, 0, rows_map(b, t)[0] - b * nchunk))
    return pl.pallas_call(
        _mlstm_kernel,
        grid=(nb, nchunk),
        in_specs=in_specs,
        out_specs=[pl.BlockSpec((1, v.shape[1], ML_CHUNK), out_map(fwd)),
                   pl.BlockSpec((1, v.shape[1], ML_CHUNK), out_map(bwd))],
        out_shape=[jax.ShapeDtypeStruct((nb, v.shape[1], seg), BF16)] * 2,
        scratch_shapes=state + state,
        compiler_params=_params("parallel", "arbitrary"),
        name="mlstm_scan",
    )(qk, v, g, qk, v, g, gate_b)


def _rms_rows(x, g_row):
    return x * lax.rsqrt(jnp.mean(x * x, axis=-1, keepdims=True) + EPS) * g_row


MLA_SCORE_SCALE = (MLA_NOPE + MLA_ROPE) ** -0.5 * LOG2_E


def _mla_q_kernel(cq_ref, g_ref, wt_ref, cos_ref, sin_ref, qt_ref):
    cq_t = jnp.transpose(_rms_rows(cq_ref[...], g_ref[...])).astype(BF16)
    q_t = _dot(wt_ref[...], cq_t) * MLA_SCORE_SCALE
    qt_ref[0] = q_t.astype(qt_ref.dtype)
    cos, sin = cos_ref[...], sin_ref[...]
    half = MLA_ROPE // 2
    for h in range(MLA_HEADS):
        r0 = h * LANES + MLA_NOPE
        x1, x2 = q_t[r0:r0 + half, :], q_t[r0 + half:r0 + MLA_ROPE, :]
        qt_ref[0, r0:r0 + half, :] = (x1 * cos - x2 * sin).astype(qt_ref.dtype)
        qt_ref[0, r0 + half:r0 + MLA_ROPE, :] = (x2 * cos + x1 * sin).astype(qt_ref.dtype)


def _mla_q(p3, gain, wq_t, tables_t, nt):
    t_tok = p3.shape[0]
    n = wq_t.shape[0]
    half = MLA_ROPE // 2
    return pl.pallas_call(
        _mla_q_kernel,
        grid=(t_tok // TM,),
        in_specs=[pl.BlockSpec((TM, MLA_RANK), lambda t: (t, 0)),
                  pl.BlockSpec((1, MLA_RANK), lambda t: (0, 0)),
                  pl.BlockSpec(wq_t.shape, lambda t: (0, 0)),
                  pl.BlockSpec((half, TM), lambda t: (0, t % nt)),
                  pl.BlockSpec((half, TM), lambda t: (0, t % nt))],
        out_specs=pl.BlockSpec((1, n, TM), lambda t: (t, 0, 0)),
        out_shape=jax.ShapeDtypeStruct((t_tok // TM, n, TM), BF16),
        compiler_params=_params("parallel"),
        name="mla_q",
    )(p3, gain.reshape(1, MLA_RANK), wq_t, *tables_t)


def _mla_kv_kernel(ckv_ref, kr_ref, g_ref, wk_ref, wvt_ref, cos_ref, sin_ref, k_ref, vt_ref):
    cn = _rms_rows(ckv_ref[...], g_ref[...])
    vt_ref[0] = _dot(wvt_ref[...], jnp.transpose(cn).astype(BF16)).astype(vt_ref.dtype)
    kn = _dot(cn.astype(BF16), wk_ref[...])
    kr = pltpu.roll(kr_ref[...], MLA_NOPE, axis=1)
    half = MLA_ROPE // 2
    lane = lax.broadcasted_iota(jnp.int32, (1, LANES), 1)
    partner = jnp.where(lane < MLA_NOPE + half, pltpu.roll(kr, LANES - half, axis=1), pltpu.roll(kr, half, axis=1))
    roped = kr * cos_ref[...] + partner * sin_ref[...]
    for h in range(MLA_HEADS):
        k_ref[:, h * LANES:(h + 1) * LANES] = (kn[:, h * LANES:(h + 1) * LANES] + roped).astype(k_ref.dtype)


def _mla_kv(p3, gain, w_k, wv_t, tables, nb, seg):
    t_tok = p3.shape[0]
    nt = seg // TM
    return pl.pallas_call(
        _mla_kv_kernel,
        grid=(nb, nt),
        in_specs=[pl.BlockSpec((TM, MLA_RANK), lambda b, j: (b * nt + j, 1)),
                  pl.BlockSpec((TM, LANES), lambda b, j: (b * nt + j, 2 * MLA_RANK // LANES)),
                  pl.BlockSpec((1, MLA_RANK), lambda b, j: (0, 0)),
                  pl.BlockSpec(w_k.shape, lambda b, j: (0, 0)),
                  pl.BlockSpec(wv_t.shape, lambda b, j: (0, 0)),
                  pl.BlockSpec((TM, LANES), lambda b, j: (j, 0)),
                  pl.BlockSpec((TM, LANES), lambda b, j: (j, 0))],
        out_specs=[pl.BlockSpec((TM, MLA_HEADS * LANES), lambda b, j: (b * nt + j, 0)),
                   pl.BlockSpec((1, MLA_HEADS * MLA_V, TM), lambda b, j: (b, 0, j))],
        out_shape=[jax.ShapeDtypeStruct((t_tok, MLA_HEADS * LANES), BF16),
                   jax.ShapeDtypeStruct((nb, MLA_HEADS * MLA_V, seg), BF16)],
        compiler_params=_params("parallel", "parallel"),
        name="mla_kv",
    )(p3, p3, gain.reshape(1, MLA_RANK), w_k, wv_t, *tables)


def _mla_attn_kernel(qt_ref, k_ref, vt_ref, o_ref):
    @pl.when(pl.program_id(2) == 0)
    def _():
        o_ref[...] = jnp.zeros_like(o_ref)

    @pl.when(pl.program_id(2) > 0)
    def _():
        heads = range(MLA_HEADS_PER_STEP)
        seg = k_ref.shape[0]
        tk = MLA_KEY_TILE if (seg - TM) % MLA_KEY_TILE == 0 else TM
        tiles = [(0, TM)] + [(a, a + tk) for a in range(TM, seg, tk)]

        def scores(hh, j):
            return _dot(k_ref[tiles[j][0]:tiles[j][1], hh * LANES:(hh + 1) * LANES],
                        qt_ref[0, hh * LANES:(hh + 1) * LANES, :])

        m = [jnp.full((1, TM), -jnp.inf, F32) for _ in heads]
        l = [jnp.zeros((1, TM), F32) for _ in heads]
        acc = [jnp.zeros((MLA_V, TM), F32) for _ in heads]
        s_next = [scores(hh, 0) for hh in heads]
        for j in range(len(tiles)):
            s_cur = s_next
            if j + 1 < len(tiles):
                s_next = [scores(hh, j + 1) for hh in heads]
            for hh in heads:
                m_new = jnp.maximum(m[hh], jnp.max(s_cur[hh], axis=0, keepdims=True))
                alpha = jnp.exp2(m[hh] - m_new)
                p = jnp.exp2(s_cur[hh] - m_new)
                l[hh] = alpha * l[hh] + jnp.sum(p, axis=0, keepdims=True)
                v_t = vt_ref[0, hh * MLA_V:(hh + 1) * MLA_V, tiles[j][0]:tiles[j][1]]
                acc[hh] = alpha * acc[hh] + _dot(v_t, p.astype(BF16))
                m[hh] = m_new
        o_t = jnp.concatenate([acc[hh] / l[hh] for hh in heads], axis=0)
        o_ref[...] = jnp.transpose(o_t).astype(o_ref.dtype)


def _mla_attention(q_t, k, v_t, nb, seg):
    t_tok = k.shape[0]
    nt = seg // TM
    hps = MLA_HEADS_PER_STEP
    return pl.pallas_call(
        _mla_attn_kernel,
        grid=(nb, MLA_HEADS // hps, nt),
        in_specs=[pl.BlockSpec((1, hps * LANES, TM), lambda b, hp, j: (b * nt + j, hp, 0)),
                  pl.BlockSpec((seg, hps * LANES), lambda b, hp, j: (b, hp)),
                  pl.BlockSpec((1, hps * MLA_V, seg), lambda b, hp, j: (b, hp, 0))],
        out_specs=pl.BlockSpec((TM, hps * MLA_V), lambda b, hp, j: (b * nt + j, hp)),
        out_shape=jax.ShapeDtypeStruct((t_tok, MLA_HEADS * MLA_V), BF16),
        compiler_params=_params("parallel", "parallel", "arbitrary"),
        name="mla_attention",
    )(q_t, k, v_t)


def _final_kernel(x_ref, g_ref, o_ref):
    o_ref[0] = _rms_rows(x_ref[...], g_ref[...])


def _final_norm(x, gain, nb, seg):
    d = x.shape[1]
    nt = seg // TM
    return pl.pallas_call(
        _final_kernel,
        grid=(nb, nt - 1),
        in_specs=[pl.BlockSpec((TM, d), lambda b, j: (b * nt + 1 + j, 0)),
                  pl.BlockSpec((1, d), lambda b, j: (0, 0))],
        out_specs=pl.BlockSpec((1, TM, d), lambda b, j: (b, j, 0)),
        out_shape=jax.ShapeDtypeStruct((nb, seg - TM, d), F32),
        compiler_params=_params("parallel", "parallel"),
        name="final_norm",
    )(x, gain.reshape(1, d))


def _rope_angles(n_lat, rot_dim):
    rows = n_lat // GRID_W
    row = jnp.repeat(jnp.arange(rows), GRID_W).astype(F32)
    col = jnp.tile(jnp.arange(GRID_W), rows).astype(F32)
    quarter = rot_dim // 4
    inv = ROPE_BASE ** (-jnp.arange(quarter, dtype=F32) / quarter)
    ang = jnp.concatenate([row[:, None] * inv, col[:, None] * inv], axis=-1)
    return jnp.cos(ang), jnp.sin(ang)


def _with_ctx_rows(tab, fill):
    return jnp.concatenate([jnp.full((TM, tab.shape[1]), fill, F32), tab], axis=0)


def _attn_rope_tables(n_lat):
    cos, sin = _rope_angles(n_lat, ATT_HEAD_DIM)
    cos_h = jnp.concatenate([cos, cos], axis=1)
    sin_h = jnp.concatenate([-sin, sin], axis=1)
    reps = LANES // ATT_HEAD_DIM
    return (_with_ctx_rows(jnp.tile(cos_h, (1, reps)), 1.0), _with_ctx_rows(jnp.tile(sin_h, (1, reps)), 0.0))


def _swap_halves(w, n_heads, dim):
    w3 = w.reshape(w.shape[0], n_heads, 2, dim // 2)
    return w3[:, :, ::-1, :].reshape(w.shape[0], n_heads * dim)


def _mla_k_tables(n_lat):
    cos, sin = _rope_angles(n_lat, MLA_ROPE)
    lo = jnp.zeros((n_lat, MLA_NOPE), F32)
    hi = jnp.zeros((n_lat, LANES - MLA_NOPE - MLA_ROPE), F32)
    cos_c = jnp.concatenate([lo, cos, cos, hi], axis=1)
    sin_c = jnp.concatenate([lo, -sin, sin, hi], axis=1)
    ctx_cos = jnp.concatenate([lo[:TM], jnp.ones((TM, MLA_ROPE), F32), hi[:TM]], axis=1)
    return (jnp.concatenate([ctx_cos, cos_c], axis=0), _with_ctx_rows(sin_c, 0.0))


def _pad_heads(w, real):
    r = w.shape[0]
    w3 = w.reshape(r, MLA_HEADS, real)
    return jnp.pad(w3, ((0, 0), (0, 0), (0, LANES - real))).reshape(r, MLA_HEADS * LANES)


def kernel(x, c, ctx, c_ctx, norm1_g, norm2_g, w_mod, b_mod, moe_w_group, moe_b_group, moe_w_expert, moe_b_expert, moe_w_gate, moe_w_up, moe_w_down, attn_w_in, attn_sink, attn_w_out, ssm_w_in, ssm_conv_w, ssm_conv_b, ssm_dt_bias, ssm_a_log, ssm_d, ssm_norm_g, ssm_w_out, mlstm_w_in, mlstm_gate_b, mlstm_norm_g, mlstm_w_out, mla_w_in, mla_q_norm_g, mla_w_q_up, mla_kv_norm_g, mla_w_kv_up, mla_w_out, final_norm_g):
    nb, n_lat, d = x.shape
    assert ctx.shape[1] == TM and d == D_MODEL and n_lat % TM == 0
    depth = w_mod.shape[0]
    seg = TM + n_lat
    nt = seg // TM
    t_tok = nb * seg

    xs = (x, ctx)

    rows = -(-(nb + 1) // 8) * 8
    cvec = jnp.concatenate([c, c_ctx[None, :], jnp.zeros((rows - nb - 1, d), F32)], axis=0)
    mods = _modulation(cvec, w_mod, b_mod).reshape(depth, rows, ADALN_CHUNKS, d)
    mods = jnp.pad(mods, ((0, 0), (0, 0), (0, MOD_ROWS - ADALN_CHUNKS), (0, 0)))

    w_router = jnp.concatenate([moe_w_expert, moe_w_group,
                                jnp.zeros((depth, d, LANES - MOE_EXPERTS - MOE_GROUPS), F32)], axis=-1)
    b_router = jnp.concatenate([moe_b_expert, moe_b_group,
                                jnp.zeros((depth, LANES - MOE_EXPERTS - MOE_GROUPS), F32)], axis=-1)
    w_gate, w_up = moe_w_gate.astype(BF16), moe_w_up.astype(BF16)
    w_down = moe_w_down.astype(BF16).reshape(depth, MOE_GROUPS, MOE_PER_GROUP * MOE_FF, d)

    for i in range(depth):
        kind = i % 4
        mod_i = mods[i]
        if kind == 0:
            nq, nk = ATT_HEADS * ATT_HEAD_DIM, ATT_KV_HEADS * ATT_HEAD_DIM
            w_in = attn_w_in[i // 4]
            w_all = jnp.concatenate([w_in, _swap_halves(w_in[:, :nq], ATT_HEADS, ATT_HEAD_DIM),
                                     _swap_halves(w_in[:, nq:nq + nk], ATT_KV_HEADS, ATT_HEAD_DIM)],
                                    axis=1).astype(BF16)
            n_in = w_in.shape[1]
            q, k, v = _normproj(xs, mod_i, norm1_g[i], w_all,
                                [(0, nq, n_in, ATT_HEAD_DIM ** -0.5 * LOG2_E), (nq, nk, n_in + nq), (nq + nk, nk, None)],
                                [BF16, BF16, BF16], nt, nb, tables=_attn_rope_tables(n_lat))
            o = _windowed_attention(q, k, v, attn_sink[i // 4], nb, seg)
            xs = _outproj([o], [nq], xs, mod_i, attn_w_out[i // 4].astype(BF16), _plain_prologue, nt, nb)
        elif kind == 1:
            j = i // 4
            w_in = jnp.pad(ssm_w_in[j], ((0, 0), (0, LANES - 2 * SSM_HEADS))).astype(BF16)
            z, xbc, dt = _normproj(xs, mod_i, norm1_g[i], w_in,
                                   [(0, SSM_D_INNER, None), (SSM_D_INNER, SSM_CONV_DIM, None),
                                    (SSM_D_INNER + SSM_CONV_DIM, LANES, None)],
                                   [BF16, BF16, F32], nt, nb)
            lane_pad = LANES - 2 * SSM_HEADS
            dt_bias = jnp.pad(ssm_dt_bias[j].reshape(1, -1), ((0, 0), (0, lane_pad)))
            a_neg = jnp.pad(-jnp.exp(ssm_a_log[j].astype(F32)).reshape(1, -1), ((0, 0), (0, lane_pad)))
            conv_w = jnp.pad(ssm_conv_w[j], ((0, 8 - SSM_CONV), (0, 0)))
            xc, bm, cm, dtv, a = _ssm_conv(xbc, dt, conv_w, ssm_conv_b[j].reshape(1, -1), dt_bias, a_neg, nt)
            yf, yb = _ssd_scan(xc, bm, cm, dtv, a, ssm_d[j].astype(F32), nb, seg)
            xs = _outproj([z], [SSM_D_INNER], xs, mod_i, ssm_w_out[j].astype(BF16), _ssm_prologue,
                          nt, nb, extra=(ssm_norm_g[j].reshape(1, -1),), ins_t=(yf, yb))
        elif kind == 2:
            j = i // 4
            nqk, nv = 2 * ML_HEADS * ML_QK_DIM, ML_HEADS * ML_V_DIM
            w_in = jnp.pad(mlstm_w_in[j], ((0, 0), (0, LANES - 4 * ML_HEADS))).astype(BF16)
            qk, v, o, g = _normproj(xs, mod_i, norm1_g[i], w_in,
                                    [(0, nqk, None), (nqk, nv, None), (nqk + nv, nv, None), (nqk + 2 * nv, LANES, None)],
                                    [BF16, BF16, BF16, F32], nt, nb)
            gate_b = jnp.pad(mlstm_gate_b[j].reshape(1, -1), ((0, 0), (0, LANES - 4 * ML_HEADS)))
            hf, hb = _mlstm_scan(qk, v, g, gate_b, nb, seg)
            xs = _outproj([o], [nv], xs, mod_i, mlstm_w_out[j].astype(BF16), _mlstm_prologue,
                          nt, nb, extra=(mlstm_norm_g[j].reshape(1, -1),), ins_t=(hf, hb))
        else:
            j = i // 4
            w_in = jnp.pad(mla_w_in[j], ((0, 0), (0, LANES - MLA_ROPE))).astype(BF16)
            (p3,) = _normproj(xs, mod_i, norm1_g[i], w_in, [(0, w_in.shape[1], None)], [F32], nt, nb)
            cos, sin = _rope_angles(n_lat, MLA_ROPE)
            tables_t = (jnp.transpose(_with_ctx_rows(cos, 1.0)), jnp.transpose(_with_ctx_rows(sin, 0.0)))
            wq_t = jnp.transpose(_pad_heads(mla_w_q_up[j], MLA_NOPE + MLA_ROPE)).astype(BF16)
            q_t = _mla_q(p3, mla_q_norm_g[j], wq_t, tables_t, nt)
            w_kv = mla_w_kv_up[j].reshape(MLA_RANK, MLA_HEADS, MLA_NOPE + MLA_V)
            w_k = _pad_heads(w_kv[:, :, :MLA_NOPE].reshape(MLA_RANK, MLA_HEADS * MLA_NOPE), MLA_NOPE).astype(BF16)
            wv_t = jnp.transpose(w_kv[:, :, MLA_NOPE:].reshape(MLA_RANK, MLA_HEADS * MLA_V)).astype(BF16)
            k, v_t = _mla_kv(p3, mla_kv_norm_g[j], w_k, wv_t, _mla_k_tables(n_lat), nb, seg)
            o = _mla_attention(q_t, k, v_t, nb, seg)
            xs = _outproj([o], [MLA_HEADS * MLA_V], xs, mod_i, mla_w_out[j].astype(BF16), _plain_prologue, nt, nb)

        xs = _moe(xs, mod_i, norm2_g[i], w_router[i], b_router[i].reshape(1, -1), w_gate, w_up, w_down, i, nt, nb)

    return _final_norm(xs, final_norm_g, nb, seg)
```

```python
import functools
import math

import jax
import jax.numpy as jnp
from jax import lax
from jax.experimental import pallas as pl
from jax.experimental.pallas import tpu as pltpu

F32 = jnp.float32
BF16 = jnp.bfloat16

D_MODEL = 1024
GRID_W = 64
EPS = 1e-6
ROPE_BASE = 10000.0
ADALN_CHUNKS = 6
CHUNK = 128
SSD_CHUNK = 256
ML_CHUNK = 128
TM = 256
MOD_ROWS = 8
LANES = 128
V7X_VMEM_LIMIT = 48 * 1024 * 1024

ATT_HEADS, ATT_KV_HEADS, ATT_HEAD_DIM, WINDOW = 16, 4, 64, 128
ATT_GROUP = ATT_HEADS // ATT_KV_HEADS
SSM_D_INNER, SSM_HEAD_DIM, SSM_HEADS, SSM_GROUPS, SSM_STATE, SSM_CONV = 2048, 64, 32, 4, 128, 5
SSM_HPG = SSM_HEADS // SSM_GROUPS
SSM_BC = SSM_GROUPS * SSM_STATE
SSM_CONV_DIM = SSM_D_INNER + 2 * SSM_BC
ML_HEADS, ML_QK_DIM, ML_V_DIM = 8, 64, 128
MLA_HEADS, MLA_RANK, MLA_NOPE, MLA_ROPE, MLA_V = 16, 256, 64, 32, 64
MOE_GROUPS, MOE_PER_GROUP, MOE_EXPERTS, MOE_FF = 4, 4, 16, 256
MOE_ROWS = 1024
MLA_KEY_TILE = 512
MLA_HEADS_PER_STEP = 4
MLA_LOOKAHEAD = 1
LOG2_E = 1.4426950408889634


def _dot(a, b):
    return jnp.dot(a, b, preferred_element_type=F32)


def _dot_nt(a, b):
    return lax.dot_general(a, b, (((1,), (1,)), ((), ())), preferred_element_type=F32)


def _split_bf16(x):
    hi = x.astype(BF16)
    lo = (x - hi.astype(F32)).astype(BF16)
    return hi, lo


def _dot_split(a, b):
    a_hi, a_lo = _split_bf16(a)
    b_hi, b_lo = _split_bf16(b)
    return _dot(a_hi, b_hi) + _dot(a_lo, b_hi) + _dot(a_hi, b_lo)


def _sigmoid(x):
    return 1.0 / (1.0 + jnp.exp(-x))


def _silu(x):
    return x * _sigmoid(x)


def _softplus(x):
    return jnp.maximum(x, 0.0) + jnp.log1p(jnp.exp(-jnp.abs(x)))


def _params(*sem):
    return pltpu.CompilerParams(dimension_semantics=sem, vmem_limit_bytes=V7X_VMEM_LIMIT)


def _mod_row(t, nt, nb):
    return jnp.where(t % nt == 0, nb, t // nt)


def _mod_kernel(c_ref, w_ref, b_ref, o_ref):
    o_ref[0] = _dot_split(_silu(c_ref[...]), w_ref[0]) + b_ref[0]


def _modulation(cvec, w_mod, b_mod):
    depth, d, n = w_mod.shape
    tn = 1536
    rows = cvec.shape[0]
    return pl.pallas_call(
        _mod_kernel,
        grid=(depth, n // tn),
        in_specs=[pl.BlockSpec((rows, d), lambda l, j: (0, 0)),
                  pl.BlockSpec((1, d, tn), lambda l, j: (l, 0, j)),
                  pl.BlockSpec((1, 1, tn), lambda l, j: (l, 0, j))],
        out_specs=pl.BlockSpec((1, rows, tn), lambda l, j: (l, 0, j)),
        out_shape=jax.ShapeDtypeStruct((depth, rows, n), F32),
        compiler_params=_params("arbitrary", "arbitrary"),
        name="modulation",
    )(cvec, w_mod, b_mod.reshape(depth, 1, n))


def _normed(x, g_row, mod, sh_row, sc_row):
    y = x * lax.rsqrt(jnp.mean(x * x, axis=-1, keepdims=True) + EPS) * g_row
    return y * (1.0 + mod[sc_row:sc_row + 1, :]) + mod[sh_row:sh_row + 1, :]


def _proj_columns(hb, w_ref, o_ref, start, width, rot_start, cos, sin, scale=None, chunk=512):
    for c in range(0, width, chunk):
        cw = min(chunk, width - c)
        acc = _dot(hb, w_ref[:, start + c:start + c + cw])
        if rot_start is not None:
            rot = _dot(hb, w_ref[:, rot_start + c:rot_start + c + cw])
            reps = cw // LANES
            acc = acc * jnp.tile(cos, (1, reps)) + rot * jnp.tile(sin, (1, reps))
        if scale is not None:
            acc = acc * scale
        o_ref[:, c:c + cw] = acc.astype(o_ref.dtype)


def _stream_specs(x, nt):
    if isinstance(x, tuple):
        d = x[0].shape[-1]
        return [pl.BlockSpec((1, TM, d), lambda t: (t // nt, jnp.maximum(t % nt - 1, 0), 0)),
                pl.BlockSpec((1, TM, d), lambda t: (t // nt, 0, 0))], list(x)
    sub = _tiles_per_step(x, nt)
    return [pl.BlockSpec((sub * TM, x.shape[1]), lambda t: (t, 0))], [x]


def _tiles_per_step(x, nt):
    if isinstance(x, tuple):
        return 1
    return 2 if (x.shape[0] // TM) % 2 == 0 else 1


def _stream_tile(x_refs, nt, s):
    if len(x_refs) == 1:
        return x_refs[0][s * TM:(s + 1) * TM, :]
    return jnp.where(pl.program_id(0) % nt == 0, x_refs[1][0], x_refs[0][0])


def _normproj_kernel(*refs, outs, has_rope, n_x, nt, nb, sub):
    x_refs, refs = refs[:n_x], refs[n_x:]
    mods_ref, g_ref, w_ref = refs[:3]
    k = 3
    cos = sin = None
    if has_rope:
        cos, sin = refs[3][...], refs[4][...]
        k = 5
    hs = []
    for s in range(sub):
        mod = mods_ref[_mod_row(pl.program_id(0) * sub + s, nt, nb)]
        hs.append(_normed(_stream_tile(x_refs, nt, s), g_ref[...], mod, 0, 1).astype(BF16))
    hb = hs[0] if sub == 1 else jnp.concatenate(hs, axis=0)
    for o_ref, spec in zip(refs[k:], outs):
        _proj_columns(hb, w_ref, o_ref, *spec[:3], cos, sin, scale=spec[3] if len(spec) > 3 else None)


def _normproj(x, mods, gain, w, outs, out_dtypes, nt, nb, tables=None):
    d, n = w.shape
    t_tok = nb * nt * TM
    sub = _tiles_per_step(x, nt)
    assert tables is None or sub == 1
    rows = sub * TM
    x_specs, x_args = _stream_specs(x, nt)
    in_specs = x_specs + [pl.BlockSpec(mods.shape, lambda t: (0, 0, 0)),
                          pl.BlockSpec((1, d), lambda t: (0, 0)),
                          pl.BlockSpec((d, n), lambda t: (0, 0))]
    args = x_args + [mods, gain.reshape(1, d), w]
    if tables is not None:
        in_specs += [pl.BlockSpec((TM, LANES), lambda t: (t % nt, 0))] * 2
        args += list(tables)
    return pl.pallas_call(
        functools.partial(_normproj_kernel, outs=tuple(outs), has_rope=tables is not None, n_x=len(x_args),
                          nt=nt, nb=nb, sub=sub),
        grid=(t_tok // rows,),
        in_specs=in_specs,
        out_specs=[pl.BlockSpec((rows, o[1]), lambda t: (t, 0)) for o in outs],
        out_shape=[jax.ShapeDtypeStruct((t_tok, o[1]), dt) for o, dt in zip(outs, out_dtypes)],
        compiler_params=_params("parallel"),
        name="normproj",
    )(*args)


def _outproj_kernel(*refs, prologue, n_t, n_tok, n_extra, n_x, nt, nb, sub):
    ins_t = refs[:n_t * sub]
    ins = refs[n_t * sub:n_t * sub + n_tok]
    extra = refs[n_t * sub + n_tok:n_t * sub + n_tok + n_extra]
    x_refs = refs[n_t * sub + n_tok + n_extra:n_t * sub + n_tok + n_extra + n_x]
    mods_ref, w_ref, o_ref = refs[n_t * sub + n_tok + n_extra + n_x:]
    parts = []
    for s in range(sub):
        tok = [r if sub == 1 else r.at[pl.ds(s * TM, TM), :] for r in ins]
        parts.append(prologue(*ins_t[s * n_t:(s + 1) * n_t], *tok, *extra))
    a = parts[0] if sub == 1 else jnp.concatenate(parts, axis=0)
    y = _dot(a, w_ref[...])
    for s in range(sub):
        rows = slice(s * TM, (s + 1) * TM)
        gate = mods_ref[_mod_row(pl.program_id(0) * sub + s, nt, nb)][2:3, :]
        o_ref[rows, :] = _stream_tile(x_refs, nt, s) + gate * y[rows, :]


def _outproj(ins, in_widths, x, mods, w, prologue, nt, nb, extra=(), ins_t=()):
    d = w.shape[1]
    t_tok = nb * nt * TM
    sub = _tiles_per_step(x, nt)
    rows = sub * TM

    def tile_map(s):
        return lambda t: ((t * sub + s) // nt, 0, (t * sub + s) % nt)

    in_specs, args = [], []
    for s in range(sub):
        in_specs += [pl.BlockSpec((1, a.shape[1], TM), tile_map(s)) for a in ins_t]
        args += list(ins_t)
    in_specs += [pl.BlockSpec((rows, wd), lambda t: (t, 0)) for wd in in_widths]
    in_specs += [pl.BlockSpec(e.shape, lambda t: (0, 0)) for e in extra]
    x_specs, x_args = _stream_specs(x, nt)
    n_before_x = len(in_specs)
    in_specs += x_specs + [pl.BlockSpec(mods.shape, lambda t: (0, 0, 0)), pl.BlockSpec(w.shape, lambda t: (0, 0))]
    return pl.pallas_call(
        functools.partial(_outproj_kernel, prologue=prologue, n_t=len(ins_t), n_tok=len(ins), n_extra=len(extra),
                          n_x=len(x_args), nt=nt, nb=nb, sub=sub),
        grid=(t_tok // rows,),
        in_specs=in_specs,
        out_specs=pl.BlockSpec((rows, d), lambda t: (t, 0)),
        out_shape=jax.ShapeDtypeStruct((t_tok, d), F32),
        input_output_aliases={} if isinstance(x, tuple) else {n_before_x: 0},
        compiler_params=_params("parallel"),
        name="outproj",
    )(*args, *ins, *extra, *x_args, mods, w)


def _group_rms(y, n_groups):
    width = y.shape[1] // n_groups
    parts = []
    for g in range(n_groups):
        yg = y[:, g * width:(g + 1) * width]
        parts.append(yg * lax.rsqrt(jnp.mean(yg * yg, axis=-1, keepdims=True) + EPS))
    return jnp.concatenate(parts, axis=1)


def _plain_prologue(o_ref):
    return o_ref[...]


def _ssm_prologue(yf_ref, yb_ref, z_ref, g_ref):
    y = jnp.transpose(yf_ref[0].astype(F32) + yb_ref[0].astype(F32)) * _silu(z_ref[...].astype(F32))
    return (_group_rms(y, SSM_GROUPS) * g_ref[...]).astype(BF16)


def _mlstm_prologue(hf_ref, hb_ref, o_ref, g_ref):
    h = jnp.transpose(hf_ref[0].astype(F32) + hb_ref[0].astype(F32))
    return (_group_rms(h, ML_HEADS) * g_ref[...] * _sigmoid(o_ref[...].astype(F32))).astype(BF16)


MOE_ROUTE_ROWS = 24


def _route(logits):
    lt = jnp.transpose(logits)[:MOE_ROUTE_ROWS, :]
    row = lax.broadcasted_iota(jnp.int32, lt.shape, 0).astype(F32)
    neg = -jnp.inf
    lg = jnp.where((row >= MOE_EXPERTS) & (row < MOE_EXPERTS + MOE_GROUPS), lt, neg)
    gmax = jnp.max(lg, axis=0, keepdims=True)
    g_sel = jnp.min(jnp.where(lg == gmax, row, LANES), axis=0, keepdims=True) - MOE_EXPERTS
    p_g = 1.0 / jnp.sum(jnp.exp(lg - gmax), axis=0, keepdims=True)
    in_group = (row >= g_sel * MOE_PER_GROUP) & (row < (g_sel + 1) * MOE_PER_GROUP)
    le = jnp.where(in_group, lt, neg)
    v1 = jnp.max(le, axis=0, keepdims=True)
    i1 = jnp.min(jnp.where(le == v1, row, LANES), axis=0, keepdims=True)
    le2 = jnp.where(row == i1, neg, le)
    v2 = jnp.max(le2, axis=0, keepdims=True)
    i2 = jnp.min(jnp.where(le2 == v2, row, LANES), axis=0, keepdims=True)
    e2 = jnp.exp(v2 - v1)
    w1 = p_g / (1.0 + e2)
    comb_t = jnp.where(row == i1, w1, 0.0) + jnp.where(row == i2, w1 * e2, 0.0)
    pad = jnp.zeros((LANES - MOE_ROUTE_ROWS, lt.shape[1]), F32)
    return jnp.transpose(jnp.concatenate([comb_t, pad], axis=0))


def _moe_kernel(x_ref, mods_ref, g_ref, wr_ref, br_ref, wg_ref, wu_ref, wd_ref, o_ref, h_scr, comb_scr,
                *, nt, nb, sub):
    i = pl.program_id(0)
    e = pl.program_id(1)

    def experts(rows):
        hb = h_scr[rows, :]
        comb = comb_scr[rows, :]
        lane = lax.broadcasted_iota(jnp.int32, comb.shape, 1)
        scaled = []
        for k in range(MOE_PER_GROUP):
            act = _silu(_dot(hb, wg_ref[0, k])) * _dot(hb, wu_ref[0, k])
            cw = jnp.sum(jnp.where(lane == e * MOE_PER_GROUP + k, comb, 0.0), axis=-1, keepdims=True)
            scaled.append((act * cw).astype(BF16))
        return _dot(jnp.concatenate(scaled, axis=1), wd_ref[0, 0])

    def gate_row(s):
        return mods_ref[_mod_row(i * sub + s, nt, nb)][5:6, :]

    @pl.when(e == 0)
    def _():
        w_hi, w_lo = _split_bf16(wr_ref[...])
        w_both = jnp.concatenate([w_hi, w_lo], axis=1)
        for s in range(sub):
            rows = slice(s * TM, (s + 1) * TM)
            mod = mods_ref[_mod_row(i * sub + s, nt, nb)]
            x = x_ref[rows, :]
            h = _normed(x, g_ref[...], mod, 3, 4)
            h_hi, h_lo = _split_bf16(h)
            h_scr[rows, :] = h_hi
            both = _dot(h_hi, w_both)
            logits = both[:, :LANES] + both[:, LANES:] + _dot(h_lo, w_hi) + br_ref[...]
            comb_scr[rows, :] = _route(logits)
            o_ref[rows, :] = x + gate_row(s) * experts(rows)

    @pl.when(e > 0)
    def _():
        y = experts(slice(0, sub * TM))
        for s in range(sub):
            rows = slice(s * TM, (s + 1) * TM)
            o_ref[rows, :] += gate_row(s) * y[rows, :]


def _moe(x, mods, gain, w_router, b_router, w_gate, w_up, w_down, layer, nt, nb):
    t_tok, d = x.shape
    pg = MOE_PER_GROUP
    rows = MOE_ROWS if t_tok % MOE_ROWS == 0 else TM
    sub = rows // TM
    return pl.pallas_call(
        functools.partial(_moe_kernel, nt=nt, nb=nb, sub=sub),
        grid=(t_tok // rows, MOE_GROUPS),
        in_specs=[pl.BlockSpec((rows, d), lambda i, e: (i, 0)),
                  pl.BlockSpec(mods.shape, lambda i, e: (0, 0, 0)),
                  pl.BlockSpec((1, d), lambda i, e: (0, 0)),
                  pl.BlockSpec((d, LANES), lambda i, e: (0, 0)),
                  pl.BlockSpec((1, LANES), lambda i, e: (0, 0)),
                  pl.BlockSpec((1, pg, d, MOE_FF), lambda i, e: (layer, e, 0, 0)),
                  pl.BlockSpec((1, pg, d, MOE_FF), lambda i, e: (layer, e, 0, 0)),
                  pl.BlockSpec((1, 1, pg * MOE_FF, d), lambda i, e: (layer, e, 0, 0))],
        out_specs=pl.BlockSpec((rows, d), lambda i, e: (i, 0)),
        out_shape=jax.ShapeDtypeStruct((t_tok, d), F32),
        scratch_shapes=[pltpu.VMEM((rows, d), BF16), pltpu.VMEM((rows, LANES), F32)],
        input_output_aliases={0: 0},
        compiler_params=_params("parallel", "arbitrary"),
        name="moe",
    )(x, mods, gain.reshape(1, d), w_router, b_router, w_gate, w_up, w_down)


def _attn_kernel(sink_ref, q_ref, kp_ref, kc_ref, kn_ref, vp_ref, vc_ref, vn_ref, kx_ref, vx_ref, o_ref, *, n_lat):
    j = pl.program_id(1)
    jl = j - 1
    dh = ATT_HEAD_DIM
    n_loc = TM + 2 * CHUNK
    k_all = jnp.concatenate([kp_ref[...], kc_ref[...], kn_ref[...], kx_ref[...]], axis=0)
    v_t = jnp.transpose(jnp.concatenate([vp_ref[...], vc_ref[...], vn_ref[...], vx_ref[...]],
                                        axis=0).astype(F32)).astype(BF16)
    q_t = jnp.transpose(q_ref[...].astype(F32)).astype(BF16)
    n_keys = k_all.shape[0]
    kj = lax.broadcasted_iota(jnp.int32, (n_keys, TM), 0)
    qi = lax.broadcasted_iota(jnp.int32, (n_keys, TM), 1)
    kpos = jl * TM - CHUNK + kj
    local_ok = (kj >= qi + CHUNK - WINDOW) & (kj <= qi + CHUNK + WINDOW) & (kpos >= 0) & (kpos < n_lat) & (jl >= 0)
    bias = jnp.where((kj >= n_loc) | local_ok, 0.0, -jnp.inf)
    bias = jnp.concatenate([bias] * ATT_GROUP, axis=1)
    lane_head = lax.broadcasted_iota(jnp.int32, (1, ATT_GROUP * TM), 1) // TM
    zeros_q = jnp.zeros((dh, ATT_GROUP * TM), BF16)
    def scores(g):
        half = g % 2
        k_pair = k_all[:, (g - half) * dh:(g - half + 2) * dh]
        qg = jnp.concatenate([q_t[(g * ATT_GROUP + a) * dh:(g * ATT_GROUP + a + 1) * dh, :]
                              for a in range(ATT_GROUP)], axis=1)
        q_m = jnp.concatenate([zeros_q, qg] if half else [qg, zeros_q], axis=0)
        return _dot(k_pair, q_m) + bias

    outs = []
    s_next = scores(0)
    for g in range(ATT_KV_HEADS):
        s = s_next
        if g + 1 < ATT_KV_HEADS:
            s_next = scores(g + 1)
        sink = jnp.zeros((1, ATT_GROUP * TM), F32)
        for a in range(ATT_GROUP):
            sink = jnp.where(lane_head == a, sink_ref[g * ATT_GROUP + a] * LOG2_E, sink)
        m = jnp.maximum(jnp.max(s, axis=0, keepdims=True), sink)
        p = jnp.exp2(s - m)
        den = jnp.sum(p, axis=0, keepdims=True) + jnp.exp2(sink - m)
        og = _dot(v_t[g * dh:(g + 1) * dh, :], p.astype(BF16)) / den
        outs += [og[:, a * TM:(a + 1) * TM] for a in range(ATT_GROUP)]
    o_ref[...] = jnp.transpose(jnp.concatenate(outs, axis=0)).astype(o_ref.dtype)


def _windowed_attention(q, k, v, sink, nb, seg):
    t_tok = q.shape[0]
    nchunk = seg // CHUNK
    nt = seg // TM
    per_tile = TM // CHUNK
    kvw = ATT_KV_HEADS * ATT_HEAD_DIM

    def halo(off):
        return lambda b, j, *_: (b * nchunk + jnp.clip(j * per_tile + off, per_tile, nchunk - 1), 0)

    tile = lambda b, j, *_: (b * nt + j, 0)
    kv_spec = [pl.BlockSpec((CHUNK, kvw), halo(-1)), pl.BlockSpec((TM, kvw), tile),
               pl.BlockSpec((CHUNK, kvw), halo(per_tile))]
    ctx_spec = pl.BlockSpec((TM, kvw), lambda b, j, *_: (b * nt, 0))
    return pl.pallas_call(
        functools.partial(_attn_kernel, n_lat=seg - TM),
        grid_spec=pltpu.PrefetchScalarGridSpec(
            num_scalar_prefetch=1,
            grid=(nb, nt),
            in_specs=[pl.BlockSpec((TM, q.shape[1]), tile)] + kv_spec + kv_spec + [ctx_spec, ctx_spec],
            out_specs=pl.BlockSpec((TM, q.shape[1]), tile),
        ),
        out_shape=jax.ShapeDtypeStruct((t_tok, q.shape[1]), BF16),
        compiler_params=_params("parallel", "parallel"),
        name="windowed_attention",
    )(sink, q, k, k, k, v, v, v, k, v)


def _conv_kernel(cur_ref, prev_ref, next_ref, dt_ref, w_ref, b_ref, dtb_ref, aneg_ref,
                 xs_ref, bm_ref, cm_ref, dtv_ref, a_ref, ext, *, nt):
    t = pl.program_id(0)
    tl = t % nt
    halo = 8
    first = (tl == 0) | (tl == 1)
    last = (tl == 0) | (tl == nt - 1)
    ext[0:halo, :] = jnp.where(first, 0.0, prev_ref[...].astype(F32)[halo:2 * halo, :])
    ext[halo:halo + TM, :] = cur_ref[...].astype(F32)
    ext[halo + TM:2 * halo + TM, :] = jnp.where(last, 0.0, next_ref[...].astype(F32)[0:halo, :])
    pad = SSM_CONV // 2
    chunk = 512
    for c in range(0, SSM_CONV_DIM, chunk):
        acc = jnp.broadcast_to(b_ref[:, c:c + chunk], (TM, chunk))
        for k in range(SSM_CONV):
            acc = acc + w_ref[k:k + 1, c:c + chunk] * ext[halo - pad + k:halo - pad + k + TM, c:c + chunk]
        y = _silu(acc)
        if c < SSM_D_INNER:
            xs_ref[0, c:c + chunk, :] = jnp.transpose(y).astype(xs_ref.dtype)
        elif c < SSM_D_INNER + SSM_BC:
            bm_ref[:, c - SSM_D_INNER:c - SSM_D_INNER + chunk] = y.astype(bm_ref.dtype)
        else:
            off = c - SSM_D_INNER - SSM_BC
            cm_ref[0, off:off + chunk, :] = jnp.transpose(y).astype(cm_ref.dtype)
    dtv = _softplus(dt_ref[...] + dtb_ref[...])
    dtv_ref[0] = jnp.transpose(dtv)
    a_ref[0] = jnp.transpose(dtv * aneg_ref[...] * LOG2_E)


def _ssm_conv(xbc, dt, conv_w, conv_b, dt_bias, a_neg, nt):
    t_tok = xbc.shape[0]
    n16 = t_tok // 16
    seg = nt * TM
    nb = t_tok // seg
    row = lambda t: (t, 0)
    col = lambda t: (t // nt, 0, t % nt)
    fixed = lambda t: (0, 0)
    return pl.pallas_call(
        functools.partial(_conv_kernel, nt=nt),
        grid=(t_tok // TM,),
        in_specs=[pl.BlockSpec((TM, SSM_CONV_DIM), row),
                  pl.BlockSpec((16, SSM_CONV_DIM), lambda t: (jnp.maximum(t * (TM // 16) - 1, 0), 0)),
                  pl.BlockSpec((16, SSM_CONV_DIM), lambda t: (jnp.minimum((t + 1) * (TM // 16), n16 - 1), 0)),
                  pl.BlockSpec((TM, LANES), row),
                  pl.BlockSpec((8, SSM_CONV_DIM), fixed),
                  pl.BlockSpec((1, SSM_CONV_DIM), fixed),
                  pl.BlockSpec((1, LANES), fixed),
                  pl.BlockSpec((1, LANES), fixed)],
        out_specs=[pl.BlockSpec((1, SSM_D_INNER, TM), col), pl.BlockSpec((TM, SSM_BC), row),
                   pl.BlockSpec((1, SSM_BC, TM), col), pl.BlockSpec((1, LANES, TM), col),
                   pl.BlockSpec((1, LANES, TM), col)],
        out_shape=[jax.ShapeDtypeStruct((nb, SSM_D_INNER, seg), BF16), jax.ShapeDtypeStruct((t_tok, SSM_BC), BF16),
                   jax.ShapeDtypeStruct((nb, SSM_BC, seg), BF16), jax.ShapeDtypeStruct((nb, LANES, seg), F32),
                   jax.ShapeDtypeStruct((nb, LANES, seg), F32)],
        scratch_shapes=[pltpu.VMEM((TM + 16, SSM_CONV_DIM), F32)],
        compiler_params=_params("parallel"),
        name="ssm_conv",
    )(xbc, xbc, xbc, dt, conv_w, conv_b, dt_bias, a_neg)


def _tri(lower, n):
    r = lax.broadcasted_iota(jnp.int32, (n, n), 0)
    c = lax.broadcasted_iota(jnp.int32, (n, n), 1)
    return (c <= r) if lower else (c >= r)


def _ssd_direction(xs_ref, bm_ref, ct_ref, dt_ref, a_ref, dsk_ref, st_ref, y_ref, d):
    a_t = a_ref[0]
    dt_t = dt_ref[0]
    upper = jnp.where(_tri(False, SSD_CHUNK), 1.0, 0.0).astype(BF16)
    a_hi, a_lo = _split_bf16(a_t)
    cum = _dot(a_hi, upper) + _dot(a_lo, upper)
    cum_end = cum[:, SSD_CHUNK - 1:SSD_CHUNK]
    if d == 0:
        lane_v, sub_v = cum, -cum
        inter = jnp.exp2(cum)
        w_upd = jnp.exp2(cum_end - cum) * dt_t
    else:
        ecum = cum - a_t
        lane_v, sub_v = -ecum, ecum
        inter = jnp.exp2(cum_end - ecum)
        w_upd = jnp.exp2(ecum) * dt_t
    sub_c = jnp.transpose(sub_v)
    decay_end = jnp.exp2(cum_end)
    mask = _tri(d == 1, SSD_CHUNK)
    p = SSM_HEAD_DIM
    for g in range(SSM_GROUPS):
        bg = bm_ref[:, g * SSM_STATE:(g + 1) * SSM_STATE]
        cg_t = ct_ref[0, g * SSM_STATE:(g + 1) * SSM_STATE, :]
        cb_t = _dot(bg, cg_t)
        state = st_ref[g]
        y_in = _dot(state.astype(BF16), cg_t)
        upd = []
        for e in range(SSM_HPG):
            h = g * SSM_HPG + e
            ln = d * SSM_HEADS + h
            rows = slice(e * p, (e + 1) * p)
            seg = jnp.where(mask, sub_c[:, ln:ln + 1] + lane_v[ln:ln + 1, :], -jnp.inf)
            m_t = (cb_t * jnp.exp2(seg)).astype(BF16)
            xf = xs_ref[0, h * p:(h + 1) * p, :].astype(F32)
            u = (xf * dt_t[ln:ln + 1, :]).astype(BF16)
            yh = _dot(u, m_t) + y_in[rows, :] * inter[ln:ln + 1, :] + dsk_ref[d, h] * xf
            y_ref[0, h * p:(h + 1) * p, :] = yh.astype(y_ref.dtype)
            upd.append((xf * w_upd[ln:ln + 1, :]).astype(BF16))
        new = _dot(jnp.concatenate(upd, axis=0), bg)
        for e in range(SSM_HPG):
            ln = d * SSM_HEADS + g * SSM_HPG + e
            rows = slice(e * p, (e + 1) * p)
            st_ref[g, rows, :] = state[rows, :] * decay_end[ln:ln + 1, :] + new[rows, :]


def _ssd_kernel(dsk_ref, xsf, bmf, ctf, dtf, af, xsb, bmb, ctb, dtb, ab, yf_ref, yb_ref, stf, stb):
    @pl.when(pl.program_id(1) == 0)
    def _():
        stf[...] = jnp.zeros_like(stf)
        stb[...] = jnp.zeros_like(stb)

    _ssd_direction(xsf, bmf, ctf, dtf, af, dsk_ref, stf, yf_ref, 0)
    _ssd_direction(xsb, bmb, ctb, dtb, ab, dsk_ref, stb, yb_ref, 1)


def _scan_maps(nchunk, chunk):
    ctx_chunks = TM // chunk
    fwd = lambda b, t: (b * nchunk + t, 0)
    bwd = lambda b, t: (b * nchunk + jnp.where(t < ctx_chunks, ctx_chunks - 1 - t, nchunk + ctx_chunks - 1 - t), 0)
    return fwd, bwd


def _ssd_scan(xs_t, bm, c_t, dt_t, a_t, d_skip, nb, seg):
    nchunk = seg // SSD_CHUNK
    fwd_rows, bwd_rows = _scan_maps(nchunk, SSD_CHUNK)

    def specs(rows_map):
        cols_map = lambda b, t, *_: (b, 0, rows_map(b, t)[0] - b * nchunk)
        return [pl.BlockSpec((1, SSM_D_INNER, SSD_CHUNK), cols_map),
                pl.BlockSpec((SSD_CHUNK, SSM_BC), lambda b, t, *_: rows_map(b, t)),
                pl.BlockSpec((1, SSM_BC, SSD_CHUNK), cols_map),
                pl.BlockSpec((1, LANES, SSD_CHUNK), cols_map),
                pl.BlockSpec((1, LANES, SSD_CHUNK), cols_map)]

    arrs = (xs_t, bm, c_t, dt_t, a_t)
    return pl.pallas_call(
        _ssd_kernel,
        grid_spec=pltpu.PrefetchScalarGridSpec(
            num_scalar_prefetch=1,
            grid=(nb, nchunk),
            in_specs=specs(fwd_rows) + specs(bwd_rows),
            out_specs=[specs(fwd_rows)[0], specs(bwd_rows)[0]],
            scratch_shapes=[pltpu.VMEM((SSM_GROUPS, SSM_HPG * SSM_HEAD_DIM, SSM_STATE), F32)] * 2,
        ),
        out_shape=[jax.ShapeDtypeStruct(xs_t.shape, BF16)] * 2,
        compiler_params=_params("parallel", "arbitrary"),
        name="ssd_scan",
    )(d_skip, *arrs, *arrs)


def _log_sigmoid(x):
    return jnp.minimum(x, 0.0) - jnp.log1p(jnp.exp(-jnp.abs(x)))


ML_STATE_ROWS = ML_V_DIM + 16


def _mlstm_direction(qk_ref, v_ref, g_ref, gb_ref, c_st, m_st, h_ref, d):
    nh, dk, dv = ML_HEADS, ML_QK_DIM, ML_V_DIM
    g_t = jnp.transpose(g_ref[...] + gb_ref[...])
    ig = g_t[16 * d:16 * d + nh, :]
    lf = _log_sigmoid(g_t[16 * d + nh:16 * d + 2 * nh, :])
    upper = jnp.where(_tri(False, ML_CHUNK), 1.0, 0.0).astype(BF16)
    lf_hi, lf_lo = _split_bf16(lf)
    fc = _dot(lf_hi, upper) + _dot(lf_lo, upper)
    tot = fc[:, ML_CHUNK - 1:ML_CHUNK]
    m_prev = m_st[...]
    if d == 0:
        lane_v, sub_v = fc, ig - fc
        inter = fc + m_prev
        logw = tot - fc + ig
    else:
        ec = fc - lf
        lane_v, sub_v = -ec, ec + ig
        inter = tot - ec + m_prev
        logw = ec + ig
    sub_c = jnp.transpose(jnp.concatenate([sub_v, jnp.zeros((LANES - nh, ML_CHUNK), F32)], axis=0))
    m_new = jnp.maximum(tot + m_prev, jnp.max(logw, axis=-1, keepdims=True))
    ws = jnp.exp(logw - m_new)
    cw = jnp.exp(tot + m_prev - m_new)
    mask = _tri(d == 1, ML_CHUNK)
    q_t = jnp.transpose(qk_ref[:, :nh * dk].astype(F32))
    v_t = jnp.transpose(v_ref[...].astype(F32))
    lane = lax.broadcasted_iota(jnp.int32, (1, LANES), 1)
    zeros_q = jnp.zeros((dk, ML_CHUNK), F32)
    for h in range(nh):
        half = h % 2
        k_pair = qk_ref[:, nh * dk + (h - half) * dk:nh * dk + (h - half + 2) * dk] * (dk ** -0.5)
        qh = q_t[h * dk:(h + 1) * dk, :]
        q_m = jnp.concatenate([zeros_q, qh] if half else [qh, zeros_q], axis=0).astype(BF16)
        logd = jnp.where(mask, sub_c[:, h:h + 1] + lane_v[h:h + 1, :], -jnp.inf)
        mt = jnp.maximum(inter[h:h + 1, :], jnp.max(logd, axis=0, keepdims=True))
        sc = _dot(k_pair, q_m) * jnp.exp(logd - mt)
        vh = v_t[h * dv:(h + 1) * dv, :]
        state = c_st[h]
        cq = _dot(state.astype(BF16), q_m)
        w_int = jnp.exp(inter[h:h + 1, :] - mt)
        num = _dot(vh.astype(BF16), sc.astype(BF16)) + w_int * cq[:dv, :]
        den = jnp.sum(sc, axis=0, keepdims=True) + w_int * cq[dv:dv + 1, :]
        h_ref[0, h * dv:(h + 1) * dv, :] = (num / jnp.maximum(jnp.abs(den), jnp.exp(-mt))).astype(h_ref.dtype)
        ws_h = ws[h:h + 1, :]
        lhs = jnp.concatenate([vh * ws_h, jnp.broadcast_to(ws_h, (ML_STATE_ROWS - dv, ML_CHUNK))], axis=0)
        own = (lane >= half * dk) & (lane < (half + 1) * dk)
        row = lax.broadcasted_iota(jnp.int32, (ML_STATE_ROWS, 1), 0)
        new = cw[h:h + 1, :] * state + _dot(lhs.astype(BF16), k_pair)
        c_st[h] = jnp.where(own & (row <= dv), new, 0.0)
    m_st[...] = jnp.broadcast_to(m_new, m_st.shape)


def _mlstm_kernel(qkf, vf, gf, qkb, vb, gb, gbias, hf_ref, hb_ref, cf, mf, cb, mb):
    @pl.when(pl.program_id(1) == 0)
    def _():
        for r in (cf, mf, cb, mb):
            r[...] = jnp.zeros_like(r)

    _mlstm_direction(qkf, vf, gf, gbias, cf, mf, hf_ref, 0)
    _mlstm_direction(qkb, vb, gb, gbias, cb, mb, hb_ref, 1)


def _mlstm_scan(qk, v, g, gate_b, nb, seg):
    t_tok = qk.shape[0]
    nchunk = seg // ML_CHUNK
    fwd, bwd = _scan_maps(nchunk, ML_CHUNK)
    widths = (qk.shape[1], v.shape[1], LANES)
    in_specs = ([pl.BlockSpec((ML_CHUNK, w), fwd) for w in widths] + [pl.BlockSpec((ML_CHUNK, w), bwd) for w in widths]
                + [pl.BlockSpec((1, LANES), lambda b, t: (0, 0))])
    state = [pltpu.VMEM((ML_HEADS, ML_STATE_ROWS, LANES), F32), pltpu.VMEM((ML_HEADS, LANES), F32)]
    out_map = lambda rows_map: (lambda b, t: (b, 0, rows_map(b, t)[0] - b * nchunk))
    return pl.pallas_call(
        _mlstm_kernel,
        grid=(nb, nchunk),
        in_specs=in_specs,
        out_specs=[pl.BlockSpec((1, v.shape[1], ML_CHUNK), out_map(fwd)),
                   pl.BlockSpec((1, v.shape[1], ML_CHUNK), out_map(bwd))],
        out_shape=[jax.ShapeDtypeStruct((nb, v.shape[1], seg), BF16)] * 2,
        scratch_shapes=state + state,
        compiler_params=_params("parallel", "arbitrary"),
        name="mlstm_scan",
    )(qk, v, g, qk, v, g, gate_b)


def _rms_rows(x, g_row):
    return x * lax.rsqrt(jnp.mean(x * x, axis=-1, keepdims=True) + EPS) * g_row


MLA_SCORE_SCALE = (MLA_NOPE + MLA_ROPE) ** -0.5 * LOG2_E


def _mla_q_kernel(cq_ref, g_ref, wt_ref, cos_ref, sin_ref, qt_ref):
    cq_t = jnp.transpose(_rms_rows(cq_ref[...], g_ref[...])).astype(BF16)
    q_t = _dot(wt_ref[...], cq_t) * MLA_SCORE_SCALE
    qt_ref[0] = q_t.astype(qt_ref.dtype)
    cos, sin = cos_ref[...], sin_ref[...]
    half = MLA_ROPE // 2
    for h in range(MLA_HEADS):
        r0 = h * LANES + MLA_NOPE
        x1, x2 = q_t[r0:r0 + half, :], q_t[r0 + half:r0 + MLA_ROPE, :]
        qt_ref[0, r0:r0 + half, :] = (x1 * cos - x2 * sin).astype(qt_ref.dtype)
        qt_ref[0, r0 + half:r0 + MLA_ROPE, :] = (x2 * cos + x1 * sin).astype(qt_ref.dtype)


def _mla_q(p3, gain, wq_t, tables_t, nt):
    t_tok = p3.shape[0]
    n = wq_t.shape[0]
    half = MLA_ROPE // 2
    return pl.pallas_call(
        _mla_q_kernel,
        grid=(t_tok // TM,),
        in_specs=[pl.BlockSpec((TM, MLA_RANK), lambda t: (t, 0)),
                  pl.BlockSpec((1, MLA_RANK), lambda t: (0, 0)),
                  pl.BlockSpec(wq_t.shape, lambda t: (0, 0)),
                  pl.BlockSpec((half, TM), lambda t: (0, t % nt)),
                  pl.BlockSpec((half, TM), lambda t: (0, t % nt))],
        out_specs=pl.BlockSpec((1, n, TM), lambda t: (t, 0, 0)),
        out_shape=jax.ShapeDtypeStruct((t_tok // TM, n, TM), BF16),
        compiler_params=_params("parallel"),
        name="mla_q",
    )(p3, gain.reshape(1, MLA_RANK), wq_t, *tables_t)


def _mla_kv_kernel(ckv_ref, kr_ref, g_ref, wk_ref, wvt_ref, cos_ref, sin_ref, k_ref, vt_ref):
    cn = _rms_rows(ckv_ref[...], g_ref[...])
    vt_ref[0] = _dot(wvt_ref[...], jnp.transpose(cn).astype(BF16)).astype(vt_ref.dtype)
    kn = _dot(cn.astype(BF16), wk_ref[...])
    kr = pltpu.roll(kr_ref[...], MLA_NOPE, axis=1)
    half = MLA_ROPE // 2
    lane = lax.broadcasted_iota(jnp.int32, (1, LANES), 1)
    partner = jnp.where(lane < MLA_NOPE + half, pltpu.roll(kr, LANES - half, axis=1), pltpu.roll(kr, half, axis=1))
    roped = kr * cos_ref[...] + partner * sin_ref[...]
    for h in range(MLA_HEADS):
        k_ref[:, h * LANES:(h + 1) * LANES] = (kn[:, h * LANES:(h + 1) * LANES] + roped).astype(k_ref.dtype)


def _mla_kv(p3, gain, w_k, wv_t, tables, nb, seg):
    t_tok = p3.shape[0]
    nt = seg // TM
    return pl.pallas_call(
        _mla_kv_kernel,
        grid=(nb, nt),
        in_specs=[pl.BlockSpec((TM, MLA_RANK), lambda b, j: (b * nt + j, 1)),
                  pl.BlockSpec((TM, LANES), lambda b, j: (b * nt + j, 2 * MLA_RANK // LANES)),
                  pl.BlockSpec((1, MLA_RANK), lambda b, j: (0, 0)),
                  pl.BlockSpec(w_k.shape, lambda b, j: (0, 0)),
                  pl.BlockSpec(wv_t.shape, lambda b, j: (0, 0)),
                  pl.BlockSpec((TM, LANES), lambda b, j: (j, 0)),
                  pl.BlockSpec((TM, LANES), lambda b, j: (j, 0))],
        out_specs=[pl.BlockSpec((TM, MLA_HEADS * LANES), lambda b, j: (b * nt + j, 0)),
                   pl.BlockSpec((1, MLA_HEADS * MLA_V, TM), lambda b, j: (b, 0, j))],
        out_shape=[jax.ShapeDtypeStruct((t_tok, MLA_HEADS * LANES), BF16),
                   jax.ShapeDtypeStruct((nb, MLA_HEADS * MLA_V, seg), BF16)],
        compiler_params=_params("parallel", "parallel"),
        name="mla_kv",
    )(p3, p3, gain.reshape(1, MLA_RANK), w_k, wv_t, *tables)


def _mla_attn_kernel(qt_ref, k_ref, vt_ref, o_ref):
    @pl.when(pl.program_id(2) == 0)
    def _():
        o_ref[...] = jnp.zeros_like(o_ref)

    @pl.when(pl.program_id(2) > 0)
    def _():
        heads = range(MLA_HEADS_PER_STEP)
        seg = k_ref.shape[0]
        tk = MLA_KEY_TILE if (seg - TM) % MLA_KEY_TILE == 0 else TM
        tiles = [(0, TM)] + [(a, a + tk) for a in range(TM, seg, tk)]

        def scores(hh, j):
            return _dot(k_ref[tiles[j][0]:tiles[j][1], hh * LANES:(hh + 1) * LANES],
                        qt_ref[0, hh * LANES:(hh + 1) * LANES, :])

        m = [jnp.full((1, TM), -jnp.inf, F32) for _ in heads]
        l = [jnp.zeros((1, TM), F32) for _ in heads]
        acc = [jnp.zeros((MLA_V, TM), F32) for _ in heads]
        ahead = min(MLA_LOOKAHEAD, len(tiles))
        pending = [[scores(hh, j) for hh in heads] for j in range(ahead)]
        for j in range(len(tiles)):
            s_cur = pending.pop(0)
            if j + ahead < len(tiles):
                pending.append([scores(hh, j + ahead) for hh in heads])
            for hh in heads:
                m_new = jnp.maximum(m[hh], jnp.max(s_cur[hh], axis=0, keepdims=True))
                alpha = jnp.exp2(m[hh] - m_new)
                p = jnp.exp2(s_cur[hh] - m_new)
                l[hh] = alpha * l[hh] + jnp.sum(p, axis=0, keepdims=True)
                v_t = vt_ref[0, hh * MLA_V:(hh + 1) * MLA_V, tiles[j][0]:tiles[j][1]]
                acc[hh] = alpha * acc[hh] + _dot(v_t, p.astype(BF16))
                m[hh] = m_new
        o_t = jnp.concatenate([acc[hh] / l[hh] for hh in heads], axis=0)
        o_ref[...] = jnp.transpose(o_t).astype(o_ref.dtype)


def _mla_attention(q_t, k, v_t, nb, seg):
    t_tok = k.shape[0]
    nt = seg // TM
    hps = MLA_HEADS_PER_STEP
    return pl.pallas_call(
        _mla_attn_kernel,
        grid=(nb, MLA_HEADS // hps, nt),
        in_specs=[pl.BlockSpec((1, hps * LANES, TM), lambda b, hp, j: (b * nt + j, hp, 0)),
                  pl.BlockSpec((seg, hps * LANES), lambda b, hp, j: (b, hp)),
                  pl.BlockSpec((1, hps * MLA_V, seg), lambda b, hp, j: (b, hp, 0))],
        out_specs=pl.BlockSpec((TM, hps * MLA_V), lambda b, hp, j: (b * nt + j, hp)),
        out_shape=jax.ShapeDtypeStruct((t_tok, MLA_HEADS * MLA_V), BF16),
        compiler_params=_params("parallel", "parallel", "arbitrary"),
        name="mla_attention",
    )(q_t, k, v_t)


def _final_kernel(x_ref, g_ref, o_ref):
    o_ref[0] = _rms_rows(x_ref[...], g_ref[...])


def _final_norm(x, gain, nb, seg):
    d = x.shape[1]
    nt = seg // TM
    return pl.pallas_call(
        _final_kernel,
        grid=(nb, nt - 1),
        in_specs=[pl.BlockSpec((TM, d), lambda b, j: (b * nt + 1 + j, 0)),
                  pl.BlockSpec((1, d), lambda b, j: (0, 0))],
        out_specs=pl.BlockSpec((1, TM, d), lambda b, j: (b, j, 0)),
        out_shape=jax.ShapeDtypeStruct((nb, seg - TM, d), F32),
        compiler_params=_params("parallel", "parallel"),
        name="final_norm",
    )(x, gain.reshape(1, d))


def _rope_angles(n_lat, rot_dim):
    rows = n_lat // GRID_W
    row = jnp.repeat(jnp.arange(rows), GRID_W).astype(F32)
    col = jnp.tile(jnp.arange(GRID_W), rows).astype(F32)
    quarter = rot_dim // 4
    inv = ROPE_BASE ** (-jnp.arange(quarter, dtype=F32) / quarter)
    ang = jnp.concatenate([row[:, None] * inv, col[:, None] * inv], axis=-1)
    return jnp.cos(ang), jnp.sin(ang)


def _with_ctx_rows(tab, fill):
    return jnp.concatenate([jnp.full((TM, tab.shape[1]), fill, F32), tab], axis=0)


def _attn_rope_tables(n_lat):
    cos, sin = _rope_angles(n_lat, ATT_HEAD_DIM)
    cos_h = jnp.concatenate([cos, cos], axis=1)
    sin_h = jnp.concatenate([-sin, sin], axis=1)
    reps = LANES // ATT_HEAD_DIM
    return (_with_ctx_rows(jnp.tile(cos_h, (1, reps)), 1.0), _with_ctx_rows(jnp.tile(sin_h, (1, reps)), 0.0))


def _swap_halves(w, n_heads, dim):
    w3 = w.reshape(w.shape[0], n_heads, 2, dim // 2)
    return w3[:, :, ::-1, :].reshape(w.shape[0], n_heads * dim)


def _mla_k_tables(n_lat):
    cos, sin = _rope_angles(n_lat, MLA_ROPE)
    lo = jnp.zeros((n_lat, MLA_NOPE), F32)
    hi = jnp.zeros((n_lat, LANES - MLA_NOPE - MLA_ROPE), F32)
    cos_c = jnp.concatenate([lo, cos, cos, hi], axis=1)
    sin_c = jnp.concatenate([lo, -sin, sin, hi], axis=1)
    ctx_cos = jnp.concatenate([lo[:TM], jnp.ones((TM, MLA_ROPE), F32), hi[:TM]], axis=1)
    return (jnp.concatenate([ctx_cos, cos_c], axis=0), _with_ctx_rows(sin_c, 0.0))


def _pad_heads(w, real):
    r = w.shape[0]
    w3 = w.reshape(r, MLA_HEADS, real)
    return jnp.pad(w3, ((0, 0), (0, 0), (0, LANES - real))).reshape(r, MLA_HEADS * LANES)


def kernel(x, c, ctx, c_ctx, norm1_g, norm2_g, w_mod, b_mod, moe_w_group, moe_b_group, moe_w_expert, moe_b_expert, moe_w_gate, moe_w_up, moe_w_down, attn_w_in, attn_sink, attn_w_out, ssm_w_in, ssm_conv_w, ssm_conv_b, ssm_dt_bias, ssm_a_log, ssm_d, ssm_norm_g, ssm_w_out, mlstm_w_in, mlstm_gate_b, mlstm_norm_g, mlstm_w_out, mla_w_in, mla_q_norm_g, mla_w_q_up, mla_kv_norm_g, mla_w_kv_up, mla_w_out, final_norm_g):
    nb, n_lat, d = x.shape
    assert ctx.shape[1] == TM and d == D_MODEL and n_lat % TM == 0
    depth = w_mod.shape[0]
    seg = TM + n_lat
    nt = seg // TM
    t_tok = nb * seg

    xs = (x, ctx)

    rows = -(-(nb + 1) // 8) * 8
    cvec = jnp.concatenate([c, c_ctx[None, :], jnp.zeros((rows - nb - 1, d), F32)], axis=0)
    mods = _modulation(cvec, w_mod, b_mod).reshape(depth, rows, ADALN_CHUNKS, d)
    mods = jnp.pad(mods, ((0, 0), (0, 0), (0, MOD_ROWS - ADALN_CHUNKS), (0, 0)))

    w_router = jnp.concatenate([moe_w_expert, moe_w_group,
                                jnp.zeros((depth, d, LANES - MOE_EXPERTS - MOE_GROUPS), F32)], axis=-1)
    b_router = jnp.concatenate([moe_b_expert, moe_b_group,
                                jnp.zeros((depth, LANES - MOE_EXPERTS - MOE_GROUPS), F32)], axis=-1)
    w_gate, w_up = moe_w_gate.astype(BF16), moe_w_up.astype(BF16)
    w_down = moe_w_down.astype(BF16).reshape(depth, MOE_GROUPS, MOE_PER_GROUP * MOE_FF, d)

    for i in range(depth):
        kind = i % 4
        mod_i = mods[i]
        if kind == 0:
            nq, nk = ATT_HEADS * ATT_HEAD_DIM, ATT_KV_HEADS * ATT_HEAD_DIM
            w_in = attn_w_in[i // 4]
            w_all = jnp.concatenate([w_in, _swap_halves(w_in[:, :nq], ATT_HEADS, ATT_HEAD_DIM),
                                     _swap_halves(w_in[:, nq:nq + nk], ATT_KV_HEADS, ATT_HEAD_DIM)],
                                    axis=1).astype(BF16)
            n_in = w_in.shape[1]
            q, k, v = _normproj(xs, mod_i, norm1_g[i], w_all,
                                [(0, nq, n_in, ATT_HEAD_DIM ** -0.5 * LOG2_E), (nq, nk, n_in + nq), (nq + nk, nk, None)],
                                [BF16, BF16, BF16], nt, nb, tables=_attn_rope_tables(n_lat))
            o = _windowed_attention(q, k, v, attn_sink[i // 4], nb, seg)
            xs = _outproj([o], [nq], xs, mod_i, attn_w_out[i // 4].astype(BF16), _plain_prologue, nt, nb)
        elif kind == 1:
            j = i // 4
            w_in = jnp.pad(ssm_w_in[j], ((0, 0), (0, LANES - 2 * SSM_HEADS))).astype(BF16)
            z, xbc, dt = _normproj(xs, mod_i, norm1_g[i], w_in,
                                   [(0, SSM_D_INNER, None), (SSM_D_INNER, SSM_CONV_DIM, None),
                                    (SSM_D_INNER + SSM_CONV_DIM, LANES, None)],
                                   [BF16, BF16, F32], nt, nb)
            lane_pad = LANES - 2 * SSM_HEADS
            dt_bias = jnp.pad(ssm_dt_bias[j].reshape(1, -1), ((0, 0), (0, lane_pad)))
            a_neg = jnp.pad(-jnp.exp(ssm_a_log[j].astype(F32)).reshape(1, -1), ((0, 0), (0, lane_pad)))
            conv_w = jnp.pad(ssm_conv_w[j], ((0, 8 - SSM_CONV), (0, 0)))
            xc, bm, cm, dtv, a = _ssm_conv(xbc, dt, conv_w, ssm_conv_b[j].reshape(1, -1), dt_bias, a_neg, nt)
            yf, yb = _ssd_scan(xc, bm, cm, dtv, a, ssm_d[j].astype(F32), nb, seg)
            xs = _outproj([z], [SSM_D_INNER], xs, mod_i, ssm_w_out[j].astype(BF16), _ssm_prologue,
                          nt, nb, extra=(ssm_norm_g[j].reshape(1, -1),), ins_t=(yf, yb))
        elif kind == 2:
            j = i // 4
            nqk, nv = 2 * ML_HEADS * ML_QK_DIM, ML_HEADS * ML_V_DIM
            w_in = jnp.pad(mlstm_w_in[j], ((0, 0), (0, LANES - 4 * ML_HEADS))).astype(BF16)
            qk, v, o, g = _normproj(xs, mod_i, norm1_g[i], w_in,
                                    [(0, nqk, None), (nqk, nv, None), (nqk + nv, nv, None), (nqk + 2 * nv, LANES, None)],
                                    [BF16, BF16, BF16, F32], nt, nb)
            gate_b = jnp.pad(mlstm_gate_b[j].reshape(1, -1), ((0, 0), (0, LANES - 4 * ML_HEADS)))
            hf, hb = _mlstm_scan(qk, v, g, gate_b, nb, seg)
            xs = _outproj([o], [nv], xs, mod_i, mlstm_w_out[j].astype(BF16), _mlstm_prologue,
                          nt, nb, extra=(mlstm_norm_g[j].reshape(1, -1),), ins_t=(hf, hb))
        else:
            j = i // 4
            w_in = jnp.pad(mla_w_in[j], ((0, 0), (0, LANES - MLA_ROPE))).astype(BF16)
            (p3,) = _normproj(xs, mod_i, norm1_g[i], w_in, [(0, w_in.shape[1], None)], [F32], nt, nb)
            cos, sin = _rope_angles(n_lat, MLA_ROPE)
            tables_t = (jnp.transpose(_with_ctx_rows(cos, 1.0)), jnp.transpose(_with_ctx_rows(sin, 0.0)))
            wq_t = jnp.transpose(_pad_heads(mla_w_q_up[j], MLA_NOPE + MLA_ROPE)).astype(BF16)
            q_t = _mla_q(p3, mla_q_norm_g[j], wq_t, tables_t, nt)
            w_kv = mla_w_kv_up[j].reshape(MLA_RANK, MLA_HEADS, MLA_NOPE + MLA_V)
            w_k = _pad_heads(w_kv[:, :, :MLA_NOPE].reshape(MLA_RANK, MLA_HEADS * MLA_NOPE), MLA_NOPE).astype(BF16)
            wv_t = jnp.transpose(w_kv[:, :, MLA_NOPE:].reshape(MLA_RANK, MLA_HEADS * MLA_V)).astype(BF16)
            k, v_t = _mla_kv(p3, mla_kv_norm_g[j], w_k, wv_t, _mla_k_tables(n_lat), nb, seg)
            o = _mla_attention(q_t, k, v_t, nb, seg)
            xs = _outproj([o], [MLA_HEADS * MLA_V], xs, mod_i, mla_w_out[j].astype(BF16), _plain_prologue, nt, nb)

        xs = _moe(xs, mod_i, norm2_g[i], w_router[i], b_router[i].reshape(1, -1), w_gate, w_up, w_down, i, nt, nb)

    return _final_norm(xs, final_norm_g, nb, seg)
```

```python
import functools
import math

import jax
import jax.numpy as jnp
from jax import lax
from jax.experimental import pallas as pl
from jax.experimental.pallas import tpu as pltpu

F32 = jnp.float32
BF16 = jnp.bfloat16

D_MODEL = 1024
GRID_W = 64
EPS = 1e-6
ROPE_BASE = 10000.0
ADALN_CHUNKS = 6
CHUNK = 128
SSD_CHUNK = 256
ML_CHUNK = 128
TM = 256
MOD_ROWS = 8
LANES = 128
V7X_VMEM_LIMIT = 48 * 1024 * 1024

ATT_HEADS, ATT_KV_HEADS, ATT_HEAD_DIM, WINDOW = 16, 4, 64, 128
ATT_GROUP = ATT_HEADS // ATT_KV_HEADS
SSM_D_INNER, SSM_HEAD_DIM, SSM_HEADS, SSM_GROUPS, SSM_STATE, SSM_CONV = 2048, 64, 32, 4, 128, 5
SSM_HPG = SSM_HEADS // SSM_GROUPS
SSM_BC = SSM_GROUPS * SSM_STATE
SSM_CONV_DIM = SSM_D_INNER + 2 * SSM_BC
ML_HEADS, ML_QK_DIM, ML_V_DIM = 8, 64, 128
MLA_HEADS, MLA_RANK, MLA_NOPE, MLA_ROPE, MLA_V = 16, 256, 64, 32, 64
MOE_GROUPS, MOE_PER_GROUP, MOE_EXPERTS, MOE_FF = 4, 4, 16, 256
MOE_ROWS = 1024
MLA_KEY_TILE = 1024
MLA_HEADS_PER_STEP = 4
MLA_LOOKAHEAD = 1
LOG2_E = 1.4426950408889634


def _dot(a, b):
    return jnp.dot(a, b, preferred_element_type=F32)


def _dot_nt(a, b):
    return lax.dot_general(a, b, (((1,), (1,)), ((), ())), preferred_element_type=F32)


def _split_bf16(x):
    hi = x.astype(BF16)
    lo = (x - hi.astype(F32)).astype(BF16)
    return hi, lo


def _dot_split(a, b):
    a_hi, a_lo = _split_bf16(a)
    b_hi, b_lo = _split_bf16(b)
    return _dot(a_hi, b_hi) + _dot(a_lo, b_hi) + _dot(a_hi, b_lo)


def _sigmoid(x):
    return 1.0 / (1.0 + jnp.exp(-x))


def _silu(x):
    return x * _sigmoid(x)


def _softplus(x):
    return jnp.maximum(x, 0.0) + jnp.log1p(jnp.exp(-jnp.abs(x)))


def _params(*sem):
    return pltpu.CompilerParams(dimension_semantics=sem, vmem_limit_bytes=V7X_VMEM_LIMIT)


def _mod_row(t, nt, nb):
    return jnp.where(t % nt == 0, nb, t // nt)


def _mod_kernel(c_ref, w_ref, b_ref, o_ref):
    o_ref[0] = _dot_split(_silu(c_ref[...]), w_ref[0]) + b_ref[0]


def _modulation(cvec, w_mod, b_mod):
    depth, d, n = w_mod.shape
    tn = 1536
    rows = cvec.shape[0]
    return pl.pallas_call(
        _mod_kernel,
        grid=(depth, n // tn),
        in_specs=[pl.BlockSpec((rows, d), lambda l, j: (0, 0)),
                  pl.BlockSpec((1, d, tn), lambda l, j: (l, 0, j)),
                  pl.BlockSpec((1, 1, tn), lambda l, j: (l, 0, j))],
        out_specs=pl.BlockSpec((1, rows, tn), lambda l, j: (l, 0, j)),
        out_shape=jax.ShapeDtypeStruct((depth, rows, n), F32),
        compiler_params=_params("arbitrary", "arbitrary"),
        name="modulation",
    )(cvec, w_mod, b_mod.reshape(depth, 1, n))


def _normed(x, g_row, mod, sh_row, sc_row):
    y = x * lax.rsqrt(jnp.mean(x * x, axis=-1, keepdims=True) + EPS) * g_row
    return y * (1.0 + mod[sc_row:sc_row + 1, :]) + mod[sh_row:sh_row + 1, :]


def _proj_columns(hb, w_ref, o_ref, start, width, rot_start, cos, sin, scale=None, chunk=512):
    for c in range(0, width, chunk):
        cw = min(chunk, width - c)
        acc = _dot(hb, w_ref[:, start + c:start + c + cw])
        if rot_start is not None:
            rot = _dot(hb, w_ref[:, rot_start + c:rot_start + c + cw])
            reps = cw // LANES
            acc = acc * jnp.tile(cos, (1, reps)) + rot * jnp.tile(sin, (1, reps))
        if scale is not None:
            acc = acc * scale
        o_ref[:, c:c + cw] = acc.astype(o_ref.dtype)


def _stream_specs(x, nt):
    if isinstance(x, tuple):
        d = x[0].shape[-1]
        return [pl.BlockSpec((1, TM, d), lambda t: (t // nt, jnp.maximum(t % nt - 1, 0), 0)),
                pl.BlockSpec((1, TM, d), lambda t: (t // nt, 0, 0))], list(x)
    sub = _tiles_per_step(x, nt)
    return [pl.BlockSpec((sub * TM, x.shape[1]), lambda t: (t, 0))], [x]


def _tiles_per_step(x, nt):
    if isinstance(x, tuple):
        return 1
    return 2 if (x.shape[0] // TM) % 2 == 0 else 1


def _stream_tile(x_refs, nt, s):
    if len(x_refs) == 1:
        return x_refs[0][s * TM:(s + 1) * TM, :]
    return jnp.where(pl.program_id(0) % nt == 0, x_refs[1][0], x_refs[0][0])


def _normproj_kernel(*refs, outs, has_rope, n_x, nt, nb, sub):
    x_refs, refs = refs[:n_x], refs[n_x:]
    mods_ref, g_ref, w_ref = refs[:3]
    k = 3
    cos = sin = None
    if has_rope:
        cos, sin = refs[3][...], refs[4][...]
        k = 5
    hs = []
    for s in range(sub):
        mod = mods_ref[_mod_row(pl.program_id(0) * sub + s, nt, nb)]
        hs.append(_normed(_stream_tile(x_refs, nt, s), g_ref[...], mod, 0, 1).astype(BF16))
    hb = hs[0] if sub == 1 else jnp.concatenate(hs, axis=0)
    for o_ref, spec in zip(refs[k:], outs):
        _proj_columns(hb, w_ref, o_ref, *spec[:3], cos, sin, scale=spec[3] if len(spec) > 3 else None)


def _normproj(x, mods, gain, w, outs, out_dtypes, nt, nb, tables=None):
    d, n = w.shape
    t_tok = nb * nt * TM
    sub = _tiles_per_step(x, nt)
    assert tables is None or sub == 1
    rows = sub * TM
    x_specs, x_args = _stream_specs(x, nt)
    in_specs = x_specs + [pl.BlockSpec(mods.shape, lambda t: (0, 0, 0)),
                          pl.BlockSpec((1, d), lambda t: (0, 0)),
                          pl.BlockSpec((d, n), lambda t: (0, 0))]
    args = x_args + [mods, gain.reshape(1, d), w]
    if tables is not None:
        in_specs += [pl.BlockSpec((TM, LANES), lambda t: (t % nt, 0))] * 2
        args += list(tables)
    return pl.pallas_call(
        functools.partial(_normproj_kernel, outs=tuple(outs), has_rope=tables is not None, n_x=len(x_args),
                          nt=nt, nb=nb, sub=sub),
        grid=(t_tok // rows,),
        in_specs=in_specs,
        out_specs=[pl.BlockSpec((rows, o[1]), lambda t: (t, 0)) for o in outs],
        out_shape=[jax.ShapeDtypeStruct((t_tok, o[1]), dt) for o, dt in zip(outs, out_dtypes)],
        compiler_params=_params("parallel"),
        name="normproj",
    )(*args)


def _outproj_kernel(*refs, prologue, n_t, n_tok, n_extra, n_x, nt, nb, sub):
    ins_t = refs[:n_t * sub]
    ins = refs[n_t * sub:n_t * sub + n_tok]
    extra = refs[n_t * sub + n_tok:n_t * sub + n_tok + n_extra]
    x_refs = refs[n_t * sub + n_tok + n_extra:n_t * sub + n_tok + n_extra + n_x]
    mods_ref, w_ref, o_ref = refs[n_t * sub + n_tok + n_extra + n_x:]
    parts = []
    for s in range(sub):
        tok = [r if sub == 1 else r.at[pl.ds(s * TM, TM), :] for r in ins]
        parts.append(prologue(*ins_t[s * n_t:(s + 1) * n_t], *tok, *extra))
    a = parts[0] if sub == 1 else jnp.concatenate(parts, axis=0)
    y = _dot(a, w_ref[...])
    for s in range(sub):
        rows = slice(s * TM, (s + 1) * TM)
        gate = mods_ref[_mod_row(pl.program_id(0) * sub + s, nt, nb)][2:3, :]
        o_ref[rows, :] = _stream_tile(x_refs, nt, s) + gate * y[rows, :]


def _outproj(ins, in_widths, x, mods, w, prologue, nt, nb, extra=(), ins_t=()):
    d = w.shape[1]
    t_tok = nb * nt * TM
    sub = _tiles_per_step(x, nt)
    rows = sub * TM

    def tile_map(s):
        return lambda t: ((t * sub + s) // nt, 0, (t * sub + s) % nt)

    in_specs, args = [], []
    for s in range(sub):
        in_specs += [pl.BlockSpec((1, a.shape[1], TM), tile_map(s)) for a in ins_t]
        args += list(ins_t)
    in_specs += [pl.BlockSpec((rows, wd), lambda t: (t, 0)) for wd in in_widths]
    in_specs += [pl.BlockSpec(e.shape, lambda t: (0, 0)) for e in extra]
    x_specs, x_args = _stream_specs(x, nt)
    n_before_x = len(in_specs)
    in_specs += x_specs + [pl.BlockSpec(mods.shape, lambda t: (0, 0, 0)), pl.BlockSpec(w.shape, lambda t: (0, 0))]
    return pl.pallas_call(
        functools.partial(_outproj_kernel, prologue=prologue, n_t=len(ins_t), n_tok=len(ins), n_extra=len(extra),
                          n_x=len(x_args), nt=nt, nb=nb, sub=sub),
        grid=(t_tok // rows,),
        in_specs=in_specs,
        out_specs=pl.BlockSpec((rows, d), lambda t: (t, 0)),
        out_shape=jax.ShapeDtypeStruct((t_tok, d), F32),
        input_output_aliases={} if isinstance(x, tuple) else {n_before_x: 0},
        compiler_params=_params("parallel"),
        name="outproj",
    )(*args, *ins, *extra, *x_args, mods, w)


def _group_rms(y, n_groups):
    width = y.shape[1] // n_groups
    parts = []
    for g in range(n_groups):
        yg = y[:, g * width:(g + 1) * width]
        parts.append(yg * lax.rsqrt(jnp.mean(yg * yg, axis=-1, keepdims=True) + EPS))
    return jnp.concatenate(parts, axis=1)


def _plain_prologue(o_ref):
    return o_ref[...]


def _ssm_prologue(yf_ref, yb_ref, z_ref, g_ref):
    y = jnp.transpose(yf_ref[0].astype(F32) + yb_ref[0].astype(F32)) * _silu(z_ref[...].astype(F32))
    return (_group_rms(y, SSM_GROUPS) * g_ref[...]).astype(BF16)


def _mlstm_prologue(hf_ref, hb_ref, o_ref, g_ref):
    h = jnp.transpose(hf_ref[0].astype(F32) + hb_ref[0].astype(F32))
    return (_group_rms(h, ML_HEADS) * g_ref[...] * _sigmoid(o_ref[...].astype(F32))).astype(BF16)


MOE_ROUTE_ROWS = 24


def _route(logits):
    lt = jnp.transpose(logits)[:MOE_ROUTE_ROWS, :]
    row = lax.broadcasted_iota(jnp.int32, lt.shape, 0).astype(F32)
    neg = -jnp.inf
    lg = jnp.where((row >= MOE_EXPERTS) & (row < MOE_EXPERTS + MOE_GROUPS), lt, neg)
    gmax = jnp.max(lg, axis=0, keepdims=True)
    g_sel = jnp.min(jnp.where(lg == gmax, row, LANES), axis=0, keepdims=True) - MOE_EXPERTS
    p_g = 1.0 / jnp.sum(jnp.exp(lg - gmax), axis=0, keepdims=True)
    in_group = (row >= g_sel * MOE_PER_GROUP) & (row < (g_sel + 1) * MOE_PER_GROUP)
    le = jnp.where(in_group, lt, neg)
    v1 = jnp.max(le, axis=0, keepdims=True)
    i1 = jnp.min(jnp.where(le == v1, row, LANES), axis=0, keepdims=True)
    le2 = jnp.where(row == i1, neg, le)
    v2 = jnp.max(le2, axis=0, keepdims=True)
    i2 = jnp.min(jnp.where(le2 == v2, row, LANES), axis=0, keepdims=True)
    e2 = jnp.exp(v2 - v1)
    w1 = p_g / (1.0 + e2)
    comb_t = jnp.where(row == i1, w1, 0.0) + jnp.where(row == i2, w1 * e2, 0.0)
    pad = jnp.zeros((LANES - MOE_ROUTE_ROWS, lt.shape[1]), F32)
    return jnp.transpose(jnp.concatenate([comb_t, pad], axis=0))


def _moe_kernel(x_ref, mods_ref, g_ref, wr_ref, br_ref, wg_ref, wu_ref, wd_ref, o_ref, h_scr, comb_scr,
                *, nt, nb, sub):
    i = pl.program_id(0)
    e = pl.program_id(1)

    def experts(rows):
        hb = h_scr[rows, :]
        comb = comb_scr[rows, :]
        lane = lax.broadcasted_iota(jnp.int32, comb.shape, 1)
        scaled = []
        for k in range(MOE_PER_GROUP):
            act = _silu(_dot(hb, wg_ref[0, k])) * _dot(hb, wu_ref[0, k])
            cw = jnp.sum(jnp.where(lane == e * MOE_PER_GROUP + k, comb, 0.0), axis=-1, keepdims=True)
            scaled.append((act * cw).astype(BF16))
        return _dot(jnp.concatenate(scaled, axis=1), wd_ref[0, 0])

    def gate_row(s):
        return mods_ref[_mod_row(i * sub + s, nt, nb)][5:6, :]

    @pl.when(e == 0)
    def _():
        w_hi, w_lo = _split_bf16(wr_ref[...])
        w_both = jnp.concatenate([w_hi, w_lo], axis=1)
        for s in range(sub):
            rows = slice(s * TM, (s + 1) * TM)
            mod = mods_ref[_mod_row(i * sub + s, nt, nb)]
            x = x_ref[rows, :]
            h = _normed(x, g_ref[...], mod, 3, 4)
            h_hi, h_lo = _split_bf16(h)
            h_scr[rows, :] = h_hi
            both = _dot(h_hi, w_both)
            logits = both[:, :LANES] + both[:, LANES:] + _dot(h_lo, w_hi) + br_ref[...]
            comb_scr[rows, :] = _route(logits)
            o_ref[rows, :] = x + gate_row(s) * experts(rows)

    @pl.when(e > 0)
    def _():
        y = experts(slice(0, sub * TM))
        for s in range(sub):
            rows = slice(s * TM, (s + 1) * TM)
            o_ref[rows, :] += gate_row(s) * y[rows, :]


def _moe(x, mods, gain, w_router, b_router, w_gate, w_up, w_down, layer, nt, nb):
    t_tok, d = x.shape
    pg = MOE_PER_GROUP
    rows = MOE_ROWS if t_tok % MOE_ROWS == 0 else TM
    sub = rows // TM
    return pl.pallas_call(
        functools.partial(_moe_kernel, nt=nt, nb=nb, sub=sub),
        grid=(t_tok // rows, MOE_GROUPS),
        in_specs=[pl.BlockSpec((rows, d), lambda i, e: (i, 0)),
                  pl.BlockSpec(mods.shape, lambda i, e: (0, 0, 0)),
                  pl.BlockSpec((1, d), lambda i, e: (0, 0)),
                  pl.BlockSpec((d, LANES), lambda i, e: (0, 0)),
                  pl.BlockSpec((1, LANES), lambda i, e: (0, 0)),
                  pl.BlockSpec((1, pg, d, MOE_FF), lambda i, e: (layer, e, 0, 0)),
                  pl.BlockSpec((1, pg, d, MOE_FF), lambda i, e: (layer, e, 0, 0)),
                  pl.BlockSpec((1, 1, pg * MOE_FF, d), lambda i, e: (layer, e, 0, 0))],
        out_specs=pl.BlockSpec((rows, d), lambda i, e: (i, 0)),
        out_shape=jax.ShapeDtypeStruct((t_tok, d), F32),
        scratch_shapes=[pltpu.VMEM((rows, d), BF16), pltpu.VMEM((rows, LANES), F32)],
        input_output_aliases={0: 0},
        compiler_params=_params("parallel", "arbitrary"),
        name="moe",
    )(x, mods, gain.reshape(1, d), w_router, b_router, w_gate, w_up, w_down)


def _attn_kernel(sink_ref, q_ref, kp_ref, kc_ref, kn_ref, vp_ref, vc_ref, vn_ref, kx_ref, vx_ref, o_ref, *, n_lat):
    j = pl.program_id(1)
    jl = j - 1
    dh = ATT_HEAD_DIM
    n_loc = TM + 2 * CHUNK
    k_all = jnp.concatenate([kp_ref[...], kc_ref[...], kn_ref[...], kx_ref[...]], axis=0)
    v_t = jnp.transpose(jnp.concatenate([vp_ref[...], vc_ref[...], vn_ref[...], vx_ref[...]],
                                        axis=0).astype(F32)).astype(BF16)
    q_t = jnp.transpose(q_ref[...].astype(F32)).astype(BF16)
    n_keys = k_all.shape[0]
    kj = lax.broadcasted_iota(jnp.int32, (n_keys, TM), 0)
    qi = lax.broadcasted_iota(jnp.int32, (n_keys, TM), 1)
    kpos = jl * TM - CHUNK + kj
    local_ok = (kj >= qi + CHUNK - WINDOW) & (kj <= qi + CHUNK + WINDOW) & (kpos >= 0) & (kpos < n_lat) & (jl >= 0)
    bias = jnp.where((kj >= n_loc) | local_ok, 0.0, -jnp.inf)
    bias = jnp.concatenate([bias] * ATT_GROUP, axis=1)
    lane_head = lax.broadcasted_iota(jnp.int32, (1, ATT_GROUP * TM), 1) // TM
    zeros_q = jnp.zeros((dh, ATT_GROUP * TM), BF16)
    def scores(g):
        half = g % 2
        k_pair = k_all[:, (g - half) * dh:(g - half + 2) * dh]
        qg = jnp.concatenate([q_t[(g * ATT_GROUP + a) * dh:(g * ATT_GROUP + a + 1) * dh, :]
                              for a in range(ATT_GROUP)], axis=1)
        q_m = jnp.concatenate([zeros_q, qg] if half else [qg, zeros_q], axis=0)
        return _dot(k_pair, q_m) + bias

    outs = []
    s_next = scores(0)
    for g in range(ATT_KV_HEADS):
        s = s_next
        if g + 1 < ATT_KV_HEADS:
            s_next = scores(g + 1)
        sink = jnp.zeros((1, ATT_GROUP * TM), F32)
        for a in range(ATT_GROUP):
            sink = jnp.where(lane_head == a, sink_ref[g * ATT_GROUP + a] * LOG2_E, sink)
        m = jnp.maximum(jnp.max(s, axis=0, keepdims=True), sink)
        p = jnp.exp2(s - m)
        den = jnp.sum(p, axis=0, keepdims=True) + jnp.exp2(sink - m)
        og = _dot(v_t[g * dh:(g + 1) * dh, :], p.astype(BF16)) / den
        outs += [og[:, a * TM:(a + 1) * TM] for a in range(ATT_GROUP)]
    o_ref[...] = jnp.transpose(jnp.concatenate(outs, axis=0)).astype(o_ref.dtype)


def _windowed_attention(q, k, v, sink, nb, seg):
    t_tok = q.shape[0]
    nchunk = seg // CHUNK
    nt = seg // TM
    per_tile = TM // CHUNK
    kvw = ATT_KV_HEADS * ATT_HEAD_DIM

    def halo(off):
        return lambda b, j, *_: (b * nchunk + jnp.clip(j * per_tile + off, per_tile, nchunk - 1), 0)

    tile = lambda b, j, *_: (b * nt + j, 0)
    kv_spec = [pl.BlockSpec((CHUNK, kvw), halo(-1)), pl.BlockSpec((TM, kvw), tile),
               pl.BlockSpec((CHUNK, kvw), halo(per_tile))]
    ctx_spec = pl.BlockSpec((TM, kvw), lambda b, j, *_: (b * nt, 0))
    return pl.pallas_call(
        functools.partial(_attn_kernel, n_lat=seg - TM),
        grid_spec=pltpu.PrefetchScalarGridSpec(
            num_scalar_prefetch=1,
            grid=(nb, nt),
            in_specs=[pl.BlockSpec((TM, q.shape[1]), tile)] + kv_spec + kv_spec + [ctx_spec, ctx_spec],
            out_specs=pl.BlockSpec((TM, q.shape[1]), tile),
        ),
        out_shape=jax.ShapeDtypeStruct((t_tok, q.shape[1]), BF16),
        compiler_params=_params("parallel", "parallel"),
        name="windowed_attention",
    )(sink, q, k, k, k, v, v, v, k, v)


def _ssm_in_kernel(x_ref, xp_ref, xn_ref, mods_ref, g_ref, w_ref, cw_ref, cb_ref, dtb_ref, aneg_ref,
                   z_ref, xs_ref, bm_ref, cm_ref, dtv_ref, a_ref, *ext, nt, nb):
    t = pl.program_id(0)
    tl = t % nt
    halo = 8
    first = (tl == 0) | (tl == 1)
    last = (tl == 0) | (tl == nt - 1)
    gain = g_ref[...]

    def normed(x, tile):
        return _normed(x, gain, mods_ref[_mod_row(tile, nt, nb)], 0, 1)

    h = normed(x_ref[...], t)
    h_prev = jnp.where(first, 0.0, normed(xp_ref[...], jnp.maximum(t - 1, 0)))
    h_next = jnp.where(last, 0.0, normed(xn_ref[...], jnp.minimum(t + 1, pl.num_programs(0) - 1)))
    hb = h.astype(BF16)
    hb_ext = jnp.concatenate([h_prev, h, h_next], axis=0).astype(BF16)
    chunk = 512
    pad = SSM_CONV // 2

    def project(c):
        proj = _dot(hb_ext, w_ref[:, SSM_D_INNER + c:SSM_D_INNER + c + chunk])
        ext[c // chunk][...] = proj

    project(0)
    for c in range(0, SSM_CONV_DIM, chunk):
        if c + chunk < SSM_CONV_DIM:
            project(c + chunk)
        if c < SSM_D_INNER:
            z_ref[:, c:c + chunk] = _dot(hb, w_ref[:, c:c + chunk]).astype(z_ref.dtype)
        acc = jnp.broadcast_to(cb_ref[:, c:c + chunk], (TM, chunk))
        for k in range(SSM_CONV):
            acc = acc + cw_ref[k:k + 1, c:c + chunk] * ext[c // chunk][halo - pad + k:halo - pad + k + TM, :]
        y = _silu(acc)
        if c < SSM_D_INNER:
            xs_ref[0, c:c + chunk, :] = jnp.transpose(y).astype(xs_ref.dtype)
        elif c < SSM_D_INNER + SSM_BC:
            bm_ref[:, c - SSM_D_INNER:c - SSM_D_INNER + chunk] = y.astype(bm_ref.dtype)
        else:
            off = c - SSM_D_INNER - SSM_BC
            cm_ref[0, off:off + chunk, :] = jnp.transpose(y).astype(cm_ref.dtype)
    dt = _dot(hb, w_ref[:, SSM_D_INNER + SSM_CONV_DIM:])
    dtv = _softplus(dt + dtb_ref[...])
    dtv_ref[0] = jnp.transpose(dtv)
    a_ref[0] = jnp.transpose(dtv * aneg_ref[...] * LOG2_E)


def _ssm_in(x, mods, gain, w, conv_w, conv_b, dt_bias, a_neg, nt, nb):
    t_tok, d = x.shape
    n8 = t_tok // 8
    seg = nt * TM
    row = lambda t: (t, 0)
    col = lambda t: (t // nt, 0, t % nt)
    fixed = lambda t: (0, 0)
    return pl.pallas_call(
        functools.partial(_ssm_in_kernel, nt=nt, nb=nb),
        grid=(t_tok // TM,),
        in_specs=[pl.BlockSpec((TM, d), row),
                  pl.BlockSpec((8, d), lambda t: (jnp.maximum(t * (TM // 8) - 1, 0), 0)),
                  pl.BlockSpec((8, d), lambda t: (jnp.minimum((t + 1) * (TM // 8), n8 - 1), 0)),
                  pl.BlockSpec(mods.shape, lambda t: (0, 0, 0)),
                  pl.BlockSpec((1, d), fixed),
                  pl.BlockSpec(w.shape, fixed),
                  pl.BlockSpec((8, SSM_CONV_DIM), fixed),
                  pl.BlockSpec((1, SSM_CONV_DIM), fixed),
                  pl.BlockSpec((1, LANES), fixed),
                  pl.BlockSpec((1, LANES), fixed)],
        out_specs=[pl.BlockSpec((TM, SSM_D_INNER), row),
                   pl.BlockSpec((1, SSM_D_INNER, TM), col), pl.BlockSpec((TM, SSM_BC), row),
                   pl.BlockSpec((1, SSM_BC, TM), col), pl.BlockSpec((1, LANES, TM), col),
                   pl.BlockSpec((1, LANES, TM), col)],
        out_shape=[jax.ShapeDtypeStruct((t_tok, SSM_D_INNER), BF16),
                   jax.ShapeDtypeStruct((nb, SSM_D_INNER, seg), BF16), jax.ShapeDtypeStruct((t_tok, SSM_BC), BF16),
                   jax.ShapeDtypeStruct((nb, SSM_BC, seg), BF16), jax.ShapeDtypeStruct((nb, LANES, seg), F32),
                   jax.ShapeDtypeStruct((nb, LANES, seg), F32)],
        scratch_shapes=[pltpu.VMEM((TM + 16, 512), F32)] * (SSM_CONV_DIM // 512),
        compiler_params=_params("parallel"),
        name="ssm_in",
    )(x, x, x, mods, gain.reshape(1, d), w, conv_w, conv_b, dt_bias, a_neg)


def _tri(lower, n):
    r = lax.broadcasted_iota(jnp.int32, (n, n), 0)
    c = lax.broadcasted_iota(jnp.int32, (n, n), 1)
    return (c <= r) if lower else (c >= r)


def _ssd_direction(xs_ref, bm_ref, ct_ref, dt_ref, a_ref, dsk_ref, st_ref, y_ref, d):
    a_t = a_ref[0]
    dt_t = dt_ref[0]
    upper = jnp.where(_tri(False, SSD_CHUNK), 1.0, 0.0).astype(BF16)
    a_hi, a_lo = _split_bf16(a_t)
    cum = _dot(a_hi, upper) + _dot(a_lo, upper)
    cum_end = cum[:, SSD_CHUNK - 1:SSD_CHUNK]
    if d == 0:
        lane_v, sub_v = cum, -cum
        inter = jnp.exp2(cum)
        w_upd = jnp.exp2(cum_end - cum) * dt_t
    else:
        ecum = cum - a_t
        lane_v, sub_v = -ecum, ecum
        inter = jnp.exp2(cum_end - ecum)
        w_upd = jnp.exp2(ecum) * dt_t
    sub_c = jnp.transpose(sub_v)
    decay_end = jnp.exp2(cum_end)
    mask = _tri(d == 1, SSD_CHUNK)
    p = SSM_HEAD_DIM
    for g in range(SSM_GROUPS):
        bg = bm_ref[:, g * SSM_STATE:(g + 1) * SSM_STATE]
        cg_t = ct_ref[0, g * SSM_STATE:(g + 1) * SSM_STATE, :]
        cb_t = _dot(bg, cg_t)
        state = st_ref[g]
        y_in = _dot(state.astype(BF16), cg_t)
        upd = []
        for e in range(SSM_HPG):
            h = g * SSM_HPG + e
            ln = d * SSM_HEADS + h
            rows = slice(e * p, (e + 1) * p)
            seg = jnp.where(mask, sub_c[:, ln:ln + 1] + lane_v[ln:ln + 1, :], -jnp.inf)
            m_t = (cb_t * jnp.exp2(seg)).astype(BF16)
            xf = xs_ref[0, h * p:(h + 1) * p, :].astype(F32)
            u = (xf * dt_t[ln:ln + 1, :]).astype(BF16)
            yh = _dot(u, m_t) + y_in[rows, :] * inter[ln:ln + 1, :] + dsk_ref[d, h] * xf
            y_ref[0, h * p:(h + 1) * p, :] = yh.astype(y_ref.dtype)
            upd.append((xf * w_upd[ln:ln + 1, :]).astype(BF16))
        new = _dot(jnp.concatenate(upd, axis=0), bg)
        for e in range(SSM_HPG):
            ln = d * SSM_HEADS + g * SSM_HPG + e
            rows = slice(e * p, (e + 1) * p)
            st_ref[g, rows, :] = state[rows, :] * decay_end[ln:ln + 1, :] + new[rows, :]


def _ssd_kernel(dsk_ref, xsf, bmf, ctf, dtf, af, xsb, bmb, ctb, dtb, ab, yf_ref, yb_ref, stf, stb):
    @pl.when(pl.program_id(1) == 0)
    def _():
        stf[...] = jnp.zeros_like(stf)
        stb[...] = jnp.zeros_like(stb)

    _ssd_direction(xsf, bmf, ctf, dtf, af, dsk_ref, stf, yf_ref, 0)
    _ssd_direction(xsb, bmb, ctb, dtb, ab, dsk_ref, stb, yb_ref, 1)


def _scan_maps(nchunk, chunk):
    ctx_chunks = TM // chunk
    fwd = lambda b, t: (b * nchunk + t, 0)
    bwd = lambda b, t: (b * nchunk + jnp.where(t < ctx_chunks, ctx_chunks - 1 - t, nchunk + ctx_chunks - 1 - t), 0)
    return fwd, bwd


def _ssd_scan(xs_t, bm, c_t, dt_t, a_t, d_skip, nb, seg):
    nchunk = seg // SSD_CHUNK
    fwd_rows, bwd_rows = _scan_maps(nchunk, SSD_CHUNK)

    def specs(rows_map):
        cols_map = lambda b, t, *_: (b, 0, rows_map(b, t)[0] - b * nchunk)
        return [pl.BlockSpec((1, SSM_D_INNER, SSD_CHUNK), cols_map),
                pl.BlockSpec((SSD_CHUNK, SSM_BC), lambda b, t, *_: rows_map(b, t)),
                pl.BlockSpec((1, SSM_BC, SSD_CHUNK), cols_map),
                pl.BlockSpec((1, LANES, SSD_CHUNK), cols_map),
                pl.BlockSpec((1, LANES, SSD_CHUNK), cols_map)]

    arrs = (xs_t, bm, c_t, dt_t, a_t)
    return pl.pallas_call(
        _ssd_kernel,
        grid_spec=pltpu.PrefetchScalarGridSpec(
            num_scalar_prefetch=1,
            grid=(nb, nchunk),
            in_specs=specs(fwd_rows) + specs(bwd_rows),
            out_specs=[specs(fwd_rows)[0], specs(bwd_rows)[0]],
            scratch_shapes=[pltpu.VMEM((SSM_GROUPS, SSM_HPG * SSM_HEAD_DIM, SSM_STATE), F32)] * 2,
        ),
        out_shape=[jax.ShapeDtypeStruct(xs_t.shape, BF16)] * 2,
        compiler_params=_params("parallel", "arbitrary"),
        name="ssd_scan",
    )(d_skip, *arrs, *arrs)


def _log_sigmoid(x):
    return jnp.minimum(x, 0.0) - jnp.log1p(jnp.exp(-jnp.abs(x)))


ML_STATE_ROWS = ML_V_DIM + 16


def _mlstm_direction(qk_ref, v_ref, g_ref, gb_ref, c_st, m_st, h_ref, d):
    nh, dk, dv = ML_HEADS, ML_QK_DIM, ML_V_DIM
    g_t = jnp.transpose(g_ref[...] + gb_ref[...])
    ig = g_t[16 * d:16 * d + nh, :]
    lf = _log_sigmoid(g_t[16 * d + nh:16 * d + 2 * nh, :])
    upper = jnp.where(_tri(False, ML_CHUNK), 1.0, 0.0).astype(BF16)
    lf_hi, lf_lo = _split_bf16(lf)
    fc = _dot(lf_hi, upper) + _dot(lf_lo, upper)
    tot = fc[:, ML_CHUNK - 1:ML_CHUNK]
    m_prev = m_st[...]
    if d == 0:
        lane_v, sub_v = fc, ig - fc
        inter = fc + m_prev
        logw = tot - fc + ig
    else:
        ec = fc - lf
        lane_v, sub_v = -ec, ec + ig
        inter = tot - ec + m_prev
        logw = ec + ig
    sub_c = jnp.transpose(jnp.concatenate([sub_v, jnp.zeros((LANES - nh, ML_CHUNK), F32)], axis=0))
    m_new = jnp.maximum(tot + m_prev, jnp.max(logw, axis=-1, keepdims=True))
    ws = jnp.exp(logw - m_new)
    cw = jnp.exp(tot + m_prev - m_new)
    mask = _tri(d == 1, ML_CHUNK)
    q_t = jnp.transpose(qk_ref[:, :nh * dk].astype(F32))
    v_t = jnp.transpose(v_ref[...].astype(F32))
    lane = lax.broadcasted_iota(jnp.int32, (1, LANES), 1)
    zeros_q = jnp.zeros((dk, ML_CHUNK), F32)
    for h in range(nh):
        half = h % 2
        k_pair = qk_ref[:, nh * dk + (h - half) * dk:nh * dk + (h - half + 2) * dk] * (dk ** -0.5)
        qh = q_t[h * dk:(h + 1) * dk, :]
        q_m = jnp.concatenate([zeros_q, qh] if half else [qh, zeros_q], axis=0).astype(BF16)
        logd = jnp.where(mask, sub_c[:, h:h + 1] + lane_v[h:h + 1, :], -jnp.inf)
        mt = jnp.maximum(inter[h:h + 1, :], jnp.max(logd, axis=0, keepdims=True))
        sc = _dot(k_pair, q_m) * jnp.exp(logd - mt)
        vh = v_t[h * dv:(h + 1) * dv, :]
        state = c_st[h]
        cq = _dot(state.astype(BF16), q_m)
        w_int = jnp.exp(inter[h:h + 1, :] - mt)
        num = _dot(vh.astype(BF16), sc.astype(BF16)) + w_int * cq[:dv, :]
        den = jnp.sum(sc, axis=0, keepdims=True) + w_int * cq[dv:dv + 1, :]
        h_ref[0, h * dv:(h + 1) * dv, :] = (num / jnp.maximum(jnp.abs(den), jnp.exp(-mt))).astype(h_ref.dtype)
        ws_h = ws[h:h + 1, :]
        lhs = jnp.concatenate([vh * ws_h, jnp.broadcast_to(ws_h, (ML_STATE_ROWS - dv, ML_CHUNK))], axis=0)
        own = (lane >= half * dk) & (lane < (half + 1) * dk)
        row = lax.broadcasted_iota(jnp.int32, (ML_STATE_ROWS, 1), 0)
        new = cw[h:h + 1, :] * state + _dot(lhs.astype(BF16), k_pair)
        c_st[h] = jnp.where(own & (row <= dv), new, 0.0)
    m_st[...] = jnp.broadcast_to(m_new, m_st.shape)


def _mlstm_kernel(qkf, vf, gf, qkb, vb, gb, gbias, hf_ref, hb_ref, cf, mf, cb, mb):
    @pl.when(pl.program_id(1) == 0)
    def _():
        for r in (cf, mf, cb, mb):
            r[...] = jnp.zeros_like(r)

    _mlstm_direction(qkf, vf, gf, gbias, cf, mf, hf_ref, 0)
    _mlstm_direction(qkb, vb, gb, gbias, cb, mb, hb_ref, 1)


def _mlstm_scan(qk, v, g, gate_b, nb, seg):
    t_tok = qk.shape[0]
    nchunk = seg // ML_CHUNK
    fwd, bwd = _scan_maps(nchunk, ML_CHUNK)
    widths = (qk.shape[1], v.shape[1], LANES)
    in_specs = ([pl.BlockSpec((ML_CHUNK, w), fwd) for w in widths] + [pl.BlockSpec((ML_CHUNK, w), bwd) for w in widths]
                + [pl.BlockSpec((1, LANES), lambda b, t: (0, 0))])
    state = [pltpu.VMEM((ML_HEADS, ML_STATE_ROWS, LANES), F32), pltpu.VMEM((ML_HEADS, LANES), F32)]
    out_map = lambda rows_map: (lambda b, t: (b, 0, rows_map(b, t)[0] - b * nchunk))
    return pl.pallas_call(
        _mlstm_kernel,
        grid=(nb, nchunk),
        in_specs=in_specs,
        out_specs=[pl.BlockSpec((1, v.shape[1], ML_CHUNK), out_map(fwd)),
                   pl.BlockSpec((1, v.shape[1], ML_CHUNK), out_map(bwd))],
        out_shape=[jax.ShapeDtypeStruct((nb, v.shape[1], seg), BF16)] * 2,
        scratch_shapes=state + state,
        compiler_params=_params("parallel", "arbitrary"),
        name="mlstm_scan",
    )(qk, v, g, qk, v, g, gate_b)


def _rms_rows(x, g_row):
    return x * lax.rsqrt(jnp.mean(x * x, axis=-1, keepdims=True) + EPS) * g_row


MLA_SCORE_SCALE = (MLA_NOPE + MLA_ROPE) ** -0.5 * LOG2_E


def _mla_q_kernel(cq_ref, g_ref, wt_ref, cos_ref, sin_ref, qt_ref):
    cq_t = jnp.transpose(_rms_rows(cq_ref[...], g_ref[...])).astype(BF16)
    q_t = _dot(wt_ref[...], cq_t) * MLA_SCORE_SCALE
    qt_ref[0] = q_t.astype(qt_ref.dtype)
    cos, sin = cos_ref[...], sin_ref[...]
    half = MLA_ROPE // 2
    for h in range(MLA_HEADS):
        r0 = h * LANES + MLA_NOPE
        x1, x2 = q_t[r0:r0 + half, :], q_t[r0 + half:r0 + MLA_ROPE, :]
        qt_ref[0, r0:r0 + half, :] = (x1 * cos - x2 * sin).astype(qt_ref.dtype)
        qt_ref[0, r0 + half:r0 + MLA_ROPE, :] = (x2 * cos + x1 * sin).astype(qt_ref.dtype)


def _mla_q(p3, gain, wq_t, tables_t, nt):
    t_tok = p3.shape[0]
    n = wq_t.shape[0]
    half = MLA_ROPE // 2
    return pl.pallas_call(
        _mla_q_kernel,
        grid=(t_tok // TM,),
        in_specs=[pl.BlockSpec((TM, MLA_RANK), lambda t: (t, 0)),
                  pl.BlockSpec((1, MLA_RANK), lambda t: (0, 0)),
                  pl.BlockSpec(wq_t.shape, lambda t: (0, 0)),
                  pl.BlockSpec((half, TM), lambda t: (0, t % nt)),
                  pl.BlockSpec((half, TM), lambda t: (0, t % nt))],
        out_specs=pl.BlockSpec((1, n, TM), lambda t: (t, 0, 0)),
        out_shape=jax.ShapeDtypeStruct((t_tok // TM, n, TM), BF16),
        compiler_params=_params("parallel"),
        name="mla_q",
    )(p3, gain.reshape(1, MLA_RANK), wq_t, *tables_t)


def _mla_kv_kernel(ckv_ref, kr_ref, g_ref, wk_ref, wvt_ref, cos_ref, sin_ref, k_ref, vt_ref):
    cn = _rms_rows(ckv_ref[...], g_ref[...])
    vt_ref[0] = _dot(wvt_ref[...], jnp.transpose(cn).astype(BF16)).astype(vt_ref.dtype)
    kn = _dot(cn.astype(BF16), wk_ref[...])
    kr = pltpu.roll(kr_ref[...], MLA_NOPE, axis=1)
    half = MLA_ROPE // 2
    lane = lax.broadcasted_iota(jnp.int32, (1, LANES), 1)
    partner = jnp.where(lane < MLA_NOPE + half, pltpu.roll(kr, LANES - half, axis=1), pltpu.roll(kr, half, axis=1))
    roped = kr * cos_ref[...] + partner * sin_ref[...]
    for h in range(MLA_HEADS):
        k_ref[:, h * LANES:(h + 1) * LANES] = (kn[:, h * LANES:(h + 1) * LANES] + roped).astype(k_ref.dtype)


def _mla_kv(p3, gain, w_k, wv_t, tables, nb, seg):
    t_tok = p3.shape[0]
    nt = seg // TM
    return pl.pallas_call(
        _mla_kv_kernel,
        grid=(nb, nt),
        in_specs=[pl.BlockSpec((TM, MLA_RANK), lambda b, j: (b * nt + j, 1)),
                  pl.BlockSpec((TM, LANES), lambda b, j: (b * nt + j, 2 * MLA_RANK // LANES)),
                  pl.BlockSpec((1, MLA_RANK), lambda b, j: (0, 0)),
                  pl.BlockSpec(w_k.shape, lambda b, j: (0, 0)),
                  pl.BlockSpec(wv_t.shape, lambda b, j: (0, 0)),
                  pl.BlockSpec((TM, LANES), lambda b, j: (j, 0)),
                  pl.BlockSpec((TM, LANES), lambda b, j: (j, 0))],
        out_specs=[pl.BlockSpec((TM, MLA_HEADS * LANES), lambda b, j: (b * nt + j, 0)),
                   pl.BlockSpec((1, MLA_HEADS * MLA_V, TM), lambda b, j: (b, 0, j))],
        out_shape=[jax.ShapeDtypeStruct((t_tok, MLA_HEADS * LANES), BF16),
                   jax.ShapeDtypeStruct((nb, MLA_HEADS * MLA_V, seg), BF16)],
        compiler_params=_params("parallel", "parallel"),
        name="mla_kv",
    )(p3, p3, gain.reshape(1, MLA_RANK), w_k, wv_t, *tables)


def _mla_attn_kernel(qt_ref, k_ref, vt_ref, o_ref):
    @pl.when(pl.program_id(2) == 0)
    def _():
        o_ref[...] = jnp.zeros_like(o_ref)

    @pl.when(pl.program_id(2) > 0)
    def _():
        heads = range(MLA_HEADS_PER_STEP)
        seg = k_ref.shape[0]
        tk = MLA_KEY_TILE if (seg - TM) % MLA_KEY_TILE == 0 else TM
        tiles = [(0, TM)] + [(a, a + tk) for a in range(TM, seg, tk)]

        def scores(hh, j):
            return _dot(k_ref[tiles[j][0]:tiles[j][1], hh * LANES:(hh + 1) * LANES],
                        qt_ref[0, hh * LANES:(hh + 1) * LANES, :])

        m = [jnp.full((1, TM), -jnp.inf, F32) for _ in heads]
        l = [jnp.zeros((1, TM), F32) for _ in heads]
        acc = [jnp.zeros((MLA_V, TM), F32) for _ in heads]
        ahead = min(MLA_LOOKAHEAD, len(tiles))
        pending = [[scores(hh, j) for hh in heads] for j in range(ahead)]
        for j in range(len(tiles)):
            s_cur = pending.pop(0)
            if j + ahead < len(tiles):
                pending.append([scores(hh, j + ahead) for hh in heads])
            for hh in heads:
                m_new = jnp.maximum(m[hh], jnp.max(s_cur[hh], axis=0, keepdims=True))
                alpha = jnp.exp2(m[hh] - m_new)
                p = jnp.exp2(s_cur[hh] - m_new)
                l[hh] = alpha * l[hh] + jnp.sum(p, axis=0, keepdims=True)
                v_t = vt_ref[0, hh * MLA_V:(hh + 1) * MLA_V, tiles[j][0]:tiles[j][1]]
                acc[hh] = alpha * acc[hh] + _dot(v_t, p.astype(BF16))
                m[hh] = m_new
        o_t = jnp.concatenate([acc[hh] / l[hh] for hh in heads], axis=0)
        o_ref[...] = jnp.transpose(o_t).astype(o_ref.dtype)


def _mla_attention(q_t, k, v_t, nb, seg):
    t_tok = k.shape[0]
    nt = seg // TM
    hps = MLA_HEADS_PER_STEP
    return pl.pallas_call(
        _mla_attn_kernel,
        grid=(nb, MLA_HEADS // hps, nt),
        in_specs=[pl.BlockSpec((1, hps * LANES, TM), lambda b, hp, j: (b * nt + j, hp, 0)),
                  pl.BlockSpec((seg, hps * LANES), lambda b, hp, j: (b, hp)),
                  pl.BlockSpec((1, hps * MLA_V, seg), lambda b, hp, j: (b, hp, 0))],
        out_specs=pl.BlockSpec((TM, hps * MLA_V), lambda b, hp, j: (b * nt + j, hp)),
        out_shape=jax.ShapeDtypeStruct((t_tok, MLA_HEADS * MLA_V), BF16),
        compiler_params=_params("parallel", "parallel", "arbitrary"),
        name="mla_attention",
    )(q_t, k, v_t)


def _final_kernel(x_ref, g_ref, o_ref):
    o_ref[0] = _rms_rows(x_ref[...], g_ref[...])


def _final_norm(x, gain, nb, seg):
    d = x.shape[1]
    nt = seg // TM
    return pl.pallas_call(
        _final_kernel,
        grid=(nb, nt - 1),
        in_specs=[pl.BlockSpec((TM, d), lambda b, j: (b * nt + 1 + j, 0)),
                  pl.BlockSpec((1, d), lambda b, j: (0, 0))],
        out_specs=pl.BlockSpec((1, TM, d), lambda b, j: (b, j, 0)),
        out_shape=jax.ShapeDtypeStruct((nb, seg - TM, d), F32),
        compiler_params=_params("parallel", "parallel"),
        name="final_norm",
    )(x, gain.reshape(1, d))


def _rope_angles(n_lat, rot_dim):
    rows = n_lat // GRID_W
    row = jnp.repeat(jnp.arange(rows), GRID_W).astype(F32)
    col = jnp.tile(jnp.arange(GRID_W), rows).astype(F32)
    quarter = rot_dim // 4
    inv = ROPE_BASE ** (-jnp.arange(quarter, dtype=F32) / quarter)
    ang = jnp.concatenate([row[:, None] * inv, col[:, None] * inv], axis=-1)
    return jnp.cos(ang), jnp.sin(ang)


def _with_ctx_rows(tab, fill):
    return jnp.concatenate([jnp.full((TM, tab.shape[1]), fill, F32), tab], axis=0)


def _attn_rope_tables(n_lat):
    cos, sin = _rope_angles(n_lat, ATT_HEAD_DIM)
    cos_h = jnp.concatenate([cos, cos], axis=1)
    sin_h = jnp.concatenate([-sin, sin], axis=1)
    reps = LANES // ATT_HEAD_DIM
    return (_with_ctx_rows(jnp.tile(cos_h, (1, reps)), 1.0), _with_ctx_rows(jnp.tile(sin_h, (1, reps)), 0.0))


def _swap_halves(w, n_heads, dim):
    w3 = w.reshape(w.shape[0], n_heads, 2, dim // 2)
    return w3[:, :, ::-1, :].reshape(w.shape[0], n_heads * dim)


def _mla_k_tables(n_lat):
    cos, sin = _rope_angles(n_lat, MLA_ROPE)
    lo = jnp.zeros((n_lat, MLA_NOPE), F32)
    hi = jnp.zeros((n_lat, LANES - MLA_NOPE - MLA_ROPE), F32)
    cos_c = jnp.concatenate([lo, cos, cos, hi], axis=1)
    sin_c = jnp.concatenate([lo, -sin, sin, hi], axis=1)
    ctx_cos = jnp.concatenate([lo[:TM], jnp.ones((TM, MLA_ROPE), F32), hi[:TM]], axis=1)
    return (jnp.concatenate([ctx_cos, cos_c], axis=0), _with_ctx_rows(sin_c, 0.0))


def _pad_heads(w, real):
    r = w.shape[0]
    w3 = w.reshape(r, MLA_HEADS, real)
    return jnp.pad(w3, ((0, 0), (0, 0), (0, LANES - real))).reshape(r, MLA_HEADS * LANES)


def kernel(x, c, ctx, c_ctx, norm1_g, norm2_g, w_mod, b_mod, moe_w_group, moe_b_group, moe_w_expert, moe_b_expert, moe_w_gate, moe_w_up, moe_w_down, attn_w_in, attn_sink, attn_w_out, ssm_w_in, ssm_conv_w, ssm_conv_b, ssm_dt_bias, ssm_a_log, ssm_d, ssm_norm_g, ssm_w_out, mlstm_w_in, mlstm_gate_b, mlstm_norm_g, mlstm_w_out, mla_w_in, mla_q_norm_g, mla_w_q_up, mla_kv_norm_g, mla_w_kv_up, mla_w_out, final_norm_g):
    nb, n_lat, d = x.shape
    assert ctx.shape[1] == TM and d == D_MODEL and n_lat % TM == 0
    depth = w_mod.shape[0]
    seg = TM + n_lat
    nt = seg // TM
    t_tok = nb * seg

    xs = (x, ctx)

    rows = -(-(nb + 1) // 8) * 8
    cvec = jnp.concatenate([c, c_ctx[None, :], jnp.zeros((rows - nb - 1, d), F32)], axis=0)
    mods = _modulation(cvec, w_mod, b_mod).reshape(depth, rows, ADALN_CHUNKS, d)
    mods = jnp.pad(mods, ((0, 0), (0, 0), (0, MOD_ROWS - ADALN_CHUNKS), (0, 0)))

    w_router = jnp.concatenate([moe_w_expert, moe_w_group,
                                jnp.zeros((depth, d, LANES - MOE_EXPERTS - MOE_GROUPS), F32)], axis=-1)
    b_router = jnp.concatenate([moe_b_expert, moe_b_group,
                                jnp.zeros((depth, LANES - MOE_EXPERTS - MOE_GROUPS), F32)], axis=-1)
    w_gate, w_up = moe_w_gate.astype(BF16), moe_w_up.astype(BF16)
    w_down = moe_w_down.astype(BF16).reshape(depth, MOE_GROUPS, MOE_PER_GROUP * MOE_FF, d)

    for i in range(depth):
        kind = i % 4
        mod_i = mods[i]
        if kind == 0:
            nq, nk = ATT_HEADS * ATT_HEAD_DIM, ATT_KV_HEADS * ATT_HEAD_DIM
            w_in = attn_w_in[i // 4]
            w_all = jnp.concatenate([w_in, _swap_halves(w_in[:, :nq], ATT_HEADS, ATT_HEAD_DIM),
                                     _swap_halves(w_in[:, nq:nq + nk], ATT_KV_HEADS, ATT_HEAD_DIM)],
                                    axis=1).astype(BF16)
            n_in = w_in.shape[1]
            q, k, v = _normproj(xs, mod_i, norm1_g[i], w_all,
                                [(0, nq, n_in, ATT_HEAD_DIM ** -0.5 * LOG2_E), (nq, nk, n_in + nq), (nq + nk, nk, None)],
                                [BF16, BF16, BF16], nt, nb, tables=_attn_rope_tables(n_lat))
            o = _windowed_attention(q, k, v, attn_sink[i // 4], nb, seg)
            xs = _outproj([o], [nq], xs, mod_i, attn_w_out[i // 4].astype(BF16), _plain_prologue, nt, nb)
        elif kind == 1:
            j = i // 4
            w_in = jnp.pad(ssm_w_in[j], ((0, 0), (0, LANES - 2 * SSM_HEADS))).astype(BF16)
            lane_pad = LANES - 2 * SSM_HEADS
            dt_bias = jnp.pad(ssm_dt_bias[j].reshape(1, -1), ((0, 0), (0, lane_pad)))
            a_neg = jnp.pad(-jnp.exp(ssm_a_log[j].astype(F32)).reshape(1, -1), ((0, 0), (0, lane_pad)))
            conv_w = jnp.pad(ssm_conv_w[j], ((0, 8 - SSM_CONV), (0, 0)))
            z, xc, bm, cm, dtv, a = _ssm_in(xs, mod_i, norm1_g[i], w_in, conv_w, ssm_conv_b[j].reshape(1, -1),
                                            dt_bias, a_neg, nt, nb)
            yf, yb = _ssd_scan(xc, bm, cm, dtv, a, ssm_d[j].astype(F32), nb, seg)
            xs = _outproj([z], [SSM_D_INNER], xs, mod_i, ssm_w_out[j].astype(BF16), _ssm_prologue,
                          nt, nb, extra=(ssm_norm_g[j].reshape(1, -1),), ins_t=(yf, yb))
        elif kind == 2:
            j = i // 4
            nqk, nv = 2 * ML_HEADS * ML_QK_DIM, ML_HEADS * ML_V_DIM
            w_in = jnp.pad(mlstm_w_in[j], ((0, 0), (0, LANES - 4 * ML_HEADS))).astype(BF16)
            qk, v, o, g = _normproj(xs, mod_i, norm1_g[i], w_in,
                                    [(0, nqk, None), (nqk, nv, None), (nqk + nv, nv, None), (nqk + 2 * nv, LANES, None)],
                                    [BF16, BF16, BF16, F32], nt, nb)
            gate_b = jnp.pad(mlstm_gate_b[j].reshape(1, -1), ((0, 0), (0, LANES - 4 * ML_HEADS)))
            hf, hb = _mlstm_scan(qk, v, g, gate_b, nb, seg)
            xs = _outproj([o], [nv], xs, mod_i, mlstm_w_out[j].astype(BF16), _mlstm_prologue,
                          nt, nb, extra=(mlstm_norm_g[j].reshape(1, -1),), ins_t=(hf, hb))
        else:
            j = i // 4
            w_in = jnp.pad(mla_w_in[j], ((0, 0), (0, LANES - MLA_ROPE))).astype(BF16)
            (p3,) = _normproj(xs, mod_i, norm1_g[i], w_in, [(0, w_in.shape[1], None)], [F32], nt, nb)
            cos, sin = _rope_angles(n_lat, MLA_ROPE)
            tables_t = (jnp.transpose(_with_ctx_rows(cos, 1.0)), jnp.transpose(_with_ctx_rows(sin, 0.0)))
            wq_t = jnp.transpose(_pad_heads(mla_w_q_up[j], MLA_NOPE + MLA_ROPE)).astype(BF16)
            q_t = _mla_q(p3, mla_q_norm_g[j], wq_t, tables_t, nt)
            w_kv = mla_w_kv_up[j].reshape(MLA_RANK, MLA_HEADS, MLA_NOPE + MLA_V)
            w_k = _pad_heads(w_kv[:, :, :MLA_NOPE].reshape(MLA_RANK, MLA_HEADS * MLA_NOPE), MLA_NOPE).astype(BF16)
            wv_t = jnp.transpose(w_kv[:, :, MLA_NOPE:].reshape(MLA_RANK, MLA_HEADS * MLA_V)).astype(BF16)
            k, v_t = _mla_kv(p3, mla_kv_norm_g[j], w_k, wv_t, _mla_k_tables(n_lat), nb, seg)
            o = _mla_attention(q_t, k, v_t, nb, seg)
            xs = _outproj([o], [MLA_HEADS * MLA_V], xs, mod_i, mla_w_out[j].astype(BF16), _plain_prologue, nt, nb)

        xs = _moe(xs, mod_i, norm2_g[i], w_router[i], b_router[i].reshape(1, -1), w_gate, w_up, w_down, i, nt, nb)

    return _final_norm(xs, final_norm_g, nb, seg)
```

```python
import functools
import math

import jax
import jax.numpy as jnp
from jax import lax
from jax.experimental import pallas as pl
from jax.experimental.pallas import tpu as pltpu

F32 = jnp.float32
BF16 = jnp.bfloat16

D_MODEL = 1024
GRID_W = 64
EPS = 1e-6
ROPE_BASE = 10000.0
ADALN_CHUNKS = 6
CHUNK = 128
SSD_CHUNK = 256
ML_CHUNK = 128
TM = 256
MOD_ROWS = 8
LANES = 128
V7X_VMEM_LIMIT = 48 * 1024 * 1024

ATT_HEADS, ATT_KV_HEADS, ATT_HEAD_DIM, WINDOW = 16, 4, 64, 128
ATT_GROUP = ATT_HEADS // ATT_KV_HEADS
ATT_UNIT_HEADS = 4
SSM_D_INNER, SSM_HEAD_DIM, SSM_HEADS, SSM_GROUPS, SSM_STATE, SSM_CONV = 2048, 64, 32, 4, 128, 5
SSM_HPG = SSM_HEADS // SSM_GROUPS
SSM_BC = SSM_GROUPS * SSM_STATE
SSM_CONV_DIM = SSM_D_INNER + 2 * SSM_BC
SSM_CONV_CHUNK = 256
ML_HEADS, ML_QK_DIM, ML_V_DIM = 8, 64, 128
MLA_HEADS, MLA_RANK, MLA_NOPE, MLA_ROPE, MLA_V = 16, 256, 64, 32, 64
MOE_GROUPS, MOE_PER_GROUP, MOE_EXPERTS, MOE_FF = 4, 4, 16, 256
MOE_ROWS = 1024
MLA_KEY_TILE = 1024
MLA_HEADS_PER_STEP = 4
MLA_LOOKAHEAD = 1
LOG2_E = 1.4426950408889634


def _dot(a, b):
    return jnp.dot(a, b, preferred_element_type=F32)


def _dot_nt(a, b):
    return lax.dot_general(a, b, (((1,), (1,)), ((), ())), preferred_element_type=F32)


def _split_bf16(x):
    hi = x.astype(BF16)
    lo = (x - hi.astype(F32)).astype(BF16)
    return hi, lo


def _dot_split(a, b):
    a_hi, a_lo = _split_bf16(a)
    b_hi, b_lo = _split_bf16(b)
    return _dot(a_hi, b_hi) + _dot(a_lo, b_hi) + _dot(a_hi, b_lo)


def _sigmoid(x):
    return 1.0 / (1.0 + jnp.exp(-x))


def _silu(x):
    return x * _sigmoid(x)


def _softplus(x):
    return jnp.maximum(x, 0.0) + jnp.log1p(jnp.exp(-jnp.abs(x)))


def _params(*sem):
    return pltpu.CompilerParams(dimension_semantics=sem, vmem_limit_bytes=V7X_VMEM_LIMIT)


def _mod_row(t, nt, nb):
    return jnp.where(t % nt == 0, nb, t // nt)


def _mod_kernel(c_ref, w_ref, b_ref, o_ref):
    o_ref[0] = _dot_split(_silu(c_ref[...]), w_ref[0]) + b_ref[0]


def _modulation(cvec, w_mod, b_mod):
    depth, d, n = w_mod.shape
    tn = 1536
    rows = cvec.shape[0]
    return pl.pallas_call(
        _mod_kernel,
        grid=(depth, n // tn),
        in_specs=[pl.BlockSpec((rows, d), lambda l, j: (0, 0)),
                  pl.BlockSpec((1, d, tn), lambda l, j: (l, 0, j)),
                  pl.BlockSpec((1, 1, tn), lambda l, j: (l, 0, j))],
        out_specs=pl.BlockSpec((1, rows, tn), lambda l, j: (l, 0, j)),
        out_shape=jax.ShapeDtypeStruct((depth, rows, n), F32),
        compiler_params=_params("arbitrary", "arbitrary"),
        name="modulation",
    )(cvec, w_mod, b_mod.reshape(depth, 1, n))


def _normed(x, g_row, mod, sh_row, sc_row):
    y = x * lax.rsqrt(jnp.mean(x * x, axis=-1, keepdims=True) + EPS) * g_row
    return y * (1.0 + mod[sc_row:sc_row + 1, :]) + mod[sh_row:sh_row + 1, :]


def _proj_columns(hb, w_ref, o_ref, start, width, rot_start, cos, sin, scale=None, chunk=512):
    for c in range(0, width, chunk):
        cw = min(chunk, width - c)
        acc = _dot(hb, w_ref[:, start + c:start + c + cw])
        if rot_start is not None:
            rot = _dot(hb, w_ref[:, rot_start + c:rot_start + c + cw])
            reps = cw // LANES
            acc = acc * jnp.tile(cos, (1, reps)) + rot * jnp.tile(sin, (1, reps))
        if scale is not None:
            acc = acc * scale
        o_ref[:, c:c + cw] = acc.astype(o_ref.dtype)


def _stream_specs(x, nt):
    if isinstance(x, tuple):
        d = x[0].shape[-1]
        return [pl.BlockSpec((1, TM, d), lambda t: (t // nt, jnp.maximum(t % nt - 1, 0), 0)),
                pl.BlockSpec((1, TM, d), lambda t: (t // nt, 0, 0))], list(x)
    sub = _tiles_per_step(x, nt)
    return [pl.BlockSpec((sub * TM, x.shape[1]), lambda t: (t, 0))], [x]


def _tiles_per_step(x, nt):
    if isinstance(x, tuple):
        return 1
    return 2 if (x.shape[0] // TM) % 2 == 0 else 1


def _stream_tile(x_refs, nt, s):
    if len(x_refs) == 1:
        return x_refs[0][s * TM:(s + 1) * TM, :]
    return jnp.where(pl.program_id(0) % nt == 0, x_refs[1][0], x_refs[0][0])


def _normproj_kernel(*refs, outs, has_rope, n_x, nt, nb, sub):
    x_refs, refs = refs[:n_x], refs[n_x:]
    mods_ref, g_ref, w_ref = refs[:3]
    k = 3
    cos = sin = None
    if has_rope:
        cos, sin = refs[3][...], refs[4][...]
        k = 5
    hs = []
    for s in range(sub):
        mod = mods_ref[_mod_row(pl.program_id(0) * sub + s, nt, nb)]
        hs.append(_normed(_stream_tile(x_refs, nt, s), g_ref[...], mod, 0, 1).astype(BF16))
    hb = hs[0] if sub == 1 else jnp.concatenate(hs, axis=0)
    for o_ref, spec in zip(refs[k:], outs):
        _proj_columns(hb, w_ref, o_ref, *spec[:3], cos, sin, scale=spec[3] if len(spec) > 3 else None)


def _normproj(x, mods, gain, w, outs, out_dtypes, nt, nb, tables=None):
    d, n = w.shape
    t_tok = nb * nt * TM
    sub = _tiles_per_step(x, nt)
    assert tables is None or sub == 1
    rows = sub * TM
    x_specs, x_args = _stream_specs(x, nt)
    in_specs = x_specs + [pl.BlockSpec(mods.shape, lambda t: (0, 0, 0)),
                          pl.BlockSpec((1, d), lambda t: (0, 0)),
                          pl.BlockSpec((d, n), lambda t: (0, 0))]
    args = x_args + [mods, gain.reshape(1, d), w]
    if tables is not None:
        in_specs += [pl.BlockSpec((TM, LANES), lambda t: (t % nt, 0))] * 2
        args += list(tables)
    return pl.pallas_call(
        functools.partial(_normproj_kernel, outs=tuple(outs), has_rope=tables is not None, n_x=len(x_args),
                          nt=nt, nb=nb, sub=sub),
        grid=(t_tok // rows,),
        in_specs=in_specs,
        out_specs=[pl.BlockSpec((rows, o[1]), lambda t: (t, 0)) for o in outs],
        out_shape=[jax.ShapeDtypeStruct((t_tok, o[1]), dt) for o, dt in zip(outs, out_dtypes)],
        compiler_params=_params("parallel"),
        name="normproj",
    )(*args)


def _outproj_kernel(*refs, prologue, n_t, n_tok, n_extra, n_x, nt, nb, sub):
    ins_t = refs[:n_t * sub]
    ins = refs[n_t * sub:n_t * sub + n_tok]
    extra = refs[n_t * sub + n_tok:n_t * sub + n_tok + n_extra]
    x_refs = refs[n_t * sub + n_tok + n_extra:n_t * sub + n_tok + n_extra + n_x]
    mods_ref, w_ref, o_ref = refs[n_t * sub + n_tok + n_extra + n_x:]
    parts = []
    for s in range(sub):
        tok = [r if sub == 1 else r.at[pl.ds(s * TM, TM), :] for r in ins]
        parts.append(prologue(*ins_t[s * n_t:(s + 1) * n_t], *tok, *extra))
    a = parts[0] if sub == 1 else jnp.concatenate(parts, axis=0)
    y = _dot(a, w_ref[...])
    for s in range(sub):
        rows = slice(s * TM, (s + 1) * TM)
        gate = mods_ref[_mod_row(pl.program_id(0) * sub + s, nt, nb)][2:3, :]
        o_ref[rows, :] = _stream_tile(x_refs, nt, s) + gate * y[rows, :]


def _outproj(ins, in_widths, x, mods, w, prologue, nt, nb, extra=(), ins_t=()):
    d = w.shape[1]
    t_tok = nb * nt * TM
    sub = _tiles_per_step(x, nt)
    rows = sub * TM

    def tile_map(s):
        return lambda t: ((t * sub + s) // nt, 0, (t * sub + s) % nt)

    in_specs, args = [], []
    for s in range(sub):
        in_specs += [pl.BlockSpec((1, a.shape[1], TM), tile_map(s)) for a in ins_t]
        args += list(ins_t)
    in_specs += [pl.BlockSpec((rows, wd), lambda t: (t, 0)) for wd in in_widths]
    in_specs += [pl.BlockSpec(e.shape, lambda t: (0, 0)) for e in extra]
    x_specs, x_args = _stream_specs(x, nt)
    n_before_x = len(in_specs)
    in_specs += x_specs + [pl.BlockSpec(mods.shape, lambda t: (0, 0, 0)), pl.BlockSpec(w.shape, lambda t: (0, 0))]
    return pl.pallas_call(
        functools.partial(_outproj_kernel, prologue=prologue, n_t=len(ins_t), n_tok=len(ins), n_extra=len(extra),
                          n_x=len(x_args), nt=nt, nb=nb, sub=sub),
        grid=(t_tok // rows,),
        in_specs=in_specs,
        out_specs=pl.BlockSpec((rows, d), lambda t: (t, 0)),
        out_shape=jax.ShapeDtypeStruct((t_tok, d), F32),
        input_output_aliases={} if isinstance(x, tuple) else {n_before_x: 0},
        compiler_params=_params("parallel"),
        name="outproj",
    )(*args, *ins, *extra, *x_args, mods, w)


def _group_rms(y, n_groups):
    width = y.shape[1] // n_groups
    parts = []
    for g in range(n_groups):
        yg = y[:, g * width:(g + 1) * width]
        parts.append(yg * lax.rsqrt(jnp.mean(yg * yg, axis=-1, keepdims=True) + EPS))
    return jnp.concatenate(parts, axis=1)


def _plain_prologue(o_ref):
    return o_ref[...]


def _ssm_prologue(yf_ref, yb_ref, z_ref, g_ref):
    y = jnp.transpose(yf_ref[0].astype(F32) + yb_ref[0].astype(F32)) * _silu(z_ref[...].astype(F32))
    return (_group_rms(y, SSM_GROUPS) * g_ref[...]).astype(BF16)


def _mlstm_prologue(hf_ref, hb_ref, o_ref, g_ref):
    h = jnp.transpose(hf_ref[0].astype(F32) + hb_ref[0].astype(F32))
    return (_group_rms(h, ML_HEADS) * g_ref[...] * _sigmoid(o_ref[...].astype(F32))).astype(BF16)


MOE_ROUTE_ROWS = 24


def _route(logits):
    lt = jnp.transpose(logits)[:MOE_ROUTE_ROWS, :]
    row = lax.broadcasted_iota(jnp.int32, lt.shape, 0).astype(F32)
    neg = -jnp.inf
    lg = jnp.where((row >= MOE_EXPERTS) & (row < MOE_EXPERTS + MOE_GROUPS), lt, neg)
    gmax = jnp.max(lg, axis=0, keepdims=True)
    g_sel = jnp.min(jnp.where(lg == gmax, row, LANES), axis=0, keepdims=True) - MOE_EXPERTS
    p_g = 1.0 / jnp.sum(jnp.exp(lg - gmax), axis=0, keepdims=True)
    in_group = (row >= g_sel * MOE_PER_GROUP) & (row < (g_sel + 1) * MOE_PER_GROUP)
    le = jnp.where(in_group, lt, neg)
    v1 = jnp.max(le, axis=0, keepdims=True)
    i1 = jnp.min(jnp.where(le == v1, row, LANES), axis=0, keepdims=True)
    le2 = jnp.where(row == i1, neg, le)
    v2 = jnp.max(le2, axis=0, keepdims=True)
    i2 = jnp.min(jnp.where(le2 == v2, row, LANES), axis=0, keepdims=True)
    e2 = jnp.exp(v2 - v1)
    w1 = p_g / (1.0 + e2)
    comb_t = jnp.where(row == i1, w1, 0.0) + jnp.where(row == i2, w1 * e2, 0.0)
    pad = jnp.zeros((LANES - MOE_ROUTE_ROWS, lt.shape[1]), F32)
    return jnp.transpose(jnp.concatenate([comb_t, pad], axis=0))


def _moe_kernel(x_ref, mods_ref, g_ref, wr_ref, br_ref, wg_ref, wu_ref, wd_ref, o_ref, h_scr, comb_scr,
                *, nt, nb, sub):
    i = pl.program_id(0)
    e = pl.program_id(1)

    def experts(rows):
        hb = h_scr[rows, :]
        comb = comb_scr[rows, :]
        lane = lax.broadcasted_iota(jnp.int32, comb.shape, 1)
        scaled = []
        for k in range(MOE_PER_GROUP):
            act = _silu(_dot(hb, wg_ref[0, k])) * _dot(hb, wu_ref[0, k])
            cw = jnp.sum(jnp.where(lane == e * MOE_PER_GROUP + k, comb, 0.0), axis=-1, keepdims=True)
            scaled.append((act * cw).astype(BF16))
        return _dot(jnp.concatenate(scaled, axis=1), wd_ref[0, 0])

    def gate_row(s):
        return mods_ref[_mod_row(i * sub + s, nt, nb)][5:6, :]

    @pl.when(e == 0)
    def _():
        w_hi, w_lo = _split_bf16(wr_ref[...])
        w_both = jnp.concatenate([w_hi, w_lo], axis=1)
        for s in range(sub):
            rows = slice(s * TM, (s + 1) * TM)
            mod = mods_ref[_mod_row(i * sub + s, nt, nb)]
            x = x_ref[rows, :]
            h = _normed(x, g_ref[...], mod, 3, 4)
            h_hi, h_lo = _split_bf16(h)
            h_scr[rows, :] = h_hi
            both = _dot(h_hi, w_both)
            logits = both[:, :LANES] + both[:, LANES:] + _dot(h_lo, w_hi) + br_ref[...]
            comb_scr[rows, :] = _route(logits)
            o_ref[rows, :] = x + gate_row(s) * experts(rows)

    @pl.when(e > 0)
    def _():
        y = experts(slice(0, sub * TM))
        for s in range(sub):
            rows = slice(s * TM, (s + 1) * TM)
            o_ref[rows, :] += gate_row(s) * y[rows, :]


def _moe(x, mods, gain, w_router, b_router, w_gate, w_up, w_down, layer, nt, nb):
    t_tok, d = x.shape
    pg = MOE_PER_GROUP
    rows = MOE_ROWS if t_tok % MOE_ROWS == 0 else TM
    sub = rows // TM
    return pl.pallas_call(
        functools.partial(_moe_kernel, nt=nt, nb=nb, sub=sub),
        grid=(t_tok // rows, MOE_GROUPS),
        in_specs=[pl.BlockSpec((rows, d), lambda i, e: (i, 0)),
                  pl.BlockSpec(mods.shape, lambda i, e: (0, 0, 0)),
                  pl.BlockSpec((1, d), lambda i, e: (0, 0)),
                  pl.BlockSpec((d, LANES), lambda i, e: (0, 0)),
                  pl.BlockSpec((1, LANES), lambda i, e: (0, 0)),
                  pl.BlockSpec((1, pg, d, MOE_FF), lambda i, e: (layer, e, 0, 0)),
                  pl.BlockSpec((1, pg, d, MOE_FF), lambda i, e: (layer, e, 0, 0)),
                  pl.BlockSpec((1, 1, pg * MOE_FF, d), lambda i, e: (layer, e, 0, 0))],
        out_specs=pl.BlockSpec((rows, d), lambda i, e: (i, 0)),
        out_shape=jax.ShapeDtypeStruct((t_tok, d), F32),
        scratch_shapes=[pltpu.VMEM((rows, d), BF16), pltpu.VMEM((rows, LANES), F32)],
        input_output_aliases={0: 0},
        compiler_params=_params("parallel", "arbitrary"),
        name="moe",
    )(x, mods, gain.reshape(1, d), w_router, b_router, w_gate, w_up, w_down)


def _attn_kernel(sink_ref, q_ref, kp_ref, kc_ref, kn_ref, vp_ref, vc_ref, vn_ref, kx_ref, vx_ref, o_ref, *, n_lat):
    j = pl.program_id(1)
    jl = j - 1
    dh = ATT_HEAD_DIM
    n_loc = TM + 2 * CHUNK
    k_all = jnp.concatenate([kp_ref[...], kc_ref[...], kn_ref[...], kx_ref[...]], axis=0)
    v_t = jnp.transpose(jnp.concatenate([vp_ref[...], vc_ref[...], vn_ref[...], vx_ref[...]],
                                        axis=0).astype(F32)).astype(BF16)
    q_t = jnp.transpose(q_ref[...].astype(F32)).astype(BF16)
    n_keys = k_all.shape[0]
    kj = lax.broadcasted_iota(jnp.int32, (n_keys, TM), 0)
    qi = lax.broadcasted_iota(jnp.int32, (n_keys, TM), 1)
    kpos = jl * TM - CHUNK + kj
    local_ok = (kj >= qi + CHUNK - WINDOW) & (kj <= qi + CHUNK + WINDOW) & (kpos >= 0) & (kpos < n_lat) & (jl >= 0)
    bias = jnp.where((kj >= n_loc) | local_ok, 0.0, -jnp.inf)
    uh = ATT_UNIT_HEADS
    bias = jnp.concatenate([bias] * uh, axis=1)
    lane_head = lax.broadcasted_iota(jnp.int32, (1, uh * TM), 1) // TM
    zeros_q = jnp.zeros((dh, uh * TM), BF16)
    units = [(h0 // ATT_GROUP, h0) for h0 in range(0, ATT_HEADS, uh)]

    def scores(u):
        g, h0 = units[u]
        half = g % 2
        k_pair = k_all[:, (g - half) * dh:(g - half + 2) * dh]
        qg = jnp.concatenate([q_t[(h0 + a) * dh:(h0 + a + 1) * dh, :] for a in range(uh)], axis=1)
        q_m = jnp.concatenate([zeros_q, qg] if half else [qg, zeros_q], axis=0)
        return _dot(k_pair, q_m) + bias

    outs = []
    s_next = scores(0)
    for u, (g, h0) in enumerate(units):
        s = s_next
        if u + 1 < len(units):
            s_next = scores(u + 1)
        sink = jnp.zeros((1, uh * TM), F32)
        for a in range(uh):
            sink = jnp.where(lane_head == a, sink_ref[h0 + a] * LOG2_E, sink)
        m = jnp.maximum(jnp.max(s, axis=0, keepdims=True), sink)
        p = jnp.exp2(s - m)
        den = jnp.sum(p, axis=0, keepdims=True) + jnp.exp2(sink - m)
        og = _dot(v_t[g * dh:(g + 1) * dh, :], p.astype(BF16)) / den
        outs += [og[:, a * TM:(a + 1) * TM] for a in range(uh)]
    o_ref[...] = jnp.transpose(jnp.concatenate(outs, axis=0)).astype(o_ref.dtype)


def _windowed_attention(q, k, v, sink, nb, seg):
    t_tok = q.shape[0]
    nchunk = seg // CHUNK
    nt = seg // TM
    per_tile = TM // CHUNK
    kvw = ATT_KV_HEADS * ATT_HEAD_DIM

    def halo(off):
        return lambda b, j, *_: (b * nchunk + jnp.clip(j * per_tile + off, per_tile, nchunk - 1), 0)

    tile = lambda b, j, *_: (b * nt + j, 0)
    kv_spec = [pl.BlockSpec((CHUNK, kvw), halo(-1)), pl.BlockSpec((TM, kvw), tile),
               pl.BlockSpec((CHUNK, kvw), halo(per_tile))]
    ctx_spec = pl.BlockSpec((TM, kvw), lambda b, j, *_: (b * nt, 0))
    return pl.pallas_call(
        functools.partial(_attn_kernel, n_lat=seg - TM),
        grid_spec=pltpu.PrefetchScalarGridSpec(
            num_scalar_prefetch=1,
            grid=(nb, nt),
            in_specs=[pl.BlockSpec((TM, q.shape[1]), tile)] + kv_spec + kv_spec + [ctx_spec, ctx_spec],
            out_specs=pl.BlockSpec((TM, q.shape[1]), tile),
        ),
        out_shape=jax.ShapeDtypeStruct((t_tok, q.shape[1]), BF16),
        compiler_params=_params("parallel", "parallel"),
        name="windowed_attention",
    )(sink, q, k, k, k, v, v, v, k, v)


def _ssm_in_kernel(x_ref, xp_ref, xn_ref, mods_ref, g_ref, w_ref, cw_ref, cb_ref, dtb_ref, aneg_ref,
                   z_ref, xs_ref, bm_ref, cm_ref, dtv_ref, a_ref, *ext, nt, nb):
    t = pl.program_id(0)
    tl = t % nt
    halo = 8
    first = (tl == 0) | (tl == 1)
    last = (tl == 0) | (tl == nt - 1)
    gain = g_ref[...]

    def normed(x, tile):
        return _normed(x, gain, mods_ref[_mod_row(tile, nt, nb)], 0, 1)

    h = normed(x_ref[...], t)
    h_prev = jnp.where(first, 0.0, normed(xp_ref[...], jnp.maximum(t - 1, 0)))
    h_next = jnp.where(last, 0.0, normed(xn_ref[...], jnp.minimum(t + 1, pl.num_programs(0) - 1)))
    hb = h.astype(BF16)
    hb_ext = jnp.concatenate([h_prev, h, h_next], axis=0).astype(BF16)
    chunk = SSM_CONV_CHUNK
    pad = SSM_CONV // 2

    def project(c):
        proj = _dot(hb_ext, w_ref[:, SSM_D_INNER + c:SSM_D_INNER + c + chunk])
        ext[c // chunk][...] = proj

    project(0)
    for c in range(0, SSM_CONV_DIM, chunk):
        if c + chunk < SSM_CONV_DIM:
            project(c + chunk)
        if c < SSM_D_INNER:
            z_ref[:, c:c + chunk] = _dot(hb, w_ref[:, c:c + chunk]).astype(z_ref.dtype)
        acc = jnp.broadcast_to(cb_ref[:, c:c + chunk], (TM, chunk))
        for k in range(SSM_CONV):
            acc = acc + cw_ref[k:k + 1, c:c + chunk] * ext[c // chunk][halo - pad + k:halo - pad + k + TM, :]
        y = _silu(acc)
        if c < SSM_D_INNER:
            xs_ref[0, c:c + chunk, :] = jnp.transpose(y).astype(xs_ref.dtype)
        elif c < SSM_D_INNER + SSM_BC:
            bm_ref[:, c - SSM_D_INNER:c - SSM_D_INNER + chunk] = y.astype(bm_ref.dtype)
        else:
            off = c - SSM_D_INNER - SSM_BC
            cm_ref[0, off:off + chunk, :] = jnp.transpose(y).astype(cm_ref.dtype)
    dt = _dot(hb, w_ref[:, SSM_D_INNER + SSM_CONV_DIM:])
    dtv = _softplus(dt + dtb_ref[...])
    dtv_ref[0] = jnp.transpose(dtv)
    a_ref[0] = jnp.transpose(dtv * aneg_ref[...] * LOG2_E)


def _ssm_in(x, mods, gain, w, conv_w, conv_b, dt_bias, a_neg, nt, nb):
    t_tok, d = x.shape
    n8 = t_tok // 8
    seg = nt * TM
    row = lambda t: (t, 0)
    col = lambda t: (t // nt, 0, t % nt)
    fixed = lambda t: (0, 0)
    return pl.pallas_call(
        functools.partial(_ssm_in_kernel, nt=nt, nb=nb),
        grid=(t_tok // TM,),
        in_specs=[pl.BlockSpec((TM, d), row),
                  pl.BlockSpec((8, d), lambda t: (jnp.maximum(t * (TM // 8) - 1, 0), 0)),
                  pl.BlockSpec((8, d), lambda t: (jnp.minimum((t + 1) * (TM // 8), n8 - 1), 0)),
                  pl.BlockSpec(mods.shape, lambda t: (0, 0, 0)),
                  pl.BlockSpec((1, d), fixed),
                  pl.BlockSpec(w.shape, fixed),
                  pl.BlockSpec((8, SSM_CONV_DIM), fixed),
                  pl.BlockSpec((1, SSM_CONV_DIM), fixed),
                  pl.BlockSpec((1, LANES), fixed),
                  pl.BlockSpec((1, LANES), fixed)],
        out_specs=[pl.BlockSpec((TM, SSM_D_INNER), row),
                   pl.BlockSpec((1, SSM_D_INNER, TM), col), pl.BlockSpec((TM, SSM_BC), row),
                   pl.BlockSpec((1, SSM_BC, TM), col), pl.BlockSpec((1, LANES, TM), col),
                   pl.BlockSpec((1, LANES, TM), col)],
        out_shape=[jax.ShapeDtypeStruct((t_tok, SSM_D_INNER), BF16),
                   jax.ShapeDtypeStruct((nb, SSM_D_INNER, seg), BF16), jax.ShapeDtypeStruct((t_tok, SSM_BC), BF16),
                   jax.ShapeDtypeStruct((nb, SSM_BC, seg), BF16), jax.ShapeDtypeStruct((nb, LANES, seg), F32),
                   jax.ShapeDtypeStruct((nb, LANES, seg), F32)],
        scratch_shapes=[pltpu.VMEM((TM + 16, SSM_CONV_CHUNK), F32)] * (SSM_CONV_DIM // SSM_CONV_CHUNK),
        compiler_params=_params("parallel"),
        name="ssm_in",
    )(x, x, x, mods, gain.reshape(1, d), w, conv_w, conv_b, dt_bias, a_neg)


def _tri(lower, n):
    r = lax.broadcasted_iota(jnp.int32, (n, n), 0)
    c = lax.broadcasted_iota(jnp.int32, (n, n), 1)
    return (c <= r) if lower else (c >= r)


def _ssd_direction(xs_ref, bm_ref, ct_ref, dt_ref, a_ref, dsk_ref, st_ref, y_ref, d):
    a_t = a_ref[0]
    dt_t = dt_ref[0]
    upper = jnp.where(_tri(False, SSD_CHUNK), 1.0, 0.0).astype(BF16)
    a_hi, a_lo = _split_bf16(a_t)
    cum = _dot(a_hi, upper) + _dot(a_lo, upper)
    cum_end = cum[:, SSD_CHUNK - 1:SSD_CHUNK]
    if d == 0:
        lane_v, sub_v = cum, -cum
        inter = jnp.exp2(cum)
        w_upd = jnp.exp2(cum_end - cum) * dt_t
    else:
        ecum = cum - a_t
        lane_v, sub_v = -ecum, ecum
        inter = jnp.exp2(cum_end - ecum)
        w_upd = jnp.exp2(ecum) * dt_t
    sub_c = jnp.transpose(sub_v)
    decay_end = jnp.exp2(cum_end)
    mask = _tri(d == 1, SSD_CHUNK)
    p = SSM_HEAD_DIM
    for g in range(SSM_GROUPS):
        bg = bm_ref[:, g * SSM_STATE:(g + 1) * SSM_STATE]
        cg_t = ct_ref[0, g * SSM_STATE:(g + 1) * SSM_STATE, :]
        cb_t = _dot(bg, cg_t)
        state = st_ref[g]
        y_in = _dot(state.astype(BF16), cg_t)
        upd = []
        for e in range(SSM_HPG):
            h = g * SSM_HPG + e
            ln = d * SSM_HEADS + h
            rows = slice(e * p, (e + 1) * p)
            seg = jnp.where(mask, sub_c[:, ln:ln + 1] + lane_v[ln:ln + 1, :], -jnp.inf)
            m_t = (cb_t * jnp.exp2(seg)).astype(BF16)
            xf = xs_ref[0, h * p:(h + 1) * p, :].astype(F32)
            u = (xf * dt_t[ln:ln + 1, :]).astype(BF16)
            yh = _dot(u, m_t) + y_in[rows, :] * inter[ln:ln + 1, :] + dsk_ref[d, h] * xf
            y_ref[0, h * p:(h + 1) * p, :] = yh.astype(y_ref.dtype)
            upd.append((xf * w_upd[ln:ln + 1, :]).astype(BF16))
        new = _dot(jnp.concatenate(upd, axis=0), bg)
        for e in range(SSM_HPG):
            ln = d * SSM_HEADS + g * SSM_HPG + e
            rows = slice(e * p, (e + 1) * p)
            st_ref[g, rows, :] = state[rows, :] * decay_end[ln:ln + 1, :] + new[rows, :]


def _ssd_kernel(dsk_ref, xsf, bmf, ctf, dtf, af, xsb, bmb, ctb, dtb, ab, yf_ref, yb_ref, stf, stb):
    @pl.when(pl.program_id(1) == 0)
    def _():
        stf[...] = jnp.zeros_like(stf)
        stb[...] = jnp.zeros_like(stb)

    _ssd_direction(xsf, bmf, ctf, dtf, af, dsk_ref, stf, yf_ref, 0)
    _ssd_direction(xsb, bmb, ctb, dtb, ab, dsk_ref, stb, yb_ref, 1)


def _scan_maps(nchunk, chunk):
    ctx_chunks = TM // chunk
    fwd = lambda b, t: (b * nchunk + t, 0)
    bwd = lambda b, t: (b * nchunk + jnp.where(t < ctx_chunks, ctx_chunks - 1 - t, nchunk + ctx_chunks - 1 - t), 0)
    return fwd, bwd


def _ssd_scan(xs_t, bm, c_t, dt_t, a_t, d_skip, nb, seg):
    nchunk = seg // SSD_CHUNK
    fwd_rows, bwd_rows = _scan_maps(nchunk, SSD_CHUNK)

    def specs(rows_map):
        cols_map = lambda b, t, *_: (b, 0, rows_map(b, t)[0] - b * nchunk)
        return [pl.BlockSpec((1, SSM_D_INNER, SSD_CHUNK), cols_map),
                pl.BlockSpec((SSD_CHUNK, SSM_BC), lambda b, t, *_: rows_map(b, t)),
                pl.BlockSpec((1, SSM_BC, SSD_CHUNK), cols_map),
                pl.BlockSpec((1, LANES, SSD_CHUNK), cols_map),
                pl.BlockSpec((1, LANES, SSD_CHUNK), cols_map)]

    arrs = (xs_t, bm, c_t, dt_t, a_t)
    return pl.pallas_call(
        _ssd_kernel,
        grid_spec=pltpu.PrefetchScalarGridSpec(
            num_scalar_prefetch=1,
            grid=(nb, nchunk),
            in_specs=specs(fwd_rows) + specs(bwd_rows),
            out_specs=[specs(fwd_rows)[0], specs(bwd_rows)[0]],
            scratch_shapes=[pltpu.VMEM((SSM_GROUPS, SSM_HPG * SSM_HEAD_DIM, SSM_STATE), F32)] * 2,
        ),
        out_shape=[jax.ShapeDtypeStruct(xs_t.shape, BF16)] * 2,
        compiler_params=_params("parallel", "arbitrary"),
        name="ssd_scan",
    )(d_skip, *arrs, *arrs)


def _log_sigmoid(x):
    return jnp.minimum(x, 0.0) - jnp.log1p(jnp.exp(-jnp.abs(x)))


ML_STATE_ROWS = ML_V_DIM + 16


def _mlstm_direction(qk_ref, v_ref, g_ref, gb_ref, c_st, m_st, h_ref, d):
    nh, dk, dv = ML_HEADS, ML_QK_DIM, ML_V_DIM
    g_t = jnp.transpose(g_ref[...] + gb_ref[...])
    ig = g_t[16 * d:16 * d + nh, :]
    lf = _log_sigmoid(g_t[16 * d + nh:16 * d + 2 * nh, :])
    upper = jnp.where(_tri(False, ML_CHUNK), 1.0, 0.0).astype(BF16)
    lf_hi, lf_lo = _split_bf16(lf)
    fc = _dot(lf_hi, upper) + _dot(lf_lo, upper)
    tot = fc[:, ML_CHUNK - 1:ML_CHUNK]
    m_prev = m_st[...]
    if d == 0:
        lane_v, sub_v = fc, ig - fc
        inter = fc + m_prev
        logw = tot - fc + ig
    else:
        ec = fc - lf
        lane_v, sub_v = -ec, ec + ig
        inter = tot - ec + m_prev
        logw = ec + ig
    sub_c = jnp.transpose(jnp.concatenate([sub_v, jnp.zeros((LANES - nh, ML_CHUNK), F32)], axis=0))
    m_new = jnp.maximum(tot + m_prev, jnp.max(logw, axis=-1, keepdims=True))
    ws = jnp.exp(logw - m_new)
    cw = jnp.exp(tot + m_prev - m_new)
    mask = _tri(d == 1, ML_CHUNK)
    q_t = jnp.transpose(qk_ref[:, :nh * dk].astype(F32))
    v_t = jnp.transpose(v_ref[...].astype(F32))
    lane = lax.broadcasted_iota(jnp.int32, (1, LANES), 1)
    zeros_q = jnp.zeros((dk, ML_CHUNK), F32)
    for h in range(nh):
        half = h % 2
        k_pair = qk_ref[:, nh * dk + (h - half) * dk:nh * dk + (h - half + 2) * dk] * (dk ** -0.5)
        qh = q_t[h * dk:(h + 1) * dk, :]
        q_m = jnp.concatenate([zeros_q, qh] if half else [qh, zeros_q], axis=0).astype(BF16)
        logd = jnp.where(mask, sub_c[:, h:h + 1] + lane_v[h:h + 1, :], -jnp.inf)
        mt = jnp.maximum(inter[h:h + 1, :], jnp.max(logd, axis=0, keepdims=True))
        sc = _dot(k_pair, q_m) * jnp.exp(logd - mt)
        vh = v_t[h * dv:(h + 1) * dv, :]
        state = c_st[h]
        cq = _dot(state.astype(BF16), q_m)
        w_int = jnp.exp(inter[h:h + 1, :] - mt)
        num = _dot(vh.astype(BF16), sc.astype(BF16)) + w_int * cq[:dv, :]
        den = jnp.sum(sc, axis=0, keepdims=True) + w_int * cq[dv:dv + 1, :]
        h_ref[0, h * dv:(h + 1) * dv, :] = (num / jnp.maximum(jnp.abs(den), jnp.exp(-mt))).astype(h_ref.dtype)
        ws_h = ws[h:h + 1, :]
        lhs = jnp.concatenate([vh * ws_h, jnp.broadcast_to(ws_h, (ML_STATE_ROWS - dv, ML_CHUNK))], axis=0)
        own = (lane >= half * dk) & (lane < (half + 1) * dk)
        row = lax.broadcasted_iota(jnp.int32, (ML_STATE_ROWS, 1), 0)
        new = cw[h:h + 1, :] * state + _dot(lhs.astype(BF16), k_pair)
        c_st[h] = jnp.where(own & (row <= dv), new, 0.0)
    m_st[...] = jnp.broadcast_to(m_new, m_st.shape)


def _mlstm_kernel(qkf, vf, gf, qkb, vb, gb, gbias, hf_ref, hb_ref, cf, mf, cb, mb):
    @pl.when(pl.program_id(1) == 0)
    def _():
        for r in (cf, mf, cb, mb):
            r[...] = jnp.zeros_like(r)

    _mlstm_direction(qkf, vf, gf, gbias, cf, mf, hf_ref, 0)
    _mlstm_direction(qkb, vb, gb, gbias, cb, mb, hb_ref, 1)


def _mlstm_scan(qk, v, g, gate_b, nb, seg):
    t_tok = qk.shape[0]
    nchunk = seg // ML_CHUNK
    fwd, bwd = _scan_maps(nchunk, ML_CHUNK)
    widths = (qk.shape[1], v.shape[1], LANES)
    in_specs = ([pl.BlockSpec((ML_CHUNK, w), fwd) for w in widths] + [pl.BlockSpec((ML_CHUNK, w), bwd) for w in widths]
                + [pl.BlockSpec((1, LANES), lambda b, t: (0, 0))])
    state = [pltpu.VMEM((ML_HEADS, ML_STATE_ROWS, LANES), F32), pltpu.VMEM((ML_HEADS, LANES), F32)]
    out_map = lambda rows_map: (lambda b, t: (b, 0, rows_map(b, t)[0] - b * nchunk))
    return pl.pallas_call(
        _mlstm_kernel,
        grid=(nb, nchunk),
        in_specs=in_specs,
        out_specs=[pl.BlockSpec((1, v.shape[1], ML_CHUNK), out_map(fwd)),
                   pl.BlockSpec((1, v.shape[1], ML_CHUNK), out_map(bwd))],
        out_shape=[jax.ShapeDtypeStruct((nb, v.shape[1], seg), BF16)] * 2,
        scratch_shapes=state + state,
        compiler_params=_params("parallel", "arbitrary"),
        name="mlstm_scan",
    )(qk, v, g, qk, v, g, gate_b)


def _rms_rows(x, g_row):
    return x * lax.rsqrt(jnp.mean(x * x, axis=-1, keepdims=True) + EPS) * g_row


MLA_SCORE_SCALE = (MLA_NOPE + MLA_ROPE) ** -0.5 * LOG2_E


def _mla_queries(cq, g_ref, wt_ref, cos_ref, sin_ref, qt_ref):
    cq_t = jnp.transpose(_rms_rows(cq, g_ref[...])).astype(BF16)
    q_t = _dot(wt_ref[...], cq_t) * MLA_SCORE_SCALE
    qt_ref[0] = q_t.astype(qt_ref.dtype)
    cos, sin = cos_ref[...], sin_ref[...]
    half = MLA_ROPE // 2
    for h in range(MLA_HEADS):
        r0 = h * LANES + MLA_NOPE
        x1, x2 = q_t[r0:r0 + half, :], q_t[r0 + half:r0 + MLA_ROPE, :]
        qt_ref[0, r0:r0 + half, :] = (x1 * cos - x2 * sin).astype(qt_ref.dtype)
        qt_ref[0, r0 + half:r0 + MLA_ROPE, :] = (x2 * cos + x1 * sin).astype(qt_ref.dtype)


def _mla_keys_values(ckv, kr, g_ref, wk_ref, wvt_ref, cos_ref, sin_ref, k_ref, vt_ref):
    cn = _rms_rows(ckv, g_ref[...])
    vt_ref[0] = _dot(wvt_ref[...], jnp.transpose(cn).astype(BF16)).astype(vt_ref.dtype)
    kn = _dot(cn.astype(BF16), wk_ref[...])
    kr = pltpu.roll(kr, MLA_NOPE, axis=1)
    half = MLA_ROPE // 2
    lane = lax.broadcasted_iota(jnp.int32, (1, LANES), 1)
    partner = jnp.where(lane < MLA_NOPE + half, pltpu.roll(kr, LANES - half, axis=1), pltpu.roll(kr, half, axis=1))
    roped = kr * cos_ref[...] + partner * sin_ref[...]
    for h in range(MLA_HEADS):
        k_ref[:, h * LANES:(h + 1) * LANES] = (kn[:, h * LANES:(h + 1) * LANES] + roped).astype(k_ref.dtype)


def _mla_in_kernel(x_ref, mods_ref, g_ref, w_ref, gq_ref, wqt_ref, gkv_ref, wk_ref, wvt_ref,
                   cos_t_ref, sin_t_ref, cos_ref, sin_ref, qt_ref, k_ref, vt_ref, *, nt, nb):
    mod = mods_ref[_mod_row(pl.program_id(0), nt, nb)]
    hb = _normed(x_ref[...], g_ref[...], mod, 0, 1).astype(BF16)
    p = _dot(hb, w_ref[...])
    _mla_queries(p[:, :MLA_RANK], gq_ref, wqt_ref, cos_t_ref, sin_t_ref, qt_ref)
    _mla_keys_values(p[:, MLA_RANK:2 * MLA_RANK], p[:, 2 * MLA_RANK:], gkv_ref, wk_ref, wvt_ref,
                     cos_ref, sin_ref, k_ref, vt_ref)


def _mla_in(x, mods, gain, w_in, gain_q, wq_t, gain_kv, w_k, wv_t, tables_t, tables, nt, nb):
    t_tok, d = x.shape
    seg = nt * TM
    half = MLA_ROPE // 2
    fixed = lambda t: (0, 0)
    return pl.pallas_call(
        functools.partial(_mla_in_kernel, nt=nt, nb=nb),
        grid=(t_tok // TM,),
        in_specs=[pl.BlockSpec((TM, d), lambda t: (t, 0)),
                  pl.BlockSpec(mods.shape, lambda t: (0, 0, 0)),
                  pl.BlockSpec((1, d), fixed),
                  pl.BlockSpec(w_in.shape, fixed),
                  pl.BlockSpec((1, MLA_RANK), fixed),
                  pl.BlockSpec(wq_t.shape, fixed),
                  pl.BlockSpec((1, MLA_RANK), fixed),
                  pl.BlockSpec(w_k.shape, fixed),
                  pl.BlockSpec(wv_t.shape, fixed),
                  pl.BlockSpec((half, TM), lambda t: (0, t % nt)),
                  pl.BlockSpec((half, TM), lambda t: (0, t % nt)),
                  pl.BlockSpec((TM, LANES), lambda t: (t % nt, 0)),
                  pl.BlockSpec((TM, LANES), lambda t: (t % nt, 0))],
        out_specs=[pl.BlockSpec((1, MLA_HEADS * LANES, TM), lambda t: (t, 0, 0)),
                   pl.BlockSpec((TM, MLA_HEADS * LANES), lambda t: (t, 0)),
                   pl.BlockSpec((1, MLA_HEADS * MLA_V, TM), lambda t: (t // nt, 0, t % nt))],
        out_shape=[jax.ShapeDtypeStruct((t_tok // TM, MLA_HEADS * LANES, TM), BF16),
                   jax.ShapeDtypeStruct((t_tok, MLA_HEADS * LANES), BF16),
                   jax.ShapeDtypeStruct((nb, MLA_HEADS * MLA_V, seg), BF16)],
        compiler_params=_params("parallel"),
        name="mla_in",
    )(x, mods, gain.reshape(1, d), w_in, gain_q.reshape(1, MLA_RANK), wq_t, gain_kv.reshape(1, MLA_RANK), w_k, wv_t,
      *tables_t, *tables)


def _mla_attn_kernel(qt_ref, k_ref, vt_ref, o_ref):
    @pl.when(pl.program_id(2) == 0)
    def _():
        o_ref[...] = jnp.zeros_like(o_ref)

    @pl.when(pl.program_id(2) > 0)
    def _():
        heads = range(MLA_HEADS_PER_STEP)
        seg = k_ref.shape[0]
        tk = MLA_KEY_TILE if (seg - TM) % MLA_KEY_TILE == 0 else TM
        tiles = [(0, TM)] + [(a, a + tk) for a in range(TM, seg, tk)]

        def scores(hh, j):
            return _dot(k_ref[tiles[j][0]:tiles[j][1], hh * LANES:(hh + 1) * LANES],
                        qt_ref[0, hh * LANES:(hh + 1) * LANES, :])

        m = [jnp.full((1, TM), -jnp.inf, F32) for _ in heads]
        l = [jnp.zeros((1, TM), F32) for _ in heads]
        acc = [jnp.zeros((MLA_V, TM), F32) for _ in heads]
        ahead = min(MLA_LOOKAHEAD, len(tiles))
        pending = [[scores(hh, j) for hh in heads] for j in range(ahead)]
        for j in range(len(tiles)):
            s_cur = pending.pop(0)
            if j + ahead < len(tiles):
                pending.append([scores(hh, j + ahead) for hh in heads])
            for hh in heads:
                m_new = jnp.maximum(m[hh], jnp.max(s_cur[hh], axis=0, keepdims=True))
                alpha = jnp.exp2(m[hh] - m_new)
                p = jnp.exp2(s_cur[hh] - m_new)
                l[hh] = alpha * l[hh] + jnp.sum(p, axis=0, keepdims=True)
                v_t = vt_ref[0, hh * MLA_V:(hh + 1) * MLA_V, tiles[j][0]:tiles[j][1]]
                acc[hh] = alpha * acc[hh] + _dot(v_t, p.astype(BF16))
                m[hh] = m_new
        o_t = jnp.concatenate([acc[hh] / l[hh] for hh in heads], axis=0)
        o_ref[...] = jnp.transpose(o_t).astype(o_ref.dtype)


def _mla_attention(q_t, k, v_t, nb, seg):
    t_tok = k.shape[0]
    nt = seg // TM
    hps = MLA_HEADS_PER_STEP
    return pl.pallas_call(
        _mla_attn_kernel,
        grid=(nb, MLA_HEADS // hps, nt),
        in_specs=[pl.BlockSpec((1, hps * LANES, TM), lambda b, hp, j: (b * nt + j, hp, 0)),
                  pl.BlockSpec((seg, hps * LANES), lambda b, hp, j: (b, hp)),
                  pl.BlockSpec((1, hps * MLA_V, seg), lambda b, hp, j: (b, hp, 0))],
        out_specs=pl.BlockSpec((TM, hps * MLA_V), lambda b, hp, j: (b * nt + j, hp)),
        out_shape=jax.ShapeDtypeStruct((t_tok, MLA_HEADS * MLA_V), BF16),
        compiler_params=_params("parallel", "parallel", "arbitrary"),
        name="mla_attention",
    )(q_t, k, v_t)


def _final_kernel(x_ref, g_ref, o_ref):
    o_ref[0] = _rms_rows(x_ref[...], g_ref[...])


def _final_norm(x, gain, nb, seg):
    d = x.shape[1]
    nt = seg // TM
    return pl.pallas_call(
        _final_kernel,
        grid=(nb, nt - 1),
        in_specs=[pl.BlockSpec((TM, d), lambda b, j: (b * nt + 1 + j, 0)),
                  pl.BlockSpec((1, d), lambda b, j: (0, 0))],
        out_specs=pl.BlockSpec((1, TM, d), lambda b, j: (b, j, 0)),
        out_shape=jax.ShapeDtypeStruct((nb, seg - TM, d), F32),
        compiler_params=_params("parallel", "parallel"),
        name="final_norm",
    )(x, gain.reshape(1, d))


def _rope_angles(n_lat, rot_dim):
    rows = n_lat // GRID_W
    row = jnp.repeat(jnp.arange(rows), GRID_W).astype(F32)
    col = jnp.tile(jnp.arange(GRID_W), rows).astype(F32)
    quarter = rot_dim // 4
    inv = ROPE_BASE ** (-jnp.arange(quarter, dtype=F32) / quarter)
    ang = jnp.concatenate([row[:, None] * inv, col[:, None] * inv], axis=-1)
    return jnp.cos(ang), jnp.sin(ang)


def _with_ctx_rows(tab, fill):
    return jnp.concatenate([jnp.full((TM, tab.shape[1]), fill, F32), tab], axis=0)


def _attn_rope_tables(n_lat):
    cos, sin = _rope_angles(n_lat, ATT_HEAD_DIM)
    cos_h = jnp.concatenate([cos, cos], axis=1)
    sin_h = jnp.concatenate([-sin, sin], axis=1)
    reps = LANES // ATT_HEAD_DIM
    return (_with_ctx_rows(jnp.tile(cos_h, (1, reps)), 1.0), _with_ctx_rows(jnp.tile(sin_h, (1, reps)), 0.0))


def _swap_halves(w, n_heads, dim):
    w3 = w.reshape(w.shape[0], n_heads, 2, dim // 2)
    return w3[:, :, ::-1, :].reshape(w.shape[0], n_heads * dim)


def _mla_k_tables(n_lat):
    cos, sin = _rope_angles(n_lat, MLA_ROPE)
    lo = jnp.zeros((n_lat, MLA_NOPE), F32)
    hi = jnp.zeros((n_lat, LANES - MLA_NOPE - MLA_ROPE), F32)
    cos_c = jnp.concatenate([lo, cos, cos, hi], axis=1)
    sin_c = jnp.concatenate([lo, -sin, sin, hi], axis=1)
    ctx_cos = jnp.concatenate([lo[:TM], jnp.ones((TM, MLA_ROPE), F32), hi[:TM]], axis=1)
    return (jnp.concatenate([ctx_cos, cos_c], axis=0), _with_ctx_rows(sin_c, 0.0))


def _pad_heads(w, real):
    r = w.shape[0]
    w3 = w.reshape(r, MLA_HEADS, real)
    return jnp.pad(w3, ((0, 0), (0, 0), (0, LANES - real))).reshape(r, MLA_HEADS * LANES)


def kernel(x, c, ctx, c_ctx, norm1_g, norm2_g, w_mod, b_mod, moe_w_group, moe_b_group, moe_w_expert, moe_b_expert, moe_w_gate, moe_w_up, moe_w_down, attn_w_in, attn_sink, attn_w_out, ssm_w_in, ssm_conv_w, ssm_conv_b, ssm_dt_bias, ssm_a_log, ssm_d, ssm_norm_g, ssm_w_out, mlstm_w_in, mlstm_gate_b, mlstm_norm_g, mlstm_w_out, mla_w_in, mla_q_norm_g, mla_w_q_up, mla_kv_norm_g, mla_w_kv_up, mla_w_out, final_norm_g):
    nb, n_lat, d = x.shape
    assert ctx.shape[1] == TM and d == D_MODEL and n_lat % TM == 0
    depth = w_mod.shape[0]
    seg = TM + n_lat
    nt = seg // TM
    t_tok = nb * seg

    xs = (x, ctx)

    rows = -(-(nb + 1) // 8) * 8
    cvec = jnp.concatenate([c, c_ctx[None, :], jnp.zeros((rows - nb - 1, d), F32)], axis=0)
    mods = _modulation(cvec, w_mod, b_mod).reshape(depth, rows, ADALN_CHUNKS, d)
    mods = jnp.pad(mods, ((0, 0), (0, 0), (0, MOD_ROWS - ADALN_CHUNKS), (0, 0)))

    w_router = jnp.concatenate([moe_w_expert, moe_w_group,
                                jnp.zeros((depth, d, LANES - MOE_EXPERTS - MOE_GROUPS), F32)], axis=-1)
    b_router = jnp.concatenate([moe_b_expert, moe_b_group,
                                jnp.zeros((depth, LANES - MOE_EXPERTS - MOE_GROUPS), F32)], axis=-1)
    w_gate, w_up = moe_w_gate.astype(BF16), moe_w_up.astype(BF16)
    w_down = moe_w_down.astype(BF16).reshape(depth, MOE_GROUPS, MOE_PER_GROUP * MOE_FF, d)

    for i in range(depth):
        kind = i % 4
        mod_i = mods[i]
        if kind == 0:
            nq, nk = ATT_HEADS * ATT_HEAD_DIM, ATT_KV_HEADS * ATT_HEAD_DIM
            w_in = attn_w_in[i // 4]
            w_all = jnp.concatenate([w_in, _swap_halves(w_in[:, :nq], ATT_HEADS, ATT_HEAD_DIM),
                                     _swap_halves(w_in[:, nq:nq + nk], ATT_KV_HEADS, ATT_HEAD_DIM)],
                                    axis=1).astype(BF16)
            n_in = w_in.shape[1]
            q, k, v = _normproj(xs, mod_i, norm1_g[i], w_all,
                                [(0, nq, n_in, ATT_HEAD_DIM ** -0.5 * LOG2_E), (nq, nk, n_in + nq), (nq + nk, nk, None)],
                                [BF16, BF16, BF16], nt, nb, tables=_attn_rope_tables(n_lat))
            o = _windowed_attention(q, k, v, attn_sink[i // 4], nb, seg)
            xs = _outproj([o], [nq], xs, mod_i, attn_w_out[i // 4].astype(BF16), _plain_prologue, nt, nb)
        elif kind == 1:
            j = i // 4
            w_in = jnp.pad(ssm_w_in[j], ((0, 0), (0, LANES - 2 * SSM_HEADS))).astype(BF16)
            lane_pad = LANES - 2 * SSM_HEADS
            dt_bias = jnp.pad(ssm_dt_bias[j].reshape(1, -1), ((0, 0), (0, lane_pad)))
            a_neg = jnp.pad(-jnp.exp(ssm_a_log[j].astype(F32)).reshape(1, -1), ((0, 0), (0, lane_pad)))
            conv_w = jnp.pad(ssm_conv_w[j], ((0, 8 - SSM_CONV), (0, 0)))
            z, xc, bm, cm, dtv, a = _ssm_in(xs, mod_i, norm1_g[i], w_in, conv_w, ssm_conv_b[j].reshape(1, -1),
                                            dt_bias, a_neg, nt, nb)
            yf, yb = _ssd_scan(xc, bm, cm, dtv, a, ssm_d[j].astype(F32), nb, seg)
            xs = _outproj([z], [SSM_D_INNER], xs, mod_i, ssm_w_out[j].astype(BF16), _ssm_prologue,
                          nt, nb, extra=(ssm_norm_g[j].reshape(1, -1),), ins_t=(yf, yb))
        elif kind == 2:
            j = i // 4
            nqk, nv = 2 * ML_HEADS * ML_QK_DIM, ML_HEADS * ML_V_DIM
            w_in = jnp.pad(mlstm_w_in[j], ((0, 0), (0, LANES - 4 * ML_HEADS))).astype(BF16)
            qk, v, o, g = _normproj(xs, mod_i, norm1_g[i], w_in,
                                    [(0, nqk, None), (nqk, nv, None), (nqk + nv, nv, None), (nqk + 2 * nv, LANES, None)],
                                    [BF16, BF16, BF16, F32], nt, nb)
            gate_b = jnp.pad(mlstm_gate_b[j].reshape(1, -1), ((0, 0), (0, LANES - 4 * ML_HEADS)))
            hf, hb = _mlstm_scan(qk, v, g, gate_b, nb, seg)
            xs = _outproj([o], [nv], xs, mod_i, mlstm_w_out[j].astype(BF16), _mlstm_prologue,
                          nt, nb, extra=(mlstm_norm_g[j].reshape(1, -1),), ins_t=(hf, hb))
        else:
            j = i // 4
            w_in = jnp.pad(mla_w_in[j], ((0, 0), (0, LANES - MLA_ROPE))).astype(BF16)
            cos, sin = _rope_angles(n_lat, MLA_ROPE)
            tables_t = (jnp.transpose(_with_ctx_rows(cos, 1.0)), jnp.transpose(_with_ctx_rows(sin, 0.0)))
            wq_t = jnp.transpose(_pad_heads(mla_w_q_up[j], MLA_NOPE + MLA_ROPE)).astype(BF16)
            w_kv = mla_w_kv_up[j].reshape(MLA_RANK, MLA_HEADS, MLA_NOPE + MLA_V)
            w_k = _pad_heads(w_kv[:, :, :MLA_NOPE].reshape(MLA_RANK, MLA_HEADS * MLA_NOPE), MLA_NOPE).astype(BF16)
            wv_t = jnp.transpose(w_kv[:, :, MLA_NOPE:].reshape(MLA_RANK, MLA_HEADS * MLA_V)).astype(BF16)
            q_t, k, v_t = _mla_in(xs, mod_i, norm1_g[i], w_in, mla_q_norm_g[j], wq_t, mla_kv_norm_g[j], w_k, wv_t,
                                  tables_t, _mla_k_tables(n_lat), nt, nb)
            o = _mla_attention(q_t, k, v_t, nb, seg)
            xs = _outproj([o], [MLA_HEADS * MLA_V], xs, mod_i, mla_w_out[j].astype(BF16), _plain_prologue, nt, nb)

        xs = _moe(xs, mod_i, norm2_g[i], w_router[i], b_router[i].reshape(1, -1), w_gate, w_up, w_down, i, nt, nb)

    return _final_norm(xs, final_norm_g, nb, seg)
```

```python
import functools
import math

import jax
import jax.numpy as jnp
from jax import lax
from jax.experimental import pallas as pl
from jax.experimental.pallas import tpu as pltpu

F32 = jnp.float32
BF16 = jnp.bfloat16

D_MODEL = 1024
GRID_W = 64
EPS = 1e-6
ROPE_BASE = 10000.0
ADALN_CHUNKS = 6
CHUNK = 128
SSD_CHUNK = 256
ML_CHUNK = 128
SSD_DECAY_BLOCK = 128
TM = 256
MOD_ROWS = 8
LANES = 128
V7X_VMEM_LIMIT = 48 * 1024 * 1024

ATT_HEADS, ATT_KV_HEADS, ATT_HEAD_DIM, WINDOW = 16, 4, 64, 128
ATT_GROUP = ATT_HEADS // ATT_KV_HEADS
ATT_UNIT_HEADS = 4
SSM_D_INNER, SSM_HEAD_DIM, SSM_HEADS, SSM_GROUPS, SSM_STATE, SSM_CONV = 2048, 64, 32, 4, 128, 5
SSM_HPG = SSM_HEADS // SSM_GROUPS
SSM_BC = SSM_GROUPS * SSM_STATE
SSM_CONV_DIM = SSM_D_INNER + 2 * SSM_BC
SSM_CONV_CHUNK = 256
ML_HEADS, ML_QK_DIM, ML_V_DIM = 8, 64, 128
MLA_HEADS, MLA_RANK, MLA_NOPE, MLA_ROPE, MLA_V = 16, 256, 64, 32, 64
MOE_GROUPS, MOE_PER_GROUP, MOE_EXPERTS, MOE_FF = 4, 4, 16, 256
MOE_ROWS = 1024
MLA_KEY_TILE = 1024
MLA_HEADS_PER_STEP = 4
MLA_LOOKAHEAD = 1
LOG2_E = 1.4426950408889634


def _dot(a, b):
    return jnp.dot(a, b, preferred_element_type=F32)


def _dot_nt(a, b):
    return lax.dot_general(a, b, (((1,), (1,)), ((), ())), preferred_element_type=F32)


def _split_bf16(x):
    hi = x.astype(BF16)
    lo = (x - hi.astype(F32)).astype(BF16)
    return hi, lo


def _dot_split(a, b):
    a_hi, a_lo = _split_bf16(a)
    b_hi, b_lo = _split_bf16(b)
    return _dot(a_hi, b_hi) + _dot(a_lo, b_hi) + _dot(a_hi, b_lo)


def _sigmoid(x):
    return 1.0 / (1.0 + jnp.exp(-x))


def _silu(x):
    return x * _sigmoid(x)


def _softplus(x):
    return jnp.maximum(x, 0.0) + jnp.log1p(jnp.exp(-jnp.abs(x)))


def _params(*sem):
    return pltpu.CompilerParams(dimension_semantics=sem, vmem_limit_bytes=V7X_VMEM_LIMIT)


def _mod_row(t, nt, nb):
    return jnp.where(t % nt == 0, nb, t // nt)


def _mod_kernel(c_ref, w_ref, b_ref, o_ref):
    o_ref[0] = _dot_split(_silu(c_ref[...]), w_ref[0]) + b_ref[0]


def _modulation(cvec, w_mod, b_mod):
    depth, d, n = w_mod.shape
    tn = 1536
    rows = cvec.shape[0]
    return pl.pallas_call(
        _mod_kernel,
        grid=(depth, n // tn),
        in_specs=[pl.BlockSpec((rows, d), lambda l, j: (0, 0)),
                  pl.BlockSpec((1, d, tn), lambda l, j: (l, 0, j)),
                  pl.BlockSpec((1, 1, tn), lambda l, j: (l, 0, j))],
        out_specs=pl.BlockSpec((1, rows, tn), lambda l, j: (l, 0, j)),
        out_shape=jax.ShapeDtypeStruct((depth, rows, n), F32),
        compiler_params=_params("arbitrary", "arbitrary"),
        name="modulation",
    )(cvec, w_mod, b_mod.reshape(depth, 1, n))


def _normed(x, g_row, mod, sh_row, sc_row):
    y = x * lax.rsqrt(jnp.mean(x * x, axis=-1, keepdims=True) + EPS) * g_row
    return y * (1.0 + mod[sc_row:sc_row + 1, :]) + mod[sh_row:sh_row + 1, :]


def _proj_columns(hb, w_ref, o_ref, start, width, rot_start, cos, sin, scale=None, chunk=512):
    for c in range(0, width, chunk):
        cw = min(chunk, width - c)
        acc = _dot(hb, w_ref[:, start + c:start + c + cw])
        if rot_start is not None:
            rot = _dot(hb, w_ref[:, rot_start + c:rot_start + c + cw])
            reps = cw // LANES
            acc = acc * jnp.tile(cos, (1, reps)) + rot * jnp.tile(sin, (1, reps))
        if scale is not None:
            acc = acc * scale
        o_ref[:, c:c + cw] = acc.astype(o_ref.dtype)


def _stream_specs(x, nt):
    if isinstance(x, tuple):
        d = x[0].shape[-1]
        return [pl.BlockSpec((1, TM, d), lambda t: (t // nt, jnp.maximum(t % nt - 1, 0), 0)),
                pl.BlockSpec((1, TM, d), lambda t: (t // nt, 0, 0))], list(x)
    sub = _tiles_per_step(x, nt)
    return [pl.BlockSpec((sub * TM, x.shape[1]), lambda t: (t, 0))], [x]


def _tiles_per_step(x, nt):
    if isinstance(x, tuple):
        return 1
    return 2 if (x.shape[0] // TM) % 2 == 0 else 1


def _stream_tile(x_refs, nt, s):
    if len(x_refs) == 1:
        return x_refs[0][s * TM:(s + 1) * TM, :]
    return jnp.where(pl.program_id(0) % nt == 0, x_refs[1][0], x_refs[0][0])


def _normproj_kernel(*refs, outs, has_rope, n_x, nt, nb, sub):
    x_refs, refs = refs[:n_x], refs[n_x:]
    mods_ref, g_ref, w_ref = refs[:3]
    k = 3
    cos = sin = None
    if has_rope:
        cos, sin = refs[3][...], refs[4][...]
        k = 5
    hs = []
    for s in range(sub):
        mod = mods_ref[_mod_row(pl.program_id(0) * sub + s, nt, nb)]
        hs.append(_normed(_stream_tile(x_refs, nt, s), g_ref[...], mod, 0, 1).astype(BF16))
    hb = hs[0] if sub == 1 else jnp.concatenate(hs, axis=0)
    for o_ref, spec in zip(refs[k:], outs):
        _proj_columns(hb, w_ref, o_ref, *spec[:3], cos, sin, scale=spec[3] if len(spec) > 3 else None)


def _normproj(x, mods, gain, w, outs, out_dtypes, nt, nb, tables=None):
    d, n = w.shape
    t_tok = nb * nt * TM
    sub = _tiles_per_step(x, nt)
    assert tables is None or sub == 1
    rows = sub * TM
    x_specs, x_args = _stream_specs(x, nt)
    in_specs = x_specs + [pl.BlockSpec(mods.shape, lambda t: (0, 0, 0)),
                          pl.BlockSpec((1, d), lambda t: (0, 0)),
                          pl.BlockSpec((d, n), lambda t: (0, 0))]
    args = x_args + [mods, gain.reshape(1, d), w]
    if tables is not None:
        in_specs += [pl.BlockSpec((TM, LANES), lambda t: (t % nt, 0))] * 2
        args += list(tables)
    return pl.pallas_call(
        functools.partial(_normproj_kernel, outs=tuple(outs), has_rope=tables is not None, n_x=len(x_args),
                          nt=nt, nb=nb, sub=sub),
        grid=(t_tok // rows,),
        in_specs=in_specs,
        out_specs=[pl.BlockSpec((rows, o[1]), lambda t: (t, 0)) for o in outs],
        out_shape=[jax.ShapeDtypeStruct((t_tok, o[1]), dt) for o, dt in zip(outs, out_dtypes)],
        compiler_params=_params("parallel"),
        name="normproj",
    )(*args)


def _outproj_kernel(*refs, prologue, n_t, n_tok, n_extra, n_x, nt, nb, sub):
    ins_t = refs[:n_t * sub]
    ins = refs[n_t * sub:n_t * sub + n_tok]
    extra = refs[n_t * sub + n_tok:n_t * sub + n_tok + n_extra]
    x_refs = refs[n_t * sub + n_tok + n_extra:n_t * sub + n_tok + n_extra + n_x]
    mods_ref, w_ref, o_ref = refs[n_t * sub + n_tok + n_extra + n_x:]
    parts = []
    for s in range(sub):
        tok = [r if sub == 1 else r.at[pl.ds(s * TM, TM), :] for r in ins]
        parts.append(prologue(*ins_t[s * n_t:(s + 1) * n_t], *tok, *extra))
    a = parts[0] if sub == 1 else jnp.concatenate(parts, axis=0)
    y = _dot(a, w_ref[...])
    for s in range(sub):
        rows = slice(s * TM, (s + 1) * TM)
        gate = mods_ref[_mod_row(pl.program_id(0) * sub + s, nt, nb)][2:3, :]
        o_ref[rows, :] = _stream_tile(x_refs, nt, s) + gate * y[rows, :]


def _outproj(ins, in_widths, x, mods, w, prologue, nt, nb, extra=(), ins_t=()):
    d = w.shape[1]
    t_tok = nb * nt * TM
    sub = _tiles_per_step(x, nt)
    rows = sub * TM

    def tile_map(s):
        return lambda t: ((t * sub + s) // nt, 0, (t * sub + s) % nt)

    in_specs, args = [], []
    for s in range(sub):
        in_specs += [pl.BlockSpec((1, a.shape[1], TM), tile_map(s)) for a in ins_t]
        args += list(ins_t)
    in_specs += [pl.BlockSpec((rows, wd), lambda t: (t, 0)) for wd in in_widths]
    in_specs += [pl.BlockSpec(e.shape, lambda t: (0, 0)) for e in extra]
    x_specs, x_args = _stream_specs(x, nt)
    n_before_x = len(in_specs)
    in_specs += x_specs + [pl.BlockSpec(mods.shape, lambda t: (0, 0, 0)), pl.BlockSpec(w.shape, lambda t: (0, 0))]
    return pl.pallas_call(
        functools.partial(_outproj_kernel, prologue=prologue, n_t=len(ins_t), n_tok=len(ins), n_extra=len(extra),
                          n_x=len(x_args), nt=nt, nb=nb, sub=sub),
        grid=(t_tok // rows,),
        in_specs=in_specs,
        out_specs=pl.BlockSpec((rows, d), lambda t: (t, 0)),
        out_shape=jax.ShapeDtypeStruct((t_tok, d), F32),
        input_output_aliases={} if isinstance(x, tuple) else {n_before_x: 0},
        compiler_params=_params("parallel"),
        name="outproj",
    )(*args, *ins, *extra, *x_args, mods, w)


def _group_rms(y, n_groups):
    width = y.shape[1] // n_groups
    parts = []
    for g in range(n_groups):
        yg = y[:, g * width:(g + 1) * width]
        parts.append(yg * lax.rsqrt(jnp.mean(yg * yg, axis=-1, keepdims=True) + EPS))
    return jnp.concatenate(parts, axis=1)


def _plain_prologue(o_ref):
    return o_ref[...]


def _ssm_prologue(yf_ref, yb_ref, z_ref, g_ref):
    y = jnp.transpose(yf_ref[0].astype(F32) + yb_ref[0].astype(F32)) * _silu(z_ref[...].astype(F32))
    return (_group_rms(y, SSM_GROUPS) * g_ref[...]).astype(BF16)


def _mlstm_prologue(hf_ref, hb_ref, o_ref, g_ref):
    h = jnp.transpose(hf_ref[0].astype(F32) + hb_ref[0].astype(F32))
    return (_group_rms(h, ML_HEADS) * g_ref[...] * _sigmoid(o_ref[...].astype(F32))).astype(BF16)


MOE_ROUTE_ROWS = 24


def _route(logits):
    lt = jnp.transpose(logits)[:MOE_ROUTE_ROWS, :]
    row = lax.broadcasted_iota(jnp.int32, lt.shape, 0).astype(F32)
    neg = -jnp.inf
    lg = jnp.where((row >= MOE_EXPERTS) & (row < MOE_EXPERTS + MOE_GROUPS), lt, neg)
    gmax = jnp.max(lg, axis=0, keepdims=True)
    g_sel = jnp.min(jnp.where(lg == gmax, row, LANES), axis=0, keepdims=True) - MOE_EXPERTS
    p_g = 1.0 / jnp.sum(jnp.exp(lg - gmax), axis=0, keepdims=True)
    in_group = (row >= g_sel * MOE_PER_GROUP) & (row < (g_sel + 1) * MOE_PER_GROUP)
    le = jnp.where(in_group, lt, neg)
    v1 = jnp.max(le, axis=0, keepdims=True)
    i1 = jnp.min(jnp.where(le == v1, row, LANES), axis=0, keepdims=True)
    le2 = jnp.where(row == i1, neg, le)
    v2 = jnp.max(le2, axis=0, keepdims=True)
    i2 = jnp.min(jnp.where(le2 == v2, row, LANES), axis=0, keepdims=True)
    e2 = jnp.exp(v2 - v1)
    w1 = p_g / (1.0 + e2)
    comb_t = jnp.where(row == i1, w1, 0.0) + jnp.where(row == i2, w1 * e2, 0.0)
    pad = jnp.zeros((LANES - MOE_ROUTE_ROWS, lt.shape[1]), F32)
    return jnp.transpose(jnp.concatenate([comb_t, pad], axis=0))


def _moe_kernel(x_ref, mods_ref, g_ref, wr_ref, br_ref, wg_ref, wu_ref, wd_ref, o_ref, h_scr, comb_scr,
                *, nt, nb, sub):
    i = pl.program_id(0)
    e = pl.program_id(1)

    def experts(rows):
        hb = h_scr[rows, :]
        comb = comb_scr[rows, :]
        lane = lax.broadcasted_iota(jnp.int32, comb.shape, 1)
        scaled = []
        for k in range(MOE_PER_GROUP):
            act = _silu(_dot(hb, wg_ref[0, k])) * _dot(hb, wu_ref[0, k])
            cw = jnp.sum(jnp.where(lane == e * MOE_PER_GROUP + k, comb, 0.0), axis=-1, keepdims=True)
            scaled.append((act * cw).astype(BF16))
        return _dot(jnp.concatenate(scaled, axis=1), wd_ref[0, 0])

    def gate_row(s):
        return mods_ref[_mod_row(i * sub + s, nt, nb)][5:6, :]

    @pl.when(e == 0)
    def _():
        w_hi, w_lo = _split_bf16(wr_ref[...])
        w_both = jnp.concatenate([w_hi, w_lo], axis=1)
        def prepare(s):
            rows = slice(s * TM, (s + 1) * TM)
            mod = mods_ref[_mod_row(i * sub + s, nt, nb)]
            h = _normed(x_ref[rows, :], g_ref[...], mod, 3, 4)
            h_hi, h_lo = _split_bf16(h)
            h_scr[rows, :] = h_hi
            both = _dot(h_hi, w_both)
            logits = both[:, :LANES] + both[:, LANES:] + _dot(h_lo, w_hi) + br_ref[...]
            comb_scr[rows, :] = _route(logits)

        for s in range(sub):
            prepare(s)
            rows = slice(s * TM, (s + 1) * TM)
            o_ref[rows, :] = x_ref[rows, :] + gate_row(s) * experts(rows)

    @pl.when(e > 0)
    def _():
        y = experts(slice(0, sub * TM))
        for s in range(sub):
            rows = slice(s * TM, (s + 1) * TM)
            o_ref[rows, :] += gate_row(s) * y[rows, :]


def _moe(x, mods, gain, w_router, b_router, w_gate, w_up, w_down, layer, nt, nb):
    t_tok, d = x.shape
    pg = MOE_PER_GROUP
    rows = MOE_ROWS if t_tok % MOE_ROWS == 0 else TM
    sub = rows // TM
    return pl.pallas_call(
        functools.partial(_moe_kernel, nt=nt, nb=nb, sub=sub),
        grid=(t_tok // rows, MOE_GROUPS),
        in_specs=[pl.BlockSpec((rows, d), lambda i, e: (i, 0)),
                  pl.BlockSpec(mods.shape, lambda i, e: (0, 0, 0)),
                  pl.BlockSpec((1, d), lambda i, e: (0, 0)),
                  pl.BlockSpec((d, LANES), lambda i, e: (0, 0)),
                  pl.BlockSpec((1, LANES), lambda i, e: (0, 0)),
                  pl.BlockSpec((1, pg, d, MOE_FF), lambda i, e: (layer, e, 0, 0)),
                  pl.BlockSpec((1, pg, d, MOE_FF), lambda i, e: (layer, e, 0, 0)),
                  pl.BlockSpec((1, 1, pg * MOE_FF, d), lambda i, e: (layer, e, 0, 0))],
        out_specs=pl.BlockSpec((rows, d), lambda i, e: (i, 0)),
        out_shape=jax.ShapeDtypeStruct((t_tok, d), F32),
        scratch_shapes=[pltpu.VMEM((rows, d), BF16), pltpu.VMEM((rows, LANES), F32)],
        input_output_aliases={0: 0},
        compiler_params=_params("parallel", "arbitrary"),
        name="moe",
    )(x, mods, gain.reshape(1, d), w_router, b_router, w_gate, w_up, w_down)


def _attn_kernel(sink_ref, q_ref, kp_ref, kc_ref, kn_ref, vp_ref, vc_ref, vn_ref, kx_ref, vx_ref, o_ref, *, n_lat):
    j = pl.program_id(1)
    jl = j - 1
    dh = ATT_HEAD_DIM
    n_loc = TM + 2 * CHUNK
    k_all = jnp.concatenate([kp_ref[...], kc_ref[...], kn_ref[...], kx_ref[...]], axis=0)
    v_t = jnp.transpose(jnp.concatenate([vp_ref[...], vc_ref[...], vn_ref[...], vx_ref[...]],
                                        axis=0).astype(F32)).astype(BF16)
    q_t = jnp.transpose(q_ref[...].astype(F32)).astype(BF16)
    n_keys = k_all.shape[0]
    kj = lax.broadcasted_iota(jnp.int32, (n_keys, TM), 0)
    qi = lax.broadcasted_iota(jnp.int32, (n_keys, TM), 1)
    kpos = jl * TM - CHUNK + kj
    local_ok = (kj >= qi + CHUNK - WINDOW) & (kj <= qi + CHUNK + WINDOW) & (kpos >= 0) & (kpos < n_lat) & (jl >= 0)
    bias = jnp.where((kj >= n_loc) | local_ok, 0.0, -jnp.inf)
    uh = ATT_UNIT_HEADS
    bias = jnp.concatenate([bias] * uh, axis=1)
    lane_head = lax.broadcasted_iota(jnp.int32, (1, uh * TM), 1) // TM
    zeros_q = jnp.zeros((dh, uh * TM), BF16)
    units = [(h0 // ATT_GROUP, h0) for h0 in range(0, ATT_HEADS, uh)]

    def scores(u):
        g, h0 = units[u]
        half = g % 2
        k_pair = k_all[:, (g - half) * dh:(g - half + 2) * dh]
        qg = jnp.concatenate([q_t[(h0 + a) * dh:(h0 + a + 1) * dh, :] for a in range(uh)], axis=1)
        q_m = jnp.concatenate([zeros_q, qg] if half else [qg, zeros_q], axis=0)
        return _dot(k_pair, q_m) + bias

    outs = []
    s_next = scores(0)
    for u, (g, h0) in enumerate(units):
        s = s_next
        if u + 1 < len(units):
            s_next = scores(u + 1)
        sink = jnp.zeros((1, uh * TM), F32)
        for a in range(uh):
            sink = jnp.where(lane_head == a, sink_ref[h0 + a] * LOG2_E, sink)
        m = jnp.maximum(jnp.max(s, axis=0, keepdims=True), sink)
        p = jnp.exp2(s - m)
        den = jnp.sum(p, axis=0, keepdims=True) + jnp.exp2(sink - m)
        og = _dot(v_t[g * dh:(g + 1) * dh, :], p.astype(BF16)) / den
        outs += [og[:, a * TM:(a + 1) * TM] for a in range(uh)]
    o_ref[...] = jnp.transpose(jnp.concatenate(outs, axis=0)).astype(o_ref.dtype)


def _windowed_attention(q, k, v, sink, nb, seg):
    t_tok = q.shape[0]
    nchunk = seg // CHUNK
    nt = seg // TM
    per_tile = TM // CHUNK
    kvw = ATT_KV_HEADS * ATT_HEAD_DIM

    def halo(off):
        return lambda b, j, *_: (b * nchunk + jnp.clip(j * per_tile + off, per_tile, nchunk - 1), 0)

    tile = lambda b, j, *_: (b * nt + j, 0)
    kv_spec = [pl.BlockSpec((CHUNK, kvw), halo(-1)), pl.BlockSpec((TM, kvw), tile),
               pl.BlockSpec((CHUNK, kvw), halo(per_tile))]
    ctx_spec = pl.BlockSpec((TM, kvw), lambda b, j, *_: (b * nt, 0))
    return pl.pallas_call(
        functools.partial(_attn_kernel, n_lat=seg - TM),
        grid_spec=pltpu.PrefetchScalarGridSpec(
            num_scalar_prefetch=1,
            grid=(nb, nt),
            in_specs=[pl.BlockSpec((TM, q.shape[1]), tile)] + kv_spec + kv_spec + [ctx_spec, ctx_spec],
            out_specs=pl.BlockSpec((TM, q.shape[1]), tile),
        ),
        out_shape=jax.ShapeDtypeStruct((t_tok, q.shape[1]), BF16),
        compiler_params=_params("parallel", "parallel"),
        name="windowed_attention",
    )(sink, q, k, k, k, v, v, v, k, v)


def _ssm_in_kernel(x_ref, xp_ref, xn_ref, mods_ref, g_ref, w_ref, cw_ref, cb_ref, dtb_ref, aneg_ref,
                   z_ref, xs_ref, bm_ref, cm_ref, dtv_ref, a_ref, *ext, nt, nb):
    t = pl.program_id(0)
    tl = t % nt
    halo = 8
    first = (tl == 0) | (tl == 1)
    last = (tl == 0) | (tl == nt - 1)
    gain = g_ref[...]

    def normed(x, tile):
        return _normed(x, gain, mods_ref[_mod_row(tile, nt, nb)], 0, 1)

    h = normed(x_ref[...], t)
    h_prev = jnp.where(first, 0.0, normed(xp_ref[...], jnp.maximum(t - 1, 0)))
    h_next = jnp.where(last, 0.0, normed(xn_ref[...], jnp.minimum(t + 1, pl.num_programs(0) - 1)))
    hb = h.astype(BF16)
    hb_ext = jnp.concatenate([h_prev, h, h_next], axis=0).astype(BF16)
    chunk = SSM_CONV_CHUNK
    pad = SSM_CONV // 2

    def project(c):
        proj = _dot(hb_ext, w_ref[:, SSM_D_INNER + c:SSM_D_INNER + c + chunk])
        ext[c // chunk][...] = proj

    project(0)
    for c in range(0, SSM_CONV_DIM, chunk):
        if c + chunk < SSM_CONV_DIM:
            project(c + chunk)
        if c < SSM_D_INNER:
            z_ref[:, c:c + chunk] = _dot(hb, w_ref[:, c:c + chunk]).astype(z_ref.dtype)
        acc = jnp.broadcast_to(cb_ref[:, c:c + chunk], (TM, chunk))
        for k in range(SSM_CONV):
            acc = acc + cw_ref[k:k + 1, c:c + chunk] * ext[c // chunk][halo - pad + k:halo - pad + k + TM, :]
        y = _silu(acc)
        if c < SSM_D_INNER:
            xs_ref[0, c:c + chunk, :] = jnp.transpose(y).astype(xs_ref.dtype)
        elif c < SSM_D_INNER + SSM_BC:
            bm_ref[:, c - SSM_D_INNER:c - SSM_D_INNER + chunk] = y.astype(bm_ref.dtype)
        else:
            off = c - SSM_D_INNER - SSM_BC
            cm_ref[0, off:off + chunk, :] = jnp.transpose(y).astype(cm_ref.dtype)
    dt = _dot(hb, w_ref[:, SSM_D_INNER + SSM_CONV_DIM:])
    dtv = _softplus(dt + dtb_ref[...])
    dtv_ref[0] = jnp.transpose(dtv)
    a_ref[0] = jnp.transpose(dtv * aneg_ref[...] * LOG2_E)


def _ssm_in(x, mods, gain, w, conv_w, conv_b, dt_bias, a_neg, nt, nb):
    t_tok, d = x.shape
    n8 = t_tok // 8
    seg = nt * TM
    row = lambda t: (t, 0)
    col = lambda t: (t // nt, 0, t % nt)
    fixed = lambda t: (0, 0)
    return pl.pallas_call(
        functools.partial(_ssm_in_kernel, nt=nt, nb=nb),
        grid=(t_tok // TM,),
        in_specs=[pl.BlockSpec((TM, d), row),
                  pl.BlockSpec((8, d), lambda t: (jnp.maximum(t * (TM // 8) - 1, 0), 0)),
                  pl.BlockSpec((8, d), lambda t: (jnp.minimum((t + 1) * (TM // 8), n8 - 1), 0)),
                  pl.BlockSpec(mods.shape, lambda t: (0, 0, 0)),
                  pl.BlockSpec((1, d), fixed),
                  pl.BlockSpec(w.shape, fixed),
                  pl.BlockSpec((8, SSM_CONV_DIM), fixed),
                  pl.BlockSpec((1, SSM_CONV_DIM), fixed),
                  pl.BlockSpec((1, LANES), fixed),
                  pl.BlockSpec((1, LANES), fixed)],
        out_specs=[pl.BlockSpec((TM, SSM_D_INNER), row),
                   pl.BlockSpec((1, SSM_D_INNER, TM), col), pl.BlockSpec((TM, SSM_BC), row),
                   pl.BlockSpec((1, SSM_BC, TM), col), pl.BlockSpec((1, LANES, TM), col),
                   pl.BlockSpec((1, LANES, TM), col)],
        out_shape=[jax.ShapeDtypeStruct((t_tok, SSM_D_INNER), BF16),
                   jax.ShapeDtypeStruct((nb, SSM_D_INNER, seg), BF16), jax.ShapeDtypeStruct((t_tok, SSM_BC), BF16),
                   jax.ShapeDtypeStruct((nb, SSM_BC, seg), BF16), jax.ShapeDtypeStruct((nb, LANES, seg), F32),
                   jax.ShapeDtypeStruct((nb, LANES, seg), F32)],
        scratch_shapes=[pltpu.VMEM((TM + 16, SSM_CONV_CHUNK), F32)] * (SSM_CONV_DIM // SSM_CONV_CHUNK),
        compiler_params=_params("parallel"),
        name="ssm_in",
    )(x, x, x, mods, gain.reshape(1, d), w, conv_w, conv_b, dt_bias, a_neg)


def _tri(lower, n):
    r = lax.broadcasted_iota(jnp.int32, (n, n), 0)
    c = lax.broadcasted_iota(jnp.int32, (n, n), 1)
    return (c <= r) if lower else (c >= r)


def _ssd_prepare(dt_ref, a_ref, d):
    a_t = a_ref[0]
    dt_t = dt_ref[0]
    upper = jnp.where(_tri(False, SSD_CHUNK), 1.0, 0.0).astype(BF16)
    a_hi, a_lo = _split_bf16(a_t)
    cum = _dot(a_hi, upper) + _dot(a_lo, upper)
    cum_end = cum[:, SSD_CHUNK - 1:SSD_CHUNK]
    if d == 0:
        lane_v, sub_v = cum, -cum
        inter = jnp.exp2(cum)
        w_upd = jnp.exp2(cum_end - cum) * dt_t
    else:
        ecum = cum - a_t
        lane_v, sub_v = -ecum, ecum
        inter = jnp.exp2(cum_end - ecum)
        w_upd = jnp.exp2(ecum) * dt_t
    sub_c = jnp.transpose(sub_v)
    decay_end = jnp.exp2(cum_end)
    return dt_t, lane_v, sub_c, inter, w_upd, decay_end


def _ssd_groups(prep, xs_ref, bm_ref, ct_ref, dsk_ref, st_ref, y_ref, d):
    dt_t, lane_v, sub_c, inter, w_upd, decay_end = prep
    mask = _tri(d == 1, SSD_CHUNK)
    p = SSM_HEAD_DIM
    for g in range(SSM_GROUPS):
        bg = bm_ref[:, g * SSM_STATE:(g + 1) * SSM_STATE]
        cg_t = ct_ref[0, g * SSM_STATE:(g + 1) * SSM_STATE, :]
        cb_t = _dot(bg, cg_t)
        state = st_ref[g]
        y_in = _dot(state.astype(BF16), cg_t)
        upd = []
        for e in range(SSM_HPG):
            h = g * SSM_HPG + e
            ln = d * SSM_HEADS + h
            rows = slice(e * p, (e + 1) * p)
            blk = SSD_DECAY_BLOCK
            block_rows = []
            for bs in range(0, SSD_CHUNK, blk):
                block_cols = []
                for bl in range(0, SSD_CHUNK, blk):
                    if (bs > bl) if d == 0 else (bs < bl):
                        block_cols.append(jnp.zeros((blk, blk), BF16))
                        continue
                    seg = sub_c[bs:bs + blk, ln:ln + 1] + lane_v[ln:ln + 1, bl:bl + blk]
                    if bs == bl:
                        seg = jnp.where(mask[bs:bs + blk, bl:bl + blk], seg, -jnp.inf)
                    block_cols.append((cb_t[bs:bs + blk, bl:bl + blk] * jnp.exp2(seg)).astype(BF16))
                block_rows.append(jnp.concatenate(block_cols, axis=1))
            m_t = jnp.concatenate(block_rows, axis=0)
            xf = xs_ref[0, h * p:(h + 1) * p, :].astype(F32)
            u = (xf * dt_t[ln:ln + 1, :]).astype(BF16)
            yh = _dot(u, m_t) + y_in[rows, :] * inter[ln:ln + 1, :] + dsk_ref[d, h] * xf
            y_ref[0, h * p:(h + 1) * p, :] = yh.astype(y_ref.dtype)
            upd.append((xf * w_upd[ln:ln + 1, :]).astype(BF16))
        new = _dot(jnp.concatenate(upd, axis=0), bg)
        for e in range(SSM_HPG):
            ln = d * SSM_HEADS + g * SSM_HPG + e
            rows = slice(e * p, (e + 1) * p)
            st_ref[g, rows, :] = state[rows, :] * decay_end[ln:ln + 1, :] + new[rows, :]


def _ssd_kernel(dsk_ref, xsf, bmf, ctf, dtf, af, xsb, bmb, ctb, dtb, ab, yf_ref, yb_ref, stf, stb):
    @pl.when(pl.program_id(1) == 0)
    def _():
        stf[...] = jnp.zeros_like(stf)
        stb[...] = jnp.zeros_like(stb)

    prep_f = _ssd_prepare(dtf, af, 0)
    prep_b = _ssd_prepare(dtb, ab, 1)
    _ssd_groups(prep_f, xsf, bmf, ctf, dsk_ref, stf, yf_ref, 0)
    _ssd_groups(prep_b, xsb, bmb, ctb, dsk_ref, stb, yb_ref, 1)


def _scan_maps(nchunk, chunk):
    ctx_chunks = TM // chunk
    fwd = lambda b, t: (b * nchunk + t, 0)
    bwd = lambda b, t: (b * nchunk + jnp.where(t < ctx_chunks, ctx_chunks - 1 - t, nchunk + ctx_chunks - 1 - t), 0)
    return fwd, bwd


def _ssd_scan(xs_t, bm, c_t, dt_t, a_t, d_skip, nb, seg):
    nchunk = seg // SSD_CHUNK
    fwd_rows, bwd_rows = _scan_maps(nchunk, SSD_CHUNK)

    def specs(rows_map):
        cols_map = lambda b, t, *_: (b, 0, rows_map(b, t)[0] - b * nchunk)
        return [pl.BlockSpec((1, SSM_D_INNER, SSD_CHUNK), cols_map),
                pl.BlockSpec((SSD_CHUNK, SSM_BC), lambda b, t, *_: rows_map(b, t)),
                pl.BlockSpec((1, SSM_BC, SSD_CHUNK), cols_map),
                pl.BlockSpec((1, LANES, SSD_CHUNK), cols_map),
                pl.BlockSpec((1, LANES, SSD_CHUNK), cols_map)]

    arrs = (xs_t, bm, c_t, dt_t, a_t)
    return pl.pallas_call(
        _ssd_kernel,
        grid_spec=pltpu.PrefetchScalarGridSpec(
            num_scalar_prefetch=1,
            grid=(nb, nchunk),
            in_specs=specs(fwd_rows) + specs(bwd_rows),
            out_specs=[specs(fwd_rows)[0], specs(bwd_rows)[0]],
            scratch_shapes=[pltpu.VMEM((SSM_GROUPS, SSM_HPG * SSM_HEAD_DIM, SSM_STATE), F32)] * 2,
        ),
        out_shape=[jax.ShapeDtypeStruct(xs_t.shape, BF16)] * 2,
        compiler_params=_params("parallel", "arbitrary"),
        name="ssd_scan",
    )(d_skip, *arrs, *arrs)


def _log_sigmoid(x):
    return jnp.minimum(x, 0.0) - jnp.log1p(jnp.exp(-jnp.abs(x)))


ML_STATE_ROWS = ML_V_DIM + 16


def _mlstm_prepare(qk_ref, v_ref, g_ref, gb_ref, m_st, d):
    nh, dk, dv = ML_HEADS, ML_QK_DIM, ML_V_DIM
    g_t = jnp.transpose(g_ref[...] + gb_ref[...])
    ig = g_t[16 * d:16 * d + nh, :]
    lf = _log_sigmoid(g_t[16 * d + nh:16 * d + 2 * nh, :])
    upper = jnp.where(_tri(False, ML_CHUNK), 1.0, 0.0).astype(BF16)
    lf_hi, lf_lo = _split_bf16(lf)
    fc = _dot(lf_hi, upper) + _dot(lf_lo, upper)
    tot = fc[:, ML_CHUNK - 1:ML_CHUNK]
    m_prev = m_st[...]
    if d == 0:
        lane_v, sub_v = fc, ig - fc
        inter = fc + m_prev
        logw = tot - fc + ig
    else:
        ec = fc - lf
        lane_v, sub_v = -ec, ec + ig
        inter = tot - ec + m_prev
        logw = ec + ig
    sub_c = jnp.transpose(jnp.concatenate([sub_v, jnp.zeros((LANES - nh, ML_CHUNK), F32)], axis=0))
    m_new = jnp.maximum(tot + m_prev, jnp.max(logw, axis=-1, keepdims=True))
    ws = jnp.exp(logw - m_new)
    cw = jnp.exp(tot + m_prev - m_new)
    q_t = jnp.transpose(qk_ref[:, :nh * dk].astype(F32))
    v_t = jnp.transpose(v_ref[...].astype(F32))
    return lane_v, sub_c, inter, ws, cw, m_new, q_t, v_t


def _mlstm_heads(prep, qk_ref, c_st, m_st, h_ref, d):
    nh, dk, dv = ML_HEADS, ML_QK_DIM, ML_V_DIM
    lane_v, sub_c, inter, ws, cw, m_new, q_t, v_t = prep
    mask = _tri(d == 1, ML_CHUNK)
    lane = lax.broadcasted_iota(jnp.int32, (1, LANES), 1)
    zeros_q = jnp.zeros((dk, ML_CHUNK), F32)
    for h in range(nh):
        half = h % 2
        k_pair = qk_ref[:, nh * dk + (h - half) * dk:nh * dk + (h - half + 2) * dk] * (dk ** -0.5)
        qh = q_t[h * dk:(h + 1) * dk, :]
        q_m = jnp.concatenate([zeros_q, qh] if half else [qh, zeros_q], axis=0).astype(BF16)
        logd = jnp.where(mask, sub_c[:, h:h + 1] + lane_v[h:h + 1, :], -jnp.inf)
        mt = jnp.maximum(inter[h:h + 1, :], jnp.max(logd, axis=0, keepdims=True))
        sc = _dot(k_pair, q_m) * jnp.exp(logd - mt)
        vh = v_t[h * dv:(h + 1) * dv, :]
        state = c_st[h]
        cq = _dot(state.astype(BF16), q_m)
        w_int = jnp.exp(inter[h:h + 1, :] - mt)
        num = _dot(vh.astype(BF16), sc.astype(BF16)) + w_int * cq[:dv, :]
        den = jnp.sum(sc, axis=0, keepdims=True) + w_int * cq[dv:dv + 1, :]
        h_ref[0, h * dv:(h + 1) * dv, :] = (num / jnp.maximum(jnp.abs(den), jnp.exp(-mt))).astype(h_ref.dtype)
        ws_h = ws[h:h + 1, :]
        lhs = jnp.concatenate([vh * ws_h, jnp.broadcast_to(ws_h, (ML_STATE_ROWS - dv, ML_CHUNK))], axis=0)
        own = (lane >= half * dk) & (lane < (half + 1) * dk)
        row = lax.broadcasted_iota(jnp.int32, (ML_STATE_ROWS, 1), 0)
        new = cw[h:h + 1, :] * state + _dot(lhs.astype(BF16), k_pair)
        c_st[h] = jnp.where(own & (row <= dv), new, 0.0)
    m_st[...] = jnp.broadcast_to(m_new, m_st.shape)


def _mlstm_kernel(qkf, vf, gf, qkb, vb, gb, gbias, hf_ref, hb_ref, cf, mf, cb, mb):
    @pl.when(pl.program_id(1) == 0)
    def _():
        for r in (cf, mf, cb, mb):
            r[...] = jnp.zeros_like(r)

    prep_f = _mlstm_prepare(qkf, vf, gf, gbias, mf, 0)
    prep_b = _mlstm_prepare(qkb, vb, gb, gbias, mb, 1)
    _mlstm_heads(prep_f, qkf, cf, mf, hf_ref, 0)
    _mlstm_heads(prep_b, qkb, cb, mb, hb_ref, 1)


def _mlstm_scan(qk, v, g, gate_b, nb, seg):
    t_tok = qk.shape[0]
    nchunk = seg // ML_CHUNK
    fwd, bwd = _scan_maps(nchunk, ML_CHUNK)
    widths = (qk.shape[1], v.shape[1], LANES)
    in_specs = ([pl.BlockSpec((ML_CHUNK, w), fwd) for w in widths] + [pl.BlockSpec((ML_CHUNK, w), bwd) for w in widths]
                + [pl.BlockSpec((1, LANES), lambda b, t: (0, 0))])
    state = [pltpu.VMEM((ML_HEADS, ML_STATE_ROWS, LANES), F32), pltpu.VMEM((ML_HEADS, LANES), F32)]
    out_map = lambda rows_map: (lambda b, t: (b, 0, rows_map(b, t)[0] - b * nchunk))
    return pl.pallas_call(
        _mlstm_kernel,
        grid=(nb, nchunk),
        in_specs=in_specs,
        out_specs=[pl.BlockSpec((1, v.shape[1], ML_CHUNK), out_map(fwd)),
                   pl.BlockSpec((1, v.shape[1], ML_CHUNK), out_map(bwd))],
        out_shape=[jax.ShapeDtypeStruct((nb, v.shape[1], seg), BF16)] * 2,
        scratch_shapes=state + state,
        compiler_params=_params("parallel", "arbitrary"),
        name="mlstm_scan",
    )(qk, v, g, qk, v, g, gate_b)


def _rms_rows(x, g_row):
    return x * lax.rsqrt(jnp.mean(x * x, axis=-1, keepdims=True) + EPS) * g_row


MLA_SCORE_SCALE = (MLA_NOPE + MLA_ROPE) ** -0.5 * LOG2_E


def _mla_queries(cq, g_ref, wt_ref, cos_ref, sin_ref, qt_ref):
    cq_t = jnp.transpose(_rms_rows(cq, g_ref[...])).astype(BF16)
    q_t = _dot(wt_ref[...], cq_t) * MLA_SCORE_SCALE
    qt_ref[0] = q_t.astype(qt_ref.dtype)
    cos, sin = cos_ref[...], sin_ref[...]
    half = MLA_ROPE // 2
    for h in range(MLA_HEADS):
        r0 = h * LANES + MLA_NOPE
        x1, x2 = q_t[r0:r0 + half, :], q_t[r0 + half:r0 + MLA_ROPE, :]
        qt_ref[0, r0:r0 + half, :] = (x1 * cos - x2 * sin).astype(qt_ref.dtype)
        qt_ref[0, r0 + half:r0 + MLA_ROPE, :] = (x2 * cos + x1 * sin).astype(qt_ref.dtype)


def _mla_keys_values(ckv, kr, g_ref, wk_ref, wvt_ref, cos_ref, sin_ref, k_ref, vt_ref):
    cn = _rms_rows(ckv, g_ref[...])
    vt_ref[0] = _dot(wvt_ref[...], jnp.transpose(cn).astype(BF16)).astype(vt_ref.dtype)
    kn = _dot(cn.astype(BF16), wk_ref[...])
    kr = pltpu.roll(kr, MLA_NOPE, axis=1)
    half = MLA_ROPE // 2
    lane = lax.broadcasted_iota(jnp.int32, (1, LANES), 1)
    partner = jnp.where(lane < MLA_NOPE + half, pltpu.roll(kr, LANES - half, axis=1), pltpu.roll(kr, half, axis=1))
    roped = kr * cos_ref[...] + partner * sin_ref[...]
    for h in range(MLA_HEADS):
        k_ref[:, h * LANES:(h + 1) * LANES] = (kn[:, h * LANES:(h + 1) * LANES] + roped).astype(k_ref.dtype)


def _mla_in_kernel(x_ref, mods_ref, g_ref, w_ref, gq_ref, wqt_ref, gkv_ref, wk_ref, wvt_ref,
                   cos_t_ref, sin_t_ref, cos_ref, sin_ref, qt_ref, k_ref, vt_ref, *, nt, nb):
    mod = mods_ref[_mod_row(pl.program_id(0), nt, nb)]
    hb = _normed(x_ref[...], g_ref[...], mod, 0, 1).astype(BF16)
    p = _dot(hb, w_ref[...])
    _mla_queries(p[:, :MLA_RANK], gq_ref, wqt_ref, cos_t_ref, sin_t_ref, qt_ref)
    _mla_keys_values(p[:, MLA_RANK:2 * MLA_RANK], p[:, 2 * MLA_RANK:], gkv_ref, wk_ref, wvt_ref,
                     cos_ref, sin_ref, k_ref, vt_ref)


def _mla_in(x, mods, gain, w_in, gain_q, wq_t, gain_kv, w_k, wv_t, tables_t, tables, nt, nb):
    t_tok, d = x.shape
    seg = nt * TM
    half = MLA_ROPE // 2
    fixed = lambda t: (0, 0)
    return pl.pallas_call(
        functools.partial(_mla_in_kernel, nt=nt, nb=nb),
        grid=(t_tok // TM,),
        in_specs=[pl.BlockSpec((TM, d), lambda t: (t, 0)),
                  pl.BlockSpec(mods.shape, lambda t: (0, 0, 0)),
                  pl.BlockSpec((1, d), fixed),
                  pl.BlockSpec(w_in.shape, fixed),
                  pl.BlockSpec((1, MLA_RANK), fixed),
                  pl.BlockSpec(wq_t.shape, fixed),
                  pl.BlockSpec((1, MLA_RANK), fixed),
                  pl.BlockSpec(w_k.shape, fixed),
                  pl.BlockSpec(wv_t.shape, fixed),
                  pl.BlockSpec((half, TM), lambda t: (0, t % nt)),
                  pl.BlockSpec((half, TM), lambda t: (0, t % nt)),
                  pl.BlockSpec((TM, LANES), lambda t: (t % nt, 0)),
                  pl.BlockSpec((TM, LANES), lambda t: (t % nt, 0))],
        out_specs=[pl.BlockSpec((1, MLA_HEADS * LANES, TM), lambda t: (t, 0, 0)),
                   pl.BlockSpec((TM, MLA_HEADS * LANES), lambda t: (t, 0)),
                   pl.BlockSpec((1, MLA_HEADS * MLA_V, TM), lambda t: (t // nt, 0, t % nt))],
        out_shape=[jax.ShapeDtypeStruct((t_tok // TM, MLA_HEADS * LANES, TM), BF16),
                   jax.ShapeDtypeStruct((t_tok, MLA_HEADS * LANES), BF16),
                   jax.ShapeDtypeStruct((nb, MLA_HEADS * MLA_V, seg), BF16)],
        compiler_params=_params("parallel"),
        name="mla_in",
    )(x, mods, gain.reshape(1, d), w_in, gain_q.reshape(1, MLA_RANK), wq_t, gain_kv.reshape(1, MLA_RANK), w_k, wv_t,
      *tables_t, *tables)


def _mla_attn_kernel(qt_ref, k_ref, vt_ref, o_ref):
    @pl.when(pl.program_id(2) == 0)
    def _():
        o_ref[...] = jnp.zeros_like(o_ref)

    @pl.when(pl.program_id(2) > 0)
    def _():
        heads = range(MLA_HEADS_PER_STEP)
        seg = k_ref.shape[0]
        tk = MLA_KEY_TILE if (seg - TM) % MLA_KEY_TILE == 0 else TM
        tiles = [(0, TM)] + [(a, a + tk) for a in range(TM, seg, tk)]

        def scores(hh, j):
            return _dot(k_ref[tiles[j][0]:tiles[j][1], hh * LANES:(hh + 1) * LANES],
                        qt_ref[0, hh * LANES:(hh + 1) * LANES, :])

        m = [jnp.full((1, TM), -jnp.inf, F32) for _ in heads]
        l = [jnp.zeros((1, TM), F32) for _ in heads]
        acc = [jnp.zeros((MLA_V, TM), F32) for _ in heads]
        ahead = min(MLA_LOOKAHEAD, len(tiles))
        pending = [[scores(hh, j) for hh in heads] for j in range(ahead)]
        for j in range(len(tiles)):
            s_cur = pending.pop(0)
            if j + ahead < len(tiles):
                pending.append([scores(hh, j + ahead) for hh in heads])
            for hh in heads:
                m_new = jnp.maximum(m[hh], jnp.max(s_cur[hh], axis=0, keepdims=True))
                alpha = jnp.exp2(m[hh] - m_new)
                p = jnp.exp2(s_cur[hh] - m_new)
                l[hh] = alpha * l[hh] + jnp.sum(p, axis=0, keepdims=True)
                v_t = vt_ref[0, hh * MLA_V:(hh + 1) * MLA_V, tiles[j][0]:tiles[j][1]]
                acc[hh] = alpha * acc[hh] + _dot(v_t, p.astype(BF16))
                m[hh] = m_new
        o_t = jnp.concatenate([acc[hh] / l[hh] for hh in heads], axis=0)
        o_ref[...] = jnp.transpose(o_t).astype(o_ref.dtype)


def _mla_attention(q_t, k, v_t, nb, seg):
    t_tok = k.shape[0]
    nt = seg // TM
    hps = MLA_HEADS_PER_STEP
    return pl.pallas_call(
        _mla_attn_kernel,
        grid=(nb, MLA_HEADS // hps, nt),
        in_specs=[pl.BlockSpec((1, hps * LANES, TM), lambda b, hp, j: (b * nt + j, hp, 0)),
                  pl.BlockSpec((seg, hps * LANES), lambda b, hp, j: (b, hp)),
                  pl.BlockSpec((1, hps * MLA_V, seg), lambda b, hp, j: (b, hp, 0))],
        out_specs=pl.BlockSpec((TM, hps * MLA_V), lambda b, hp, j: (b * nt + j, hp)),
        out_shape=jax.ShapeDtypeStruct((t_tok, MLA_HEADS * MLA_V), BF16),
        compiler_params=_params("parallel", "parallel", "arbitrary"),
        name="mla_attention",
    )(q_t, k, v_t)


def _final_kernel(x_ref, g_ref, o_ref):
    o_ref[0] = _rms_rows(x_ref[...], g_ref[...])


def _final_norm(x, gain, nb, seg):
    d = x.shape[1]
    nt = seg // TM
    return pl.pallas_call(
        _final_kernel,
        grid=(nb, nt - 1),
        in_specs=[pl.BlockSpec((TM, d), lambda b, j: (b * nt + 1 + j, 0)),
                  pl.BlockSpec((1, d), lambda b, j: (0, 0))],
        out_specs=pl.BlockSpec((1, TM, d), lambda b, j: (b, j, 0)),
        out_shape=jax.ShapeDtypeStruct((nb, seg - TM, d), F32),
        compiler_params=_params("parallel", "parallel"),
        name="final_norm",
    )(x, gain.reshape(1, d))


def _rope_angles(n_lat, rot_dim):
    rows = n_lat // GRID_W
    row = jnp.repeat(jnp.arange(rows), GRID_W).astype(F32)
    col = jnp.tile(jnp.arange(GRID_W), rows).astype(F32)
    quarter = rot_dim // 4
    inv = ROPE_BASE ** (-jnp.arange(quarter, dtype=F32) / quarter)
    ang = jnp.concatenate([row[:, None] * inv, col[:, None] * inv], axis=-1)
    return jnp.cos(ang), jnp.sin(ang)


def _with_ctx_rows(tab, fill):
    return jnp.concatenate([jnp.full((TM, tab.shape[1]), fill, F32), tab], axis=0)


def _attn_rope_tables(n_lat):
    cos, sin = _rope_angles(n_lat, ATT_HEAD_DIM)
    cos_h = jnp.concatenate([cos, cos], axis=1)
    sin_h = jnp.concatenate([-sin, sin], axis=1)
    reps = LANES // ATT_HEAD_DIM
    return (_with_ctx_rows(jnp.tile(cos_h, (1, reps)), 1.0), _with_ctx_rows(jnp.tile(sin_h, (1, reps)), 0.0))


def _swap_halves(w, n_heads, dim):
    w3 = w.reshape(w.shape[0], n_heads, 2, dim // 2)
    return w3[:, :, ::-1, :].reshape(w.shape[0], n_heads * dim)


def _mla_k_tables(n_lat):
    cos, sin = _rope_angles(n_lat, MLA_ROPE)
    lo = jnp.zeros((n_lat, MLA_NOPE), F32)
    hi = jnp.zeros((n_lat, LANES - MLA_NOPE - MLA_ROPE), F32)
    cos_c = jnp.concatenate([lo, cos, cos, hi], axis=1)
    sin_c = jnp.concatenate([lo, -sin, sin, hi], axis=1)
    ctx_cos = jnp.concatenate([lo[:TM], jnp.ones((TM, MLA_ROPE), F32), hi[:TM]], axis=1)
    return (jnp.concatenate([ctx_cos, cos_c], axis=0), _with_ctx_rows(sin_c, 0.0))


def _pad_heads(w, real):
    r = w.shape[0]
    w3 = w.reshape(r, MLA_HEADS, real)
    return jnp.pad(w3, ((0, 0), (0, 0), (0, LANES - real))).reshape(r, MLA_HEADS * LANES)


def kernel(x, c, ctx, c_ctx, norm1_g, norm2_g, w_mod, b_mod, moe_w_group, moe_b_group, moe_w_expert, moe_b_expert, moe_w_gate, moe_w_up, moe_w_down, attn_w_in, attn_sink, attn_w_out, ssm_w_in, ssm_conv_w, ssm_conv_b, ssm_dt_bias, ssm_a_log, ssm_d, ssm_norm_g, ssm_w_out, mlstm_w_in, mlstm_gate_b, mlstm_norm_g, mlstm_w_out, mla_w_in, mla_q_norm_g, mla_w_q_up, mla_kv_norm_g, mla_w_kv_up, mla_w_out, final_norm_g):
    nb, n_lat, d = x.shape
    assert ctx.shape[1] == TM and d == D_MODEL and n_lat % TM == 0
    depth = w_mod.shape[0]
    seg = TM + n_lat
    nt = seg // TM
    t_tok = nb * seg

    xs = (x, ctx)

    rows = -(-(nb + 1) // 8) * 8
    cvec = jnp.concatenate([c, c_ctx[None, :], jnp.zeros((rows - nb - 1, d), F32)], axis=0)
    mods = _modulation(cvec, w_mod, b_mod).reshape(depth, rows, ADALN_CHUNKS, d)
    mods = jnp.pad(mods, ((0, 0), (0, 0), (0, MOD_ROWS - ADALN_CHUNKS), (0, 0)))

    w_router = jnp.concatenate([moe_w_expert, moe_w_group,
                                jnp.zeros((depth, d, LANES - MOE_EXPERTS - MOE_GROUPS), F32)], axis=-1)
    b_router = jnp.concatenate([moe_b_expert, moe_b_group,
                                jnp.zeros((depth, LANES - MOE_EXPERTS - MOE_GROUPS), F32)], axis=-1)
    w_gate, w_up = moe_w_gate.astype(BF16), moe_w_up.astype(BF16)
    w_down = moe_w_down.astype(BF16).reshape(depth, MOE_GROUPS, MOE_PER_GROUP * MOE_FF, d)

    for i in range(depth):
        kind = i % 4
        mod_i = mods[i]
        if kind == 0:
            nq, nk = ATT_HEADS * ATT_HEAD_DIM, ATT_KV_HEADS * ATT_HEAD_DIM
            w_in = attn_w_in[i // 4]
            w_all = jnp.concatenate([w_in, _swap_halves(w_in[:, :nq], ATT_HEADS, ATT_HEAD_DIM),
                                     _swap_halves(w_in[:, nq:nq + nk], ATT_KV_HEADS, ATT_HEAD_DIM)],
                                    axis=1).astype(BF16)
            n_in = w_in.shape[1]
            q, k, v = _normproj(xs, mod_i, norm1_g[i], w_all,
                                [(0, nq, n_in, ATT_HEAD_DIM ** -0.5 * LOG2_E), (nq, nk, n_in + nq), (nq + nk, nk, None)],
                                [BF16, BF16, BF16], nt, nb, tables=_attn_rope_tables(n_lat))
            o = _windowed_attention(q, k, v, attn_sink[i // 4], nb, seg)
            xs = _outproj([o], [nq], xs, mod_i, attn_w_out[i // 4].astype(BF16), _plain_prologue, nt, nb)
        elif kind == 1:
            j = i // 4
            w_in = jnp.pad(ssm_w_in[j], ((0, 0), (0, LANES - 2 * SSM_HEADS))).astype(BF16)
            lane_pad = LANES - 2 * SSM_HEADS
            dt_bias = jnp.pad(ssm_dt_bias[j].reshape(1, -1), ((0, 0), (0, lane_pad)))
            a_neg = jnp.pad(-jnp.exp(ssm_a_log[j].astype(F32)).reshape(1, -1), ((0, 0), (0, lane_pad)))
            conv_w = jnp.pad(ssm_conv_w[j], ((0, 8 - SSM_CONV), (0, 0)))
            z, xc, bm, cm, dtv, a = _ssm_in(xs, mod_i, norm1_g[i], w_in, conv_w, ssm_conv_b[j].reshape(1, -1),
                                            dt_bias, a_neg, nt, nb)
            yf, yb = _ssd_scan(xc, bm, cm, dtv, a, ssm_d[j].astype(F32), nb, seg)
            xs = _outproj([z], [SSM_D_INNER], xs, mod_i, ssm_w_out[j].astype(BF16), _ssm_prologue,
                          nt, nb, extra=(ssm_norm_g[j].reshape(1, -1),), ins_t=(yf, yb))
        elif kind == 2:
            j = i // 4
            nqk, nv = 2 * ML_HEADS * ML_QK_DIM, ML_HEADS * ML_V_DIM
            w_in = jnp.pad(mlstm_w_in[j], ((0, 0), (0, LANES - 4 * ML_HEADS))).astype(BF16)
            qk, v, o, g = _normproj(xs, mod_i, norm1_g[i], w_in,
                                    [(0, nqk, None), (nqk, nv, None), (nqk + nv, nv, None), (nqk + 2 * nv, LANES, None)],
                                    [BF16, BF16, BF16, F32], nt, nb)
            gate_b = jnp.pad(mlstm_gate_b[j].reshape(1, -1), ((0, 0), (0, LANES - 4 * ML_HEADS)))
            hf, hb = _mlstm_scan(qk, v, g, gate_b, nb, seg)
            xs = _outproj([o], [nv], xs, mod_i, mlstm_w_out[j].astype(BF16), _mlstm_prologue,
                          nt, nb, extra=(mlstm_norm_g[j].reshape(1, -1),), ins_t=(hf, hb))
        else:
            j = i // 4
            w_in = jnp.pad(mla_w_in[j], ((0, 0), (0, LANES - MLA_ROPE))).astype(BF16)
            cos, sin = _rope_angles(n_lat, MLA_ROPE)
            tables_t = (jnp.transpose(_with_ctx_rows(cos, 1.0)), jnp.transpose(_with_ctx_rows(sin, 0.0)))
            wq_t = jnp.transpose(_pad_heads(mla_w_q_up[j], MLA_NOPE + MLA_ROPE)).astype(BF16)
            w_kv = mla_w_kv_up[j].reshape(MLA_RANK, MLA_HEADS, MLA_NOPE + MLA_V)
            w_k = _pad_heads(w_kv[:, :, :MLA_NOPE].reshape(MLA_RANK, MLA_HEADS * MLA_NOPE), MLA_NOPE).astype(BF16)
            wv_t = jnp.transpose(w_kv[:, :, MLA_NOPE:].reshape(MLA_RANK, MLA_HEADS * MLA_V)).astype(BF16)
            q_t, k, v_t = _mla_in(xs, mod_i, norm1_g[i], w_in, mla_q_norm_g[j], wq_t, mla_kv_norm_g[j], w_k, wv_t,
                                  tables_t, _mla_k_tables(n_lat), nt, nb)
            o = _mla_attention(q_t, k, v_t, nb, seg)
            xs = _outproj([o], [MLA_HEADS * MLA_V], xs, mod_i, mla_w_out[j].astype(BF16), _plain_prologue, nt, nb)

        xs = _moe(xs, mod_i, norm2_g[i], w_router[i], b_router[i].reshape(1, -1), w_gate, w_up, w_down, i, nt, nb)

    return _final_norm(xs, final_norm_g, nb, seg)
```

```python
import functools
import math

import jax
import jax.numpy as jnp
from jax import lax
from jax.experimental import pallas as pl
from jax.experimental.pallas import tpu as pltpu

F32 = jnp.float32
BF16 = jnp.bfloat16

D_MODEL = 1024
GRID_W = 64
EPS = 1e-6
ROPE_BASE = 10000.0
ADALN_CHUNKS = 6
CHUNK = 128
SSD_CHUNK = 256
ML_CHUNK = 128
SSD_DECAY_BLOCK = 128
TM = 256
MOD_ROWS = 8
LANES = 128
V7X_VMEM_LIMIT = 48 * 1024 * 1024

ATT_HEADS, ATT_KV_HEADS, ATT_HEAD_DIM, WINDOW = 16, 4, 64, 128
ATT_GROUP = ATT_HEADS // ATT_KV_HEADS
ATT_UNIT_HEADS = 4
SSM_D_INNER, SSM_HEAD_DIM, SSM_HEADS, SSM_GROUPS, SSM_STATE, SSM_CONV = 2048, 64, 32, 4, 128, 5
SSM_HPG = SSM_HEADS // SSM_GROUPS
SSM_BC = SSM_GROUPS * SSM_STATE
SSM_CONV_DIM = SSM_D_INNER + 2 * SSM_BC
SSM_CONV_CHUNK = 256
ML_HEADS, ML_QK_DIM, ML_V_DIM = 8, 64, 128
MLA_HEADS, MLA_RANK, MLA_NOPE, MLA_ROPE, MLA_V = 16, 256, 64, 32, 64
MOE_GROUPS, MOE_PER_GROUP, MOE_EXPERTS, MOE_FF = 4, 4, 16, 256
MOE_ROWS = 1024
MLA_KEY_TILE = 1024
MLA_HEADS_PER_STEP = 4
MLA_LOOKAHEAD = 1
LOG2_E = 1.4426950408889634


def _dot(a, b):
    return jnp.dot(a, b, preferred_element_type=F32)


def _dot_nt(a, b):
    return lax.dot_general(a, b, (((1,), (1,)), ((), ())), preferred_element_type=F32)


def _split_bf16(x):
    hi = x.astype(BF16)
    lo = (x - hi.astype(F32)).astype(BF16)
    return hi, lo


def _dot_split(a, b):
    a_hi, a_lo = _split_bf16(a)
    b_hi, b_lo = _split_bf16(b)
    return _dot(a_hi, b_hi) + _dot(a_lo, b_hi) + _dot(a_hi, b_lo)


def _sigmoid(x):
    return 1.0 / (1.0 + jnp.exp(-x))


def _silu(x):
    return x * _sigmoid(x)


def _softplus(x):
    return jnp.maximum(x, 0.0) + jnp.log1p(jnp.exp(-jnp.abs(x)))


def _params(*sem):
    return pltpu.CompilerParams(dimension_semantics=sem, vmem_limit_bytes=V7X_VMEM_LIMIT)


def _mod_row(t, nt, nb):
    return jnp.where(t % nt == 0, nb, t // nt)


def _mod_kernel(c_ref, w_ref, b_ref, o_ref):
    o_ref[0] = _dot_split(_silu(c_ref[...]), w_ref[0]) + b_ref[0]


def _modulation(cvec, w_mod, b_mod):
    depth, d, n = w_mod.shape
    tn = 1536
    rows = cvec.shape[0]
    return pl.pallas_call(
        _mod_kernel,
        grid=(depth, n // tn),
        in_specs=[pl.BlockSpec((rows, d), lambda l, j: (0, 0)),
                  pl.BlockSpec((1, d, tn), lambda l, j: (l, 0, j)),
                  pl.BlockSpec((1, 1, tn), lambda l, j: (l, 0, j))],
        out_specs=pl.BlockSpec((1, rows, tn), lambda l, j: (l, 0, j)),
        out_shape=jax.ShapeDtypeStruct((depth, rows, n), F32),
        compiler_params=_params("arbitrary", "arbitrary"),
        name="modulation",
    )(cvec, w_mod, b_mod.reshape(depth, 1, n))


def _normed(x, g_row, mod, sh_row, sc_row):
    y = x * lax.rsqrt(jnp.mean(x * x, axis=-1, keepdims=True) + EPS) * g_row
    return y * (1.0 + mod[sc_row:sc_row + 1, :]) + mod[sh_row:sh_row + 1, :]


def _proj_columns(hb, w_ref, o_ref, start, width, rope_half, cos, sin, scale=None, chunk=512):
    for c in range(0, width, chunk):
        cw = min(chunk, width - c)
        acc = _dot(hb, w_ref[:, start + c:start + c + cw])
        if rope_half is not None:
            lane = lax.broadcasted_iota(jnp.int32, (1, cw), 1)
            rot = jnp.where(lane % (2 * rope_half) < rope_half, pltpu.roll(acc, cw - rope_half, axis=1),
                            pltpu.roll(acc, rope_half, axis=1))
            reps = cw // LANES
            acc = acc * jnp.tile(cos, (1, reps)) + rot * jnp.tile(sin, (1, reps))
        if scale is not None:
            acc = acc * scale
        o_ref[:, c:c + cw] = acc.astype(o_ref.dtype)


def _stream_specs(x, nt):
    if isinstance(x, tuple):
        d = x[0].shape[-1]
        return [pl.BlockSpec((1, TM, d), lambda t: (t // nt, jnp.maximum(t % nt - 1, 0), 0)),
                pl.BlockSpec((1, TM, d), lambda t: (t // nt, 0, 0))], list(x)
    sub = _tiles_per_step(x, nt)
    return [pl.BlockSpec((sub * TM, x.shape[1]), lambda t: (t, 0))], [x]


def _tiles_per_step(x, nt):
    if isinstance(x, tuple):
        return 1
    return 2 if (x.shape[0] // TM) % 2 == 0 else 1


def _stream_tile(x_refs, nt, s):
    if len(x_refs) == 1:
        return x_refs[0][s * TM:(s + 1) * TM, :]
    return jnp.where(pl.program_id(0) % nt == 0, x_refs[1][0], x_refs[0][0])


def _normproj_kernel(*refs, outs, has_rope, n_x, nt, nb, sub):
    x_refs, refs = refs[:n_x], refs[n_x:]
    mods_ref, g_ref, w_ref = refs[:3]
    k = 3
    cos = sin = None
    if has_rope:
        cos, sin = refs[3][...], refs[4][...]
        k = 5
    hs = []
    for s in range(sub):
        mod = mods_ref[_mod_row(pl.program_id(0) * sub + s, nt, nb)]
        hs.append(_normed(_stream_tile(x_refs, nt, s), g_ref[...], mod, 0, 1).astype(BF16))
    hb = hs[0] if sub == 1 else jnp.concatenate(hs, axis=0)
    for o_ref, spec in zip(refs[k:], outs):
        _proj_columns(hb, w_ref, o_ref, *spec[:3], cos, sin, scale=spec[3] if len(spec) > 3 else None)


def _normproj(x, mods, gain, w, outs, out_dtypes, nt, nb, tables=None):
    d, n = w.shape
    t_tok = nb * nt * TM
    sub = _tiles_per_step(x, nt)
    assert tables is None or sub == 1
    rows = sub * TM
    x_specs, x_args = _stream_specs(x, nt)
    in_specs = x_specs + [pl.BlockSpec(mods.shape, lambda t: (0, 0, 0)),
                          pl.BlockSpec((1, d), lambda t: (0, 0)),
                          pl.BlockSpec((d, n), lambda t: (0, 0))]
    args = x_args + [mods, gain.reshape(1, d), w]
    if tables is not None:
        in_specs += [pl.BlockSpec((TM, LANES), lambda t: (t % nt, 0))] * 2
        args += list(tables)
    return pl.pallas_call(
        functools.partial(_normproj_kernel, outs=tuple(outs), has_rope=tables is not None, n_x=len(x_args),
                          nt=nt, nb=nb, sub=sub),
        grid=(t_tok // rows,),
        in_specs=in_specs,
        out_specs=[pl.BlockSpec((rows, o[1]), lambda t: (t, 0)) for o in outs],
        out_shape=[jax.ShapeDtypeStruct((t_tok, o[1]), dt) for o, dt in zip(outs, out_dtypes)],
        compiler_params=_params("parallel"),
        name="normproj",
    )(*args)


def _outproj_kernel(*refs, prologue, n_t, n_tok, n_extra, n_x, nt, nb, sub):
    ins_t = refs[:n_t * sub]
    ins = refs[n_t * sub:n_t * sub + n_tok]
    extra = refs[n_t * sub + n_tok:n_t * sub + n_tok + n_extra]
    x_refs = refs[n_t * sub + n_tok + n_extra:n_t * sub + n_tok + n_extra + n_x]
    mods_ref, w_ref, o_ref = refs[n_t * sub + n_tok + n_extra + n_x:]
    parts = []
    for s in range(sub):
        tok = [r if sub == 1 else r.at[pl.ds(s * TM, TM), :] for r in ins]
        parts.append(prologue(*ins_t[s * n_t:(s + 1) * n_t], *tok, *extra))
    a = parts[0] if sub == 1 else jnp.concatenate(parts, axis=0)
    y = _dot(a, w_ref[...])
    for s in range(sub):
        rows = slice(s * TM, (s + 1) * TM)
        gate = mods_ref[_mod_row(pl.program_id(0) * sub + s, nt, nb)][2:3, :]
        o_ref[rows, :] = _stream_tile(x_refs, nt, s) + gate * y[rows, :]


def _outproj(ins, in_widths, x, mods, w, prologue, nt, nb, extra=(), ins_t=()):
    d = w.shape[1]
    t_tok = nb * nt * TM
    sub = _tiles_per_step(x, nt)
    rows = sub * TM

    def tile_map(s):
        return lambda t: ((t * sub + s) // nt, 0, (t * sub + s) % nt)

    in_specs, args = [], []
    for s in range(sub):
        in_specs += [pl.BlockSpec((1, a.shape[1], TM), tile_map(s)) for a in ins_t]
        args += list(ins_t)
    in_specs += [pl.BlockSpec((rows, wd), lambda t: (t, 0)) for wd in in_widths]
    in_specs += [pl.BlockSpec(e.shape, lambda t: (0, 0)) for e in extra]
    x_specs, x_args = _stream_specs(x, nt)
    n_before_x = len(in_specs)
    in_specs += x_specs + [pl.BlockSpec(mods.shape, lambda t: (0, 0, 0)), pl.BlockSpec(w.shape, lambda t: (0, 0))]
    return pl.pallas_call(
        functools.partial(_outproj_kernel, prologue=prologue, n_t=len(ins_t), n_tok=len(ins), n_extra=len(extra),
                          n_x=len(x_args), nt=nt, nb=nb, sub=sub),
        grid=(t_tok // rows,),
        in_specs=in_specs,
        out_specs=pl.BlockSpec((rows, d), lambda t: (t, 0)),
        out_shape=jax.ShapeDtypeStruct((t_tok, d), F32),
        input_output_aliases={} if isinstance(x, tuple) else {n_before_x: 0},
        compiler_params=_params("parallel"),
        name="outproj",
    )(*args, *ins, *extra, *x_args, mods, w)


def _group_rms(y, n_groups):
    width = y.shape[1] // n_groups
    parts = []
    for g in range(n_groups):
        yg = y[:, g * width:(g + 1) * width]
        parts.append(yg * lax.rsqrt(jnp.mean(yg * yg, axis=-1, keepdims=True) + EPS))
    return jnp.concatenate(parts, axis=1)


def _plain_prologue(o_ref):
    return o_ref[...]


def _ssm_prologue(yf_ref, yb_ref, z_ref, g_ref):
    y = jnp.transpose(yf_ref[0].astype(F32) + yb_ref[0].astype(F32)) * _silu(z_ref[...].astype(F32))
    return (_group_rms(y, SSM_GROUPS) * g_ref[...]).astype(BF16)


def _mlstm_prologue(hf_ref, hb_ref, o_ref, g_ref):
    h = jnp.transpose(hf_ref[0].astype(F32) + hb_ref[0].astype(F32))
    return (_group_rms(h, ML_HEADS) * g_ref[...] * _sigmoid(o_ref[...].astype(F32))).astype(BF16)


MOE_ROUTE_ROWS = 24


def _route(logits):
    lt = jnp.transpose(logits)[:MOE_ROUTE_ROWS, :]
    row = lax.broadcasted_iota(jnp.int32, lt.shape, 0).astype(F32)
    neg = -jnp.inf
    lg = jnp.where((row >= MOE_EXPERTS) & (row < MOE_EXPERTS + MOE_GROUPS), lt, neg)
    gmax = jnp.max(lg, axis=0, keepdims=True)
    g_sel = jnp.min(jnp.where(lg == gmax, row, LANES), axis=0, keepdims=True) - MOE_EXPERTS
    p_g = 1.0 / jnp.sum(jnp.exp(lg - gmax), axis=0, keepdims=True)
    in_group = (row >= g_sel * MOE_PER_GROUP) & (row < (g_sel + 1) * MOE_PER_GROUP)
    le = jnp.where(in_group, lt, neg)
    v1 = jnp.max(le, axis=0, keepdims=True)
    i1 = jnp.min(jnp.where(le == v1, row, LANES), axis=0, keepdims=True)
    le2 = jnp.where(row == i1, neg, le)
    v2 = jnp.max(le2, axis=0, keepdims=True)
    i2 = jnp.min(jnp.where(le2 == v2, row, LANES), axis=0, keepdims=True)
    e2 = jnp.exp(v2 - v1)
    w1 = p_g / (1.0 + e2)
    comb_t = jnp.where(row == i1, w1, 0.0) + jnp.where(row == i2, w1 * e2, 0.0)
    pad = jnp.zeros((LANES - MOE_ROUTE_ROWS, lt.shape[1]), F32)
    return jnp.transpose(jnp.concatenate([comb_t, pad], axis=0))


def _moe_kernel(x_ref, mods_ref, g_ref, wr_ref, br_ref, wg_ref, wu_ref, wd_ref, o_ref, h_scr, comb_scr,
                *, nt, nb, sub):
    i = pl.program_id(0)
    e = pl.program_id(1)

    def experts(rows):
        hb = h_scr[rows, :]
        comb = comb_scr[rows, :]
        lane = lax.broadcasted_iota(jnp.int32, comb.shape, 1)
        scaled = []
        for k in range(MOE_PER_GROUP):
            act = _silu(_dot(hb, wg_ref[0, k])) * _dot(hb, wu_ref[0, k])
            cw = jnp.sum(jnp.where(lane == e * MOE_PER_GROUP + k, comb, 0.0), axis=-1, keepdims=True)
            scaled.append((act * cw).astype(BF16))
        return _dot(jnp.concatenate(scaled, axis=1), wd_ref[0, 0])

    def gate_row(s):
        return mods_ref[_mod_row(i * sub + s, nt, nb)][5:6, :]

    @pl.when(e == 0)
    def _():
        w_hi, w_lo = _split_bf16(wr_ref[...])
        w_both = jnp.concatenate([w_hi, w_lo], axis=1)
        def prepare(s):
            rows = slice(s * TM, (s + 1) * TM)
            mod = mods_ref[_mod_row(i * sub + s, nt, nb)]
            h = _normed(x_ref[rows, :], g_ref[...], mod, 3, 4)
            h_hi, h_lo = _split_bf16(h)
            h_scr[rows, :] = h_hi
            both = _dot(h_hi, w_both)
            logits = both[:, :LANES] + both[:, LANES:] + _dot(h_lo, w_hi) + br_ref[...]
            comb_scr[rows, :] = _route(logits)

        for s in range(sub):
            prepare(s)
            rows = slice(s * TM, (s + 1) * TM)
            o_ref[rows, :] = x_ref[rows, :] + gate_row(s) * experts(rows)

    @pl.when(e > 0)
    def _():
        y = experts(slice(0, sub * TM))
        for s in range(sub):
            rows = slice(s * TM, (s + 1) * TM)
            o_ref[rows, :] += gate_row(s) * y[rows, :]


def _moe(x, mods, gain, w_router, b_router, w_gate, w_up, w_down, layer, nt, nb):
    t_tok, d = x.shape
    pg = MOE_PER_GROUP
    rows = MOE_ROWS if t_tok % MOE_ROWS == 0 else TM
    sub = rows // TM
    return pl.pallas_call(
        functools.partial(_moe_kernel, nt=nt, nb=nb, sub=sub),
        grid=(t_tok // rows, MOE_GROUPS),
        in_specs=[pl.BlockSpec((rows, d), lambda i, e: (i, 0)),
                  pl.BlockSpec(mods.shape, lambda i, e: (0, 0, 0)),
                  pl.BlockSpec((1, d), lambda i, e: (0, 0)),
                  pl.BlockSpec((d, LANES), lambda i, e: (0, 0)),
                  pl.BlockSpec((1, LANES), lambda i, e: (0, 0)),
                  pl.BlockSpec((1, pg, d, MOE_FF), lambda i, e: (layer, e, 0, 0)),
                  pl.BlockSpec((1, pg, d, MOE_FF), lambda i, e: (layer, e, 0, 0)),
                  pl.BlockSpec((1, 1, pg * MOE_FF, d), lambda i, e: (layer, e, 0, 0))],
        out_specs=pl.BlockSpec((rows, d), lambda i, e: (i, 0)),
        out_shape=jax.ShapeDtypeStruct((t_tok, d), F32),
        scratch_shapes=[pltpu.VMEM((rows, d), BF16), pltpu.VMEM((rows, LANES), F32)],
        input_output_aliases={0: 0},
        compiler_params=_params("parallel", "arbitrary"),
        name="moe",
    )(x, mods, gain.reshape(1, d), w_router, b_router, w_gate, w_up, w_down)


def _attn_kernel(sink_ref, q_ref, kp_ref, kc_ref, kn_ref, vp_ref, vc_ref, vn_ref, kx_ref, vx_ref, o_ref, *, n_lat):
    j = pl.program_id(1)
    jl = j - 1
    dh = ATT_HEAD_DIM
    n_loc = TM + 2 * CHUNK
    k_all = jnp.concatenate([kp_ref[...], kc_ref[...], kn_ref[...], kx_ref[...]], axis=0)
    q_t = jnp.transpose(q_ref[...].astype(F32)).astype(BF16)
    n_keys = k_all.shape[0]
    kj = lax.broadcasted_iota(jnp.int32, (n_keys, TM), 0)
    qi = lax.broadcasted_iota(jnp.int32, (n_keys, TM), 1)
    kpos = jl * TM - CHUNK + kj
    local_ok = (kj >= qi + CHUNK - WINDOW) & (kj <= qi + CHUNK + WINDOW) & (kpos >= 0) & (kpos < n_lat) & (jl >= 0)
    bias = jnp.where((kj >= n_loc) | local_ok, 0.0, -jnp.inf)
    uh = ATT_UNIT_HEADS
    bias = jnp.concatenate([bias] * uh, axis=1)
    lane_head = lax.broadcasted_iota(jnp.int32, (1, uh * TM), 1) // TM
    zeros_q = jnp.zeros((dh, uh * TM), BF16)
    units = [(h0 // ATT_GROUP, h0) for h0 in range(0, ATT_HEADS, uh)]

    def scores(u):
        g, h0 = units[u]
        half = g % 2
        k_pair = k_all[:, (g - half) * dh:(g - half + 2) * dh]
        qg = jnp.concatenate([q_t[(h0 + a) * dh:(h0 + a + 1) * dh, :] for a in range(uh)], axis=1)
        q_m = jnp.concatenate([zeros_q, qg] if half else [qg, zeros_q], axis=0)
        return _dot(k_pair, q_m) + bias

    outs = []
    s_next = scores(0)
    v_t = jnp.transpose(jnp.concatenate([vp_ref[...], vc_ref[...], vn_ref[...], vx_ref[...]],
                                        axis=0).astype(F32)).astype(BF16)
    for u, (g, h0) in enumerate(units):
        s = s_next
        if u + 1 < len(units):
            s_next = scores(u + 1)
        sink = jnp.zeros((1, uh * TM), F32)
        for a in range(uh):
            sink = jnp.where(lane_head == a, sink_ref[h0 + a] * LOG2_E, sink)
        m = jnp.maximum(jnp.max(s, axis=0, keepdims=True), sink)
        p = jnp.exp2(s - m)
        den = jnp.sum(p, axis=0, keepdims=True) + jnp.exp2(sink - m)
        og = _dot(v_t[g * dh:(g + 1) * dh, :], p.astype(BF16)) / den
        outs += [og[:, a * TM:(a + 1) * TM] for a in range(uh)]
    o_ref[...] = jnp.transpose(jnp.concatenate(outs, axis=0)).astype(o_ref.dtype)


def _windowed_attention(q, k, v, sink, nb, seg):
    t_tok = q.shape[0]
    nchunk = seg // CHUNK
    nt = seg // TM
    per_tile = TM // CHUNK
    kvw = ATT_KV_HEADS * ATT_HEAD_DIM

    def halo(off):
        return lambda b, j, *_: (b * nchunk + jnp.clip(j * per_tile + off, per_tile, nchunk - 1), 0)

    tile = lambda b, j, *_: (b * nt + j, 0)
    kv_spec = [pl.BlockSpec((CHUNK, kvw), halo(-1)), pl.BlockSpec((TM, kvw), tile),
               pl.BlockSpec((CHUNK, kvw), halo(per_tile))]
    ctx_spec = pl.BlockSpec((TM, kvw), lambda b, j, *_: (b * nt, 0))
    return pl.pallas_call(
        functools.partial(_attn_kernel, n_lat=seg - TM),
        grid_spec=pltpu.PrefetchScalarGridSpec(
            num_scalar_prefetch=1,
            grid=(nb, nt),
            in_specs=[pl.BlockSpec((TM, q.shape[1]), tile)] + kv_spec + kv_spec + [ctx_spec, ctx_spec],
            out_specs=pl.BlockSpec((TM, q.shape[1]), tile),
        ),
        out_shape=jax.ShapeDtypeStruct((t_tok, q.shape[1]), BF16),
        compiler_params=_params("parallel", "parallel"),
        name="windowed_attention",
    )(sink, q, k, k, k, v, v, v, k, v)


def _ssm_in_kernel(x_ref, xp_ref, xn_ref, mods_ref, g_ref, w_ref, cw_ref, cb_ref, dtb_ref, aneg_ref,
                   z_ref, xs_ref, bm_ref, cm_ref, dtv_ref, a_ref, *ext, nt, nb):
    t = pl.program_id(0)
    tl = t % nt
    halo = 8
    first = (tl == 0) | (tl == 1)
    last = (tl == 0) | (tl == nt - 1)
    gain = g_ref[...]

    def normed(x, tile):
        return _normed(x, gain, mods_ref[_mod_row(tile, nt, nb)], 0, 1)

    h = normed(x_ref[...], t)
    h_prev = jnp.where(first, 0.0, normed(xp_ref[...], jnp.maximum(t - 1, 0)))
    h_next = jnp.where(last, 0.0, normed(xn_ref[...], jnp.minimum(t + 1, pl.num_programs(0) - 1)))
    hb = h.astype(BF16)
    hb_ext = jnp.concatenate([h_prev, h, h_next], axis=0).astype(BF16)
    chunk = SSM_CONV_CHUNK
    pad = SSM_CONV // 2

    def project(c):
        proj = _dot(hb_ext, w_ref[:, SSM_D_INNER + c:SSM_D_INNER + c + chunk])
        ext[c // chunk][...] = proj

    project(0)
    for c in range(0, SSM_CONV_DIM, chunk):
        if c + chunk < SSM_CONV_DIM:
            project(c + chunk)
        if c < SSM_D_INNER:
            z_ref[:, c:c + chunk] = _dot(hb, w_ref[:, c:c + chunk]).astype(z_ref.dtype)
        acc = jnp.broadcast_to(cb_ref[:, c:c + chunk], (TM, chunk))
        for k in range(SSM_CONV):
            acc = acc + cw_ref[k:k + 1, c:c + chunk] * ext[c // chunk][halo - pad + k:halo - pad + k + TM, :]
        y = _silu(acc)
        if c < SSM_D_INNER:
            xs_ref[0, c:c + chunk, :] = jnp.transpose(y).astype(xs_ref.dtype)
        elif c < SSM_D_INNER + SSM_BC:
            bm_ref[:, c - SSM_D_INNER:c - SSM_D_INNER + chunk] = y.astype(bm_ref.dtype)
        else:
            off = c - SSM_D_INNER - SSM_BC
            cm_ref[0, off:off + chunk, :] = jnp.transpose(y).astype(cm_ref.dtype)
    dt = _dot(hb, w_ref[:, SSM_D_INNER + SSM_CONV_DIM:])
    dtv = _softplus(dt + dtb_ref[...])
    dtv_ref[0] = jnp.transpose(dtv)
    a_ref[0] = jnp.transpose(dtv * aneg_ref[...] * LOG2_E)


def _ssm_in(x, mods, gain, w, conv_w, conv_b, dt_bias, a_neg, nt, nb):
    t_tok, d = x.shape
    n8 = t_tok // 8
    seg = nt * TM
    row = lambda t: (t, 0)
    col = lambda t: (t // nt, 0, t % nt)
    fixed = lambda t: (0, 0)
    return pl.pallas_call(
        functools.partial(_ssm_in_kernel, nt=nt, nb=nb),
        grid=(t_tok // TM,),
        in_specs=[pl.BlockSpec((TM, d), row),
                  pl.BlockSpec((8, d), lambda t: (jnp.maximum(t * (TM // 8) - 1, 0), 0)),
                  pl.BlockSpec((8, d), lambda t: (jnp.minimum((t + 1) * (TM // 8), n8 - 1), 0)),
                  pl.BlockSpec(mods.shape, lambda t: (0, 0, 0)),
                  pl.BlockSpec((1, d), fixed),
                  pl.BlockSpec(w.shape, fixed),
                  pl.BlockSpec((8, SSM_CONV_DIM), fixed),
                  pl.BlockSpec((1, SSM_CONV_DIM), fixed),
                  pl.BlockSpec((1, LANES), fixed),
                  pl.BlockSpec((1, LANES), fixed)],
        out_specs=[pl.BlockSpec((TM, SSM_D_INNER), row),
                   pl.BlockSpec((1, SSM_D_INNER, TM), col), pl.BlockSpec((TM, SSM_BC), row),
                   pl.BlockSpec((1, SSM_BC, TM), col), pl.BlockSpec((1, LANES, TM), col),
                   pl.BlockSpec((1, LANES, TM), col)],
        out_shape=[jax.ShapeDtypeStruct((t_tok, SSM_D_INNER), BF16),
                   jax.ShapeDtypeStruct((nb, SSM_D_INNER, seg), BF16), jax.ShapeDtypeStruct((t_tok, SSM_BC), BF16),
                   jax.ShapeDtypeStruct((nb, SSM_BC, seg), BF16), jax.ShapeDtypeStruct((nb, LANES, seg), F32),
                   jax.ShapeDtypeStruct((nb, LANES, seg), F32)],
        scratch_shapes=[pltpu.VMEM((TM + 16, SSM_CONV_CHUNK), F32)] * (SSM_CONV_DIM // SSM_CONV_CHUNK),
        compiler_params=_params("parallel"),
        name="ssm_in",
    )(x, x, x, mods, gain.reshape(1, d), w, conv_w, conv_b, dt_bias, a_neg)


def _tri(lower, n):
    r = lax.broadcasted_iota(jnp.int32, (n, n), 0)
    c = lax.broadcasted_iota(jnp.int32, (n, n), 1)
    return (c <= r) if lower else (c >= r)


def _ssd_prepare(dt_ref, a_ref, d):
    a_t = a_ref[0]
    dt_t = dt_ref[0]
    upper = jnp.where(_tri(False, SSD_CHUNK), 1.0, 0.0).astype(BF16)
    a_hi, a_lo = _split_bf16(a_t)
    cum = _dot(a_hi, upper) + _dot(a_lo, upper)
    cum_end = cum[:, SSD_CHUNK - 1:SSD_CHUNK]
    if d == 0:
        lane_v, sub_v = cum, -cum
        inter = jnp.exp2(cum)
        w_upd = jnp.exp2(cum_end - cum) * dt_t
    else:
        ecum = cum - a_t
        lane_v, sub_v = -ecum, ecum
        inter = jnp.exp2(cum_end - ecum)
        w_upd = jnp.exp2(ecum) * dt_t
    sub_c = jnp.transpose(sub_v)
    decay_end = jnp.exp2(cum_end)
    return dt_t, lane_v, sub_c, inter, w_upd, decay_end


def _ssd_groups(prep, xs_ref, bm_ref, ct_ref, dsk_ref, st_ref, y_ref, d):
    dt_t, lane_v, sub_c, inter, w_upd, decay_end = prep
    mask = _tri(d == 1, SSD_CHUNK)
    p = SSM_HEAD_DIM
    for g in range(SSM_GROUPS):
        bg = bm_ref[:, g * SSM_STATE:(g + 1) * SSM_STATE]
        cg_t = ct_ref[0, g * SSM_STATE:(g + 1) * SSM_STATE, :]
        cb_t = _dot(bg, cg_t)
        state = st_ref[g]
        y_in = _dot(state.astype(BF16), cg_t)
        upd = []
        for e in range(SSM_HPG):
            h = g * SSM_HPG + e
            ln = d * SSM_HEADS + h
            rows = slice(e * p, (e + 1) * p)
            blk = SSD_DECAY_BLOCK
            block_rows = []
            for bs in range(0, SSD_CHUNK, blk):
                block_cols = []
                for bl in range(0, SSD_CHUNK, blk):
                    if (bs > bl) if d == 0 else (bs < bl):
                        block_cols.append(jnp.zeros((blk, blk), BF16))
                        continue
                    seg = sub_c[bs:bs + blk, ln:ln + 1] + lane_v[ln:ln + 1, bl:bl + blk]
                    if bs == bl:
                        seg = jnp.where(mask[bs:bs + blk, bl:bl + blk], seg, -jnp.inf)
                    block_cols.append((cb_t[bs:bs + blk, bl:bl + blk] * jnp.exp2(seg)).astype(BF16))
                block_rows.append(jnp.concatenate(block_cols, axis=1))
            m_t = jnp.concatenate(block_rows, axis=0)
            xf = xs_ref[0, h * p:(h + 1) * p, :].astype(F32)
            u = (xf * dt_t[ln:ln + 1, :]).astype(BF16)
            yh = _dot(u, m_t) + y_in[rows, :] * inter[ln:ln + 1, :] + dsk_ref[d, h] * xf
            y_ref[0, h * p:(h + 1) * p, :] = yh.astype(y_ref.dtype)
            upd.append((xf * w_upd[ln:ln + 1, :]).astype(BF16))
        new = _dot(jnp.concatenate(upd, axis=0), bg)
        for e in range(SSM_HPG):
            ln = d * SSM_HEADS + g * SSM_HPG + e
            rows = slice(e * p, (e + 1) * p)
            st_ref[g, rows, :] = state[rows, :] * decay_end[ln:ln + 1, :] + new[rows, :]


def _ssd_kernel(dsk_ref, xsf, bmf, ctf, dtf, af, xsb, bmb, ctb, dtb, ab, yf_ref, yb_ref, stf, stb):
    @pl.when(pl.program_id(1) == 0)
    def _():
        stf[...] = jnp.zeros_like(stf)
        stb[...] = jnp.zeros_like(stb)

    prep_f = _ssd_prepare(dtf, af, 0)
    prep_b = _ssd_prepare(dtb, ab, 1)
    _ssd_groups(prep_f, xsf, bmf, ctf, dsk_ref, stf, yf_ref, 0)
    _ssd_groups(prep_b, xsb, bmb, ctb, dsk_ref, stb, yb_ref, 1)


def _scan_maps(nchunk, chunk):
    ctx_chunks = TM // chunk
    fwd = lambda b, t: (b * nchunk + t, 0)
    bwd = lambda b, t: (b * nchunk + jnp.where(t < ctx_chunks, ctx_chunks - 1 - t, nchunk + ctx_chunks - 1 - t), 0)
    return fwd, bwd


def _ssd_scan(xs_t, bm, c_t, dt_t, a_t, d_skip, nb, seg):
    nchunk = seg // SSD_CHUNK
    fwd_rows, bwd_rows = _scan_maps(nchunk, SSD_CHUNK)

    def specs(rows_map):
        cols_map = lambda b, t, *_: (b, 0, rows_map(b, t)[0] - b * nchunk)
        return [pl.BlockSpec((1, SSM_D_INNER, SSD_CHUNK), cols_map),
                pl.BlockSpec((SSD_CHUNK, SSM_BC), lambda b, t, *_: rows_map(b, t)),
                pl.BlockSpec((1, SSM_BC, SSD_CHUNK), cols_map),
                pl.BlockSpec((1, LANES, SSD_CHUNK), cols_map),
                pl.BlockSpec((1, LANES, SSD_CHUNK), cols_map)]

    arrs = (xs_t, bm, c_t, dt_t, a_t)
    return pl.pallas_call(
        _ssd_kernel,
        grid_spec=pltpu.PrefetchScalarGridSpec(
            num_scalar_prefetch=1,
            grid=(nb, nchunk),
            in_specs=specs(fwd_rows) + specs(bwd_rows),
            out_specs=[specs(fwd_rows)[0], specs(bwd_rows)[0]],
            scratch_shapes=[pltpu.VMEM((SSM_GROUPS, SSM_HPG * SSM_HEAD_DIM, SSM_STATE), F32)] * 2,
        ),
        out_shape=[jax.ShapeDtypeStruct(xs_t.shape, BF16)] * 2,
        compiler_params=_params("parallel", "arbitrary"),
        name="ssd_scan",
    )(d_skip, *arrs, *arrs)


def _log_sigmoid(x):
    return jnp.minimum(x, 0.0) - jnp.log1p(jnp.exp(-jnp.abs(x)))


ML_STATE_ROWS = ML_V_DIM + 16


def _mlstm_prepare(qk_ref, v_ref, g_ref, gb_ref, m_st, d):
    nh, dk, dv = ML_HEADS, ML_QK_DIM, ML_V_DIM
    g_t = jnp.transpose(g_ref[...] + gb_ref[...])
    ig = g_t[16 * d:16 * d + nh, :]
    lf = _log_sigmoid(g_t[16 * d + nh:16 * d + 2 * nh, :])
    upper = jnp.where(_tri(False, ML_CHUNK), 1.0, 0.0).astype(BF16)
    lf_hi, lf_lo = _split_bf16(lf)
    fc = _dot(lf_hi, upper) + _dot(lf_lo, upper)
    tot = fc[:, ML_CHUNK - 1:ML_CHUNK]
    m_prev = m_st[...]
    if d == 0:
        lane_v, sub_v = fc, ig - fc
        inter = fc + m_prev
        logw = tot - fc + ig
    else:
        ec = fc - lf
        lane_v, sub_v = -ec, ec + ig
        inter = tot - ec + m_prev
        logw = ec + ig
    sub_c = jnp.transpose(jnp.concatenate([sub_v, jnp.zeros((LANES - nh, ML_CHUNK), F32)], axis=0))
    m_new = jnp.maximum(tot + m_prev, jnp.max(logw, axis=-1, keepdims=True))
    ws = jnp.exp(logw - m_new)
    cw = jnp.exp(tot + m_prev - m_new)
    q_t = jnp.transpose(qk_ref[:, :nh * dk].astype(F32))
    v_t = jnp.transpose(v_ref[...].astype(F32))
    return lane_v, sub_c, inter, ws, cw, m_new, q_t, v_t


def _mlstm_heads(prep, qk_ref, c_st, m_st, h_ref, d):
    nh, dk, dv = ML_HEADS, ML_QK_DIM, ML_V_DIM
    lane_v, sub_c, inter, ws, cw, m_new, q_t, v_t = prep
    mask = _tri(d == 1, ML_CHUNK)
    lane = lax.broadcasted_iota(jnp.int32, (1, LANES), 1)
    zeros_q = jnp.zeros((dk, ML_CHUNK), F32)
    for h in range(nh):
        half = h % 2
        k_pair = qk_ref[:, nh * dk + (h - half) * dk:nh * dk + (h - half + 2) * dk] * (dk ** -0.5)
        qh = q_t[h * dk:(h + 1) * dk, :]
        q_m = jnp.concatenate([zeros_q, qh] if half else [qh, zeros_q], axis=0).astype(BF16)
        logd = jnp.where(mask, sub_c[:, h:h + 1] + lane_v[h:h + 1, :], -jnp.inf)
        mt = jnp.maximum(inter[h:h + 1, :], jnp.max(logd, axis=0, keepdims=True))
        sc = _dot(k_pair, q_m) * jnp.exp(logd - mt)
        vh = v_t[h * dv:(h + 1) * dv, :]
        state = c_st[h]
        cq = _dot(state.astype(BF16), q_m)
        w_int = jnp.exp(inter[h:h + 1, :] - mt)
        num = _dot(vh.astype(BF16), sc.astype(BF16)) + w_int * cq[:dv, :]
        den = jnp.sum(sc, axis=0, keepdims=True) + w_int * cq[dv:dv + 1, :]
        h_ref[0, h * dv:(h + 1) * dv, :] = (num / jnp.maximum(jnp.abs(den), jnp.exp(-mt))).astype(h_ref.dtype)
        ws_h = ws[h:h + 1, :]
        lhs = jnp.concatenate([vh * ws_h, jnp.broadcast_to(ws_h, (ML_STATE_ROWS - dv, ML_CHUNK))], axis=0)
        own = (lane >= half * dk) & (lane < (half + 1) * dk)
        row = lax.broadcasted_iota(jnp.int32, (ML_STATE_ROWS, 1), 0)
        new = cw[h:h + 1, :LANES] * state + _dot(lhs.astype(BF16), k_pair)
        c_st[h] = jnp.where(own & (row <= dv), new, 0.0)
    m_st[...] = jnp.broadcast_to(m_new, m_st.shape)


def _mlstm_kernel(qkf, vf, gf, qkb, vb, gb, gbias, hf_ref, hb_ref, cf, mf, cb, mb):
    @pl.when(pl.program_id(1) == 0)
    def _():
        for r in (cf, mf, cb, mb):
            r[...] = jnp.zeros_like(r)

    prep_f = _mlstm_prepare(qkf, vf, gf, gbias, mf, 0)
    prep_b = _mlstm_prepare(qkb, vb, gb, gbias, mb, 1)
    _mlstm_heads(prep_f, qkf, cf, mf, hf_ref, 0)
    _mlstm_heads(prep_b, qkb, cb, mb, hb_ref, 1)


def _mlstm_scan(qk, v, g, gate_b, nb, seg):
    t_tok = qk.shape[0]
    nchunk = seg // ML_CHUNK
    fwd, bwd = _scan_maps(nchunk, ML_CHUNK)
    widths = (qk.shape[1], v.shape[1], LANES)
    in_specs = ([pl.BlockSpec((ML_CHUNK, w), fwd) for w in widths] + [pl.BlockSpec((ML_CHUNK, w), bwd) for w in widths]
                + [pl.BlockSpec((1, LANES), lambda b, t: (0, 0))])
    state = [pltpu.VMEM((ML_HEADS, ML_STATE_ROWS, LANES), F32), pltpu.VMEM((ML_HEADS, ML_CHUNK), F32)]
    out_map = lambda rows_map: (lambda b, t: (b, 0, rows_map(b, t)[0] - b * nchunk))
    return pl.pallas_call(
        _mlstm_kernel,
        grid=(nb, nchunk),
        in_specs=in_specs,
        out_specs=[pl.BlockSpec((1, v.shape[1], ML_CHUNK), out_map(fwd)),
                   pl.BlockSpec((1, v.shape[1], ML_CHUNK), out_map(bwd))],
        out_shape=[jax.ShapeDtypeStruct((nb, v.shape[1], seg), BF16)] * 2,
        scratch_shapes=state + state,
        compiler_params=_params("parallel", "arbitrary"),
        name="mlstm_scan",
    )(qk, v, g, qk, v, g, gate_b)


def _rms_rows(x, g_row):
    return x * lax.rsqrt(jnp.mean(x * x, axis=-1, keepdims=True) + EPS) * g_row


MLA_SCORE_SCALE = (MLA_NOPE + MLA_ROPE) ** -0.5 * LOG2_E


def _mla_queries(cq, g_ref, wt_ref, cos_ref, sin_ref, qt_ref):
    cq_t = jnp.transpose(_rms_rows(cq, g_ref[...])).astype(BF16)
    q_t = _dot(wt_ref[...], cq_t) * MLA_SCORE_SCALE
    qt_ref[0] = q_t.astype(qt_ref.dtype)
    cos, sin = cos_ref[...], sin_ref[...]
    half = MLA_ROPE // 2
    for h in range(MLA_HEADS):
        r0 = h * LANES + MLA_NOPE
        x1, x2 = q_t[r0:r0 + half, :], q_t[r0 + half:r0 + MLA_ROPE, :]
        qt_ref[0, r0:r0 + half, :] = (x1 * cos - x2 * sin).astype(qt_ref.dtype)
        qt_ref[0, r0 + half:r0 + MLA_ROPE, :] = (x2 * cos + x1 * sin).astype(qt_ref.dtype)


def _mla_keys_values(ckv, kr, g_ref, wk_ref, wvt_ref, cos_ref, sin_ref, k_ref, vt_ref):
    cn = _rms_rows(ckv, g_ref[...])
    vt_ref[0] = _dot(wvt_ref[...], jnp.transpose(cn).astype(BF16)).astype(vt_ref.dtype)
    kn = _dot(cn.astype(BF16), wk_ref[...])
    kr = pltpu.roll(kr, MLA_NOPE, axis=1)
    half = MLA_ROPE // 2
    lane = lax.broadcasted_iota(jnp.int32, (1, LANES), 1)
    partner = jnp.where(lane < MLA_NOPE + half, pltpu.roll(kr, LANES - half, axis=1), pltpu.roll(kr, half, axis=1))
    roped = kr * cos_ref[...] + partner * sin_ref[...]
    for h in range(MLA_HEADS):
        k_ref[:, h * LANES:(h + 1) * LANES] = (kn[:, h * LANES:(h + 1) * LANES] + roped).astype(k_ref.dtype)


def _mla_in_kernel(x_ref, mods_ref, g_ref, w_ref, gq_ref, wqt_ref, gkv_ref, wk_ref, wvt_ref,
                   cos_t_ref, sin_t_ref, cos_ref, sin_ref, qt_ref, k_ref, vt_ref, *, nt, nb):
    mod = mods_ref[_mod_row(pl.program_id(0), nt, nb)]
    hb = _normed(x_ref[...], g_ref[...], mod, 0, 1).astype(BF16)
    p = _dot(hb, w_ref[...])
    _mla_queries(p[:, :MLA_RANK], gq_ref, wqt_ref, cos_t_ref, sin_t_ref, qt_ref)
    _mla_keys_values(p[:, MLA_RANK:2 * MLA_RANK], p[:, 2 * MLA_RANK:], gkv_ref, wk_ref, wvt_ref,
                     cos_ref, sin_ref, k_ref, vt_ref)


def _mla_in(x, mods, gain, w_in, gain_q, wq_t, gain_kv, w_k, wv_t, tables_t, tables, nt, nb):
    t_tok, d = x.shape
    seg = nt * TM
    half = MLA_ROPE // 2
    fixed = lambda t: (0, 0)
    return pl.pallas_call(
        functools.partial(_mla_in_kernel, nt=nt, nb=nb),
        grid=(t_tok // TM,),
        in_specs=[pl.BlockSpec((TM, d), lambda t: (t, 0)),
                  pl.BlockSpec(mods.shape, lambda t: (0, 0, 0)),
                  pl.BlockSpec((1, d), fixed),
                  pl.BlockSpec(w_in.shape, fixed),
                  pl.BlockSpec((1, MLA_RANK), fixed),
                  pl.BlockSpec(wq_t.shape, fixed),
                  pl.BlockSpec((1, MLA_RANK), fixed),
                  pl.BlockSpec(w_k.shape, fixed),
                  pl.BlockSpec(wv_t.shape, fixed),
                  pl.BlockSpec((half, TM), lambda t: (0, t % nt)),
                  pl.BlockSpec((half, TM), lambda t: (0, t % nt)),
                  pl.BlockSpec((TM, LANES), lambda t: (t % nt, 0)),
                  pl.BlockSpec((TM, LANES), lambda t: (t % nt, 0))],
        out_specs=[pl.BlockSpec((1, MLA_HEADS * LANES, TM), lambda t: (t, 0, 0)),
                   pl.BlockSpec((TM, MLA_HEADS * LANES), lambda t: (t, 0)),
                   pl.BlockSpec((1, MLA_HEADS * MLA_V, TM), lambda t: (t // nt, 0, t % nt))],
        out_shape=[jax.ShapeDtypeStruct((t_tok // TM, MLA_HEADS * LANES, TM), BF16),
                   jax.ShapeDtypeStruct((t_tok, MLA_HEADS * LANES), BF16),
                   jax.ShapeDtypeStruct((nb, MLA_HEADS * MLA_V, seg), BF16)],
        compiler_params=_params("parallel"),
        name="mla_in",
    )(x, mods, gain.reshape(1, d), w_in, gain_q.reshape(1, MLA_RANK), wq_t, gain_kv.reshape(1, MLA_RANK), w_k, wv_t,
      *tables_t, *tables)


def _mla_attn_kernel(qt_ref, k_ref, vt_ref, o_ref):
    @pl.when(pl.program_id(2) == 0)
    def _():
        o_ref[...] = jnp.zeros_like(o_ref)

    @pl.when(pl.program_id(2) > 0)
    def _():
        heads = range(MLA_HEADS_PER_STEP)
        seg = k_ref.shape[0]
        tk = MLA_KEY_TILE if (seg - TM) % MLA_KEY_TILE == 0 else TM
        tiles = [(0, TM)] + [(a, a + tk) for a in range(TM, seg, tk)]

        def scores(hh, j):
            return _dot(k_ref[tiles[j][0]:tiles[j][1], hh * LANES:(hh + 1) * LANES],
                        qt_ref[0, hh * LANES:(hh + 1) * LANES, :])

        m = [jnp.full((1, TM), -jnp.inf, F32) for _ in heads]
        l = [jnp.zeros((1, TM), F32) for _ in heads]
        acc = [jnp.zeros((MLA_V, TM), F32) for _ in heads]
        ahead = min(MLA_LOOKAHEAD, len(tiles))
        pending = [[scores(hh, j) for hh in heads] for j in range(ahead)]
        for j in range(len(tiles)):
            s_cur = pending.pop(0)
            if j + ahead < len(tiles):
                pending.append([scores(hh, j + ahead) for hh in heads])
            for hh in heads:
                m_new = jnp.maximum(m[hh], jnp.max(s_cur[hh], axis=0, keepdims=True))
                alpha = jnp.exp2(m[hh] - m_new)
                p = jnp.exp2(s_cur[hh] - m_new)
                l[hh] = alpha * l[hh] + jnp.sum(p, axis=0, keepdims=True)
                v_t = vt_ref[0, hh * MLA_V:(hh + 1) * MLA_V, tiles[j][0]:tiles[j][1]]
                acc[hh] = alpha * acc[hh] + _dot(v_t, p.astype(BF16))
                m[hh] = m_new
        o_t = jnp.concatenate([acc[hh] / l[hh] for hh in heads], axis=0)
        o_ref[...] = jnp.transpose(o_t).astype(o_ref.dtype)


def _mla_attention(q_t, k, v_t, nb, seg):
    t_tok = k.shape[0]
    nt = seg // TM
    hps = MLA_HEADS_PER_STEP
    return pl.pallas_call(
        _mla_attn_kernel,
        grid=(nb, MLA_HEADS // hps, nt),
        in_specs=[pl.BlockSpec((1, hps * LANES, TM), lambda b, hp, j: (b * nt + j, hp, 0)),
                  pl.BlockSpec((seg, hps * LANES), lambda b, hp, j: (b, hp)),
                  pl.BlockSpec((1, hps * MLA_V, seg), lambda b, hp, j: (b, hp, 0))],
        out_specs=pl.BlockSpec((TM, hps * MLA_V), lambda b, hp, j: (b * nt + j, hp)),
        out_shape=jax.ShapeDtypeStruct((t_tok, MLA_HEADS * MLA_V), BF16),
        compiler_params=_params("parallel", "parallel", "arbitrary"),
        name="mla_attention",
    )(q_t, k, v_t)


def _final_kernel(x_ref, g_ref, o_ref):
    o_ref[0] = _rms_rows(x_ref[...], g_ref[...])


def _final_norm(x, gain, nb, seg):
    d = x.shape[1]
    nt = seg // TM
    return pl.pallas_call(
        _final_kernel,
        grid=(nb, nt - 1),
        in_specs=[pl.BlockSpec((TM, d), lambda b, j: (b * nt + 1 + j, 0)),
                  pl.BlockSpec((1, d), lambda b, j: (0, 0))],
        out_specs=pl.BlockSpec((1, TM, d), lambda b, j: (b, j, 0)),
        out_shape=jax.ShapeDtypeStruct((nb, seg - TM, d), F32),
        compiler_params=_params("parallel", "parallel"),
        name="final_norm",
    )(x, gain.reshape(1, d))


def _rope_angles(n_lat, rot_dim):
    rows = n_lat // GRID_W
    row = jnp.repeat(jnp.arange(rows), GRID_W).astype(F32)
    col = jnp.tile(jnp.arange(GRID_W), rows).astype(F32)
    quarter = rot_dim // 4
    inv = ROPE_BASE ** (-jnp.arange(quarter, dtype=F32) / quarter)
    ang = jnp.concatenate([row[:, None] * inv, col[:, None] * inv], axis=-1)
    return jnp.cos(ang), jnp.sin(ang)


def _with_ctx_rows(tab, fill):
    return jnp.concatenate([jnp.full((TM, tab.shape[1]), fill, F32), tab], axis=0)


def _attn_rope_tables(n_lat):
    cos, sin = _rope_angles(n_lat, ATT_HEAD_DIM)
    cos_h = jnp.concatenate([cos, cos], axis=1)
    sin_h = jnp.concatenate([-sin, sin], axis=1)
    reps = LANES // ATT_HEAD_DIM
    return (_with_ctx_rows(jnp.tile(cos_h, (1, reps)), 1.0), _with_ctx_rows(jnp.tile(sin_h, (1, reps)), 0.0))


def _mla_k_tables(n_lat):
    cos, sin = _rope_angles(n_lat, MLA_ROPE)
    lo = jnp.zeros((n_lat, MLA_NOPE), F32)
    hi = jnp.zeros((n_lat, LANES - MLA_NOPE - MLA_ROPE), F32)
    cos_c = jnp.concatenate([lo, cos, cos, hi], axis=1)
    sin_c = jnp.concatenate([lo, -sin, sin, hi], axis=1)
    ctx_cos = jnp.concatenate([lo[:TM], jnp.ones((TM, MLA_ROPE), F32), hi[:TM]], axis=1)
    return (jnp.concatenate([ctx_cos, cos_c], axis=0), _with_ctx_rows(sin_c, 0.0))


def _pad_heads(w, real):
    r = w.shape[0]
    w3 = w.reshape(r, MLA_HEADS, real)
    return jnp.pad(w3, ((0, 0), (0, 0), (0, LANES - real))).reshape(r, MLA_HEADS * LANES)


def kernel(x, c, ctx, c_ctx, norm1_g, norm2_g, w_mod, b_mod, moe_w_group, moe_b_group, moe_w_expert, moe_b_expert, moe_w_gate, moe_w_up, moe_w_down, attn_w_in, attn_sink, attn_w_out, ssm_w_in, ssm_conv_w, ssm_conv_b, ssm_dt_bias, ssm_a_log, ssm_d, ssm_norm_g, ssm_w_out, mlstm_w_in, mlstm_gate_b, mlstm_norm_g, mlstm_w_out, mla_w_in, mla_q_norm_g, mla_w_q_up, mla_kv_norm_g, mla_w_kv_up, mla_w_out, final_norm_g):
    nb, n_lat, d = x.shape
    assert ctx.shape[1] == TM and d == D_MODEL and n_lat % TM == 0
    depth = w_mod.shape[0]
    seg = TM + n_lat
    nt = seg // TM
    t_tok = nb * seg

    xs = (x, ctx)

    rows = -(-(nb + 1) // 8) * 8
    cvec = jnp.concatenate([c, c_ctx[None, :], jnp.zeros((rows - nb - 1, d), F32)], axis=0)
    mods = _modulation(cvec, w_mod, b_mod).reshape(depth, rows, ADALN_CHUNKS, d)
    mods = jnp.pad(mods, ((0, 0), (0, 0), (0, MOD_ROWS - ADALN_CHUNKS), (0, 0)))

    w_router = jnp.concatenate([moe_w_expert, moe_w_group,
                                jnp.zeros((depth, d, LANES - MOE_EXPERTS - MOE_GROUPS), F32)], axis=-1)
    b_router = jnp.concatenate([moe_b_expert, moe_b_group,
                                jnp.zeros((depth, LANES - MOE_EXPERTS - MOE_GROUPS), F32)], axis=-1)
    w_gate, w_up = moe_w_gate.astype(BF16), moe_w_up.astype(BF16)
    w_down = moe_w_down.astype(BF16).reshape(depth, MOE_GROUPS, MOE_PER_GROUP * MOE_FF, d)

    for i in range(depth):
        kind = i % 4
        mod_i = mods[i]
        if kind == 0:
            nq, nk = ATT_HEADS * ATT_HEAD_DIM, ATT_KV_HEADS * ATT_HEAD_DIM
            half = ATT_HEAD_DIM // 2
            q, k, v = _normproj(xs, mod_i, norm1_g[i], attn_w_in[i // 4].astype(BF16),
                                [(0, nq, half, ATT_HEAD_DIM ** -0.5 * LOG2_E), (nq, nk, half), (nq + nk, nk, None)],
                                [BF16, BF16, BF16], nt, nb, tables=_attn_rope_tables(n_lat))
            o = _windowed_attention(q, k, v, attn_sink[i // 4], nb, seg)
            xs = _outproj([o], [nq], xs, mod_i, attn_w_out[i // 4].astype(BF16), _plain_prologue, nt, nb)
        elif kind == 1:
            j = i // 4
            w_in = jnp.pad(ssm_w_in[j], ((0, 0), (0, LANES - 2 * SSM_HEADS))).astype(BF16)
            lane_pad = LANES - 2 * SSM_HEADS
            dt_bias = jnp.pad(ssm_dt_bias[j].reshape(1, -1), ((0, 0), (0, lane_pad)))
            a_neg = jnp.pad(-jnp.exp(ssm_a_log[j].astype(F32)).reshape(1, -1), ((0, 0), (0, lane_pad)))
            conv_w = jnp.pad(ssm_conv_w[j], ((0, 8 - SSM_CONV), (0, 0)))
            z, xc, bm, cm, dtv, a = _ssm_in(xs, mod_i, norm1_g[i], w_in, conv_w, ssm_conv_b[j].reshape(1, -1),
                                            dt_bias, a_neg, nt, nb)
            yf, yb = _ssd_scan(xc, bm, cm, dtv, a, ssm_d[j].astype(F32), nb, seg)
            xs = _outproj([z], [SSM_D_INNER], xs, mod_i, ssm_w_out[j].astype(BF16), _ssm_prologue,
                          nt, nb, extra=(ssm_norm_g[j].reshape(1, -1),), ins_t=(yf, yb))
        elif kind == 2:
            j = i // 4
            nqk, nv = 2 * ML_HEADS * ML_QK_DIM, ML_HEADS * ML_V_DIM
            w_in = jnp.pad(mlstm_w_in[j], ((0, 0), (0, LANES - 4 * ML_HEADS))).astype(BF16)
            qk, v, o, g = _normproj(xs, mod_i, norm1_g[i], w_in,
                                    [(0, nqk, None), (nqk, nv, None), (nqk + nv, nv, None), (nqk + 2 * nv, LANES, None)],
                                    [BF16, BF16, BF16, F32], nt, nb)
            gate_b = jnp.pad(mlstm_gate_b[j].reshape(1, -1), ((0, 0), (0, LANES - 4 * ML_HEADS)))
            hf, hb = _mlstm_scan(qk, v, g, gate_b, nb, seg)
            xs = _outproj([o], [nv], xs, mod_i, mlstm_w_out[j].astype(BF16), _mlstm_prologue,
                          nt, nb, extra=(mlstm_norm_g[j].reshape(1, -1),), ins_t=(hf, hb))
        else:
            j = i // 4
            w_in = jnp.pad(mla_w_in[j], ((0, 0), (0, LANES - MLA_ROPE))).astype(BF16)
            cos, sin = _rope_angles(n_lat, MLA_ROPE)
            tables_t = (jnp.transpose(_with_ctx_rows(cos, 1.0)), jnp.transpose(_with_ctx_rows(sin, 0.0)))
            wq_t = jnp.transpose(_pad_heads(mla_w_q_up[j], MLA_NOPE + MLA_ROPE)).astype(BF16)
            w_kv = mla_w_kv_up[j].reshape(MLA_RANK, MLA_HEADS, MLA_NOPE + MLA_V)
            w_k = _pad_heads(w_kv[:, :, :MLA_NOPE].reshape(MLA_RANK, MLA_HEADS * MLA_NOPE), MLA_NOPE).astype(BF16)
            wv_t = jnp.transpose(w_kv[:, :, MLA_NOPE:].reshape(MLA_RANK, MLA_HEADS * MLA_V)).astype(BF16)
            q_t, k, v_t = _mla_in(xs, mod_i, norm1_g[i], w_in, mla_q_norm_g[j], wq_t, mla_kv_norm_g[j], w_k, wv_t,
                                  tables_t, _mla_k_tables(n_lat), nt, nb)
            o = _mla_attention(q_t, k, v_t, nb, seg)
            xs = _outproj([o], [MLA_HEADS * MLA_V], xs, mod_i, mla_w_out[j].astype(BF16), _plain_prologue, nt, nb)

        xs = _moe(xs, mod_i, norm2_g[i], w_router[i], b_router[i].reshape(1, -1), w_gate, w_up, w_down, i, nt, nb)

    return _final_norm(xs, final_norm_g, nb, seg)
```

```python
import functools
import math

import jax
import jax.numpy as jnp
from jax import lax
from jax.experimental import pallas as pl
from jax.experimental.pallas import tpu as pltpu

F32 = jnp.float32
BF16 = jnp.bfloat16

D_MODEL = 1024
GRID_W = 64
EPS = 1e-6
ROPE_BASE = 10000.0
ADALN_CHUNKS = 6
CHUNK = 128
SSD_CHUNK = 256
ML_CHUNK = 128
SSD_DECAY_BLOCK = 128
TM = 256
MOD_ROWS = 8
LANES = 128
V7X_VMEM_LIMIT = 48 * 1024 * 1024

ATT_HEADS, ATT_KV_HEADS, ATT_HEAD_DIM, WINDOW = 16, 4, 64, 128
ATT_GROUP = ATT_HEADS // ATT_KV_HEADS
ATT_UNIT_HEADS = 4
SSM_D_INNER, SSM_HEAD_DIM, SSM_HEADS, SSM_GROUPS, SSM_STATE, SSM_CONV = 2048, 64, 32, 4, 128, 5
SSM_HPG = SSM_HEADS // SSM_GROUPS
SSM_BC = SSM_GROUPS * SSM_STATE
SSM_CONV_DIM = SSM_D_INNER + 2 * SSM_BC
SSM_CONV_CHUNK = 256
ML_HEADS, ML_QK_DIM, ML_V_DIM = 8, 64, 128
MLA_HEADS, MLA_RANK, MLA_NOPE, MLA_ROPE, MLA_V = 16, 256, 64, 32, 64
MOE_GROUPS, MOE_PER_GROUP, MOE_EXPERTS, MOE_FF = 4, 4, 16, 256
MOE_ROWS = 1024
MLA_KEY_TILE = 1024
MLA_HEADS_PER_STEP = 4
MLA_LOOKAHEAD = 1
LOG2_E = 1.4426950408889634


def _dot(a, b):
    return jnp.dot(a, b, preferred_element_type=F32)


def _dot_nt(a, b):
    return lax.dot_general(a, b, (((1,), (1,)), ((), ())), preferred_element_type=F32)


def _split_bf16(x):
    hi = x.astype(BF16)
    lo = (x - hi.astype(F32)).astype(BF16)
    return hi, lo


def _dot_split(a, b):
    a_hi, a_lo = _split_bf16(a)
    b_hi, b_lo = _split_bf16(b)
    return _dot(a_hi, b_hi) + _dot(a_lo, b_hi) + _dot(a_hi, b_lo)


def _sigmoid(x):
    return 1.0 / (1.0 + jnp.exp(-x))


def _silu(x):
    return x * _sigmoid(x)


def _softplus(x):
    return jnp.maximum(x, 0.0) + jnp.log1p(jnp.exp(-jnp.abs(x)))


def _params(*sem):
    return pltpu.CompilerParams(dimension_semantics=sem, vmem_limit_bytes=V7X_VMEM_LIMIT)


def _mod_row(t, nt, nb):
    return jnp.where(t % nt == 0, nb, t // nt)


def _mod_kernel(c_ref, w_ref, b_ref, o_ref):
    o_ref[0] = _dot_split(_silu(c_ref[...]), w_ref[0]) + b_ref[0]


def _modulation(cvec, w_mod, b_mod):
    depth, d, n = w_mod.shape
    tn = 1536
    rows = cvec.shape[0]
    return pl.pallas_call(
        _mod_kernel,
        grid=(depth, n // tn),
        in_specs=[pl.BlockSpec((rows, d), lambda l, j: (0, 0)),
                  pl.BlockSpec((1, d, tn), lambda l, j: (l, 0, j)),
                  pl.BlockSpec((1, 1, tn), lambda l, j: (l, 0, j))],
        out_specs=pl.BlockSpec((1, rows, tn), lambda l, j: (l, 0, j)),
        out_shape=jax.ShapeDtypeStruct((depth, rows, n), F32),
        compiler_params=_params("arbitrary", "arbitrary"),
        name="modulation",
    )(cvec, w_mod, b_mod.reshape(depth, 1, n))


def _normed(x, g_row, mod, sh_row, sc_row):
    y = x * lax.rsqrt(jnp.mean(x * x, axis=-1, keepdims=True) + EPS) * g_row
    return y * (1.0 + mod[sc_row:sc_row + 1, :]) + mod[sh_row:sh_row + 1, :]


def _proj_columns(hb, w_ref, o_ref, start, width, rope_half, cos, sin, scale=None, chunk=512):
    for c in range(0, width, chunk):
        cw = min(chunk, width - c)
        acc = _dot(hb, w_ref[:, start + c:start + c + cw])
        if rope_half is not None:
            lane = lax.broadcasted_iota(jnp.int32, (1, cw), 1)
            rot = jnp.where(lane % (2 * rope_half) < rope_half, pltpu.roll(acc, cw - rope_half, axis=1),
                            pltpu.roll(acc, rope_half, axis=1))
            reps = cw // LANES
            acc = acc * jnp.tile(cos, (1, reps)) + rot * jnp.tile(sin, (1, reps))
        if scale is not None:
            acc = acc * scale
        o_ref[:, c:c + cw] = acc.astype(o_ref.dtype)


def _stream_specs(x, nt, sub):
    if isinstance(x, tuple):
        d = x[0].shape[-1]
        specs = []
        for s in range(sub):
            tile = lambda t, s=s: t * sub + s
            specs += [pl.BlockSpec((1, TM, d), lambda t, tile=tile: (tile(t) // nt, jnp.maximum(tile(t) % nt - 1, 0), 0)),
                      pl.BlockSpec((1, TM, d), lambda t, tile=tile: (tile(t) // nt, 0, 0))]
        return specs, list(x) * sub
    return [pl.BlockSpec((sub * TM, x.shape[1]), lambda t: (t, 0))], [x]


def _tiles_per_step(nt, nb, per_tile_tables=False):
    return 2 if (nt * nb) % 2 == 0 and not per_tile_tables else 1


def _stream_tile(x_refs, nt, s, sub):
    if len(x_refs) == 1:
        return x_refs[0][s * TM:(s + 1) * TM, :]
    is_ctx = (pl.program_id(0) * sub + s) % nt == 0
    return jnp.where(is_ctx, x_refs[2 * s + 1][0], x_refs[2 * s][0])


def _normproj_kernel(*refs, outs, has_rope, n_x, nt, nb, sub):
    x_refs, refs = refs[:n_x], refs[n_x:]
    mods_ref, g_ref, w_ref = refs[:3]
    k = 3
    cos = sin = None
    if has_rope:
        cos, sin = refs[3][...], refs[4][...]
        k = 5
    hs = []
    for s in range(sub):
        mod = mods_ref[_mod_row(pl.program_id(0) * sub + s, nt, nb)]
        hs.append(_normed(_stream_tile(x_refs, nt, s, sub), g_ref[...], mod, 0, 1).astype(BF16))
    hb = hs[0] if sub == 1 else jnp.concatenate(hs, axis=0)
    for o_ref, spec in zip(refs[k:], outs):
        _proj_columns(hb, w_ref, o_ref, *spec[:3], cos, sin, scale=spec[3] if len(spec) > 3 else None)


def _normproj(x, mods, gain, w, outs, out_dtypes, nt, nb, tables=None):
    d, n = w.shape
    t_tok = nb * nt * TM
    sub = _tiles_per_step(nt, nb, per_tile_tables=tables is not None)
    rows = sub * TM
    x_specs, x_args = _stream_specs(x, nt, sub)
    in_specs = x_specs + [pl.BlockSpec(mods.shape, lambda t: (0, 0, 0)),
                          pl.BlockSpec((1, d), lambda t: (0, 0)),
                          pl.BlockSpec((d, n), lambda t: (0, 0))]
    args = x_args + [mods, gain.reshape(1, d), w]
    if tables is not None:
        in_specs += [pl.BlockSpec((TM, LANES), lambda t: (t % nt, 0))] * 2
        args += list(tables)
    return pl.pallas_call(
        functools.partial(_normproj_kernel, outs=tuple(outs), has_rope=tables is not None, n_x=len(x_args),
                          nt=nt, nb=nb, sub=sub),
        grid=(t_tok // rows,),
        in_specs=in_specs,
        out_specs=[pl.BlockSpec((rows, o[1]), lambda t: (t, 0)) for o in outs],
        out_shape=[jax.ShapeDtypeStruct((t_tok, o[1]), dt) for o, dt in zip(outs, out_dtypes)],
        compiler_params=_params("parallel"),
        name="normproj",
    )(*args)


def _outproj_kernel(*refs, prologue, n_t, n_tok, n_extra, n_x, nt, nb, sub):
    ins_t = refs[:n_t * sub]
    ins = refs[n_t * sub:n_t * sub + n_tok]
    extra = refs[n_t * sub + n_tok:n_t * sub + n_tok + n_extra]
    x_refs = refs[n_t * sub + n_tok + n_extra:n_t * sub + n_tok + n_extra + n_x]
    mods_ref, w_ref, o_ref = refs[n_t * sub + n_tok + n_extra + n_x:]
    parts = []
    for s in range(sub):
        tok = [r if sub == 1 else r.at[pl.ds(s * TM, TM), :] for r in ins]
        parts.append(prologue(*ins_t[s * n_t:(s + 1) * n_t], *tok, *extra))
    a = parts[0] if sub == 1 else jnp.concatenate(parts, axis=0)
    y = _dot(a, w_ref[...])
    for s in range(sub):
        rows = slice(s * TM, (s + 1) * TM)
        gate = mods_ref[_mod_row(pl.program_id(0) * sub + s, nt, nb)][2:3, :]
        o_ref[rows, :] = _stream_tile(x_refs, nt, s, sub) + gate * y[rows, :]


def _outproj(ins, in_widths, x, mods, w, prologue, nt, nb, extra=(), ins_t=()):
    d = w.shape[1]
    t_tok = nb * nt * TM
    sub = _tiles_per_step(nt, nb)
    rows = sub * TM

    def tile_map(s):
        return lambda t: ((t * sub + s) // nt, 0, (t * sub + s) % nt)

    in_specs, args = [], []
    for s in range(sub):
        in_specs += [pl.BlockSpec((1, a.shape[1], TM), tile_map(s)) for a in ins_t]
        args += list(ins_t)
    in_specs += [pl.BlockSpec((rows, wd), lambda t: (t, 0)) for wd in in_widths]
    in_specs += [pl.BlockSpec(e.shape, lambda t: (0, 0)) for e in extra]
    x_specs, x_args = _stream_specs(x, nt, sub)
    n_before_x = len(in_specs)
    in_specs += x_specs + [pl.BlockSpec(mods.shape, lambda t: (0, 0, 0)), pl.BlockSpec(w.shape, lambda t: (0, 0))]
    return pl.pallas_call(
        functools.partial(_outproj_kernel, prologue=prologue, n_t=len(ins_t), n_tok=len(ins), n_extra=len(extra),
                          n_x=len(x_args), nt=nt, nb=nb, sub=sub),
        grid=(t_tok // rows,),
        in_specs=in_specs,
        out_specs=pl.BlockSpec((rows, d), lambda t: (t, 0)),
        out_shape=jax.ShapeDtypeStruct((t_tok, d), F32),
        input_output_aliases={} if isinstance(x, tuple) else {n_before_x: 0},
        compiler_params=_params("parallel"),
        name="outproj",
    )(*args, *ins, *extra, *x_args, mods, w)


def _group_rms(y, n_groups):
    width = y.shape[1] // n_groups
    parts = []
    for g in range(n_groups):
        yg = y[:, g * width:(g + 1) * width]
        parts.append(yg * lax.rsqrt(jnp.mean(yg * yg, axis=-1, keepdims=True) + EPS))
    return jnp.concatenate(parts, axis=1)


def _plain_prologue(o_ref):
    return o_ref[...]


def _ssm_prologue(yf_ref, yb_ref, z_ref, g_ref):
    y = jnp.transpose(yf_ref[0].astype(F32) + yb_ref[0].astype(F32)) * _silu(z_ref[...].astype(F32))
    return (_group_rms(y, SSM_GROUPS) * g_ref[...]).astype(BF16)


def _mlstm_prologue(hf_ref, hb_ref, o_ref, g_ref):
    h = jnp.transpose(hf_ref[0].astype(F32) + hb_ref[0].astype(F32))
    return (_group_rms(h, ML_HEADS) * g_ref[...] * _sigmoid(o_ref[...].astype(F32))).astype(BF16)


MOE_ROUTE_ROWS = 24


def _route(logits):
    lt = jnp.transpose(logits)[:MOE_ROUTE_ROWS, :]
    row = lax.broadcasted_iota(jnp.int32, lt.shape, 0).astype(F32)
    neg = -jnp.inf
    lg = jnp.where((row >= MOE_EXPERTS) & (row < MOE_EXPERTS + MOE_GROUPS), lt, neg)
    gmax = jnp.max(lg, axis=0, keepdims=True)
    g_sel = jnp.min(jnp.where(lg == gmax, row, LANES), axis=0, keepdims=True) - MOE_EXPERTS
    p_g = 1.0 / jnp.sum(jnp.exp(lg - gmax), axis=0, keepdims=True)
    in_group = (row >= g_sel * MOE_PER_GROUP) & (row < (g_sel + 1) * MOE_PER_GROUP)
    le = jnp.where(in_group, lt, neg)
    v1 = jnp.max(le, axis=0, keepdims=True)
    i1 = jnp.min(jnp.where(le == v1, row, LANES), axis=0, keepdims=True)
    le2 = jnp.where(row == i1, neg, le)
    v2 = jnp.max(le2, axis=0, keepdims=True)
    i2 = jnp.min(jnp.where(le2 == v2, row, LANES), axis=0, keepdims=True)
    e2 = jnp.exp(v2 - v1)
    w1 = p_g / (1.0 + e2)
    comb_t = jnp.where(row == i1, w1, 0.0) + jnp.where(row == i2, w1 * e2, 0.0)
    pad = jnp.zeros((LANES - MOE_ROUTE_ROWS, lt.shape[1]), F32)
    return jnp.transpose(jnp.concatenate([comb_t, pad], axis=0))


def _moe_kernel(x_ref, mods_ref, g_ref, wr_ref, br_ref, wg_ref, wu_ref, wd_ref, o_ref, h_scr, comb_scr,
                *, nt, nb, sub):
    i = pl.program_id(0)
    e = pl.program_id(1)

    def experts(rows):
        hb = h_scr[rows, :]
        comb = comb_scr[rows, :]
        lane = lax.broadcasted_iota(jnp.int32, comb.shape, 1)
        scaled = []
        for k in range(MOE_PER_GROUP):
            act = _silu(_dot(hb, wg_ref[0, k])) * _dot(hb, wu_ref[0, k])
            cw = jnp.sum(jnp.where(lane == e * MOE_PER_GROUP + k, comb, 0.0), axis=-1, keepdims=True)
            scaled.append((act * cw).astype(BF16))
        return _dot(jnp.concatenate(scaled, axis=1), wd_ref[0, 0])

    def gate_row(s):
        return mods_ref[_mod_row(i * sub + s, nt, nb)][5:6, :]

    @pl.when(e == 0)
    def _():
        w_hi, w_lo = _split_bf16(wr_ref[...])
        w_both = jnp.concatenate([w_hi, w_lo], axis=1)
        def prepare(s):
            rows = slice(s * TM, (s + 1) * TM)
            mod = mods_ref[_mod_row(i * sub + s, nt, nb)]
            h = _normed(x_ref[rows, :], g_ref[...], mod, 3, 4)
            h_hi, h_lo = _split_bf16(h)
            h_scr[rows, :] = h_hi
            both = _dot(h_hi, w_both)
            logits = both[:, :LANES] + both[:, LANES:] + _dot(h_lo, w_hi) + br_ref[...]
            comb_scr[rows, :] = _route(logits)

        for s in range(sub):
            prepare(s)
            rows = slice(s * TM, (s + 1) * TM)
            o_ref[rows, :] = x_ref[rows, :] + gate_row(s) * experts(rows)

    @pl.when(e > 0)
    def _():
        y = experts(slice(0, sub * TM))
        for s in range(sub):
            rows = slice(s * TM, (s + 1) * TM)
            o_ref[rows, :] += gate_row(s) * y[rows, :]


def _moe(x, mods, gain, w_router, b_router, w_gate, w_up, w_down, layer, nt, nb):
    t_tok, d = x.shape
    pg = MOE_PER_GROUP
    rows = MOE_ROWS if t_tok % MOE_ROWS == 0 else TM
    sub = rows // TM
    return pl.pallas_call(
        functools.partial(_moe_kernel, nt=nt, nb=nb, sub=sub),
        grid=(t_tok // rows, MOE_GROUPS),
        in_specs=[pl.BlockSpec((rows, d), lambda i, e: (i, 0)),
                  pl.BlockSpec(mods.shape, lambda i, e: (0, 0, 0)),
                  pl.BlockSpec((1, d), lambda i, e: (0, 0)),
                  pl.BlockSpec((d, LANES), lambda i, e: (0, 0)),
                  pl.BlockSpec((1, LANES), lambda i, e: (0, 0)),
                  pl.BlockSpec((1, pg, d, MOE_FF), lambda i, e: (layer, e, 0, 0)),
                  pl.BlockSpec((1, pg, d, MOE_FF), lambda i, e: (layer, e, 0, 0)),
                  pl.BlockSpec((1, 1, pg * MOE_FF, d), lambda i, e: (layer, e, 0, 0))],
        out_specs=pl.BlockSpec((rows, d), lambda i, e: (i, 0)),
        out_shape=jax.ShapeDtypeStruct((t_tok, d), F32),
        scratch_shapes=[pltpu.VMEM((rows, d), BF16), pltpu.VMEM((rows, LANES), F32)],
        input_output_aliases={0: 0},
        compiler_params=_params("parallel", "arbitrary"),
        name="moe",
    )(x, mods, gain.reshape(1, d), w_router, b_router, w_gate, w_up, w_down)


def _attn_kernel(sink_ref, q_ref, kp_ref, kc_ref, kn_ref, vp_ref, vc_ref, vn_ref, kx_ref, vx_ref, o_ref, *, n_lat):
    j = pl.program_id(1)
    jl = j - 1
    dh = ATT_HEAD_DIM
    n_loc = TM + 2 * CHUNK
    k_all = jnp.concatenate([kp_ref[...], kc_ref[...], kn_ref[...], kx_ref[...]], axis=0)
    q_t = jnp.transpose(q_ref[...].astype(F32)).astype(BF16)
    n_keys = k_all.shape[0]
    kj = lax.broadcasted_iota(jnp.int32, (n_keys, TM), 0)
    qi = lax.broadcasted_iota(jnp.int32, (n_keys, TM), 1)
    kpos = jl * TM - CHUNK + kj
    local_ok = (kj >= qi + CHUNK - WINDOW) & (kj <= qi + CHUNK + WINDOW) & (kpos >= 0) & (kpos < n_lat) & (jl >= 0)
    bias = jnp.where((kj >= n_loc) | local_ok, 0.0, -jnp.inf)
    uh = ATT_UNIT_HEADS
    bias = jnp.concatenate([bias] * uh, axis=1)
    lane_head = lax.broadcasted_iota(jnp.int32, (1, uh * TM), 1) // TM
    zeros_q = jnp.zeros((dh, uh * TM), BF16)
    units = [(h0 // ATT_GROUP, h0) for h0 in range(0, ATT_HEADS, uh)]

    def scores(u):
        g, h0 = units[u]
        half = g % 2
        k_pair = k_all[:, (g - half) * dh:(g - half + 2) * dh]
        qg = jnp.concatenate([q_t[(h0 + a) * dh:(h0 + a + 1) * dh, :] for a in range(uh)], axis=1)
        q_m = jnp.concatenate([zeros_q, qg] if half else [qg, zeros_q], axis=0)
        return _dot(k_pair, q_m) + bias

    outs = []
    s_next = scores(0)
    v_t = jnp.transpose(jnp.concatenate([vp_ref[...], vc_ref[...], vn_ref[...], vx_ref[...]],
                                        axis=0).astype(F32)).astype(BF16)
    for u, (g, h0) in enumerate(units):
        s = s_next
        if u + 1 < len(units):
            s_next = scores(u + 1)
        sink = jnp.zeros((1, uh * TM), F32)
        for a in range(uh):
            sink = jnp.where(lane_head == a, sink_ref[h0 + a] * LOG2_E, sink)
        m = jnp.maximum(jnp.max(s, axis=0, keepdims=True), sink)
        p = jnp.exp2(s - m)
        den = jnp.sum(p, axis=0, keepdims=True) + jnp.exp2(sink - m)
        og = _dot(v_t[g * dh:(g + 1) * dh, :], p.astype(BF16)) / den
        outs += [og[:, a * TM:(a + 1) * TM] for a in range(uh)]
    o_ref[...] = jnp.transpose(jnp.concatenate(outs, axis=0)).astype(o_ref.dtype)


def _windowed_attention(q, k, v, sink, nb, seg):
    t_tok = q.shape[0]
    nchunk = seg // CHUNK
    nt = seg // TM
    per_tile = TM // CHUNK
    kvw = ATT_KV_HEADS * ATT_HEAD_DIM

    def halo(off):
        return lambda b, j, *_: (b * nchunk + jnp.clip(j * per_tile + off, per_tile, nchunk - 1), 0)

    tile = lambda b, j, *_: (b * nt + j, 0)
    kv_spec = [pl.BlockSpec((CHUNK, kvw), halo(-1)), pl.BlockSpec((TM, kvw), tile),
               pl.BlockSpec((CHUNK, kvw), halo(per_tile))]
    ctx_spec = pl.BlockSpec((TM, kvw), lambda b, j, *_: (b * nt, 0))
    return pl.pallas_call(
        functools.partial(_attn_kernel, n_lat=seg - TM),
        grid_spec=pltpu.PrefetchScalarGridSpec(
            num_scalar_prefetch=1,
            grid=(nb, nt),
            in_specs=[pl.BlockSpec((TM, q.shape[1]), tile)] + kv_spec + kv_spec + [ctx_spec, ctx_spec],
            out_specs=pl.BlockSpec((TM, q.shape[1]), tile),
        ),
        out_shape=jax.ShapeDtypeStruct((t_tok, q.shape[1]), BF16),
        compiler_params=_params("parallel", "parallel"),
        name="windowed_attention",
    )(sink, q, k, k, k, v, v, v, k, v)


def _ssm_in_kernel(x_ref, xp_ref, xn_ref, mods_ref, g_ref, w_ref, cw_ref, cb_ref, dtb_ref, aneg_ref,
                   z_ref, xs_ref, bm_ref, cm_ref, dtv_ref, a_ref, *ext, nt, nb):
    t = pl.program_id(0)
    tl = t % nt
    halo = 8
    first = (tl == 0) | (tl == 1)
    last = (tl == 0) | (tl == nt - 1)
    gain = g_ref[...]

    def normed(x, tile):
        return _normed(x, gain, mods_ref[_mod_row(tile, nt, nb)], 0, 1)

    h = normed(x_ref[...], t)
    h_prev = jnp.where(first, 0.0, normed(xp_ref[...], jnp.maximum(t - 1, 0)))
    h_next = jnp.where(last, 0.0, normed(xn_ref[...], jnp.minimum(t + 1, pl.num_programs(0) - 1)))
    hb = h.astype(BF16)
    hb_ext = jnp.concatenate([h_prev, h, h_next], axis=0).astype(BF16)
    chunk = SSM_CONV_CHUNK
    pad = SSM_CONV // 2

    def project(c):
        proj = _dot(hb_ext, w_ref[:, SSM_D_INNER + c:SSM_D_INNER + c + chunk])
        ext[c // chunk][...] = proj

    project(0)
    for c in range(0, SSM_CONV_DIM, chunk):
        if c + chunk < SSM_CONV_DIM:
            project(c + chunk)
        if c < SSM_D_INNER:
            z_ref[:, c:c + chunk] = _dot(hb, w_ref[:, c:c + chunk]).astype(z_ref.dtype)
        acc = jnp.broadcast_to(cb_ref[:, c:c + chunk], (TM, chunk))
        for k in range(SSM_CONV):
            acc = acc + cw_ref[k:k + 1, c:c + chunk] * ext[c // chunk][halo - pad + k:halo - pad + k + TM, :]
        y = _silu(acc)
        if c < SSM_D_INNER:
            xs_ref[0, c:c + chunk, :] = jnp.transpose(y).astype(xs_ref.dtype)
        elif c < SSM_D_INNER + SSM_BC:
            bm_ref[:, c - SSM_D_INNER:c - SSM_D_INNER + chunk] = y.astype(bm_ref.dtype)
        else:
            off = c - SSM_D_INNER - SSM_BC
            cm_ref[0, off:off + chunk, :] = jnp.transpose(y).astype(cm_ref.dtype)
    dt = _dot(hb, w_ref[:, SSM_D_INNER + SSM_CONV_DIM:])
    dtv = _softplus(dt + dtb_ref[...])
    dtv_ref[0] = jnp.transpose(dtv)
    a_ref[0] = jnp.transpose(dtv * aneg_ref[...] * LOG2_E)


def _ssm_in(x, mods, gain, w, conv_w, conv_b, dt_bias, a_neg, nt, nb):
    t_tok, d = x.shape
    n8 = t_tok // 8
    seg = nt * TM
    row = lambda t: (t, 0)
    col = lambda t: (t // nt, 0, t % nt)
    fixed = lambda t: (0, 0)
    return pl.pallas_call(
        functools.partial(_ssm_in_kernel, nt=nt, nb=nb),
        grid=(t_tok // TM,),
        in_specs=[pl.BlockSpec((TM, d), row),
                  pl.BlockSpec((8, d), lambda t: (jnp.maximum(t * (TM // 8) - 1, 0), 0)),
                  pl.BlockSpec((8, d), lambda t: (jnp.minimum((t + 1) * (TM // 8), n8 - 1), 0)),
                  pl.BlockSpec(mods.shape, lambda t: (0, 0, 0)),
                  pl.BlockSpec((1, d), fixed),
                  pl.BlockSpec(w.shape, fixed),
                  pl.BlockSpec((8, SSM_CONV_DIM), fixed),
                  pl.BlockSpec((1, SSM_CONV_DIM), fixed),
                  pl.BlockSpec((1, LANES), fixed),
                  pl.BlockSpec((1, LANES), fixed)],
        out_specs=[pl.BlockSpec((TM, SSM_D_INNER), row),
                   pl.BlockSpec((1, SSM_D_INNER, TM), col), pl.BlockSpec((TM, SSM_BC), row),
                   pl.BlockSpec((1, SSM_BC, TM), col), pl.BlockSpec((1, LANES, TM), col),
                   pl.BlockSpec((1, LANES, TM), col)],
        out_shape=[jax.ShapeDtypeStruct((t_tok, SSM_D_INNER), BF16),
                   jax.ShapeDtypeStruct((nb, SSM_D_INNER, seg), BF16), jax.ShapeDtypeStruct((t_tok, SSM_BC), BF16),
                   jax.ShapeDtypeStruct((nb, SSM_BC, seg), BF16), jax.ShapeDtypeStruct((nb, LANES, seg), F32),
                   jax.ShapeDtypeStruct((nb, LANES, seg), F32)],
        scratch_shapes=[pltpu.VMEM((TM + 16, SSM_CONV_CHUNK), F32)] * (SSM_CONV_DIM // SSM_CONV_CHUNK),
        compiler_params=_params("parallel"),
        name="ssm_in",
    )(x, x, x, mods, gain.reshape(1, d), w, conv_w, conv_b, dt_bias, a_neg)


def _tri(lower, n):
    r = lax.broadcasted_iota(jnp.int32, (n, n), 0)
    c = lax.broadcasted_iota(jnp.int32, (n, n), 1)
    return (c <= r) if lower else (c >= r)


def _ssd_prepare(dt_ref, a_ref, d):
    a_t = a_ref[0]
    dt_t = dt_ref[0]
    upper = jnp.where(_tri(False, SSD_CHUNK), 1.0, 0.0).astype(BF16)
    a_hi, a_lo = _split_bf16(a_t)
    cum = _dot(a_hi, upper) + _dot(a_lo, upper)
    cum_end = cum[:, SSD_CHUNK - 1:SSD_CHUNK]
    if d == 0:
        lane_v, sub_v = cum, -cum
        inter = jnp.exp2(cum)
        w_upd = jnp.exp2(cum_end - cum) * dt_t
    else:
        ecum = cum - a_t
        lane_v, sub_v = -ecum, ecum
        inter = jnp.exp2(cum_end - ecum)
        w_upd = jnp.exp2(ecum) * dt_t
    sub_c = jnp.transpose(sub_v)
    decay_end = jnp.exp2(cum_end)
    return dt_t, lane_v, sub_c, inter, w_upd, decay_end


def _ssd_groups(prep, xs_ref, bm_ref, ct_ref, dsk_ref, st_ref, y_ref, d):
    dt_t, lane_v, sub_c, inter, w_upd, decay_end = prep
    mask = _tri(d == 1, SSD_CHUNK)
    p = SSM_HEAD_DIM
    for g in range(SSM_GROUPS):
        bg = bm_ref[:, g * SSM_STATE:(g + 1) * SSM_STATE]
        cg_t = ct_ref[0, g * SSM_STATE:(g + 1) * SSM_STATE, :]
        cb_t = _dot(bg, cg_t)
        state = st_ref[g]
        y_in = _dot(state.astype(BF16), cg_t)
        upd = []
        for e in range(SSM_HPG):
            h = g * SSM_HPG + e
            ln = d * SSM_HEADS + h
            rows = slice(e * p, (e + 1) * p)
            blk = SSD_DECAY_BLOCK
            block_rows = []
            for bs in range(0, SSD_CHUNK, blk):
                block_cols = []
                for bl in range(0, SSD_CHUNK, blk):
                    if (bs > bl) if d == 0 else (bs < bl):
                        block_cols.append(jnp.zeros((blk, blk), BF16))
                        continue
                    seg = sub_c[bs:bs + blk, ln:ln + 1] + lane_v[ln:ln + 1, bl:bl + blk]
                    if bs == bl:
                        seg = jnp.where(mask[bs:bs + blk, bl:bl + blk], seg, -jnp.inf)
                    block_cols.append((cb_t[bs:bs + blk, bl:bl + blk] * jnp.exp2(seg)).astype(BF16))
                block_rows.append(jnp.concatenate(block_cols, axis=1))
            m_t = jnp.concatenate(block_rows, axis=0)
            xf = xs_ref[0, h * p:(h + 1) * p, :].astype(F32)
            u = (xf * dt_t[ln:ln + 1, :]).astype(BF16)
            yh = _dot(u, m_t) + y_in[rows, :] * inter[ln:ln + 1, :] + dsk_ref[d, h] * xf
            y_ref[0, h * p:(h + 1) * p, :] = yh.astype(y_ref.dtype)
            upd.append((xf * w_upd[ln:ln + 1, :]).astype(BF16))
        new = _dot(jnp.concatenate(upd, axis=0), bg)
        for e in range(SSM_HPG):
            ln = d * SSM_HEADS + g * SSM_HPG + e
            rows = slice(e * p, (e + 1) * p)
            st_ref[g, rows, :] = state[rows, :] * decay_end[ln:ln + 1, :] + new[rows, :]


def _ssd_kernel(dsk_ref, xsf, bmf, ctf, dtf, af, xsb, bmb, ctb, dtb, ab, yf_ref, yb_ref, stf, stb):
    @pl.when(pl.program_id(1) == 0)
    def _():
        stf[...] = jnp.zeros_like(stf)
        stb[...] = jnp.zeros_like(stb)

    prep_f = _ssd_prepare(dtf, af, 0)
    prep_b = _ssd_prepare(dtb, ab, 1)
    _ssd_groups(prep_f, xsf, bmf, ctf, dsk_ref, stf, yf_ref, 0)
    _ssd_groups(prep_b, xsb, bmb, ctb, dsk_ref, stb, yb_ref, 1)


def _scan_maps(nchunk, chunk):
    ctx_chunks = TM // chunk
    fwd = lambda b, t: (b * nchunk + t, 0)
    bwd = lambda b, t: (b * nchunk + jnp.where(t < ctx_chunks, ctx_chunks - 1 - t, nchunk + ctx_chunks - 1 - t), 0)
    return fwd, bwd


def _ssd_scan(xs_t, bm, c_t, dt_t, a_t, d_skip, nb, seg):
    nchunk = seg // SSD_CHUNK
    fwd_rows, bwd_rows = _scan_maps(nchunk, SSD_CHUNK)

    def specs(rows_map):
        cols_map = lambda b, t, *_: (b, 0, rows_map(b, t)[0] - b * nchunk)
        return [pl.BlockSpec((1, SSM_D_INNER, SSD_CHUNK), cols_map),
                pl.BlockSpec((SSD_CHUNK, SSM_BC), lambda b, t, *_: rows_map(b, t)),
                pl.BlockSpec((1, SSM_BC, SSD_CHUNK), cols_map),
                pl.BlockSpec((1, LANES, SSD_CHUNK), cols_map),
                pl.BlockSpec((1, LANES, SSD_CHUNK), cols_map)]

    arrs = (xs_t, bm, c_t, dt_t, a_t)
    return pl.pallas_call(
        _ssd_kernel,
        grid_spec=pltpu.PrefetchScalarGridSpec(
            num_scalar_prefetch=1,
            grid=(nb, nchunk),
            in_specs=specs(fwd_rows) + specs(bwd_rows),
            out_specs=[specs(fwd_rows)[0], specs(bwd_rows)[0]],
            scratch_shapes=[pltpu.VMEM((SSM_GROUPS, SSM_HPG * SSM_HEAD_DIM, SSM_STATE), F32)] * 2,
        ),
        out_shape=[jax.ShapeDtypeStruct(xs_t.shape, BF16)] * 2,
        compiler_params=_params("parallel", "arbitrary"),
        name="ssd_scan",
    )(d_skip, *arrs, *arrs)


def _log_sigmoid(x):
    return jnp.minimum(x, 0.0) - jnp.log1p(jnp.exp(-jnp.abs(x)))


ML_STATE_ROWS = ML_V_DIM + 16


def _mlstm_prepare(qk_ref, v_ref, g_ref, gb_ref, m_st, d):
    nh, dk, dv = ML_HEADS, ML_QK_DIM, ML_V_DIM
    g_t = jnp.transpose(g_ref[...] + gb_ref[...])
    ig = g_t[16 * d:16 * d + nh, :]
    lf = _log_sigmoid(g_t[16 * d + nh:16 * d + 2 * nh, :])
    upper = jnp.where(_tri(False, ML_CHUNK), 1.0, 0.0).astype(BF16)
    lf_hi, lf_lo = _split_bf16(lf)
    fc = _dot(lf_hi, upper) + _dot(lf_lo, upper)
    tot = fc[:, ML_CHUNK - 1:ML_CHUNK]
    m_prev = m_st[...]
    if d == 0:
        lane_v, sub_v = fc, ig - fc
        inter = fc + m_prev
        logw = tot - fc + ig
    else:
        ec = fc - lf
        lane_v, sub_v = -ec, ec + ig
        inter = tot - ec + m_prev
        logw = ec + ig
    sub_c = jnp.transpose(jnp.concatenate([sub_v, jnp.zeros((LANES - nh, ML_CHUNK), F32)], axis=0))
    m_new = jnp.maximum(tot + m_prev, jnp.max(logw, axis=-1, keepdims=True))
    ws = jnp.exp(logw - m_new)
    cw = jnp.exp(tot + m_prev - m_new)
    q_t = jnp.transpose(qk_ref[:, :nh * dk].astype(F32))
    v_t = jnp.transpose(v_ref[...].astype(F32))
    return lane_v, sub_c, inter, ws, cw, m_new, q_t, v_t


def _mlstm_heads(prep, qk_ref, c_st, m_st, h_ref, d):
    nh, dk, dv = ML_HEADS, ML_QK_DIM, ML_V_DIM
    lane_v, sub_c, inter, ws, cw, m_new, q_t, v_t = prep
    mask = _tri(d == 1, ML_CHUNK)
    lane = lax.broadcasted_iota(jnp.int32, (1, LANES), 1)
    zeros_q = jnp.zeros((dk, ML_CHUNK), F32)
    for h in range(nh):
        half = h % 2
        k_pair = qk_ref[:, nh * dk + (h - half) * dk:nh * dk + (h - half + 2) * dk] * (dk ** -0.5)
        qh = q_t[h * dk:(h + 1) * dk, :]
        q_m = jnp.concatenate([zeros_q, qh] if half else [qh, zeros_q], axis=0).astype(BF16)
        logd = jnp.where(mask, sub_c[:, h:h + 1] + lane_v[h:h + 1, :], -jnp.inf)
        mt = jnp.maximum(inter[h:h + 1, :], jnp.max(logd, axis=0, keepdims=True))
        sc = _dot(k_pair, q_m) * jnp.exp(logd - mt)
        vh = v_t[h * dv:(h + 1) * dv, :]
        state = c_st[h]
        cq = _dot(state.astype(BF16), q_m)
        w_int = jnp.exp(inter[h:h + 1, :] - mt)
        num = _dot(vh.astype(BF16), sc.astype(BF16)) + w_int * cq[:dv, :]
        den = jnp.sum(sc, axis=0, keepdims=True) + w_int * cq[dv:dv + 1, :]
        h_ref[0, h * dv:(h + 1) * dv, :] = (num / jnp.maximum(jnp.abs(den), jnp.exp(-mt))).astype(h_ref.dtype)
        ws_h = ws[h:h + 1, :]
        lhs = jnp.concatenate([vh * ws_h, jnp.broadcast_to(ws_h, (ML_STATE_ROWS - dv, ML_CHUNK))], axis=0)
        own = (lane >= half * dk) & (lane < (half + 1) * dk)
        row = lax.broadcasted_iota(jnp.int32, (ML_STATE_ROWS, 1), 0)
        new = cw[h:h + 1, :LANES] * state + _dot(lhs.astype(BF16), k_pair)
        c_st[h] = jnp.where(own & (row <= dv), new, 0.0)
    m_st[...] = jnp.broadcast_to(m_new, m_st.shape)


def _mlstm_kernel(qkf, vf, gf, qkb, vb, gb, gbias, hf_ref, hb_ref, cf, mf, cb, mb):
    @pl.when(pl.program_id(1) == 0)
    def _():
        for r in (cf, mf, cb, mb):
            r[...] = jnp.zeros_like(r)

    prep_f = _mlstm_prepare(qkf, vf, gf, gbias, mf, 0)
    prep_b = _mlstm_prepare(qkb, vb, gb, gbias, mb, 1)
    _mlstm_heads(prep_f, qkf, cf, mf, hf_ref, 0)
    _mlstm_heads(prep_b, qkb, cb, mb, hb_ref, 1)


def _mlstm_scan(qk, v, g, gate_b, nb, seg):
    t_tok = qk.shape[0]
    nchunk = seg // ML_CHUNK
    fwd, bwd = _scan_maps(nchunk, ML_CHUNK)
    widths = (qk.shape[1], v.shape[1], LANES)
    in_specs = ([pl.BlockSpec((ML_CHUNK, w), fwd) for w in widths] + [pl.BlockSpec((ML_CHUNK, w), bwd) for w in widths]
                + [pl.BlockSpec((1, LANES), lambda b, t: (0, 0))])
    state = [pltpu.VMEM((ML_HEADS, ML_STATE_ROWS, LANES), F32), pltpu.VMEM((ML_HEADS, ML_CHUNK), F32)]
    out_map = lambda rows_map: (lambda b, t: (b, 0, rows_map(b, t)[0] - b * nchunk))
    return pl.pallas_call(
        _mlstm_kernel,
        grid=(nb, nchunk),
        in_specs=in_specs,
        out_specs=[pl.BlockSpec((1, v.shape[1], ML_CHUNK), out_map(fwd)),
                   pl.BlockSpec((1, v.shape[1], ML_CHUNK), out_map(bwd))],
        out_shape=[jax.ShapeDtypeStruct((nb, v.shape[1], seg), BF16)] * 2,
        scratch_shapes=state + state,
        compiler_params=_params("parallel", "arbitrary"),
        name="mlstm_scan",
    )(qk, v, g, qk, v, g, gate_b)


def _rms_rows(x, g_row):
    return x * lax.rsqrt(jnp.mean(x * x, axis=-1, keepdims=True) + EPS) * g_row


MLA_SCORE_SCALE = (MLA_NOPE + MLA_ROPE) ** -0.5 * LOG2_E


def _mla_queries(cq, g_ref, wt_ref, cos_ref, sin_ref, qt_ref):
    cq_t = jnp.transpose(_rms_rows(cq, g_ref[...])).astype(BF16)
    q_t = _dot(wt_ref[...], cq_t) * MLA_SCORE_SCALE
    qt_ref[0] = q_t.astype(qt_ref.dtype)
    cos, sin = cos_ref[...], sin_ref[...]
    half = MLA_ROPE // 2
    for h in range(MLA_HEADS):
        r0 = h * LANES + MLA_NOPE
        x1, x2 = q_t[r0:r0 + half, :], q_t[r0 + half:r0 + MLA_ROPE, :]
        qt_ref[0, r0:r0 + half, :] = (x1 * cos - x2 * sin).astype(qt_ref.dtype)
        qt_ref[0, r0 + half:r0 + MLA_ROPE, :] = (x2 * cos + x1 * sin).astype(qt_ref.dtype)


def _mla_keys_values(ckv, kr, g_ref, wk_ref, wvt_ref, cos_ref, sin_ref, k_ref, vt_ref):
    cn = _rms_rows(ckv, g_ref[...])
    vt_ref[0] = _dot(wvt_ref[...], jnp.transpose(cn).astype(BF16)).astype(vt_ref.dtype)
    kn = _dot(cn.astype(BF16), wk_ref[...])
    kr = pltpu.roll(kr, MLA_NOPE, axis=1)
    half = MLA_ROPE // 2
    lane = lax.broadcasted_iota(jnp.int32, (1, LANES), 1)
    partner = jnp.where(lane < MLA_NOPE + half, pltpu.roll(kr, LANES - half, axis=1), pltpu.roll(kr, half, axis=1))
    roped = kr * cos_ref[...] + partner * sin_ref[...]
    for h in range(MLA_HEADS):
        k_ref[:, h * LANES:(h + 1) * LANES] = (kn[:, h * LANES:(h + 1) * LANES] + roped).astype(k_ref.dtype)


def _mla_in_kernel(x_ref, mods_ref, g_ref, w_ref, gq_ref, wqt_ref, gkv_ref, wk_ref, wvt_ref,
                   cos_t_ref, sin_t_ref, cos_ref, sin_ref, qt_ref, k_ref, vt_ref, *, nt, nb):
    mod = mods_ref[_mod_row(pl.program_id(0), nt, nb)]
    hb = _normed(x_ref[...], g_ref[...], mod, 0, 1).astype(BF16)
    p = _dot(hb, w_ref[...])
    _mla_queries(p[:, :MLA_RANK], gq_ref, wqt_ref, cos_t_ref, sin_t_ref, qt_ref)
    _mla_keys_values(p[:, MLA_RANK:2 * MLA_RANK], p[:, 2 * MLA_RANK:], gkv_ref, wk_ref, wvt_ref,
                     cos_ref, sin_ref, k_ref, vt_ref)


def _mla_in(x, mods, gain, w_in, gain_q, wq_t, gain_kv, w_k, wv_t, tables_t, tables, nt, nb):
    t_tok, d = x.shape
    seg = nt * TM
    half = MLA_ROPE // 2
    fixed = lambda t: (0, 0)
    return pl.pallas_call(
        functools.partial(_mla_in_kernel, nt=nt, nb=nb),
        grid=(t_tok // TM,),
        in_specs=[pl.BlockSpec((TM, d), lambda t: (t, 0)),
                  pl.BlockSpec(mods.shape, lambda t: (0, 0, 0)),
                  pl.BlockSpec((1, d), fixed),
                  pl.BlockSpec(w_in.shape, fixed),
                  pl.BlockSpec((1, MLA_RANK), fixed),
                  pl.BlockSpec(wq_t.shape, fixed),
                  pl.BlockSpec((1, MLA_RANK), fixed),
                  pl.BlockSpec(w_k.shape, fixed),
                  pl.BlockSpec(wv_t.shape, fixed),
                  pl.BlockSpec((half, TM), lambda t: (0, t % nt)),
                  pl.BlockSpec((half, TM), lambda t: (0, t % nt)),
                  pl.BlockSpec((TM, LANES), lambda t: (t % nt, 0)),
                  pl.BlockSpec((TM, LANES), lambda t: (t % nt, 0))],
        out_specs=[pl.BlockSpec((1, MLA_HEADS * LANES, TM), lambda t: (t, 0, 0)),
                   pl.BlockSpec((TM, MLA_HEADS * LANES), lambda t: (t, 0)),
                   pl.BlockSpec((1, MLA_HEADS * MLA_V, TM), lambda t: (t // nt, 0, t % nt))],
        out_shape=[jax.ShapeDtypeStruct((t_tok // TM, MLA_HEADS * LANES, TM), BF16),
                   jax.ShapeDtypeStruct((t_tok, MLA_HEADS * LANES), BF16),
                   jax.ShapeDtypeStruct((nb, MLA_HEADS * MLA_V, seg), BF16)],
        compiler_params=_params("parallel"),
        name="mla_in",
    )(x, mods, gain.reshape(1, d), w_in, gain_q.reshape(1, MLA_RANK), wq_t, gain_kv.reshape(1, MLA_RANK), w_k, wv_t,
      *tables_t, *tables)


def _mla_attn_kernel(qt_ref, k_ref, vt_ref, o_ref):
    @pl.when(pl.program_id(2) == 0)
    def _():
        o_ref[...] = jnp.zeros_like(o_ref)

    @pl.when(pl.program_id(2) > 0)
    def _():
        heads = range(MLA_HEADS_PER_STEP)
        seg = k_ref.shape[0]
        tk = MLA_KEY_TILE if (seg - TM) % MLA_KEY_TILE == 0 else TM
        tiles = [(0, TM)] + [(a, a + tk) for a in range(TM, seg, tk)]

        def scores(hh, j):
            return _dot(k_ref[tiles[j][0]:tiles[j][1], hh * LANES:(hh + 1) * LANES],
                        qt_ref[0, hh * LANES:(hh + 1) * LANES, :])

        m = [jnp.full((1, TM), -jnp.inf, F32) for _ in heads]
        l = [jnp.zeros((1, TM), F32) for _ in heads]
        acc = [jnp.zeros((MLA_V, TM), F32) for _ in heads]
        ahead = min(MLA_LOOKAHEAD, len(tiles))
        pending = [[scores(hh, j) for hh in heads] for j in range(ahead)]
        for j in range(len(tiles)):
            s_cur = pending.pop(0)
            if j + ahead < len(tiles):
                pending.append([scores(hh, j + ahead) for hh in heads])
            for hh in heads:
                m_new = jnp.maximum(m[hh], jnp.max(s_cur[hh], axis=0, keepdims=True))
                alpha = jnp.exp2(m[hh] - m_new)
                p = jnp.exp2(s_cur[hh] - m_new)
                l[hh] = alpha * l[hh] + jnp.sum(p, axis=0, keepdims=True)
                v_t = vt_ref[0, hh * MLA_V:(hh + 1) * MLA_V, tiles[j][0]:tiles[j][1]]
                acc[hh] = alpha * acc[hh] + _dot(v_t, p.astype(BF16))
                m[hh] = m_new
        o_t = jnp.concatenate([acc[hh] / l[hh] for hh in heads], axis=0)
        o_ref[...] = jnp.transpose(o_t).astype(o_ref.dtype)


def _mla_attention(q_t, k, v_t, nb, seg):
    t_tok = k.shape[0]
    nt = seg // TM
    hps = MLA_HEADS_PER_STEP
    return pl.pallas_call(
        _mla_attn_kernel,
        grid=(nb, MLA_HEADS // hps, nt),
        in_specs=[pl.BlockSpec((1, hps * LANES, TM), lambda b, hp, j: (b * nt + j, hp, 0)),
                  pl.BlockSpec((seg, hps * LANES), lambda b, hp, j: (b, hp)),
                  pl.BlockSpec((1, hps * MLA_V, seg), lambda b, hp, j: (b, hp, 0))],
        out_specs=pl.BlockSpec((TM, hps * MLA_V), lambda b, hp, j: (b * nt + j, hp)),
        out_shape=jax.ShapeDtypeStruct((t_tok, MLA_HEADS * MLA_V), BF16),
        compiler_params=_params("parallel", "parallel", "arbitrary"),
        name="mla_attention",
    )(q_t, k, v_t)


def _final_kernel(x_ref, g_ref, o_ref):
    o_ref[0] = _rms_rows(x_ref[...], g_ref[...])


def _final_norm(x, gain, nb, seg):
    d = x.shape[1]
    nt = seg // TM
    return pl.pallas_call(
        _final_kernel,
        grid=(nb, nt - 1),
        in_specs=[pl.BlockSpec((TM, d), lambda b, j: (b * nt + 1 + j, 0)),
                  pl.BlockSpec((1, d), lambda b, j: (0, 0))],
        out_specs=pl.BlockSpec((1, TM, d), lambda b, j: (b, j, 0)),
        out_shape=jax.ShapeDtypeStruct((nb, seg - TM, d), F32),
        compiler_params=_params("parallel", "parallel"),
        name="final_norm",
    )(x, gain.reshape(1, d))


def _rope_angles(n_lat, rot_dim):
    rows = n_lat // GRID_W
    row = jnp.repeat(jnp.arange(rows), GRID_W).astype(F32)
    col = jnp.tile(jnp.arange(GRID_W), rows).astype(F32)
    quarter = rot_dim // 4
    inv = ROPE_BASE ** (-jnp.arange(quarter, dtype=F32) / quarter)
    ang = jnp.concatenate([row[:, None] * inv, col[:, None] * inv], axis=-1)
    return jnp.cos(ang), jnp.sin(ang)


def _with_ctx_rows(tab, fill):
    return jnp.concatenate([jnp.full((TM, tab.shape[1]), fill, F32), tab], axis=0)


def _attn_rope_tables(n_lat):
    cos, sin = _rope_angles(n_lat, ATT_HEAD_DIM)
    cos_h = jnp.concatenate([cos, cos], axis=1)
    sin_h = jnp.concatenate([-sin, sin], axis=1)
    reps = LANES // ATT_HEAD_DIM
    return (_with_ctx_rows(jnp.tile(cos_h, (1, reps)), 1.0), _with_ctx_rows(jnp.tile(sin_h, (1, reps)), 0.0))


def _mla_k_tables(n_lat):
    cos, sin = _rope_angles(n_lat, MLA_ROPE)
    lo = jnp.zeros((n_lat, MLA_NOPE), F32)
    hi = jnp.zeros((n_lat, LANES - MLA_NOPE - MLA_ROPE), F32)
    cos_c = jnp.concatenate([lo, cos, cos, hi], axis=1)
    sin_c = jnp.concatenate([lo, -sin, sin, hi], axis=1)
    ctx_cos = jnp.concatenate([lo[:TM], jnp.ones((TM, MLA_ROPE), F32), hi[:TM]], axis=1)
    return (jnp.concatenate([ctx_cos, cos_c], axis=0), _with_ctx_rows(sin_c, 0.0))


def _pad_heads(w, real):
    r = w.shape[0]
    w3 = w.reshape(r, MLA_HEADS, real)
    return jnp.pad(w3, ((0, 0), (0, 0), (0, LANES - real))).reshape(r, MLA_HEADS * LANES)


def kernel(x, c, ctx, c_ctx, norm1_g, norm2_g, w_mod, b_mod, moe_w_group, moe_b_group, moe_w_expert, moe_b_expert, moe_w_gate, moe_w_up, moe_w_down, attn_w_in, attn_sink, attn_w_out, ssm_w_in, ssm_conv_w, ssm_conv_b, ssm_dt_bias, ssm_a_log, ssm_d, ssm_norm_g, ssm_w_out, mlstm_w_in, mlstm_gate_b, mlstm_norm_g, mlstm_w_out, mla_w_in, mla_q_norm_g, mla_w_q_up, mla_kv_norm_g, mla_w_kv_up, mla_w_out, final_norm_g):
    nb, n_lat, d = x.shape
    assert ctx.shape[1] == TM and d == D_MODEL and n_lat % TM == 0
    depth = w_mod.shape[0]
    seg = TM + n_lat
    nt = seg // TM
    t_tok = nb * seg

    xs = (x, ctx)

    rows = -(-(nb + 1) // 8) * 8
    cvec = jnp.concatenate([c, c_ctx[None, :], jnp.zeros((rows - nb - 1, d), F32)], axis=0)
    mods = _modulation(cvec, w_mod, b_mod).reshape(depth, rows, ADALN_CHUNKS, d)
    mods = jnp.pad(mods, ((0, 0), (0, 0), (0, MOD_ROWS - ADALN_CHUNKS), (0, 0)))

    w_router = jnp.concatenate([moe_w_expert, moe_w_group,
                                jnp.zeros((depth, d, LANES - MOE_EXPERTS - MOE_GROUPS), F32)], axis=-1)
    b_router = jnp.concatenate([moe_b_expert, moe_b_group,
                                jnp.zeros((depth, LANES - MOE_EXPERTS - MOE_GROUPS), F32)], axis=-1)
    w_gate, w_up = moe_w_gate.astype(BF16), moe_w_up.astype(BF16)
    w_down = moe_w_down.astype(BF16).reshape(depth, MOE_GROUPS, MOE_PER_GROUP * MOE_FF, d)

    for i in range(depth):
        kind = i % 4
        mod_i = mods[i]
        if kind == 0:
            nq, nk = ATT_HEADS * ATT_HEAD_DIM, ATT_KV_HEADS * ATT_HEAD_DIM
            half = ATT_HEAD_DIM // 2
            q, k, v = _normproj(xs, mod_i, norm1_g[i], attn_w_in[i // 4].astype(BF16),
                                [(0, nq, half, ATT_HEAD_DIM ** -0.5 * LOG2_E), (nq, nk, half), (nq + nk, nk, None)],
                                [BF16, BF16, BF16], nt, nb, tables=_attn_rope_tables(n_lat))
            o = _windowed_attention(q, k, v, attn_sink[i // 4], nb, seg)
            xs = _outproj([o], [nq], xs, mod_i, attn_w_out[i // 4].astype(BF16), _plain_prologue, nt, nb)
        elif kind == 1:
            j = i // 4
            w_in = jnp.pad(ssm_w_in[j], ((0, 0), (0, LANES - 2 * SSM_HEADS))).astype(BF16)
            lane_pad = LANES - 2 * SSM_HEADS
            dt_bias = jnp.pad(ssm_dt_bias[j].reshape(1, -1), ((0, 0), (0, lane_pad)))
            a_neg = jnp.pad(-jnp.exp(ssm_a_log[j].astype(F32)).reshape(1, -1), ((0, 0), (0, lane_pad)))
            conv_w = jnp.pad(ssm_conv_w[j], ((0, 8 - SSM_CONV), (0, 0)))
            z, xc, bm, cm, dtv, a = _ssm_in(xs, mod_i, norm1_g[i], w_in, conv_w, ssm_conv_b[j].reshape(1, -1),
                                            dt_bias, a_neg, nt, nb)
            yf, yb = _ssd_scan(xc, bm, cm, dtv, a, ssm_d[j].astype(F32), nb, seg)
            xs = _outproj([z], [SSM_D_INNER], xs, mod_i, ssm_w_out[j].astype(BF16), _ssm_prologue,
                          nt, nb, extra=(ssm_norm_g[j].reshape(1, -1),), ins_t=(yf, yb))
        elif kind == 2:
            j = i // 4
            nqk, nv = 2 * ML_HEADS * ML_QK_DIM, ML_HEADS * ML_V_DIM
            w_in = jnp.pad(mlstm_w_in[j], ((0, 0), (0, LANES - 4 * ML_HEADS))).astype(BF16)
            qk, v, o, g = _normproj(xs, mod_i, norm1_g[i], w_in,
                                    [(0, nqk, None), (nqk, nv, None), (nqk + nv, nv, None), (nqk + 2 * nv, LANES, None)],
                                    [BF16, BF16, BF16, F32], nt, nb)
            gate_b = jnp.pad(mlstm_gate_b[j].reshape(1, -1), ((0, 0), (0, LANES - 4 * ML_HEADS)))
            hf, hb = _mlstm_scan(qk, v, g, gate_b, nb, seg)
            xs = _outproj([o], [nv], xs, mod_i, mlstm_w_out[j].astype(BF16), _mlstm_prologue,
                          nt, nb, extra=(mlstm_norm_g[j].reshape(1, -1),), ins_t=(hf, hb))
        else:
            j = i // 4
            w_in = jnp.pad(mla_w_in[j], ((0, 0), (0, LANES - MLA_ROPE))).astype(BF16)
            cos, sin = _rope_angles(n_lat, MLA_ROPE)
            tables_t = (jnp.transpose(_with_ctx_rows(cos, 1.0)), jnp.transpose(_with_ctx_rows(sin, 0.0)))
            wq_t = jnp.transpose(_pad_heads(mla_w_q_up[j], MLA_NOPE + MLA_ROPE)).astype(BF16)
            w_kv = mla_w_kv_up[j].reshape(MLA_RANK, MLA_HEADS, MLA_NOPE + MLA_V)
            w_k = _pad_heads(w_kv[:, :, :MLA_NOPE].reshape(MLA_RANK, MLA_HEADS * MLA_NOPE), MLA_NOPE).astype(BF16)
            wv_t = jnp.transpose(w_kv[:, :, MLA_NOPE:].reshape(MLA_RANK, MLA_HEADS * MLA_V)).astype(BF16)
            q_t, k, v_t = _mla_in(xs, mod_i, norm1_g[i], w_in, mla_q_norm_g[j], wq_t, mla_kv_norm_g[j], w_k, wv_t,
                                  tables_t, _mla_k_tables(n_lat), nt, nb)
            o = _mla_attention(q_t, k, v_t, nb, seg)
            xs = _outproj([o], [MLA_HEADS * MLA_V], xs, mod_i, mla_w_out[j].astype(BF16), _plain_prologue, nt, nb)

        xs = _moe(xs, mod_i, norm2_g[i], w_router[i], b_router[i].reshape(1, -1), w_gate, w_up, w_down, i, nt, nb)

    return _final_norm(xs, final_norm_g, nb, seg)
```

```python
import functools
import math

import jax
import jax.numpy as jnp
from jax import lax
from jax.experimental import pallas as pl
from jax.experimental.pallas import tpu as pltpu

F32 = jnp.float32
BF16 = jnp.bfloat16

D_MODEL = 1024
GRID_W = 64
EPS = 1e-6
ROPE_BASE = 10000.0
ADALN_CHUNKS = 6
CHUNK = 128
SSD_CHUNK = 256
ML_CHUNK = 128
SSD_DECAY_BLOCK = 128
TM = 256
MOD_ROWS = 8
LANES = 128
V7X_VMEM_LIMIT = 48 * 1024 * 1024

ATT_HEADS, ATT_KV_HEADS, ATT_HEAD_DIM, WINDOW = 16, 4, 64, 128
ATT_GROUP = ATT_HEADS // ATT_KV_HEADS
ATT_UNIT_HEADS = 4
SSM_D_INNER, SSM_HEAD_DIM, SSM_HEADS, SSM_GROUPS, SSM_STATE, SSM_CONV = 2048, 64, 32, 4, 128, 5
SSM_HPG = SSM_HEADS // SSM_GROUPS
SSM_BC = SSM_GROUPS * SSM_STATE
SSM_CONV_DIM = SSM_D_INNER + 2 * SSM_BC
SSM_CONV_CHUNK = 256
ML_HEADS, ML_QK_DIM, ML_V_DIM = 8, 64, 128
MLA_HEADS, MLA_RANK, MLA_NOPE, MLA_ROPE, MLA_V = 16, 256, 64, 32, 64
MOE_GROUPS, MOE_PER_GROUP, MOE_EXPERTS, MOE_FF = 4, 4, 16, 256
MOE_ROWS = 1024
MLA_KEY_TILE = 1024
MLA_HEADS_PER_STEP = 4
MLA_LOOKAHEAD = 1
LOG2_E = 1.4426950408889634


def _dot(a, b):
    return jnp.dot(a, b, preferred_element_type=F32)


def _dot_nt(a, b):
    return lax.dot_general(a, b, (((1,), (1,)), ((), ())), preferred_element_type=F32)


def _split_bf16(x):
    hi = x.astype(BF16)
    lo = (x - hi.astype(F32)).astype(BF16)
    return hi, lo


def _dot_split(a, b):
    a_hi, a_lo = _split_bf16(a)
    b_hi, b_lo = _split_bf16(b)
    return _dot(a_hi, b_hi) + _dot(a_lo, b_hi) + _dot(a_hi, b_lo)


def _sigmoid(x):
    return 1.0 / (1.0 + jnp.exp(-x))


def _silu(x):
    return x * _sigmoid(x)


def _softplus(x):
    return jnp.maximum(x, 0.0) + jnp.log1p(jnp.exp(-jnp.abs(x)))


def _params(*sem):
    return pltpu.CompilerParams(dimension_semantics=sem, vmem_limit_bytes=V7X_VMEM_LIMIT)


def _mod_row(t, nt, nb):
    return jnp.where(t % nt == 0, nb, t // nt)


def _mod_kernel(c_ref, w_ref, b_ref, o_ref):
    o_ref[0] = _dot_split(_silu(c_ref[...]), w_ref[0]) + b_ref[0]


def _modulation(cvec, w_mod, b_mod):
    depth, d, n = w_mod.shape
    tn = 1536
    rows = cvec.shape[0]
    return pl.pallas_call(
        _mod_kernel,
        grid=(depth, n // tn),
        in_specs=[pl.BlockSpec((rows, d), lambda l, j: (0, 0)),
                  pl.BlockSpec((1, d, tn), lambda l, j: (l, 0, j)),
                  pl.BlockSpec((1, 1, tn), lambda l, j: (l, 0, j))],
        out_specs=pl.BlockSpec((1, rows, tn), lambda l, j: (l, 0, j)),
        out_shape=jax.ShapeDtypeStruct((depth, rows, n), F32),
        compiler_params=_params("arbitrary", "arbitrary"),
        name="modulation",
    )(cvec, w_mod, b_mod.reshape(depth, 1, n))


def _normed(x, g_row, mod, sh_row, sc_row):
    y = x * lax.rsqrt(jnp.mean(x * x, axis=-1, keepdims=True) + EPS) * g_row
    return y * (1.0 + mod[sc_row:sc_row + 1, :]) + mod[sh_row:sh_row + 1, :]


def _proj_columns(hb, w_ref, o_ref, start, width, rope_half, cos, sin, scale=None, chunk=512):
    for c in range(0, width, chunk):
        cw = min(chunk, width - c)
        acc = _dot(hb, w_ref[:, start + c:start + c + cw])
        if rope_half is not None:
            lane = lax.broadcasted_iota(jnp.int32, (1, cw), 1)
            rot = jnp.where(lane % (2 * rope_half) < rope_half, pltpu.roll(acc, cw - rope_half, axis=1),
                            pltpu.roll(acc, rope_half, axis=1))
            reps = cw // LANES
            acc = acc * jnp.tile(cos, (1, reps)) + rot * jnp.tile(sin, (1, reps))
        if scale is not None:
            acc = acc * scale
        o_ref[:, c:c + cw] = acc.astype(o_ref.dtype)


def _stream_specs(x, nt, sub):
    if isinstance(x, tuple):
        d = x[0].shape[-1]
        specs = []
        for s in range(sub):
            tile = lambda t, s=s: t * sub + s
            specs += [pl.BlockSpec((1, TM, d), lambda t, tile=tile: (tile(t) // nt, jnp.maximum(tile(t) % nt - 1, 0), 0)),
                      pl.BlockSpec((1, TM, d), lambda t, tile=tile: (tile(t) // nt, 0, 0))]
        return specs, list(x) * sub
    return [pl.BlockSpec((sub * TM, x.shape[1]), lambda t: (t, 0))], [x]


def _tiles_per_step(nt, nb, per_tile_tables=False, max_tiles=4):
    if per_tile_tables:
        return 1
    return next(k for k in (4, 2, 1) if k <= max_tiles and (nt * nb) % k == 0)


def _stream_tile(x_refs, nt, s, sub):
    if len(x_refs) == 1:
        return x_refs[0][s * TM:(s + 1) * TM, :]
    is_ctx = (pl.program_id(0) * sub + s) % nt == 0
    return jnp.where(is_ctx, x_refs[2 * s + 1][0], x_refs[2 * s][0])


def _normproj_kernel(*refs, outs, has_rope, n_x, nt, nb, sub):
    x_refs, refs = refs[:n_x], refs[n_x:]
    mods_ref, g_ref, w_ref = refs[:3]
    k = 3
    cos = sin = None
    if has_rope:
        cos, sin = refs[3][...], refs[4][...]
        k = 5
    hs = []
    for s in range(sub):
        mod = mods_ref[_mod_row(pl.program_id(0) * sub + s, nt, nb)]
        hs.append(_normed(_stream_tile(x_refs, nt, s, sub), g_ref[...], mod, 0, 1).astype(BF16))
    hb = hs[0] if sub == 1 else jnp.concatenate(hs, axis=0)
    for o_ref, spec in zip(refs[k:], outs):
        _proj_columns(hb, w_ref, o_ref, *spec[:3], cos, sin, scale=spec[3] if len(spec) > 3 else None)


def _normproj(x, mods, gain, w, outs, out_dtypes, nt, nb, tables=None):
    d, n = w.shape
    t_tok = nb * nt * TM
    sub = _tiles_per_step(nt, nb, per_tile_tables=tables is not None)
    rows = sub * TM
    x_specs, x_args = _stream_specs(x, nt, sub)
    in_specs = x_specs + [pl.BlockSpec(mods.shape, lambda t: (0, 0, 0)),
                          pl.BlockSpec((1, d), lambda t: (0, 0)),
                          pl.BlockSpec((d, n), lambda t: (0, 0))]
    args = x_args + [mods, gain.reshape(1, d), w]
    if tables is not None:
        in_specs += [pl.BlockSpec((TM, LANES), lambda t: (t % nt, 0))] * 2
        args += list(tables)
    return pl.pallas_call(
        functools.partial(_normproj_kernel, outs=tuple(outs), has_rope=tables is not None, n_x=len(x_args),
                          nt=nt, nb=nb, sub=sub),
        grid=(t_tok // rows,),
        in_specs=in_specs,
        out_specs=[pl.BlockSpec((rows, o[1]), lambda t: (t, 0)) for o in outs],
        out_shape=[jax.ShapeDtypeStruct((t_tok, o[1]), dt) for o, dt in zip(outs, out_dtypes)],
        compiler_params=_params("parallel"),
        name="normproj",
    )(*args)


def _outproj_kernel(*refs, prologue, n_t, n_tok, n_extra, n_x, nt, nb, sub):
    ins_t = refs[:n_t * sub]
    ins = refs[n_t * sub:n_t * sub + n_tok]
    extra = refs[n_t * sub + n_tok:n_t * sub + n_tok + n_extra]
    x_refs = refs[n_t * sub + n_tok + n_extra:n_t * sub + n_tok + n_extra + n_x]
    mods_ref, w_ref, o_ref = refs[n_t * sub + n_tok + n_extra + n_x:]
    parts = []
    for s in range(sub):
        tok = [r if sub == 1 else r.at[pl.ds(s * TM, TM), :] for r in ins]
        parts.append(prologue(*ins_t[s * n_t:(s + 1) * n_t], *tok, *extra))
    a = parts[0] if sub == 1 else jnp.concatenate(parts, axis=0)
    y = _dot(a, w_ref[...])
    for s in range(sub):
        rows = slice(s * TM, (s + 1) * TM)
        gate = mods_ref[_mod_row(pl.program_id(0) * sub + s, nt, nb)][2:3, :]
        o_ref[rows, :] = _stream_tile(x_refs, nt, s, sub) + gate * y[rows, :]


def _outproj(ins, in_widths, x, mods, w, prologue, nt, nb, extra=(), ins_t=()):
    d = w.shape[1]
    t_tok = nb * nt * TM
    sub = _tiles_per_step(nt, nb, max_tiles=2 if ins_t else 4)
    rows = sub * TM

    def tile_map(s):
        return lambda t: ((t * sub + s) // nt, 0, (t * sub + s) % nt)

    in_specs, args = [], []
    for s in range(sub):
        in_specs += [pl.BlockSpec((1, a.shape[1], TM), tile_map(s)) for a in ins_t]
        args += list(ins_t)
    in_specs += [pl.BlockSpec((rows, wd), lambda t: (t, 0)) for wd in in_widths]
    in_specs += [pl.BlockSpec(e.shape, lambda t: (0, 0)) for e in extra]
    x_specs, x_args = _stream_specs(x, nt, sub)
    n_before_x = len(in_specs)
    in_specs += x_specs + [pl.BlockSpec(mods.shape, lambda t: (0, 0, 0)), pl.BlockSpec(w.shape, lambda t: (0, 0))]
    return pl.pallas_call(
        functools.partial(_outproj_kernel, prologue=prologue, n_t=len(ins_t), n_tok=len(ins), n_extra=len(extra),
                          n_x=len(x_args), nt=nt, nb=nb, sub=sub),
        grid=(t_tok // rows,),
        in_specs=in_specs,
        out_specs=pl.BlockSpec((rows, d), lambda t: (t, 0)),
        out_shape=jax.ShapeDtypeStruct((t_tok, d), F32),
        input_output_aliases={} if isinstance(x, tuple) else {n_before_x: 0},
        compiler_params=_params("parallel"),
        name="outproj",
    )(*args, *ins, *extra, *x_args, mods, w)


def _group_rms(y, n_groups):
    width = y.shape[1] // n_groups
    parts = []
    for g in range(n_groups):
        yg = y[:, g * width:(g + 1) * width]
        parts.append(yg * lax.rsqrt(jnp.mean(yg * yg, axis=-1, keepdims=True) + EPS))
    return jnp.concatenate(parts, axis=1)


def _plain_prologue(o_ref):
    return o_ref[...]


def _ssm_prologue(yf_ref, yb_ref, z_ref, g_ref):
    y = jnp.transpose(yf_ref[0].astype(F32) + yb_ref[0].astype(F32)) * _silu(z_ref[...].astype(F32))
    return (_group_rms(y, SSM_GROUPS) * g_ref[...]).astype(BF16)


def _mlstm_prologue(hf_ref, hb_ref, o_ref, g_ref):
    h = jnp.transpose(hf_ref[0].astype(F32) + hb_ref[0].astype(F32))
    return (_group_rms(h, ML_HEADS) * g_ref[...] * _sigmoid(o_ref[...].astype(F32))).astype(BF16)


MOE_ROUTE_ROWS = 24


def _route(logits):
    lt = jnp.transpose(logits)[:MOE_ROUTE_ROWS, :]
    row = lax.broadcasted_iota(jnp.int32, lt.shape, 0).astype(F32)
    neg = -jnp.inf
    lg = jnp.where((row >= MOE_EXPERTS) & (row < MOE_EXPERTS + MOE_GROUPS), lt, neg)
    gmax = jnp.max(lg, axis=0, keepdims=True)
    g_sel = jnp.min(jnp.where(lg == gmax, row, LANES), axis=0, keepdims=True) - MOE_EXPERTS
    p_g = 1.0 / jnp.sum(jnp.exp(lg - gmax), axis=0, keepdims=True)
    in_group = (row >= g_sel * MOE_PER_GROUP) & (row < (g_sel + 1) * MOE_PER_GROUP)
    le = jnp.where(in_group, lt, neg)
    v1 = jnp.max(le, axis=0, keepdims=True)
    i1 = jnp.min(jnp.where(le == v1, row, LANES), axis=0, keepdims=True)
    le2 = jnp.where(row == i1, neg, le)
    v2 = jnp.max(le2, axis=0, keepdims=True)
    i2 = jnp.min(jnp.where(le2 == v2, row, LANES), axis=0, keepdims=True)
    e2 = jnp.exp(v2 - v1)
    w1 = p_g / (1.0 + e2)
    comb_t = jnp.where(row == i1, w1, 0.0) + jnp.where(row == i2, w1 * e2, 0.0)
    pad = jnp.zeros((LANES - MOE_ROUTE_ROWS, lt.shape[1]), F32)
    return jnp.transpose(jnp.concatenate([comb_t, pad], axis=0))


def _moe_kernel(x_ref, mods_ref, g_ref, wr_ref, br_ref, wg_ref, wu_ref, wd_ref, o_ref, h_scr, comb_scr,
                *, nt, nb, sub):
    i = pl.program_id(0)
    e = pl.program_id(1)

    def experts(rows):
        hb = h_scr[rows, :]
        comb = comb_scr[rows, :]
        lane = lax.broadcasted_iota(jnp.int32, comb.shape, 1)
        scaled = []
        for k in range(MOE_PER_GROUP):
            act = _silu(_dot(hb, wg_ref[0, k])) * _dot(hb, wu_ref[0, k])
            cw = jnp.sum(jnp.where(lane == e * MOE_PER_GROUP + k, comb, 0.0), axis=-1, keepdims=True)
            scaled.append((act * cw).astype(BF16))
        return _dot(jnp.concatenate(scaled, axis=1), wd_ref[0, 0])

    def gate_row(s):
        return mods_ref[_mod_row(i * sub + s, nt, nb)][5:6, :]

    @pl.when(e == 0)
    def _():
        w_hi, w_lo = _split_bf16(wr_ref[...])
        w_both = jnp.concatenate([w_hi, w_lo], axis=1)
        def prepare(s):
            rows = slice(s * TM, (s + 1) * TM)
            mod = mods_ref[_mod_row(i * sub + s, nt, nb)]
            h = _normed(x_ref[rows, :], g_ref[...], mod, 3, 4)
            h_hi, h_lo = _split_bf16(h)
            h_scr[rows, :] = h_hi
            both = _dot(h_hi, w_both)
            logits = both[:, :LANES] + both[:, LANES:] + _dot(h_lo, w_hi) + br_ref[...]
            comb_scr[rows, :] = _route(logits)

        for s in range(sub):
            prepare(s)
            rows = slice(s * TM, (s + 1) * TM)
            o_ref[rows, :] = x_ref[rows, :] + gate_row(s) * experts(rows)

    @pl.when(e > 0)
    def _():
        y = experts(slice(0, sub * TM))
        for s in range(sub):
            rows = slice(s * TM, (s + 1) * TM)
            o_ref[rows, :] += gate_row(s) * y[rows, :]


def _moe(x, mods, gain, w_router, b_router, w_gate, w_up, w_down, layer, nt, nb):
    t_tok, d = x.shape
    pg = MOE_PER_GROUP
    rows = MOE_ROWS if t_tok % MOE_ROWS == 0 else TM
    sub = rows // TM
    return pl.pallas_call(
        functools.partial(_moe_kernel, nt=nt, nb=nb, sub=sub),
        grid=(t_tok // rows, MOE_GROUPS),
        in_specs=[pl.BlockSpec((rows, d), lambda i, e: (i, 0)),
                  pl.BlockSpec(mods.shape, lambda i, e: (0, 0, 0)),
                  pl.BlockSpec((1, d), lambda i, e: (0, 0)),
                  pl.BlockSpec((d, LANES), lambda i, e: (0, 0)),
                  pl.BlockSpec((1, LANES), lambda i, e: (0, 0)),
                  pl.BlockSpec((1, pg, d, MOE_FF), lambda i, e: (layer, e, 0, 0)),
                  pl.BlockSpec((1, pg, d, MOE_FF), lambda i, e: (layer, e, 0, 0)),
                  pl.BlockSpec((1, 1, pg * MOE_FF, d), lambda i, e: (layer, e, 0, 0))],
        out_specs=pl.BlockSpec((rows, d), lambda i, e: (i, 0)),
        out_shape=jax.ShapeDtypeStruct((t_tok, d), F32),
        scratch_shapes=[pltpu.VMEM((rows, d), BF16), pltpu.VMEM((rows, LANES), F32)],
        input_output_aliases={0: 0},
        compiler_params=_params("parallel", "arbitrary"),
        name="moe",
    )(x, mods, gain.reshape(1, d), w_router, b_router, w_gate, w_up, w_down)


def _attn_kernel(sink_ref, q_ref, kp_ref, kc_ref, kn_ref, vp_ref, vc_ref, vn_ref, kx_ref, vx_ref, o_ref, *, n_lat):
    j = pl.program_id(1)
    jl = j - 1
    dh = ATT_HEAD_DIM
    n_loc = TM + 2 * CHUNK
    k_all = jnp.concatenate([kp_ref[...], kc_ref[...], kn_ref[...], kx_ref[...]], axis=0)
    q_t = jnp.transpose(q_ref[...].astype(F32)).astype(BF16)
    n_keys = k_all.shape[0]
    kj = lax.broadcasted_iota(jnp.int32, (n_keys, TM), 0)
    qi = lax.broadcasted_iota(jnp.int32, (n_keys, TM), 1)
    kpos = jl * TM - CHUNK + kj
    local_ok = (kj >= qi + CHUNK - WINDOW) & (kj <= qi + CHUNK + WINDOW) & (kpos >= 0) & (kpos < n_lat) & (jl >= 0)
    bias = jnp.where((kj >= n_loc) | local_ok, 0.0, -jnp.inf)
    uh = ATT_UNIT_HEADS
    bias = jnp.concatenate([bias] * uh, axis=1)
    lane_head = lax.broadcasted_iota(jnp.int32, (1, uh * TM), 1) // TM
    zeros_q = jnp.zeros((dh, uh * TM), BF16)
    units = [(h0 // ATT_GROUP, h0) for h0 in range(0, ATT_HEADS, uh)]

    def scores(u):
        g, h0 = units[u]
        half = g % 2
        k_pair = k_all[:, (g - half) * dh:(g - half + 2) * dh]
        qg = jnp.concatenate([q_t[(h0 + a) * dh:(h0 + a + 1) * dh, :] for a in range(uh)], axis=1)
        q_m = jnp.concatenate([zeros_q, qg] if half else [qg, zeros_q], axis=0)
        return _dot(k_pair, q_m) + bias

    outs = []
    s_next = scores(0)
    v_t = jnp.transpose(jnp.concatenate([vp_ref[...], vc_ref[...], vn_ref[...], vx_ref[...]],
                                        axis=0).astype(F32)).astype(BF16)
    for u, (g, h0) in enumerate(units):
        s = s_next
        if u + 1 < len(units):
            s_next = scores(u + 1)
        sink = jnp.zeros((1, uh * TM), F32)
        for a in range(uh):
            sink = jnp.where(lane_head == a, sink_ref[h0 + a] * LOG2_E, sink)
        m = jnp.maximum(jnp.max(s, axis=0, keepdims=True), sink)
        p = jnp.exp2(s - m)
        den = jnp.sum(p, axis=0, keepdims=True) + jnp.exp2(sink - m)
        og = _dot(v_t[g * dh:(g + 1) * dh, :], p.astype(BF16)) / den
        outs += [og[:, a * TM:(a + 1) * TM] for a in range(uh)]
    o_ref[...] = jnp.transpose(jnp.concatenate(outs, axis=0)).astype(o_ref.dtype)


def _windowed_attention(q, k, v, sink, nb, seg):
    t_tok = q.shape[0]
    nchunk = seg // CHUNK
    nt = seg // TM
    per_tile = TM // CHUNK
    kvw = ATT_KV_HEADS * ATT_HEAD_DIM

    def halo(off):
        return lambda b, j, *_: (b * nchunk + jnp.clip(j * per_tile + off, per_tile, nchunk - 1), 0)

    tile = lambda b, j, *_: (b * nt + j, 0)
    kv_spec = [pl.BlockSpec((CHUNK, kvw), halo(-1)), pl.BlockSpec((TM, kvw), tile),
               pl.BlockSpec((CHUNK, kvw), halo(per_tile))]
    ctx_spec = pl.BlockSpec((TM, kvw), lambda b, j, *_: (b * nt, 0))
    return pl.pallas_call(
        functools.partial(_attn_kernel, n_lat=seg - TM),
        grid_spec=pltpu.PrefetchScalarGridSpec(
            num_scalar_prefetch=1,
            grid=(nb, nt),
            in_specs=[pl.BlockSpec((TM, q.shape[1]), tile)] + kv_spec + kv_spec + [ctx_spec, ctx_spec],
            out_specs=pl.BlockSpec((TM, q.shape[1]), tile),
        ),
        out_shape=jax.ShapeDtypeStruct((t_tok, q.shape[1]), BF16),
        compiler_params=_params("parallel", "parallel"),
        name="windowed_attention",
    )(sink, q, k, k, k, v, v, v, k, v)


def _ssm_in_kernel(x_ref, xp_ref, xn_ref, mods_ref, g_ref, w_ref, cw_ref, cb_ref, dtb_ref, aneg_ref,
                   z_ref, xs_ref, bm_ref, cm_ref, dtv_ref, a_ref, *ext, nt, nb):
    t = pl.program_id(0)
    tl = t % nt
    halo = 8
    first = (tl == 0) | (tl == 1)
    last = (tl == 0) | (tl == nt - 1)
    gain = g_ref[...]

    def normed(x, tile):
        return _normed(x, gain, mods_ref[_mod_row(tile, nt, nb)], 0, 1)

    h = normed(x_ref[...], t)
    h_prev = jnp.where(first, 0.0, normed(xp_ref[...], jnp.maximum(t - 1, 0)))
    h_next = jnp.where(last, 0.0, normed(xn_ref[...], jnp.minimum(t + 1, pl.num_programs(0) - 1)))
    hb = h.astype(BF16)
    hb_ext = jnp.concatenate([h_prev, h, h_next], axis=0).astype(BF16)
    chunk = SSM_CONV_CHUNK
    pad = SSM_CONV // 2

    def project(c):
        proj = _dot(hb_ext, w_ref[:, SSM_D_INNER + c:SSM_D_INNER + c + chunk])
        ext[c // chunk][...] = proj

    project(0)
    for c in range(0, SSM_CONV_DIM, chunk):
        if c + chunk < SSM_CONV_DIM:
            project(c + chunk)
        if c < SSM_D_INNER:
            z_ref[:, c:c + chunk] = _dot(hb, w_ref[:, c:c + chunk]).astype(z_ref.dtype)
        acc = jnp.broadcast_to(cb_ref[:, c:c + chunk], (TM, chunk))
        for k in range(SSM_CONV):
            acc = acc + cw_ref[k:k + 1, c:c + chunk] * ext[c // chunk][halo - pad + k:halo - pad + k + TM, :]
        y = _silu(acc)
        if c < SSM_D_INNER:
            xs_ref[0, c:c + chunk, :] = jnp.transpose(y).astype(xs_ref.dtype)
        elif c < SSM_D_INNER + SSM_BC:
            bm_ref[:, c - SSM_D_INNER:c - SSM_D_INNER + chunk] = y.astype(bm_ref.dtype)
        else:
            off = c - SSM_D_INNER - SSM_BC
            cm_ref[0, off:off + chunk, :] = jnp.transpose(y).astype(cm_ref.dtype)
    dt = _dot(hb, w_ref[:, SSM_D_INNER + SSM_CONV_DIM:])
    dtv = _softplus(dt + dtb_ref[...])
    dtv_ref[0] = jnp.transpose(dtv)
    a_ref[0] = jnp.transpose(dtv * aneg_ref[...] * LOG2_E)


def _ssm_in(x, mods, gain, w, conv_w, conv_b, dt_bias, a_neg, nt, nb):
    t_tok, d = x.shape
    n8 = t_tok // 8
    seg = nt * TM
    row = lambda t: (t, 0)
    col = lambda t: (t // nt, 0, t % nt)
    fixed = lambda t: (0, 0)
    return pl.pallas_call(
        functools.partial(_ssm_in_kernel, nt=nt, nb=nb),
        grid=(t_tok // TM,),
        in_specs=[pl.BlockSpec((TM, d), row),
                  pl.BlockSpec((8, d), lambda t: (jnp.maximum(t * (TM // 8) - 1, 0), 0)),
                  pl.BlockSpec((8, d), lambda t: (jnp.minimum((t + 1) * (TM // 8), n8 - 1), 0)),
                  pl.BlockSpec(mods.shape, lambda t: (0, 0, 0)),
                  pl.BlockSpec((1, d), fixed),
                  pl.BlockSpec(w.shape, fixed),
                  pl.BlockSpec((8, SSM_CONV_DIM), fixed),
                  pl.BlockSpec((1, SSM_CONV_DIM), fixed),
                  pl.BlockSpec((1, LANES), fixed),
                  pl.BlockSpec((1, LANES), fixed)],
        out_specs=[pl.BlockSpec((TM, SSM_D_INNER), row),
                   pl.BlockSpec((1, SSM_D_INNER, TM), col), pl.BlockSpec((TM, SSM_BC), row),
                   pl.BlockSpec((1, SSM_BC, TM), col), pl.BlockSpec((1, LANES, TM), col),
                   pl.BlockSpec((1, LANES, TM), col)],
        out_shape=[jax.ShapeDtypeStruct((t_tok, SSM_D_INNER), BF16),
                   jax.ShapeDtypeStruct((nb, SSM_D_INNER, seg), BF16), jax.ShapeDtypeStruct((t_tok, SSM_BC), BF16),
                   jax.ShapeDtypeStruct((nb, SSM_BC, seg), BF16), jax.ShapeDtypeStruct((nb, LANES, seg), F32),
                   jax.ShapeDtypeStruct((nb, LANES, seg), F32)],
        scratch_shapes=[pltpu.VMEM((TM + 16, SSM_CONV_CHUNK), F32)] * (SSM_CONV_DIM // SSM_CONV_CHUNK),
        compiler_params=_params("parallel"),
        name="ssm_in",
    )(x, x, x, mods, gain.reshape(1, d), w, conv_w, conv_b, dt_bias, a_neg)


def _tri(lower, n):
    r = lax.broadcasted_iota(jnp.int32, (n, n), 0)
    c = lax.broadcasted_iota(jnp.int32, (n, n), 1)
    return (c <= r) if lower else (c >= r)


def _ssd_prepare(dt_ref, a_ref, d):
    a_t = a_ref[0]
    dt_t = dt_ref[0]
    upper = jnp.where(_tri(False, SSD_CHUNK), 1.0, 0.0).astype(BF16)
    a_hi, a_lo = _split_bf16(a_t)
    cum = _dot(a_hi, upper) + _dot(a_lo, upper)
    cum_end = cum[:, SSD_CHUNK - 1:SSD_CHUNK]
    if d == 0:
        lane_v, sub_v = cum, -cum
        inter = jnp.exp2(cum)
        w_upd = jnp.exp2(cum_end - cum) * dt_t
    else:
        ecum = cum - a_t
        lane_v, sub_v = -ecum, ecum
        inter = jnp.exp2(cum_end - ecum)
        w_upd = jnp.exp2(ecum) * dt_t
    sub_c = jnp.transpose(sub_v)
    decay_end = jnp.exp2(cum_end)
    return dt_t, lane_v, sub_c, inter, w_upd, decay_end


def _ssd_groups(prep, xs_ref, bm_ref, ct_ref, dsk_ref, st_ref, y_ref, d):
    dt_t, lane_v, sub_c, inter, w_upd, decay_end = prep
    mask = _tri(d == 1, SSD_CHUNK)
    p = SSM_HEAD_DIM
    for g in range(SSM_GROUPS):
        bg = bm_ref[:, g * SSM_STATE:(g + 1) * SSM_STATE]
        cg_t = ct_ref[0, g * SSM_STATE:(g + 1) * SSM_STATE, :]
        cb_t = _dot(bg, cg_t)
        state = st_ref[g]
        y_in = _dot(state.astype(BF16), cg_t)
        upd = []
        for e in range(SSM_HPG):
            h = g * SSM_HPG + e
            ln = d * SSM_HEADS + h
            rows = slice(e * p, (e + 1) * p)
            blk = SSD_DECAY_BLOCK
            block_rows = []
            for bs in range(0, SSD_CHUNK, blk):
                block_cols = []
                for bl in range(0, SSD_CHUNK, blk):
                    if (bs > bl) if d == 0 else (bs < bl):
                        block_cols.append(jnp.zeros((blk, blk), BF16))
                        continue
                    seg = sub_c[bs:bs + blk, ln:ln + 1] + lane_v[ln:ln + 1, bl:bl + blk]
                    if bs == bl:
                        seg = jnp.where(mask[bs:bs + blk, bl:bl + blk], seg, -jnp.inf)
                    block_cols.append((cb_t[bs:bs + blk, bl:bl + blk] * jnp.exp2(seg)).astype(BF16))
                block_rows.append(jnp.concatenate(block_cols, axis=1))
            m_t = jnp.concatenate(block_rows, axis=0)
            xf = xs_ref[0, h * p:(h + 1) * p, :].astype(F32)
            u = (xf * dt_t[ln:ln + 1, :]).astype(BF16)
            yh = _dot(u, m_t) + y_in[rows, :] * inter[ln:ln + 1, :] + dsk_ref[d, h] * xf
            y_ref[0, h * p:(h + 1) * p, :] = yh.astype(y_ref.dtype)
            upd.append((xf * w_upd[ln:ln + 1, :]).astype(BF16))
        new = _dot(jnp.concatenate(upd, axis=0), bg)
        for e in range(SSM_HPG):
            ln = d * SSM_HEADS + g * SSM_HPG + e
            rows = slice(e * p, (e + 1) * p)
            st_ref[g, rows, :] = state[rows, :] * decay_end[ln:ln + 1, :] + new[rows, :]


def _ssd_kernel(dsk_ref, xsf, bmf, ctf, dtf, af, xsb, bmb, ctb, dtb, ab, yf_ref, yb_ref, stf, stb):
    @pl.when(pl.program_id(1) == 0)
    def _():
        stf[...] = jnp.zeros_like(stf)
        stb[...] = jnp.zeros_like(stb)

    prep_f = _ssd_prepare(dtf, af, 0)
    prep_b = _ssd_prepare(dtb, ab, 1)
    _ssd_groups(prep_f, xsf, bmf, ctf, dsk_ref, stf, yf_ref, 0)
    _ssd_groups(prep_b, xsb, bmb, ctb, dsk_ref, stb, yb_ref, 1)


def _scan_maps(nchunk, chunk):
    ctx_chunks = TM // chunk
    fwd = lambda b, t: (b * nchunk + t, 0)
    bwd = lambda b, t: (b * nchunk + jnp.where(t < ctx_chunks, ctx_chunks - 1 - t, nchunk + ctx_chunks - 1 - t), 0)
    return fwd, bwd


def _ssd_scan(xs_t, bm, c_t, dt_t, a_t, d_skip, nb, seg):
    nchunk = seg // SSD_CHUNK
    fwd_rows, bwd_rows = _scan_maps(nchunk, SSD_CHUNK)

    def specs(rows_map):
        cols_map = lambda b, t, *_: (b, 0, rows_map(b, t)[0] - b * nchunk)
        return [pl.BlockSpec((1, SSM_D_INNER, SSD_CHUNK), cols_map),
                pl.BlockSpec((SSD_CHUNK, SSM_BC), lambda b, t, *_: rows_map(b, t)),
                pl.BlockSpec((1, SSM_BC, SSD_CHUNK), cols_map),
                pl.BlockSpec((1, LANES, SSD_CHUNK), cols_map),
                pl.BlockSpec((1, LANES, SSD_CHUNK), cols_map)]

    arrs = (xs_t, bm, c_t, dt_t, a_t)
    return pl.pallas_call(
        _ssd_kernel,
        grid_spec=pltpu.PrefetchScalarGridSpec(
            num_scalar_prefetch=1,
            grid=(nb, nchunk),
            in_specs=specs(fwd_rows) + specs(bwd_rows),
            out_specs=[specs(fwd_rows)[0], specs(bwd_rows)[0]],
            scratch_shapes=[pltpu.VMEM((SSM_GROUPS, SSM_HPG * SSM_HEAD_DIM, SSM_STATE), F32)] * 2,
        ),
        out_shape=[jax.ShapeDtypeStruct(xs_t.shape, BF16)] * 2,
        compiler_params=_params("parallel", "arbitrary"),
        name="ssd_scan",
    )(d_skip, *arrs, *arrs)


def _log_sigmoid(x):
    return jnp.minimum(x, 0.0) - jnp.log1p(jnp.exp(-jnp.abs(x)))


ML_STATE_ROWS = ML_V_DIM + 16


def _mlstm_prepare(qk_ref, v_ref, g_ref, gb_ref, m_st, d):
    nh, dk, dv = ML_HEADS, ML_QK_DIM, ML_V_DIM
    g_t = jnp.transpose(g_ref[...] + gb_ref[...])
    ig = g_t[16 * d:16 * d + nh, :]
    lf = _log_sigmoid(g_t[16 * d + nh:16 * d + 2 * nh, :])
    upper = jnp.where(_tri(False, ML_CHUNK), 1.0, 0.0).astype(BF16)
    lf_hi, lf_lo = _split_bf16(lf)
    fc = _dot(lf_hi, upper) + _dot(lf_lo, upper)
    tot = fc[:, ML_CHUNK - 1:ML_CHUNK]
    m_prev = m_st[...]
    if d == 0:
        lane_v, sub_v = fc, ig - fc
        inter = fc + m_prev
        logw = tot - fc + ig
    else:
        ec = fc - lf
        lane_v, sub_v = -ec, ec + ig
        inter = tot - ec + m_prev
        logw = ec + ig
    sub_c = jnp.transpose(jnp.concatenate([sub_v, jnp.zeros((LANES - nh, ML_CHUNK), F32)], axis=0))
    m_new = jnp.maximum(tot + m_prev, jnp.max(logw, axis=-1, keepdims=True))
    ws = jnp.exp(logw - m_new)
    cw = jnp.exp(tot + m_prev - m_new)
    q_t = jnp.transpose(qk_ref[:, :nh * dk].astype(F32))
    v_t = jnp.transpose(v_ref[...].astype(F32))
    return lane_v, sub_c, inter, ws, cw, m_new, q_t, v_t


def _mlstm_heads(prep, qk_ref, c_st, m_st, h_ref, d):
    nh, dk, dv = ML_HEADS, ML_QK_DIM, ML_V_DIM
    lane_v, sub_c, inter, ws, cw, m_new, q_t, v_t = prep
    mask = _tri(d == 1, ML_CHUNK)
    lane = lax.broadcasted_iota(jnp.int32, (1, LANES), 1)
    zeros_q = jnp.zeros((dk, ML_CHUNK), F32)
    for h in range(nh):
        half = h % 2
        k_pair = qk_ref[:, nh * dk + (h - half) * dk:nh * dk + (h - half + 2) * dk] * (dk ** -0.5)
        qh = q_t[h * dk:(h + 1) * dk, :]
        q_m = jnp.concatenate([zeros_q, qh] if half else [qh, zeros_q], axis=0).astype(BF16)
        logd = jnp.where(mask, sub_c[:, h:h + 1] + lane_v[h:h + 1, :], -jnp.inf)
        mt = jnp.maximum(inter[h:h + 1, :], jnp.max(logd, axis=0, keepdims=True))
        sc = _dot(k_pair, q_m) * jnp.exp(logd - mt)
        vh = v_t[h * dv:(h + 1) * dv, :]
        state = c_st[h]
        cq = _dot(state.astype(BF16), q_m)
        w_int = jnp.exp(inter[h:h + 1, :] - mt)
        num = _dot(vh.astype(BF16), sc.astype(BF16)) + w_int * cq[:dv, :]
        den = jnp.sum(sc, axis=0, keepdims=True) + w_int * cq[dv:dv + 1, :]
        h_ref[0, h * dv:(h + 1) * dv, :] = (num / jnp.maximum(jnp.abs(den), jnp.exp(-mt))).astype(h_ref.dtype)
        ws_h = ws[h:h + 1, :]
        lhs = jnp.concatenate([vh * ws_h, jnp.broadcast_to(ws_h, (ML_STATE_ROWS - dv, ML_CHUNK))], axis=0)
        own = (lane >= half * dk) & (lane < (half + 1) * dk)
        row = lax.broadcasted_iota(jnp.int32, (ML_STATE_ROWS, 1), 0)
        new = cw[h:h + 1, :LANES] * state + _dot(lhs.astype(BF16), k_pair)
        c_st[h] = jnp.where(own & (row <= dv), new, 0.0)
    m_st[...] = jnp.broadcast_to(m_new, m_st.shape)


def _mlstm_kernel(qkf, vf, gf, qkb, vb, gb, gbias, hf_ref, hb_ref, cf, mf, cb, mb):
    @pl.when(pl.program_id(1) == 0)
    def _():
        for r in (cf, mf, cb, mb):
            r[...] = jnp.zeros_like(r)

    prep_f = _mlstm_prepare(qkf, vf, gf, gbias, mf, 0)
    prep_b = _mlstm_prepare(qkb, vb, gb, gbias, mb, 1)
    _mlstm_heads(prep_f, qkf, cf, mf, hf_ref, 0)
    _mlstm_heads(prep_b, qkb, cb, mb, hb_ref, 1)


def _mlstm_scan(qk, v, g, gate_b, nb, seg):
    t_tok = qk.shape[0]
    nchunk = seg // ML_CHUNK
    fwd, bwd = _scan_maps(nchunk, ML_CHUNK)
    widths = (qk.shape[1], v.shape[1], LANES)
    in_specs = ([pl.BlockSpec((ML_CHUNK, w), fwd) for w in widths] + [pl.BlockSpec((ML_CHUNK, w), bwd) for w in widths]
                + [pl.BlockSpec((1, LANES), lambda b, t: (0, 0))])
    state = [pltpu.VMEM((ML_HEADS, ML_STATE_ROWS, LANES), F32), pltpu.VMEM((ML_HEADS, ML_CHUNK), F32)]
    out_map = lambda rows_map: (lambda b, t: (b, 0, rows_map(b, t)[0] - b * nchunk))
    return pl.pallas_call(
        _mlstm_kernel,
        grid=(nb, nchunk),
        in_specs=in_specs,
        out_specs=[pl.BlockSpec((1, v.shape[1], ML_CHUNK), out_map(fwd)),
                   pl.BlockSpec((1, v.shape[1], ML_CHUNK), out_map(bwd))],
        out_shape=[jax.ShapeDtypeStruct((nb, v.shape[1], seg), BF16)] * 2,
        scratch_shapes=state + state,
        compiler_params=_params("parallel", "arbitrary"),
        name="mlstm_scan",
    )(qk, v, g, qk, v, g, gate_b)


def _rms_rows(x, g_row):
    return x * lax.rsqrt(jnp.mean(x * x, axis=-1, keepdims=True) + EPS) * g_row


MLA_SCORE_SCALE = (MLA_NOPE + MLA_ROPE) ** -0.5 * LOG2_E


def _mla_queries(cq, g_ref, wt_ref, cos_ref, sin_ref, qt_ref):
    cq_t = jnp.transpose(_rms_rows(cq, g_ref[...])).astype(BF16)
    q_t = _dot(wt_ref[...], cq_t) * MLA_SCORE_SCALE
    qt_ref[0] = q_t.astype(qt_ref.dtype)
    cos, sin = cos_ref[...], sin_ref[...]
    half = MLA_ROPE // 2
    for h in range(MLA_HEADS):
        r0 = h * LANES + MLA_NOPE
        x1, x2 = q_t[r0:r0 + half, :], q_t[r0 + half:r0 + MLA_ROPE, :]
        qt_ref[0, r0:r0 + half, :] = (x1 * cos - x2 * sin).astype(qt_ref.dtype)
        qt_ref[0, r0 + half:r0 + MLA_ROPE, :] = (x2 * cos + x1 * sin).astype(qt_ref.dtype)


def _mla_keys_values(ckv, kr, g_ref, wk_ref, wvt_ref, cos_ref, sin_ref, k_ref, vt_ref):
    cn = _rms_rows(ckv, g_ref[...])
    vt_ref[0] = _dot(wvt_ref[...], jnp.transpose(cn).astype(BF16)).astype(vt_ref.dtype)
    kn = _dot(cn.astype(BF16), wk_ref[...])
    kr = pltpu.roll(kr, MLA_NOPE, axis=1)
    half = MLA_ROPE // 2
    lane = lax.broadcasted_iota(jnp.int32, (1, LANES), 1)
    partner = jnp.where(lane < MLA_NOPE + half, pltpu.roll(kr, LANES - half, axis=1), pltpu.roll(kr, half, axis=1))
    roped = kr * cos_ref[...] + partner * sin_ref[...]
    for h in range(MLA_HEADS):
        k_ref[:, h * LANES:(h + 1) * LANES] = (kn[:, h * LANES:(h + 1) * LANES] + roped).astype(k_ref.dtype)


def _mla_in_kernel(x_ref, mods_ref, g_ref, w_ref, gq_ref, wqt_ref, gkv_ref, wk_ref, wvt_ref,
                   cos_t_ref, sin_t_ref, cos_ref, sin_ref, qt_ref, k_ref, vt_ref, *, nt, nb):
    mod = mods_ref[_mod_row(pl.program_id(0), nt, nb)]
    hb = _normed(x_ref[...], g_ref[...], mod, 0, 1).astype(BF16)
    p = _dot(hb, w_ref[...])
    _mla_queries(p[:, :MLA_RANK], gq_ref, wqt_ref, cos_t_ref, sin_t_ref, qt_ref)
    _mla_keys_values(p[:, MLA_RANK:2 * MLA_RANK], p[:, 2 * MLA_RANK:], gkv_ref, wk_ref, wvt_ref,
                     cos_ref, sin_ref, k_ref, vt_ref)


def _mla_in(x, mods, gain, w_in, gain_q, wq_t, gain_kv, w_k, wv_t, tables_t, tables, nt, nb):
    t_tok, d = x.shape
    seg = nt * TM
    half = MLA_ROPE // 2
    fixed = lambda t: (0, 0)
    return pl.pallas_call(
        functools.partial(_mla_in_kernel, nt=nt, nb=nb),
        grid=(t_tok // TM,),
        in_specs=[pl.BlockSpec((TM, d), lambda t: (t, 0)),
                  pl.BlockSpec(mods.shape, lambda t: (0, 0, 0)),
                  pl.BlockSpec((1, d), fixed),
                  pl.BlockSpec(w_in.shape, fixed),
                  pl.BlockSpec((1, MLA_RANK), fixed),
                  pl.BlockSpec(wq_t.shape, fixed),
                  pl.BlockSpec((1, MLA_RANK), fixed),
                  pl.BlockSpec(w_k.shape, fixed),
                  pl.BlockSpec(wv_t.shape, fixed),
                  pl.BlockSpec((half, TM), lambda t: (0, t % nt)),
                  pl.BlockSpec((half, TM), lambda t: (0, t % nt)),
                  pl.BlockSpec((TM, LANES), lambda t: (t % nt, 0)),
                  pl.BlockSpec((TM, LANES), lambda t: (t % nt, 0))],
        out_specs=[pl.BlockSpec((1, MLA_HEADS * LANES, TM), lambda t: (t, 0, 0)),
                   pl.BlockSpec((TM, MLA_HEADS * LANES), lambda t: (t, 0)),
                   pl.BlockSpec((1, MLA_HEADS * MLA_V, TM), lambda t: (t // nt, 0, t % nt))],
        out_shape=[jax.ShapeDtypeStruct((t_tok // TM, MLA_HEADS * LANES, TM), BF16),
                   jax.ShapeDtypeStruct((t_tok, MLA_HEADS * LANES), BF16),
                   jax.ShapeDtypeStruct((nb, MLA_HEADS * MLA_V, seg), BF16)],
        compiler_params=_params("parallel"),
        name="mla_in",
    )(x, mods, gain.reshape(1, d), w_in, gain_q.reshape(1, MLA_RANK), wq_t, gain_kv.reshape(1, MLA_RANK), w_k, wv_t,
      *tables_t, *tables)


def _mla_attn_kernel(qt_ref, k_ref, vt_ref, o_ref):
    @pl.when(pl.program_id(2) == 0)
    def _():
        o_ref[...] = jnp.zeros_like(o_ref)

    @pl.when(pl.program_id(2) > 0)
    def _():
        heads = range(MLA_HEADS_PER_STEP)
        seg = k_ref.shape[0]
        tk = MLA_KEY_TILE if (seg - TM) % MLA_KEY_TILE == 0 else TM
        tiles = [(0, TM)] + [(a, a + tk) for a in range(TM, seg, tk)]

        def scores(hh, j):
            return _dot(k_ref[tiles[j][0]:tiles[j][1], hh * LANES:(hh + 1) * LANES],
                        qt_ref[0, hh * LANES:(hh + 1) * LANES, :])

        m = [jnp.full((1, TM), -jnp.inf, F32) for _ in heads]
        l = [jnp.zeros((1, TM), F32) for _ in heads]
        acc = [jnp.zeros((MLA_V, TM), F32) for _ in heads]
        ahead = min(MLA_LOOKAHEAD, len(tiles))
        pending = [[scores(hh, j) for hh in heads] for j in range(ahead)]
        for j in range(len(tiles)):
            s_cur = pending.pop(0)
            if j + ahead < len(tiles):
                pending.append([scores(hh, j + ahead) for hh in heads])
            for hh in heads:
                m_new = jnp.maximum(m[hh], jnp.max(s_cur[hh], axis=0, keepdims=True))
                alpha = jnp.exp2(m[hh] - m_new)
                p = jnp.exp2(s_cur[hh] - m_new)
                l[hh] = alpha * l[hh] + jnp.sum(p, axis=0, keepdims=True)
                v_t = vt_ref[0, hh * MLA_V:(hh + 1) * MLA_V, tiles[j][0]:tiles[j][1]]
                acc[hh] = alpha * acc[hh] + _dot(v_t, p.astype(BF16))
                m[hh] = m_new
        o_t = jnp.concatenate([acc[hh] / l[hh] for hh in heads], axis=0)
        o_ref[...] = jnp.transpose(o_t).astype(o_ref.dtype)


def _mla_attention(q_t, k, v_t, nb, seg):
    t_tok = k.shape[0]
    nt = seg // TM
    hps = MLA_HEADS_PER_STEP
    return pl.pallas_call(
        _mla_attn_kernel,
        grid=(nb, MLA_HEADS // hps, nt),
        in_specs=[pl.BlockSpec((1, hps * LANES, TM), lambda b, hp, j: (b * nt + j, hp, 0)),
                  pl.BlockSpec((seg, hps * LANES), lambda b, hp, j: (b, hp)),
                  pl.BlockSpec((1, hps * MLA_V, seg), lambda b, hp, j: (b, hp, 0))],
        out_specs=pl.BlockSpec((TM, hps * MLA_V), lambda b, hp, j: (b * nt + j, hp)),
        out_shape=jax.ShapeDtypeStruct((t_tok, MLA_HEADS * MLA_V), BF16),
        compiler_params=_params("parallel", "parallel", "arbitrary"),
        name="mla_attention",
    )(q_t, k, v_t)


def _final_kernel(x_ref, g_ref, o_ref):
    o_ref[0] = _rms_rows(x_ref[...], g_ref[...])


def _final_norm(x, gain, nb, seg):
    d = x.shape[1]
    nt = seg // TM
    return pl.pallas_call(
        _final_kernel,
        grid=(nb, nt - 1),
        in_specs=[pl.BlockSpec((TM, d), lambda b, j: (b * nt + 1 + j, 0)),
                  pl.BlockSpec((1, d), lambda b, j: (0, 0))],
        out_specs=pl.BlockSpec((1, TM, d), lambda b, j: (b, j, 0)),
        out_shape=jax.ShapeDtypeStruct((nb, seg - TM, d), F32),
        compiler_params=_params("parallel", "parallel"),
        name="final_norm",
    )(x, gain.reshape(1, d))


def _rope_angles(n_lat, rot_dim):
    rows = n_lat // GRID_W
    row = jnp.repeat(jnp.arange(rows), GRID_W).astype(F32)
    col = jnp.tile(jnp.arange(GRID_W), rows).astype(F32)
    quarter = rot_dim // 4
    inv = ROPE_BASE ** (-jnp.arange(quarter, dtype=F32) / quarter)
    ang = jnp.concatenate([row[:, None] * inv, col[:, None] * inv], axis=-1)
    return jnp.cos(ang), jnp.sin(ang)


def _with_ctx_rows(tab, fill):
    return jnp.concatenate([jnp.full((TM, tab.shape[1]), fill, F32), tab], axis=0)


def _attn_rope_tables(n_lat):
    cos, sin = _rope_angles(n_lat, ATT_HEAD_DIM)
    cos_h = jnp.concatenate([cos, cos], axis=1)
    sin_h = jnp.concatenate([-sin, sin], axis=1)
    reps = LANES // ATT_HEAD_DIM
    return (_with_ctx_rows(jnp.tile(cos_h, (1, reps)), 1.0), _with_ctx_rows(jnp.tile(sin_h, (1, reps)), 0.0))


def _mla_k_tables(n_lat):
    cos, sin = _rope_angles(n_lat, MLA_ROPE)
    lo = jnp.zeros((n_lat, MLA_NOPE), F32)
    hi = jnp.zeros((n_lat, LANES - MLA_NOPE - MLA_ROPE), F32)
    cos_c = jnp.concatenate([lo, cos, cos, hi], axis=1)
    sin_c = jnp.concatenate([lo, -sin, sin, hi], axis=1)
    ctx_cos = jnp.concatenate([lo[:TM], jnp.ones((TM, MLA_ROPE), F32), hi[:TM]], axis=1)
    return (jnp.concatenate([ctx_cos, cos_c], axis=0), _with_ctx_rows(sin_c, 0.0))


def _pad_heads(w, real):
    r = w.shape[0]
    w3 = w.reshape(r, MLA_HEADS, real)
    return jnp.pad(w3, ((0, 0), (0, 0), (0, LANES - real))).reshape(r, MLA_HEADS * LANES)


def kernel(x, c, ctx, c_ctx, norm1_g, norm2_g, w_mod, b_mod, moe_w_group, moe_b_group, moe_w_expert, moe_b_expert, moe_w_gate, moe_w_up, moe_w_down, attn_w_in, attn_sink, attn_w_out, ssm_w_in, ssm_conv_w, ssm_conv_b, ssm_dt_bias, ssm_a_log, ssm_d, ssm_norm_g, ssm_w_out, mlstm_w_in, mlstm_gate_b, mlstm_norm_g, mlstm_w_out, mla_w_in, mla_q_norm_g, mla_w_q_up, mla_kv_norm_g, mla_w_kv_up, mla_w_out, final_norm_g):
    nb, n_lat, d = x.shape
    assert ctx.shape[1] == TM and d == D_MODEL and n_lat % TM == 0
    depth = w_mod.shape[0]
    seg = TM + n_lat
    nt = seg // TM
    t_tok = nb * seg

    xs = (x, ctx)

    rows = -(-(nb + 1) // 8) * 8
    cvec = jnp.concatenate([c, c_ctx[None, :], jnp.zeros((rows - nb - 1, d), F32)], axis=0)
    mods = _modulation(cvec, w_mod, b_mod).reshape(depth, rows, ADALN_CHUNKS, d)
    mods = jnp.pad(mods, ((0, 0), (0, 0), (0, MOD_ROWS - ADALN_CHUNKS), (0, 0)))

    w_router = jnp.concatenate([moe_w_expert, moe_w_group,
                                jnp.zeros((depth, d, LANES - MOE_EXPERTS - MOE_GROUPS), F32)], axis=-1)
    b_router = jnp.concatenate([moe_b_expert, moe_b_group,
                                jnp.zeros((depth, LANES - MOE_EXPERTS - MOE_GROUPS), F32)], axis=-1)
    w_gate, w_up = moe_w_gate.astype(BF16), moe_w_up.astype(BF16)
    w_down = moe_w_down.astype(BF16).reshape(depth, MOE_GROUPS, MOE_PER_GROUP * MOE_FF, d)

    for i in range(depth):
        kind = i % 4
        mod_i = mods[i]
        if kind == 0:
            nq, nk = ATT_HEADS * ATT_HEAD_DIM, ATT_KV_HEADS * ATT_HEAD_DIM
            half = ATT_HEAD_DIM // 2
            q, k, v = _normproj(xs, mod_i, norm1_g[i], attn_w_in[i // 4].astype(BF16),
                                [(0, nq, half, ATT_HEAD_DIM ** -0.5 * LOG2_E), (nq, nk, half), (nq + nk, nk, None)],
                                [BF16, BF16, BF16], nt, nb, tables=_attn_rope_tables(n_lat))
            o = _windowed_attention(q, k, v, attn_sink[i // 4], nb, seg)
            xs = _outproj([o], [nq], xs, mod_i, attn_w_out[i // 4].astype(BF16), _plain_prologue, nt, nb)
        elif kind == 1:
            j = i // 4
            w_in = jnp.pad(ssm_w_in[j], ((0, 0), (0, LANES - 2 * SSM_HEADS))).astype(BF16)
            lane_pad = LANES - 2 * SSM_HEADS
            dt_bias = jnp.pad(ssm_dt_bias[j].reshape(1, -1), ((0, 0), (0, lane_pad)))
            a_neg = jnp.pad(-jnp.exp(ssm_a_log[j].astype(F32)).reshape(1, -1), ((0, 0), (0, lane_pad)))
            conv_w = jnp.pad(ssm_conv_w[j], ((0, 8 - SSM_CONV), (0, 0)))
            z, xc, bm, cm, dtv, a = _ssm_in(xs, mod_i, norm1_g[i], w_in, conv_w, ssm_conv_b[j].reshape(1, -1),
                                            dt_bias, a_neg, nt, nb)
            yf, yb = _ssd_scan(xc, bm, cm, dtv, a, ssm_d[j].astype(F32), nb, seg)
            xs = _outproj([z], [SSM_D_INNER], xs, mod_i, ssm_w_out[j].astype(BF16), _ssm_prologue,
                          nt, nb, extra=(ssm_norm_g[j].reshape(1, -1),), ins_t=(yf, yb))
        elif kind == 2:
            j = i // 4
            nqk, nv = 2 * ML_HEADS * ML_QK_DIM, ML_HEADS * ML_V_DIM
            w_in = jnp.pad(mlstm_w_in[j], ((0, 0), (0, LANES - 4 * ML_HEADS))).astype(BF16)
            qk, v, o, g = _normproj(xs, mod_i, norm1_g[i], w_in,
                                    [(0, nqk, None), (nqk, nv, None), (nqk + nv, nv, None), (nqk + 2 * nv, LANES, None)],
                                    [BF16, BF16, BF16, F32], nt, nb)
            gate_b = jnp.pad(mlstm_gate_b[j].reshape(1, -1), ((0, 0), (0, LANES - 4 * ML_HEADS)))
            hf, hb = _mlstm_scan(qk, v, g, gate_b, nb, seg)
            xs = _outproj([o], [nv], xs, mod_i, mlstm_w_out[j].astype(BF16), _mlstm_prologue,
                          nt, nb, extra=(mlstm_norm_g[j].reshape(1, -1),), ins_t=(hf, hb))
        else:
            j = i // 4
            w_in = jnp.pad(mla_w_in[j], ((0, 0), (0, LANES - MLA_ROPE))).astype(BF16)
            cos, sin = _rope_angles(n_lat, MLA_ROPE)
            tables_t = (jnp.transpose(_with_ctx_rows(cos, 1.0)), jnp.transpose(_with_ctx_rows(sin, 0.0)))
            wq_t = jnp.transpose(_pad_heads(mla_w_q_up[j], MLA_NOPE + MLA_ROPE)).astype(BF16)
            w_kv = mla_w_kv_up[j].reshape(MLA_RANK, MLA_HEADS, MLA_NOPE + MLA_V)
            w_k = _pad_heads(w_kv[:, :, :MLA_NOPE].reshape(MLA_RANK, MLA_HEADS * MLA_NOPE), MLA_NOPE).astype(BF16)
            wv_t = jnp.transpose(w_kv[:, :, MLA_NOPE:].reshape(MLA_RANK, MLA_HEADS * MLA_V)).astype(BF16)
            q_t, k, v_t = _mla_in(xs, mod_i, norm1_g[i], w_in, mla_q_norm_g[j], wq_t, mla_kv_norm_g[j], w_k, wv_t,
                                  tables_t, _mla_k_tables(n_lat), nt, nb)
            o = _mla_attention(q_t, k, v_t, nb, seg)
            xs = _outproj([o], [MLA_HEADS * MLA_V], xs, mod_i, mla_w_out[j].astype(BF16), _plain_prologue, nt, nb)

        xs = _moe(xs, mod_i, norm2_g[i], w_router[i], b_router[i].reshape(1, -1), w_gate, w_up, w_down, i, nt, nb)

    return _final_norm(xs, final_norm_g, nb, seg)
```

```python
import functools
import math

import jax
import jax.numpy as jnp
from jax import lax
from jax.experimental import pallas as pl
from jax.experimental.pallas import tpu as pltpu

F32 = jnp.float32
BF16 = jnp.bfloat16

D_MODEL = 1024
GRID_W = 64
EPS = 1e-6
ROPE_BASE = 10000.0
ADALN_CHUNKS = 6
CHUNK = 128
SSD_CHUNK = 256
ML_CHUNK = 128
SSD_DECAY_BLOCK = 128
TM = 256
MOD_ROWS = 8
LANES = 128
V7X_VMEM_LIMIT = 48 * 1024 * 1024

ATT_HEADS, ATT_KV_HEADS, ATT_HEAD_DIM, WINDOW = 16, 4, 64, 128
ATT_GROUP = ATT_HEADS // ATT_KV_HEADS
ATT_UNIT_HEADS = 4
SSM_D_INNER, SSM_HEAD_DIM, SSM_HEADS, SSM_GROUPS, SSM_STATE, SSM_CONV = 2048, 64, 32, 4, 128, 5
SSM_HPG = SSM_HEADS // SSM_GROUPS
SSM_BC = SSM_GROUPS * SSM_STATE
SSM_CONV_DIM = SSM_D_INNER + 2 * SSM_BC
SSM_CONV_CHUNK = 256
ML_HEADS, ML_QK_DIM, ML_V_DIM = 8, 64, 128
MLA_HEADS, MLA_RANK, MLA_NOPE, MLA_ROPE, MLA_V = 16, 256, 64, 32, 64
MOE_GROUPS, MOE_PER_GROUP, MOE_EXPERTS, MOE_FF = 4, 4, 16, 256
MOE_ROWS = 1024
MLA_KEY_TILE = 1024
MLA_HEADS_PER_STEP = 4
MLA_LOOKAHEAD = 1
LOG2_E = 1.4426950408889634


def _dot(a, b):
    return jnp.dot(a, b, preferred_element_type=F32)


def _dot_nt(a, b):
    return lax.dot_general(a, b, (((1,), (1,)), ((), ())), preferred_element_type=F32)


def _split_bf16(x):
    hi = x.astype(BF16)
    lo = (x - hi.astype(F32)).astype(BF16)
    return hi, lo


def _dot_split(a, b):
    a_hi, a_lo = _split_bf16(a)
    b_hi, b_lo = _split_bf16(b)
    return _dot(a_hi, b_hi) + _dot(a_lo, b_hi) + _dot(a_hi, b_lo)


def _sigmoid(x):
    return 1.0 / (1.0 + jnp.exp(-x))


def _silu(x):
    return x * _sigmoid(x)


def _softplus(x):
    return jnp.maximum(x, 0.0) + jnp.log1p(jnp.exp(-jnp.abs(x)))


def _params(*sem):
    return pltpu.CompilerParams(dimension_semantics=sem, vmem_limit_bytes=V7X_VMEM_LIMIT)


def _mod_row(t, nt, nb):
    return jnp.where(t % nt == 0, nb, t // nt)


def _mod_kernel(c_ref, w_ref, b_ref, o_ref):
    o_ref[0] = _dot_split(_silu(c_ref[...]), w_ref[0]) + b_ref[0]


def _modulation(cvec, w_mod, b_mod):
    depth, d, n = w_mod.shape
    tn = 1536
    rows = cvec.shape[0]
    return pl.pallas_call(
        _mod_kernel,
        grid=(depth, n // tn),
        in_specs=[pl.BlockSpec((rows, d), lambda l, j: (0, 0)),
                  pl.BlockSpec((1, d, tn), lambda l, j: (l, 0, j)),
                  pl.BlockSpec((1, 1, tn), lambda l, j: (l, 0, j))],
        out_specs=pl.BlockSpec((1, rows, tn), lambda l, j: (l, 0, j)),
        out_shape=jax.ShapeDtypeStruct((depth, rows, n), F32),
        compiler_params=_params("arbitrary", "arbitrary"),
        name="modulation",
    )(cvec, w_mod, b_mod.reshape(depth, 1, n))


def _normed(x, g_row, mod, sh_row, sc_row):
    y = x * lax.rsqrt(jnp.mean(x * x, axis=-1, keepdims=True) + EPS) * g_row
    return y * (1.0 + mod[sc_row:sc_row + 1, :]) + mod[sh_row:sh_row + 1, :]


def _proj_columns(hb, w_ref, o_ref, start, width, rope_half, cos, sin, scale=None, chunk=512):
    for c in range(0, width, chunk):
        cw = min(chunk, width - c)
        acc = _dot(hb, w_ref[:, start + c:start + c + cw])
        if rope_half is not None:
            lane = lax.broadcasted_iota(jnp.int32, (1, cw), 1)
            rot = jnp.where(lane % (2 * rope_half) < rope_half, pltpu.roll(acc, cw - rope_half, axis=1),
                            pltpu.roll(acc, rope_half, axis=1))
            reps = cw // LANES
            acc = acc * jnp.tile(cos, (1, reps)) + rot * jnp.tile(sin, (1, reps))
        if scale is not None:
            acc = acc * scale
        o_ref[:, c:c + cw] = acc.astype(o_ref.dtype)


def _stream_specs(x, nt, sub):
    if isinstance(x, tuple):
        d = x[0].shape[-1]
        specs = []
        for s in range(sub):
            tile = lambda t, s=s: t * sub + s
            specs += [pl.BlockSpec((1, TM, d), lambda t, tile=tile: (tile(t) // nt, jnp.maximum(tile(t) % nt - 1, 0), 0)),
                      pl.BlockSpec((1, TM, d), lambda t, tile=tile: (tile(t) // nt, 0, 0))]
        return specs, list(x) * sub
    return [pl.BlockSpec((sub * TM, x.shape[1]), lambda t: (t, 0))], [x]


def _tiles_per_step(nt, nb, per_tile_tables=False, max_tiles=4):
    if per_tile_tables:
        return 1
    return next(k for k in (4, 2, 1) if k <= max_tiles and (nt * nb) % k == 0)


def _stream_tile(x_refs, nt, s, sub):
    if len(x_refs) == 1:
        return x_refs[0][s * TM:(s + 1) * TM, :]
    is_ctx = (pl.program_id(0) * sub + s) % nt == 0
    return jnp.where(is_ctx, x_refs[2 * s + 1][0], x_refs[2 * s][0])


def _normproj_kernel(*refs, outs, has_rope, n_x, nt, nb, sub):
    x_refs, refs = refs[:n_x], refs[n_x:]
    mods_ref, g_ref, w_ref = refs[:3]
    k = 3
    cos = sin = None
    if has_rope:
        cos, sin = refs[3][...], refs[4][...]
        k = 5
    hs = []
    for s in range(sub):
        mod = mods_ref[_mod_row(pl.program_id(0) * sub + s, nt, nb)]
        hs.append(_normed(_stream_tile(x_refs, nt, s, sub), g_ref[...], mod, 0, 1).astype(BF16))
    hb = hs[0] if sub == 1 else jnp.concatenate(hs, axis=0)
    for o_ref, spec in zip(refs[k:], outs):
        _proj_columns(hb, w_ref, o_ref, *spec[:3], cos, sin, scale=spec[3] if len(spec) > 3 else None)


def _normproj(x, mods, gain, w, outs, out_dtypes, nt, nb, tables=None):
    d, n = w.shape
    t_tok = nb * nt * TM
    sub = _tiles_per_step(nt, nb, per_tile_tables=tables is not None)
    rows = sub * TM
    x_specs, x_args = _stream_specs(x, nt, sub)
    in_specs = x_specs + [pl.BlockSpec(mods.shape, lambda t: (0, 0, 0)),
                          pl.BlockSpec((1, d), lambda t: (0, 0)),
                          pl.BlockSpec((d, n), lambda t: (0, 0))]
    args = x_args + [mods, gain.reshape(1, d), w]
    if tables is not None:
        in_specs += [pl.BlockSpec((TM, LANES), lambda t: (t % nt, 0))] * 2
        args += list(tables)
    return pl.pallas_call(
        functools.partial(_normproj_kernel, outs=tuple(outs), has_rope=tables is not None, n_x=len(x_args),
                          nt=nt, nb=nb, sub=sub),
        grid=(t_tok // rows,),
        in_specs=in_specs,
        out_specs=[pl.BlockSpec((rows, o[1]), lambda t: (t, 0)) for o in outs],
        out_shape=[jax.ShapeDtypeStruct((t_tok, o[1]), dt) for o, dt in zip(outs, out_dtypes)],
        compiler_params=_params("parallel"),
        name="normproj",
    )(*args)


def _outproj_kernel(*refs, prologue, n_t, n_tok, n_extra, n_x, nt, nb, sub):
    ins_t = refs[:n_t * sub]
    ins = refs[n_t * sub:n_t * sub + n_tok]
    extra = refs[n_t * sub + n_tok:n_t * sub + n_tok + n_extra]
    x_refs = refs[n_t * sub + n_tok + n_extra:n_t * sub + n_tok + n_extra + n_x]
    mods_ref, w_ref, o_ref = refs[n_t * sub + n_tok + n_extra + n_x:]
    parts = []
    for s in range(sub):
        tok = [r if sub == 1 else r.at[pl.ds(s * TM, TM), :] for r in ins]
        parts.append(prologue(*ins_t[s * n_t:(s + 1) * n_t], *tok, *extra))
    a = parts[0] if sub == 1 else jnp.concatenate(parts, axis=0)
    y = _dot(a, w_ref[...])
    for s in range(sub):
        rows = slice(s * TM, (s + 1) * TM)
        gate = mods_ref[_mod_row(pl.program_id(0) * sub + s, nt, nb)][2:3, :]
        o_ref[rows, :] = _stream_tile(x_refs, nt, s, sub) + gate * y[rows, :]


def _outproj(ins, in_widths, x, mods, w, prologue, nt, nb, extra=(), ins_t=()):
    d = w.shape[1]
    t_tok = nb * nt * TM
    sub = _tiles_per_step(nt, nb, max_tiles=2 if ins_t else 4)
    rows = sub * TM

    def tile_map(s):
        return lambda t: ((t * sub + s) // nt, 0, (t * sub + s) % nt)

    in_specs, args = [], []
    for s in range(sub):
        in_specs += [pl.BlockSpec((1, a.shape[1], TM), tile_map(s)) for a in ins_t]
        args += list(ins_t)
    in_specs += [pl.BlockSpec((rows, wd), lambda t: (t, 0)) for wd in in_widths]
    in_specs += [pl.BlockSpec(e.shape, lambda t: (0, 0)) for e in extra]
    x_specs, x_args = _stream_specs(x, nt, sub)
    n_before_x = len(in_specs)
    in_specs += x_specs + [pl.BlockSpec(mods.shape, lambda t: (0, 0, 0)), pl.BlockSpec(w.shape, lambda t: (0, 0))]
    return pl.pallas_call(
        functools.partial(_outproj_kernel, prologue=prologue, n_t=len(ins_t), n_tok=len(ins), n_extra=len(extra),
                          n_x=len(x_args), nt=nt, nb=nb, sub=sub),
        grid=(t_tok // rows,),
        in_specs=in_specs,
        out_specs=pl.BlockSpec((rows, d), lambda t: (t, 0)),
        out_shape=jax.ShapeDtypeStruct((t_tok, d), F32),
        input_output_aliases={} if isinstance(x, tuple) else {n_before_x: 0},
        compiler_params=_params("parallel"),
        name="outproj",
    )(*args, *ins, *extra, *x_args, mods, w)


def _group_rms(y, n_groups):
    width = y.shape[1] // n_groups
    parts = []
    for g in range(n_groups):
        yg = y[:, g * width:(g + 1) * width]
        parts.append(yg * lax.rsqrt(jnp.mean(yg * yg, axis=-1, keepdims=True) + EPS))
    return jnp.concatenate(parts, axis=1)


def _plain_prologue(o_ref):
    return o_ref[...]


def _ssm_prologue(yf_ref, yb_ref, z_ref, g_ref):
    y = jnp.transpose(yf_ref[0].astype(F32) + yb_ref[0].astype(F32)) * _silu(z_ref[...].astype(F32))
    return (_group_rms(y, SSM_GROUPS) * g_ref[...]).astype(BF16)


def _mlstm_prologue(hf_ref, hb_ref, o_ref, g_ref):
    h = jnp.transpose(hf_ref[0].astype(F32) + hb_ref[0].astype(F32))
    return (_group_rms(h, ML_HEADS) * g_ref[...] * _sigmoid(o_ref[...].astype(F32))).astype(BF16)


MOE_ROUTE_ROWS = 24


def _route(logits):
    lt = jnp.transpose(logits)[:MOE_ROUTE_ROWS, :]
    row = lax.broadcasted_iota(jnp.int32, lt.shape, 0).astype(F32)
    neg = -jnp.inf
    lg = jnp.where((row >= MOE_EXPERTS) & (row < MOE_EXPERTS + MOE_GROUPS), lt, neg)
    gmax = jnp.max(lg, axis=0, keepdims=True)
    g_sel = jnp.min(jnp.where(lg == gmax, row, LANES), axis=0, keepdims=True) - MOE_EXPERTS
    p_g = 1.0 / jnp.sum(jnp.exp(lg - gmax), axis=0, keepdims=True)
    in_group = (row >= g_sel * MOE_PER_GROUP) & (row < (g_sel + 1) * MOE_PER_GROUP)
    le = jnp.where(in_group, lt, neg)
    v1 = jnp.max(le, axis=0, keepdims=True)
    i1 = jnp.min(jnp.where(le == v1, row, LANES), axis=0, keepdims=True)
    le2 = jnp.where(row == i1, neg, le)
    v2 = jnp.max(le2, axis=0, keepdims=True)
    i2 = jnp.min(jnp.where(le2 == v2, row, LANES), axis=0, keepdims=True)
    e2 = jnp.exp(v2 - v1)
    w1 = p_g / (1.0 + e2)
    comb_t = jnp.where(row == i1, w1, 0.0) + jnp.where(row == i2, w1 * e2, 0.0)
    pad = jnp.zeros((LANES - MOE_ROUTE_ROWS, lt.shape[1]), F32)
    return jnp.transpose(jnp.concatenate([comb_t, pad], axis=0))


def _moe_kernel(x_ref, mods_ref, g_ref, wr_ref, br_ref, wg_ref, wu_ref, wd_ref, o_ref, h_scr, comb_scr,
                *, nt, nb, sub):
    i = pl.program_id(0)
    e = pl.program_id(1)

    def experts(rows):
        hb = h_scr[rows, :]
        comb = comb_scr[rows, :]
        lane = lax.broadcasted_iota(jnp.int32, comb.shape, 1)
        scaled = []
        for k in range(MOE_PER_GROUP):
            act = _silu(_dot(hb, wg_ref[0, k])) * _dot(hb, wu_ref[0, k])
            cw = jnp.sum(jnp.where(lane == e * MOE_PER_GROUP + k, comb, 0.0), axis=-1, keepdims=True)
            scaled.append((act * cw).astype(BF16))
        return _dot(jnp.concatenate(scaled, axis=1), wd_ref[0, 0])

    def gate_row(s):
        return mods_ref[_mod_row(i * sub + s, nt, nb)][5:6, :]

    @pl.when(e == 0)
    def _():
        w_hi, w_lo = _split_bf16(wr_ref[...])
        w_both = jnp.concatenate([w_hi, w_lo], axis=1)
        def prepare(s):
            rows = slice(s * TM, (s + 1) * TM)
            mod = mods_ref[_mod_row(i * sub + s, nt, nb)]
            h = _normed(x_ref[rows, :], g_ref[...], mod, 3, 4)
            h_hi, h_lo = _split_bf16(h)
            h_scr[rows, :] = h_hi
            both = _dot(h_hi, w_both)
            logits = both[:, :LANES] + both[:, LANES:] + _dot(h_lo, w_hi) + br_ref[...]
            comb_scr[rows, :] = _route(logits)

        for s in range(sub):
            prepare(s)
            rows = slice(s * TM, (s + 1) * TM)
            o_ref[rows, :] = x_ref[rows, :] + gate_row(s) * experts(rows)

    @pl.when(e > 0)
    def _():
        y = experts(slice(0, sub * TM))
        for s in range(sub):
            rows = slice(s * TM, (s + 1) * TM)
            o_ref[rows, :] += gate_row(s) * y[rows, :]


def _moe(x, mods, gain, w_router, b_router, w_gate, w_up, w_down, layer, nt, nb):
    t_tok, d = x.shape
    pg = MOE_PER_GROUP
    rows = MOE_ROWS if t_tok % MOE_ROWS == 0 else TM
    sub = rows // TM
    return pl.pallas_call(
        functools.partial(_moe_kernel, nt=nt, nb=nb, sub=sub),
        grid=(t_tok // rows, MOE_GROUPS),
        in_specs=[pl.BlockSpec((rows, d), lambda i, e: (i, 0)),
                  pl.BlockSpec(mods.shape, lambda i, e: (0, 0, 0)),
                  pl.BlockSpec((1, d), lambda i, e: (0, 0)),
                  pl.BlockSpec((d, LANES), lambda i, e: (0, 0)),
                  pl.BlockSpec((1, LANES), lambda i, e: (0, 0)),
                  pl.BlockSpec((1, pg, d, MOE_FF), lambda i, e: (layer, e, 0, 0)),
                  pl.BlockSpec((1, pg, d, MOE_FF), lambda i, e: (layer, e, 0, 0)),
                  pl.BlockSpec((1, 1, pg * MOE_FF, d), lambda i, e: (layer, e, 0, 0))],
        out_specs=pl.BlockSpec((rows, d), lambda i, e: (i, 0)),
        out_shape=jax.ShapeDtypeStruct((t_tok, d), F32),
        scratch_shapes=[pltpu.VMEM((rows, d), BF16), pltpu.VMEM((rows, LANES), F32)],
        input_output_aliases={0: 0},
        compiler_params=_params("parallel", "arbitrary"),
        name="moe",
    )(x, mods, gain.reshape(1, d), w_router, b_router, w_gate, w_up, w_down)


def _attn_kernel(sink_ref, q_ref, kp_ref, kc_ref, kn_ref, vp_ref, vc_ref, vn_ref, kx_ref, vx_ref, o_ref, *, n_lat):
    j = pl.program_id(1)
    jl = j - 1
    dh = ATT_HEAD_DIM
    n_loc = TM + 2 * CHUNK
    k_all = jnp.concatenate([kp_ref[...], kc_ref[...], kn_ref[...], kx_ref[...]], axis=0)
    q_t = jnp.transpose(q_ref[...].astype(F32)).astype(BF16)
    n_keys = k_all.shape[0]
    kj = lax.broadcasted_iota(jnp.int32, (n_keys, TM), 0)
    qi = lax.broadcasted_iota(jnp.int32, (n_keys, TM), 1)
    kpos = jl * TM - CHUNK + kj
    local_ok = (kj >= qi + CHUNK - WINDOW) & (kj <= qi + CHUNK + WINDOW) & (kpos >= 0) & (kpos < n_lat) & (jl >= 0)
    bias = jnp.where((kj >= n_loc) | local_ok, 0.0, -jnp.inf)
    uh = ATT_UNIT_HEADS
    bias = jnp.concatenate([bias] * uh, axis=1)
    lane_head = lax.broadcasted_iota(jnp.int32, (1, uh * TM), 1) // TM
    zeros_q = jnp.zeros((dh, uh * TM), BF16)
    units = [(h0 // ATT_GROUP, h0) for h0 in range(0, ATT_HEADS, uh)]

    def scores(u):
        g, h0 = units[u]
        half = g % 2
        k_pair = k_all[:, (g - half) * dh:(g - half + 2) * dh]
        qg = jnp.concatenate([q_t[(h0 + a) * dh:(h0 + a + 1) * dh, :] for a in range(uh)], axis=1)
        q_m = jnp.concatenate([zeros_q, qg] if half else [qg, zeros_q], axis=0)
        return _dot(k_pair, q_m) + bias

    outs = []
    s_next = scores(0)
    v_t = jnp.transpose(jnp.concatenate([vp_ref[...], vc_ref[...], vn_ref[...], vx_ref[...]],
                                        axis=0).astype(F32)).astype(BF16)
    for u, (g, h0) in enumerate(units):
        s = s_next
        if u + 1 < len(units):
            s_next = scores(u + 1)
        sink = jnp.zeros((1, uh * TM), F32)
        for a in range(uh):
            sink = jnp.where(lane_head == a, sink_ref[h0 + a] * LOG2_E, sink)
        m = jnp.maximum(jnp.max(s, axis=0, keepdims=True), sink)
        p = jnp.exp2(s - m)
        den = jnp.sum(p, axis=0, keepdims=True) + jnp.exp2(sink - m)
        og = _dot(v_t[g * dh:(g + 1) * dh, :], p.astype(BF16)) / den
        outs += [og[:, a * TM:(a + 1) * TM] for a in range(uh)]
    o_ref[...] = jnp.transpose(jnp.concatenate(outs, axis=0)).astype(o_ref.dtype)


def _windowed_attention(q, k, v, sink, nb, seg):
    t_tok = q.shape[0]
    nchunk = seg // CHUNK
    nt = seg // TM
    per_tile = TM // CHUNK
    kvw = ATT_KV_HEADS * ATT_HEAD_DIM

    def halo(off):
        return lambda b, j, *_: (b * nchunk + jnp.clip(j * per_tile + off, per_tile, nchunk - 1), 0)

    tile = lambda b, j, *_: (b * nt + j, 0)
    kv_spec = [pl.BlockSpec((CHUNK, kvw), halo(-1)), pl.BlockSpec((TM, kvw), tile),
               pl.BlockSpec((CHUNK, kvw), halo(per_tile))]
    ctx_spec = pl.BlockSpec((TM, kvw), lambda b, j, *_: (b * nt, 0))
    return pl.pallas_call(
        functools.partial(_attn_kernel, n_lat=seg - TM),
        grid_spec=pltpu.PrefetchScalarGridSpec(
            num_scalar_prefetch=1,
            grid=(nb, nt),
            in_specs=[pl.BlockSpec((TM, q.shape[1]), tile)] + kv_spec + kv_spec + [ctx_spec, ctx_spec],
            out_specs=pl.BlockSpec((TM, q.shape[1]), tile),
        ),
        out_shape=jax.ShapeDtypeStruct((t_tok, q.shape[1]), BF16),
        compiler_params=_params("parallel", "parallel"),
        name="windowed_attention",
    )(sink, q, k, k, k, v, v, v, k, v)


def _ssm_in_kernel(x_ref, xp_ref, xn_ref, mods_ref, g_ref, w_ref, cw_ref, cb_ref, dtb_ref, aneg_ref,
                   z_ref, xs_ref, bm_ref, cm_ref, dtv_ref, a_ref, *ext, nt, nb):
    t = pl.program_id(0)
    tl = t % nt
    halo = 8
    first = (tl == 0) | (tl == 1)
    last = (tl == 0) | (tl == nt - 1)
    gain = g_ref[...]

    def normed(x, tile):
        return _normed(x, gain, mods_ref[_mod_row(tile, nt, nb)], 0, 1)

    h = normed(x_ref[...], t)
    h_prev = jnp.where(first, 0.0, normed(xp_ref[...], jnp.maximum(t - 1, 0)))
    h_next = jnp.where(last, 0.0, normed(xn_ref[...], jnp.minimum(t + 1, pl.num_programs(0) - 1)))
    hb = h.astype(BF16)
    hb_ext = jnp.concatenate([h_prev, h, h_next], axis=0).astype(BF16)
    chunk = SSM_CONV_CHUNK
    pad = SSM_CONV // 2

    def project(c):
        proj = _dot(hb_ext, w_ref[:, SSM_D_INNER + c:SSM_D_INNER + c + chunk])
        ext[c // chunk][...] = proj

    project(0)
    for c in range(0, SSM_CONV_DIM, chunk):
        if c + chunk < SSM_CONV_DIM:
            project(c + chunk)
        if c < SSM_D_INNER:
            z_ref[:, c:c + chunk] = _dot(hb, w_ref[:, c:c + chunk]).astype(z_ref.dtype)
        acc = jnp.broadcast_to(cb_ref[:, c:c + chunk], (TM, chunk))
        for k in range(SSM_CONV):
            acc = acc + cw_ref[k:k + 1, c:c + chunk] * ext[c // chunk][halo - pad + k:halo - pad + k + TM, :]
        y = _silu(acc)
        if c < SSM_D_INNER:
            xs_ref[0, c:c + chunk, :] = jnp.transpose(y).astype(xs_ref.dtype)
        elif c < SSM_D_INNER + SSM_BC:
            bm_ref[:, c - SSM_D_INNER:c - SSM_D_INNER + chunk] = y.astype(bm_ref.dtype)
        else:
            off = c - SSM_D_INNER - SSM_BC
            cm_ref[0, off:off + chunk, :] = jnp.transpose(y).astype(cm_ref.dtype)
    dt = _dot(hb, w_ref[:, SSM_D_INNER + SSM_CONV_DIM:])
    dtv = _softplus(dt + dtb_ref[...])
    dtv_ref[0] = jnp.transpose(dtv)
    a_ref[0] = jnp.transpose(dtv * aneg_ref[...] * LOG2_E)


def _ssm_in(x, mods, gain, w, conv_w, conv_b, dt_bias, a_neg, nt, nb):
    t_tok, d = x.shape
    n8 = t_tok // 8
    seg = nt * TM
    row = lambda t: (t, 0)
    col = lambda t: (t // nt, 0, t % nt)
    fixed = lambda t: (0, 0)
    return pl.pallas_call(
        functools.partial(_ssm_in_kernel, nt=nt, nb=nb),
        grid=(t_tok // TM,),
        in_specs=[pl.BlockSpec((TM, d), row),
                  pl.BlockSpec((8, d), lambda t: (jnp.maximum(t * (TM // 8) - 1, 0), 0)),
                  pl.BlockSpec((8, d), lambda t: (jnp.minimum((t + 1) * (TM // 8), n8 - 1), 0)),
                  pl.BlockSpec(mods.shape, lambda t: (0, 0, 0)),
                  pl.BlockSpec((1, d), fixed),
                  pl.BlockSpec(w.shape, fixed),
                  pl.BlockSpec((8, SSM_CONV_DIM), fixed),
                  pl.BlockSpec((1, SSM_CONV_DIM), fixed),
                  pl.BlockSpec((1, LANES), fixed),
                  pl.BlockSpec((1, LANES), fixed)],
        out_specs=[pl.BlockSpec((TM, SSM_D_INNER), row),
                   pl.BlockSpec((1, SSM_D_INNER, TM), col), pl.BlockSpec((TM, SSM_BC), row),
                   pl.BlockSpec((1, SSM_BC, TM), col), pl.BlockSpec((1, LANES, TM), col),
                   pl.BlockSpec((1, LANES, TM), col)],
        out_shape=[jax.ShapeDtypeStruct((t_tok, SSM_D_INNER), BF16),
                   jax.ShapeDtypeStruct((nb, SSM_D_INNER, seg), BF16), jax.ShapeDtypeStruct((t_tok, SSM_BC), BF16),
                   jax.ShapeDtypeStruct((nb, SSM_BC, seg), BF16), jax.ShapeDtypeStruct((nb, LANES, seg), F32),
                   jax.ShapeDtypeStruct((nb, LANES, seg), F32)],
        scratch_shapes=[pltpu.VMEM((TM + 16, SSM_CONV_CHUNK), F32)] * (SSM_CONV_DIM // SSM_CONV_CHUNK),
        compiler_params=_params("parallel"),
        name="ssm_in",
    )(x, x, x, mods, gain.reshape(1, d), w, conv_w, conv_b, dt_bias, a_neg)


def _tri(lower, n):
    r = lax.broadcasted_iota(jnp.int32, (n, n), 0)
    c = lax.broadcasted_iota(jnp.int32, (n, n), 1)
    return (c <= r) if lower else (c >= r)


def _ssd_prepare(dt_ref, a_ref, d):
    a_t = a_ref[0]
    dt_t = dt_ref[0]
    upper = jnp.where(_tri(False, SSD_CHUNK), 1.0, 0.0).astype(BF16)
    a_hi, a_lo = _split_bf16(a_t)
    cum = _dot(a_hi, upper) + _dot(a_lo, upper)
    cum_end = cum[:, SSD_CHUNK - 1:SSD_CHUNK]
    if d == 0:
        lane_v, sub_v = cum, -cum
        inter = jnp.exp2(cum)
        w_upd = jnp.exp2(cum_end - cum) * dt_t
    else:
        ecum = cum - a_t
        lane_v, sub_v = -ecum, ecum
        inter = jnp.exp2(cum_end - ecum)
        w_upd = jnp.exp2(ecum) * dt_t
    sub_c = jnp.transpose(sub_v)
    decay_end = jnp.exp2(cum_end)
    return dt_t, lane_v, sub_c, inter, w_upd, decay_end


def _ssd_groups(prep, xs_ref, bm_ref, ct_ref, dsk_ref, st_ref, y_ref, d):
    dt_t, lane_v, sub_c, inter, w_upd, decay_end = prep
    mask = _tri(d == 1, SSD_CHUNK)
    p = SSM_HEAD_DIM
    for g in range(SSM_GROUPS):
        bg = bm_ref[:, g * SSM_STATE:(g + 1) * SSM_STATE]
        cg_t = ct_ref[0, g * SSM_STATE:(g + 1) * SSM_STATE, :]
        cb_t = _dot(bg, cg_t)
        state = st_ref[g]
        y_in = _dot(state.astype(BF16), cg_t)
        upd = []
        for e in range(SSM_HPG):
            h = g * SSM_HPG + e
            ln = d * SSM_HEADS + h
            rows = slice(e * p, (e + 1) * p)
            blk = SSD_DECAY_BLOCK
            block_rows = []
            for bs in range(0, SSD_CHUNK, blk):
                block_cols = []
                for bl in range(0, SSD_CHUNK, blk):
                    if (bs > bl) if d == 0 else (bs < bl):
                        block_cols.append(jnp.zeros((blk, blk), BF16))
                        continue
                    seg = sub_c[bs:bs + blk, ln:ln + 1] + lane_v[ln:ln + 1, bl:bl + blk]
                    if bs == bl:
                        seg = jnp.where(mask[bs:bs + blk, bl:bl + blk], seg, -jnp.inf)
                    block_cols.append((cb_t[bs:bs + blk, bl:bl + blk] * jnp.exp2(seg)).astype(BF16))
                block_rows.append(jnp.concatenate(block_cols, axis=1))
            m_t = jnp.concatenate(block_rows, axis=0)
            xf = xs_ref[0, h * p:(h + 1) * p, :].astype(F32)
            u = (xf * dt_t[ln:ln + 1, :]).astype(BF16)
            yh = _dot(u, m_t) + y_in[rows, :] * inter[ln:ln + 1, :] + dsk_ref[d, h] * xf
            y_ref[0, h * p:(h + 1) * p, :] = yh.astype(y_ref.dtype)
            upd.append((xf * w_upd[ln:ln + 1, :]).astype(BF16))
        new = _dot(jnp.concatenate(upd, axis=0), bg)
        for e in range(SSM_HPG):
            ln = d * SSM_HEADS + g * SSM_HPG + e
            rows = slice(e * p, (e + 1) * p)
            st_ref[g, rows, :] = state[rows, :] * decay_end[ln:ln + 1, :] + new[rows, :]


def _ssd_kernel(dsk_ref, xsf, bmf, ctf, dtf, af, xsb, bmb, ctb, dtb, ab, yf_ref, yb_ref, stf, stb):
    @pl.when(pl.program_id(1) == 0)
    def _():
        stf[...] = jnp.zeros_like(stf)
        stb[...] = jnp.zeros_like(stb)

    prep_f = _ssd_prepare(dtf, af, 0)
    prep_b = _ssd_prepare(dtb, ab, 1)
    _ssd_groups(prep_f, xsf, bmf, ctf, dsk_ref, stf, yf_ref, 0)
    _ssd_groups(prep_b, xsb, bmb, ctb, dsk_ref, stb, yb_ref, 1)


def _scan_maps(nchunk, chunk):
    ctx_chunks = TM // chunk
    fwd = lambda b, t: (b * nchunk + t, 0)
    bwd = lambda b, t: (b * nchunk + jnp.where(t < ctx_chunks, ctx_chunks - 1 - t, nchunk + ctx_chunks - 1 - t), 0)
    return fwd, bwd


def _ssd_scan(xs_t, bm, c_t, dt_t, a_t, d_skip, nb, seg):
    nchunk = seg // SSD_CHUNK
    fwd_rows, bwd_rows = _scan_maps(nchunk, SSD_CHUNK)

    def specs(rows_map):
        cols_map = lambda b, t, *_: (b, 0, rows_map(b, t)[0] - b * nchunk)
        return [pl.BlockSpec((1, SSM_D_INNER, SSD_CHUNK), cols_map),
                pl.BlockSpec((SSD_CHUNK, SSM_BC), lambda b, t, *_: rows_map(b, t)),
                pl.BlockSpec((1, SSM_BC, SSD_CHUNK), cols_map),
                pl.BlockSpec((1, LANES, SSD_CHUNK), cols_map),
                pl.BlockSpec((1, LANES, SSD_CHUNK), cols_map)]

    arrs = (xs_t, bm, c_t, dt_t, a_t)
    return pl.pallas_call(
        _ssd_kernel,
        grid_spec=pltpu.PrefetchScalarGridSpec(
            num_scalar_prefetch=1,
            grid=(nb, nchunk),
            in_specs=specs(fwd_rows) + specs(bwd_rows),
            out_specs=[specs(fwd_rows)[0], specs(bwd_rows)[0]],
            scratch_shapes=[pltpu.VMEM((SSM_GROUPS, SSM_HPG * SSM_HEAD_DIM, SSM_STATE), F32)] * 2,
        ),
        out_shape=[jax.ShapeDtypeStruct(xs_t.shape, BF16)] * 2,
        compiler_params=_params("parallel", "arbitrary"),
        name="ssd_scan",
    )(d_skip, *arrs, *arrs)


def _log_sigmoid(x):
    return jnp.minimum(x, 0.0) - jnp.log1p(jnp.exp(-jnp.abs(x)))


ML_STATE_ROWS = ML_V_DIM + 16


def _mlstm_prepare(qk_ref, v_ref, g_ref, gb_ref, m_st, d):
    nh, dk, dv = ML_HEADS, ML_QK_DIM, ML_V_DIM
    g_t = jnp.transpose(g_ref[...] + gb_ref[...])
    ig = g_t[16 * d:16 * d + nh, :]
    lf = _log_sigmoid(g_t[16 * d + nh:16 * d + 2 * nh, :])
    upper = jnp.where(_tri(False, ML_CHUNK), 1.0, 0.0).astype(BF16)
    lf_hi, lf_lo = _split_bf16(lf)
    fc = _dot(lf_hi, upper) + _dot(lf_lo, upper)
    tot = fc[:, ML_CHUNK - 1:ML_CHUNK]
    m_prev = m_st[...]
    if d == 0:
        lane_v, sub_v = fc, ig - fc
        inter = fc + m_prev
        logw = tot - fc + ig
    else:
        ec = fc - lf
        lane_v, sub_v = -ec, ec + ig
        inter = tot - ec + m_prev
        logw = ec + ig
    sub_c = jnp.transpose(jnp.concatenate([sub_v, jnp.zeros((LANES - nh, ML_CHUNK), F32)], axis=0))
    m_new = jnp.maximum(tot + m_prev, jnp.max(logw, axis=-1, keepdims=True))
    ws = jnp.exp(logw - m_new)
    cw = jnp.exp(tot + m_prev - m_new)
    q_t = jnp.transpose(qk_ref[:, :nh * dk].astype(F32))
    v_t = jnp.transpose(v_ref[...].astype(F32))
    return lane_v, sub_c, inter, ws, cw, m_new, q_t, v_t


def _mlstm_heads(prep, qk_ref, c_st, m_st, h_ref, d):
    nh, dk, dv = ML_HEADS, ML_QK_DIM, ML_V_DIM
    lane_v, sub_c, inter, ws, cw, m_new, q_t, v_t = prep
    mask = _tri(d == 1, ML_CHUNK)
    lane = lax.broadcasted_iota(jnp.int32, (1, LANES), 1)
    zeros_q = jnp.zeros((dk, ML_CHUNK), F32)
    for h in range(nh):
        half = h % 2
        k_pair = qk_ref[:, nh * dk + (h - half) * dk:nh * dk + (h - half + 2) * dk] * (dk ** -0.5)
        qh = q_t[h * dk:(h + 1) * dk, :]
        q_m = jnp.concatenate([zeros_q, qh] if half else [qh, zeros_q], axis=0).astype(BF16)
        logd = jnp.where(mask, sub_c[:, h:h + 1] + lane_v[h:h + 1, :], -jnp.inf)
        mt = jnp.maximum(inter[h:h + 1, :], jnp.max(logd, axis=0, keepdims=True))
        sc = _dot(k_pair, q_m) * jnp.exp(logd - mt)
        vh = v_t[h * dv:(h + 1) * dv, :]
        state = c_st[h]
        cq = _dot(state.astype(BF16), q_m)
        w_int = jnp.exp(inter[h:h + 1, :] - mt)
        num = _dot(vh.astype(BF16), sc.astype(BF16)) + w_int * cq[:dv, :]
        den = jnp.sum(sc, axis=0, keepdims=True) + w_int * cq[dv:dv + 1, :]
        h_ref[0, h * dv:(h + 1) * dv, :] = (num / jnp.maximum(jnp.abs(den), jnp.exp(-mt))).astype(h_ref.dtype)
        ws_h = ws[h:h + 1, :]
        lhs = jnp.concatenate([vh * ws_h, jnp.broadcast_to(ws_h, (ML_STATE_ROWS - dv, ML_CHUNK))], axis=0)
        own = (lane >= half * dk) & (lane < (half + 1) * dk)
        row = lax.broadcasted_iota(jnp.int32, (ML_STATE_ROWS, 1), 0)
        new = cw[h:h + 1, :LANES] * state + _dot(lhs.astype(BF16), k_pair)
        c_st[h] = jnp.where(own & (row <= dv), new, 0.0)
    m_st[...] = jnp.broadcast_to(m_new, m_st.shape)


def _mlstm_kernel(qkf, vf, gf, qkb, vb, gb, gbias, hf_ref, hb_ref, cf, mf, cb, mb):
    @pl.when(pl.program_id(1) == 0)
    def _():
        for r in (cf, mf, cb, mb):
            r[...] = jnp.zeros_like(r)

    prep_f = _mlstm_prepare(qkf, vf, gf, gbias, mf, 0)
    prep_b = _mlstm_prepare(qkb, vb, gb, gbias, mb, 1)
    _mlstm_heads(prep_f, qkf, cf, mf, hf_ref, 0)
    _mlstm_heads(prep_b, qkb, cb, mb, hb_ref, 1)


def _mlstm_scan(qk, v, g, gate_b, nb, seg):
    t_tok = qk.shape[0]
    nchunk = seg // ML_CHUNK
    fwd, bwd = _scan_maps(nchunk, ML_CHUNK)
    widths = (qk.shape[1], v.shape[1], LANES)
    in_specs = ([pl.BlockSpec((ML_CHUNK, w), fwd) for w in widths] + [pl.BlockSpec((ML_CHUNK, w), bwd) for w in widths]
                + [pl.BlockSpec((1, LANES), lambda b, t: (0, 0))])
    state = [pltpu.VMEM((ML_HEADS, ML_STATE_ROWS, LANES), F32), pltpu.VMEM((ML_HEADS, ML_CHUNK), F32)]
    out_map = lambda rows_map: (lambda b, t: (b, 0, rows_map(b, t)[0] - b * nchunk))
    return pl.pallas_call(
        _mlstm_kernel,
        grid=(nb, nchunk),
        in_specs=in_specs,
        out_specs=[pl.BlockSpec((1, v.shape[1], ML_CHUNK), out_map(fwd)),
                   pl.BlockSpec((1, v.shape[1], ML_CHUNK), out_map(bwd))],
        out_shape=[jax.ShapeDtypeStruct((nb, v.shape[1], seg), BF16)] * 2,
        scratch_shapes=state + state,
        compiler_params=_params("parallel", "arbitrary"),
        name="mlstm_scan",
    )(qk, v, g, qk, v, g, gate_b)


def _rms_rows(x, g_row):
    return x * lax.rsqrt(jnp.mean(x * x, axis=-1, keepdims=True) + EPS) * g_row


MLA_SCORE_SCALE = (MLA_NOPE + MLA_ROPE) ** -0.5 * LOG2_E


def _mla_queries(cq, g_ref, wt_ref, cos_ref, sin_ref, qt_ref):
    cq_t = jnp.transpose(_rms_rows(cq, g_ref[...])).astype(BF16)
    q_t = _dot(wt_ref[...], cq_t) * MLA_SCORE_SCALE
    qt_ref[0] = q_t.astype(qt_ref.dtype)
    cos, sin = cos_ref[...], sin_ref[...]
    half = MLA_ROPE // 2
    for h in range(MLA_HEADS):
        r0 = h * LANES + MLA_NOPE
        x1, x2 = q_t[r0:r0 + half, :], q_t[r0 + half:r0 + MLA_ROPE, :]
        qt_ref[0, r0:r0 + half, :] = (x1 * cos - x2 * sin).astype(qt_ref.dtype)
        qt_ref[0, r0 + half:r0 + MLA_ROPE, :] = (x2 * cos + x1 * sin).astype(qt_ref.dtype)


def _mla_keys_values(ckv, kr, g_ref, wk_ref, wvt_ref, cos_ref, sin_ref, k_ref, vt_ref):
    cn = _rms_rows(ckv, g_ref[...])
    vt_ref[0] = _dot(wvt_ref[...], jnp.transpose(cn).astype(BF16)).astype(vt_ref.dtype)
    kn = _dot(cn.astype(BF16), wk_ref[...])
    kr = pltpu.roll(kr, MLA_NOPE, axis=1)
    half = MLA_ROPE // 2
    lane = lax.broadcasted_iota(jnp.int32, (1, LANES), 1)
    partner = jnp.where(lane < MLA_NOPE + half, pltpu.roll(kr, LANES - half, axis=1), pltpu.roll(kr, half, axis=1))
    roped = kr * cos_ref[...] + partner * sin_ref[...]
    for h in range(MLA_HEADS):
        k_ref[:, h * LANES:(h + 1) * LANES] = (kn[:, h * LANES:(h + 1) * LANES] + roped).astype(k_ref.dtype)


def _mla_in_kernel(x_ref, mods_ref, g_ref, w_ref, gq_ref, wqt_ref, gkv_ref, wk_ref, wvt_ref,
                   cos_t_ref, sin_t_ref, cos_ref, sin_ref, qt_ref, k_ref, vt_ref, *, nt, nb):
    mod = mods_ref[_mod_row(pl.program_id(0), nt, nb)]
    hb = _normed(x_ref[...], g_ref[...], mod, 0, 1).astype(BF16)
    p = _dot(hb, w_ref[...])
    _mla_queries(p[:, :MLA_RANK], gq_ref, wqt_ref, cos_t_ref, sin_t_ref, qt_ref)
    _mla_keys_values(p[:, MLA_RANK:2 * MLA_RANK], p[:, 2 * MLA_RANK:], gkv_ref, wk_ref, wvt_ref,
                     cos_ref, sin_ref, k_ref, vt_ref)


def _mla_in(x, mods, gain, w_in, gain_q, wq_t, gain_kv, w_k, wv_t, tables_t, tables, nt, nb):
    t_tok, d = x.shape
    seg = nt * TM
    half = MLA_ROPE // 2
    fixed = lambda t: (0, 0)
    return pl.pallas_call(
        functools.partial(_mla_in_kernel, nt=nt, nb=nb),
        grid=(t_tok // TM,),
        in_specs=[pl.BlockSpec((TM, d), lambda t: (t, 0)),
                  pl.BlockSpec(mods.shape, lambda t: (0, 0, 0)),
                  pl.BlockSpec((1, d), fixed),
                  pl.BlockSpec(w_in.shape, fixed),
                  pl.BlockSpec((1, MLA_RANK), fixed),
                  pl.BlockSpec(wq_t.shape, fixed),
                  pl.BlockSpec((1, MLA_RANK), fixed),
                  pl.BlockSpec(w_k.shape, fixed),
                  pl.BlockSpec(wv_t.shape, fixed),
                  pl.BlockSpec((half, TM), lambda t: (0, t % nt)),
                  pl.BlockSpec((half, TM), lambda t: (0, t % nt)),
                  pl.BlockSpec((TM, LANES), lambda t: (t % nt, 0)),
                  pl.BlockSpec((TM, LANES), lambda t: (t % nt, 0))],
        out_specs=[pl.BlockSpec((1, MLA_HEADS * LANES, TM), lambda t: (t, 0, 0)),
                   pl.BlockSpec((TM, MLA_HEADS * LANES), lambda t: (t, 0)),
                   pl.BlockSpec((1, MLA_HEADS * MLA_V, TM), lambda t: (t // nt, 0, t % nt))],
        out_shape=[jax.ShapeDtypeStruct((t_tok // TM, MLA_HEADS * LANES, TM), BF16),
                   jax.ShapeDtypeStruct((t_tok, MLA_HEADS * LANES), BF16),
                   jax.ShapeDtypeStruct((nb, MLA_HEADS * MLA_V, seg), BF16)],
        compiler_params=_params("parallel"),
        name="mla_in",
    )(x, mods, gain.reshape(1, d), w_in, gain_q.reshape(1, MLA_RANK), wq_t, gain_kv.reshape(1, MLA_RANK), w_k, wv_t,
      *tables_t, *tables)


def _mla_attn_kernel(qt_ref, k_ref, vt_ref, o_ref):
    @pl.when(pl.program_id(2) == 0)
    def _():
        o_ref[...] = jnp.zeros_like(o_ref)

    @pl.when(pl.program_id(2) > 0)
    def _():
        heads = range(MLA_HEADS_PER_STEP)
        seg = k_ref.shape[0]
        tk = MLA_KEY_TILE if (seg - TM) % MLA_KEY_TILE == 0 else TM
        tiles = [(0, TM)] + [(a, a + tk) for a in range(TM, seg, tk)]

        def scores(hh, j):
            return _dot(k_ref[tiles[j][0]:tiles[j][1], hh * LANES:(hh + 1) * LANES],
                        qt_ref[0, hh * LANES:(hh + 1) * LANES, :])

        m = [jnp.full((1, TM), -jnp.inf, F32) for _ in heads]
        l = [jnp.zeros((1, TM), F32) for _ in heads]
        acc = [jnp.zeros((MLA_V, TM), F32) for _ in heads]
        ahead = min(MLA_LOOKAHEAD, len(tiles))
        pending = [[scores(hh, j) for hh in heads] for j in range(ahead)]
        for j in range(len(tiles)):
            s_cur = pending.pop(0)
            if j + ahead < len(tiles):
                pending.append([scores(hh, j + ahead) for hh in heads])
            for hh in heads:
                m_new = jnp.maximum(m[hh], jnp.max(s_cur[hh], axis=0, keepdims=True))
                alpha = jnp.exp2(m[hh] - m_new)
                p = jnp.exp2(s_cur[hh] - m_new)
                l[hh] = alpha * l[hh] + jnp.sum(p, axis=0, keepdims=True)
                v_t = vt_ref[0, hh * MLA_V:(hh + 1) * MLA_V, tiles[j][0]:tiles[j][1]]
                acc[hh] = alpha * acc[hh] + _dot(v_t, p.astype(BF16))
                m[hh] = m_new
        o_t = jnp.concatenate([acc[hh] / l[hh] for hh in heads], axis=0)
        o_ref[...] = jnp.transpose(o_t).astype(o_ref.dtype)


def _mla_attention(q_t, k, v_t, nb, seg):
    t_tok = k.shape[0]
    nt = seg // TM
    hps = MLA_HEADS_PER_STEP
    return pl.pallas_call(
        _mla_attn_kernel,
        grid=(nb, MLA_HEADS // hps, nt),
        in_specs=[pl.BlockSpec((1, hps * LANES, TM), lambda b, hp, j: (b * nt + j, hp, 0)),
                  pl.BlockSpec((seg, hps * LANES), lambda b, hp, j: (b, hp)),
                  pl.BlockSpec((1, hps * MLA_V, seg), lambda b, hp, j: (b, hp, 0))],
        out_specs=pl.BlockSpec((TM, hps * MLA_V), lambda b, hp, j: (b * nt + j, hp)),
        out_shape=jax.ShapeDtypeStruct((t_tok, MLA_HEADS * MLA_V), BF16),
        compiler_params=_params("parallel", "parallel", "arbitrary"),
        name="mla_attention",
    )(q_t, k, v_t)


def _final_kernel(*refs, per):
    g_ref, o_ref = refs[per], refs[per + 1]
    for s in range(per):
        o_ref[0, s * TM:(s + 1) * TM, :] = _rms_rows(refs[s][...], g_ref[...])


def _final_norm(x, gain, nb, seg):
    d = x.shape[1]
    nt = seg // TM
    per = next(k for k in (4, 2, 1) if (nt - 1) % k == 0)
    tile = lambda s: (lambda b, j: (b * nt + 1 + j * per + s, 0))
    return pl.pallas_call(
        functools.partial(_final_kernel, per=per),
        grid=(nb, (nt - 1) // per),
        in_specs=[pl.BlockSpec((TM, d), tile(s)) for s in range(per)] + [pl.BlockSpec((1, d), lambda b, j: (0, 0))],
        out_specs=pl.BlockSpec((1, per * TM, d), lambda b, j: (b, j, 0)),
        out_shape=jax.ShapeDtypeStruct((nb, seg - TM, d), F32),
        compiler_params=_params("parallel", "parallel"),
        name="final_norm",
    )(*([x] * per), gain.reshape(1, d))


def _rope_angles(n_lat, rot_dim):
    rows = n_lat // GRID_W
    row = jnp.repeat(jnp.arange(rows), GRID_W).astype(F32)
    col = jnp.tile(jnp.arange(GRID_W), rows).astype(F32)
    quarter = rot_dim // 4
    inv = ROPE_BASE ** (-jnp.arange(quarter, dtype=F32) / quarter)
    ang = jnp.concatenate([row[:, None] * inv, col[:, None] * inv], axis=-1)
    return jnp.cos(ang), jnp.sin(ang)


def _with_ctx_rows(tab, fill):
    return jnp.concatenate([jnp.full((TM, tab.shape[1]), fill, F32), tab], axis=0)


def _attn_rope_tables(n_lat):
    cos, sin = _rope_angles(n_lat, ATT_HEAD_DIM)
    cos_h = jnp.concatenate([cos, cos], axis=1)
    sin_h = jnp.concatenate([-sin, sin], axis=1)
    reps = LANES // ATT_HEAD_DIM
    return (_with_ctx_rows(jnp.tile(cos_h, (1, reps)), 1.0), _with_ctx_rows(jnp.tile(sin_h, (1, reps)), 0.0))


def _mla_k_tables(n_lat):
    cos, sin = _rope_angles(n_lat, MLA_ROPE)
    lo = jnp.zeros((n_lat, MLA_NOPE), F32)
    hi = jnp.zeros((n_lat, LANES - MLA_NOPE - MLA_ROPE), F32)
    cos_c = jnp.concatenate([lo, cos, cos, hi], axis=1)
    sin_c = jnp.concatenate([lo, -sin, sin, hi], axis=1)
    ctx_cos = jnp.concatenate([lo[:TM], jnp.ones((TM, MLA_ROPE), F32), hi[:TM]], axis=1)
    return (jnp.concatenate([ctx_cos, cos_c], axis=0), _with_ctx_rows(sin_c, 0.0))


def _pad_heads(w, real):
    r = w.shape[0]
    w3 = w.reshape(r, MLA_HEADS, real)
    return jnp.pad(w3, ((0, 0), (0, 0), (0, LANES - real))).reshape(r, MLA_HEADS * LANES)


def kernel(x, c, ctx, c_ctx, norm1_g, norm2_g, w_mod, b_mod, moe_w_group, moe_b_group, moe_w_expert, moe_b_expert, moe_w_gate, moe_w_up, moe_w_down, attn_w_in, attn_sink, attn_w_out, ssm_w_in, ssm_conv_w, ssm_conv_b, ssm_dt_bias, ssm_a_log, ssm_d, ssm_norm_g, ssm_w_out, mlstm_w_in, mlstm_gate_b, mlstm_norm_g, mlstm_w_out, mla_w_in, mla_q_norm_g, mla_w_q_up, mla_kv_norm_g, mla_w_kv_up, mla_w_out, final_norm_g):
    nb, n_lat, d = x.shape
    assert ctx.shape[1] == TM and d == D_MODEL and n_lat % TM == 0
    depth = w_mod.shape[0]
    seg = TM + n_lat
    nt = seg // TM
    t_tok = nb * seg

    xs = (x, ctx)

    rows = -(-(nb + 1) // 8) * 8
    cvec = jnp.concatenate([c, c_ctx[None, :], jnp.zeros((rows - nb - 1, d), F32)], axis=0)
    mods = _modulation(cvec, w_mod, b_mod).reshape(depth, rows, ADALN_CHUNKS, d)
    mods = jnp.pad(mods, ((0, 0), (0, 0), (0, MOD_ROWS - ADALN_CHUNKS), (0, 0)))

    w_router = jnp.concatenate([moe_w_expert, moe_w_group,
                                jnp.zeros((depth, d, LANES - MOE_EXPERTS - MOE_GROUPS), F32)], axis=-1)
    b_router = jnp.concatenate([moe_b_expert, moe_b_group,
                                jnp.zeros((depth, LANES - MOE_EXPERTS - MOE_GROUPS), F32)], axis=-1)
    w_gate, w_up = moe_w_gate.astype(BF16), moe_w_up.astype(BF16)
    w_down = moe_w_down.astype(BF16).reshape(depth, MOE_GROUPS, MOE_PER_GROUP * MOE_FF, d)

    for i in range(depth):
        kind = i % 4
        mod_i = mods[i]
        if kind == 0:
            nq, nk = ATT_HEADS * ATT_HEAD_DIM, ATT_KV_HEADS * ATT_HEAD_DIM
            half = ATT_HEAD_DIM // 2
            q, k, v = _normproj(xs, mod_i, norm1_g[i], attn_w_in[i // 4].astype(BF16),
                                [(0, nq, half, ATT_HEAD_DIM ** -0.5 * LOG2_E), (nq, nk, half), (nq + nk, nk, None)],
                                [BF16, BF16, BF16], nt, nb, tables=_attn_rope_tables(n_lat))
            o = _windowed_attention(q, k, v, attn_sink[i // 4], nb, seg)
            xs = _outproj([o], [nq], xs, mod_i, attn_w_out[i // 4].astype(BF16), _plain_prologue, nt, nb)
        elif kind == 1:
            j = i // 4
            w_in = jnp.pad(ssm_w_in[j], ((0, 0), (0, LANES - 2 * SSM_HEADS))).astype(BF16)
            lane_pad = LANES - 2 * SSM_HEADS
            dt_bias = jnp.pad(ssm_dt_bias[j].reshape(1, -1), ((0, 0), (0, lane_pad)))
            a_neg = jnp.pad(-jnp.exp(ssm_a_log[j].astype(F32)).reshape(1, -1), ((0, 0), (0, lane_pad)))
            conv_w = jnp.pad(ssm_conv_w[j], ((0, 8 - SSM_CONV), (0, 0)))
            z, xc, bm, cm, dtv, a = _ssm_in(xs, mod_i, norm1_g[i], w_in, conv_w, ssm_conv_b[j].reshape(1, -1),
                                            dt_bias, a_neg, nt, nb)
            yf, yb = _ssd_scan(xc, bm, cm, dtv, a, ssm_d[j].astype(F32), nb, seg)
            xs = _outproj([z], [SSM_D_INNER], xs, mod_i, ssm_w_out[j].astype(BF16), _ssm_prologue,
                          nt, nb, extra=(ssm_norm_g[j].reshape(1, -1),), ins_t=(yf, yb))
        elif kind == 2:
            j = i // 4
            nqk, nv = 2 * ML_HEADS * ML_QK_DIM, ML_HEADS * ML_V_DIM
            w_in = jnp.pad(mlstm_w_in[j], ((0, 0), (0, LANES - 4 * ML_HEADS))).astype(BF16)
            qk, v, o, g = _normproj(xs, mod_i, norm1_g[i], w_in,
                                    [(0, nqk, None), (nqk, nv, None), (nqk + nv, nv, None), (nqk + 2 * nv, LANES, None)],
                                    [BF16, BF16, BF16, F32], nt, nb)
            gate_b = jnp.pad(mlstm_gate_b[j].reshape(1, -1), ((0, 0), (0, LANES - 4 * ML_HEADS)))
            hf, hb = _mlstm_scan(qk, v, g, gate_b, nb, seg)
            xs = _outproj([o], [nv], xs, mod_i, mlstm_w_out[j].astype(BF16), _mlstm_prologue,
                          nt, nb, extra=(mlstm_norm_g[j].reshape(1, -1),), ins_t=(hf, hb))
        else:
            j = i // 4
            w_in = jnp.pad(mla_w_in[j], ((0, 0), (0, LANES - MLA_ROPE))).astype(BF16)
            cos, sin = _rope_angles(n_lat, MLA_ROPE)
            tables_t = (jnp.transpose(_with_ctx_rows(cos, 1.0)), jnp.transpose(_with_ctx_rows(sin, 0.0)))
            wq_t = jnp.transpose(_pad_heads(mla_w_q_up[j], MLA_NOPE + MLA_ROPE)).astype(BF16)
            w_kv = mla_w_kv_up[j].reshape(MLA_RANK, MLA_HEADS, MLA_NOPE + MLA_V)
            w_k = _pad_heads(w_kv[:, :, :MLA_NOPE].reshape(MLA_RANK, MLA_HEADS * MLA_NOPE), MLA_NOPE).astype(BF16)
            wv_t = jnp.transpose(w_kv[:, :, MLA_NOPE:].reshape(MLA_RANK, MLA_HEADS * MLA_V)).astype(BF16)
            q_t, k, v_t = _mla_in(xs, mod_i, norm1_g[i], w_in, mla_q_norm_g[j], wq_t, mla_kv_norm_g[j], w_k, wv_t,
                                  tables_t, _mla_k_tables(n_lat), nt, nb)
            o = _mla_attention(q_t, k, v_t, nb, seg)
            xs = _outproj([o], [MLA_HEADS * MLA_V], xs, mod_i, mla_w_out[j].astype(BF16), _plain_prologue, nt, nb)

        xs = _moe(xs, mod_i, norm2_g[i], w_router[i], b_router[i].reshape(1, -1), w_gate, w_up, w_down, i, nt, nb)

    return _final_norm(xs, final_norm_g, nb, seg)
```

```python
import functools
import math

import jax
import jax.numpy as jnp
from jax import lax
from jax.experimental import pallas as pl
from jax.experimental.pallas import tpu as pltpu

F32 = jnp.float32
BF16 = jnp.bfloat16

D_MODEL = 1024
GRID_W = 64
EPS = 1e-6
ROPE_BASE = 10000.0
ADALN_CHUNKS = 6
CHUNK = 128
SSD_CHUNK = 256
ML_CHUNK = 128
SSD_DECAY_BLOCK = 128
TM = 256
MOD_ROWS = 8
LANES = 128
V7X_VMEM_LIMIT = 48 * 1024 * 1024

ATT_HEADS, ATT_KV_HEADS, ATT_HEAD_DIM, WINDOW = 16, 4, 64, 128
ATT_GROUP = ATT_HEADS // ATT_KV_HEADS
ATT_UNIT_HEADS = 4
SSM_D_INNER, SSM_HEAD_DIM, SSM_HEADS, SSM_GROUPS, SSM_STATE, SSM_CONV = 2048, 64, 32, 4, 128, 5
SSM_HPG = SSM_HEADS // SSM_GROUPS
SSM_BC = SSM_GROUPS * SSM_STATE
SSM_CONV_DIM = SSM_D_INNER + 2 * SSM_BC
SSM_CONV_CHUNK = 256
ML_HEADS, ML_QK_DIM, ML_V_DIM = 8, 64, 128
MLA_HEADS, MLA_RANK, MLA_NOPE, MLA_ROPE, MLA_V = 16, 256, 64, 32, 64
MOE_GROUPS, MOE_PER_GROUP, MOE_EXPERTS, MOE_FF = 4, 4, 16, 256
MOE_ROWS = 1024
OUTPROJ_T_WIDTH_LIMIT = 2048
MLA_KEY_TILE = 1024
MLA_HEADS_PER_STEP = 4
MLA_LOOKAHEAD = 1
LOG2_E = 1.4426950408889634


def _dot(a, b):
    return jnp.dot(a, b, preferred_element_type=F32)


def _dot_nt(a, b):
    return lax.dot_general(a, b, (((1,), (1,)), ((), ())), preferred_element_type=F32)


def _split_bf16(x):
    hi = x.astype(BF16)
    lo = (x - hi.astype(F32)).astype(BF16)
    return hi, lo


def _dot_split(a, b):
    a_hi, a_lo = _split_bf16(a)
    b_hi, b_lo = _split_bf16(b)
    return _dot(a_hi, b_hi) + _dot(a_lo, b_hi) + _dot(a_hi, b_lo)


def _sigmoid(x):
    return 1.0 / (1.0 + jnp.exp(-x))


def _silu(x):
    return x * _sigmoid(x)


def _softplus(x):
    return jnp.maximum(x, 0.0) + jnp.log1p(jnp.exp(-jnp.abs(x)))


def _params(*sem):
    return pltpu.CompilerParams(dimension_semantics=sem, vmem_limit_bytes=V7X_VMEM_LIMIT)


def _mod_row(t, nt, nb):
    return jnp.where(t % nt == 0, nb, t // nt)


def _mod_kernel(c_ref, w_ref, b_ref, o_ref):
    o_ref[0] = _dot_split(_silu(c_ref[...]), w_ref[0]) + b_ref[0]


def _modulation(cvec, w_mod, b_mod):
    depth, d, n = w_mod.shape
    tn = 1536
    rows = cvec.shape[0]
    return pl.pallas_call(
        _mod_kernel,
        grid=(depth, n // tn),
        in_specs=[pl.BlockSpec((rows, d), lambda l, j: (0, 0)),
                  pl.BlockSpec((1, d, tn), lambda l, j: (l, 0, j)),
                  pl.BlockSpec((1, 1, tn), lambda l, j: (l, 0, j))],
        out_specs=pl.BlockSpec((1, rows, tn), lambda l, j: (l, 0, j)),
        out_shape=jax.ShapeDtypeStruct((depth, rows, n), F32),
        compiler_params=_params("arbitrary", "arbitrary"),
        name="modulation",
    )(cvec, w_mod, b_mod.reshape(depth, 1, n))


def _normed(x, g_row, mod, sh_row, sc_row):
    y = x * lax.rsqrt(jnp.mean(x * x, axis=-1, keepdims=True) + EPS) * g_row
    return y * (1.0 + mod[sc_row:sc_row + 1, :]) + mod[sh_row:sh_row + 1, :]


def _proj_columns(hb, w_ref, o_ref, start, width, rope_half, cos, sin, scale=None, chunk=512):
    for c in range(0, width, chunk):
        cw = min(chunk, width - c)
        acc = _dot(hb, w_ref[:, start + c:start + c + cw])
        if rope_half is not None:
            lane = lax.broadcasted_iota(jnp.int32, (1, cw), 1)
            rot = jnp.where(lane % (2 * rope_half) < rope_half, pltpu.roll(acc, cw - rope_half, axis=1),
                            pltpu.roll(acc, rope_half, axis=1))
            reps = cw // LANES
            acc = acc * jnp.tile(cos, (1, reps)) + rot * jnp.tile(sin, (1, reps))
        if scale is not None:
            acc = acc * scale
        o_ref[:, c:c + cw] = acc.astype(o_ref.dtype)


def _stream_specs(x, nt, sub):
    if isinstance(x, tuple):
        d = x[0].shape[-1]
        specs = []
        for s in range(sub):
            tile = lambda t, s=s: t * sub + s
            specs += [pl.BlockSpec((1, TM, d), lambda t, tile=tile: (tile(t) // nt, jnp.maximum(tile(t) % nt - 1, 0), 0)),
                      pl.BlockSpec((1, TM, d), lambda t, tile=tile: (tile(t) // nt, 0, 0))]
        return specs, list(x) * sub
    return [pl.BlockSpec((sub * TM, x.shape[1]), lambda t: (t, 0))], [x]


def _tiles_per_step(nt, nb, per_tile_tables=False, max_tiles=4):
    if per_tile_tables:
        return 1
    return next(k for k in (4, 2, 1) if k <= max_tiles and (nt * nb) % k == 0)


def _stream_tile(x_refs, nt, s, sub):
    if len(x_refs) == 1:
        return x_refs[0][s * TM:(s + 1) * TM, :]
    is_ctx = (pl.program_id(0) * sub + s) % nt == 0
    return jnp.where(is_ctx, x_refs[2 * s + 1][0], x_refs[2 * s][0])


def _normproj_kernel(*refs, outs, has_rope, n_x, nt, nb, sub):
    x_refs, refs = refs[:n_x], refs[n_x:]
    mods_ref, g_ref, w_ref = refs[:3]
    k = 3
    cos = sin = None
    if has_rope:
        cos, sin = refs[3][...], refs[4][...]
        k = 5
    hs = []
    for s in range(sub):
        mod = mods_ref[_mod_row(pl.program_id(0) * sub + s, nt, nb)]
        hs.append(_normed(_stream_tile(x_refs, nt, s, sub), g_ref[...], mod, 0, 1).astype(BF16))
    hb = hs[0] if sub == 1 else jnp.concatenate(hs, axis=0)
    for o_ref, spec in zip(refs[k:], outs):
        _proj_columns(hb, w_ref, o_ref, *spec[:3], cos, sin, scale=spec[3] if len(spec) > 3 else None)


def _normproj(x, mods, gain, w, outs, out_dtypes, nt, nb, tables=None):
    d, n = w.shape
    t_tok = nb * nt * TM
    sub = _tiles_per_step(nt, nb, per_tile_tables=tables is not None)
    rows = sub * TM
    x_specs, x_args = _stream_specs(x, nt, sub)
    in_specs = x_specs + [pl.BlockSpec(mods.shape, lambda t: (0, 0, 0)),
                          pl.BlockSpec((1, d), lambda t: (0, 0)),
                          pl.BlockSpec((d, n), lambda t: (0, 0))]
    args = x_args + [mods, gain.reshape(1, d), w]
    if tables is not None:
        in_specs += [pl.BlockSpec((TM, LANES), lambda t: (t % nt, 0))] * 2
        args += list(tables)
    return pl.pallas_call(
        functools.partial(_normproj_kernel, outs=tuple(outs), has_rope=tables is not None, n_x=len(x_args),
                          nt=nt, nb=nb, sub=sub),
        grid=(t_tok // rows,),
        in_specs=in_specs,
        out_specs=[pl.BlockSpec((rows, o[1]), lambda t: (t, 0)) for o in outs],
        out_shape=[jax.ShapeDtypeStruct((t_tok, o[1]), dt) for o, dt in zip(outs, out_dtypes)],
        compiler_params=_params("parallel"),
        name="normproj",
    )(*args)


def _outproj_kernel(*refs, prologue, n_t, n_tok, n_extra, n_x, nt, nb, sub):
    ins_t = refs[:n_t * sub]
    ins = refs[n_t * sub:n_t * sub + n_tok]
    extra = refs[n_t * sub + n_tok:n_t * sub + n_tok + n_extra]
    x_refs = refs[n_t * sub + n_tok + n_extra:n_t * sub + n_tok + n_extra + n_x]
    mods_ref, w_ref, o_ref = refs[n_t * sub + n_tok + n_extra + n_x:]
    parts = []
    for s in range(sub):
        tok = [r if sub == 1 else r.at[pl.ds(s * TM, TM), :] for r in ins]
        parts.append(prologue(*ins_t[s * n_t:(s + 1) * n_t], *tok, *extra))
    a = parts[0] if sub == 1 else jnp.concatenate(parts, axis=0)
    y = _dot(a, w_ref[...])
    for s in range(sub):
        rows = slice(s * TM, (s + 1) * TM)
        gate = mods_ref[_mod_row(pl.program_id(0) * sub + s, nt, nb)][2:3, :]
        o_ref[rows, :] = _stream_tile(x_refs, nt, s, sub) + gate * y[rows, :]


def _outproj(ins, in_widths, x, mods, w, prologue, nt, nb, extra=(), ins_t=()):
    d = w.shape[1]
    t_tok = nb * nt * TM
    wide_t = sum(a.shape[1] for a in ins_t) > OUTPROJ_T_WIDTH_LIMIT
    sub = _tiles_per_step(nt, nb, max_tiles=2 if wide_t else 4)
    rows = sub * TM

    def tile_map(s):
        return lambda t: ((t * sub + s) // nt, 0, (t * sub + s) % nt)

    in_specs, args = [], []
    for s in range(sub):
        in_specs += [pl.BlockSpec((1, a.shape[1], TM), tile_map(s)) for a in ins_t]
        args += list(ins_t)
    in_specs += [pl.BlockSpec((rows, wd), lambda t: (t, 0)) for wd in in_widths]
    in_specs += [pl.BlockSpec(e.shape, lambda t: (0, 0)) for e in extra]
    x_specs, x_args = _stream_specs(x, nt, sub)
    n_before_x = len(in_specs)
    in_specs += x_specs + [pl.BlockSpec(mods.shape, lambda t: (0, 0, 0)),
                           pl.BlockSpec(w.shape, lambda t: (0, 0), pipeline_mode=pl.Buffered(1))]
    return pl.pallas_call(
        functools.partial(_outproj_kernel, prologue=prologue, n_t=len(ins_t), n_tok=len(ins), n_extra=len(extra),
                          n_x=len(x_args), nt=nt, nb=nb, sub=sub),
        grid=(t_tok // rows,),
        in_specs=in_specs,
        out_specs=pl.BlockSpec((rows, d), lambda t: (t, 0)),
        out_shape=jax.ShapeDtypeStruct((t_tok, d), F32),
        input_output_aliases={} if isinstance(x, tuple) else {n_before_x: 0},
        compiler_params=_params("parallel"),
        name="outproj",
    )(*args, *ins, *extra, *x_args, mods, w)


def _group_rms(y, n_groups):
    width = y.shape[1] // n_groups
    parts = []
    for g in range(n_groups):
        yg = y[:, g * width:(g + 1) * width]
        parts.append(yg * lax.rsqrt(jnp.mean(yg * yg, axis=-1, keepdims=True) + EPS))
    return jnp.concatenate(parts, axis=1)


def _plain_prologue(o_ref):
    return o_ref[...]


def _ssm_prologue(yf_ref, yb_ref, z_ref, g_ref):
    y = jnp.transpose(yf_ref[0].astype(F32) + yb_ref[0].astype(F32)) * _silu(z_ref[...].astype(F32))
    return (_group_rms(y, SSM_GROUPS) * g_ref[...]).astype(BF16)


def _mlstm_prologue(hf_ref, hb_ref, o_ref, g_ref):
    h = jnp.transpose(hf_ref[0].astype(F32) + hb_ref[0].astype(F32))
    return (_group_rms(h, ML_HEADS) * g_ref[...] * _sigmoid(o_ref[...].astype(F32))).astype(BF16)


MOE_ROUTE_ROWS = 24


def _route(logits):
    lt = jnp.transpose(logits)[:MOE_ROUTE_ROWS, :]
    row = lax.broadcasted_iota(jnp.int32, lt.shape, 0).astype(F32)
    neg = -jnp.inf
    lg = jnp.where((row >= MOE_EXPERTS) & (row < MOE_EXPERTS + MOE_GROUPS), lt, neg)
    gmax = jnp.max(lg, axis=0, keepdims=True)
    g_sel = jnp.min(jnp.where(lg == gmax, row, LANES), axis=0, keepdims=True) - MOE_EXPERTS
    p_g = 1.0 / jnp.sum(jnp.exp(lg - gmax), axis=0, keepdims=True)
    in_group = (row >= g_sel * MOE_PER_GROUP) & (row < (g_sel + 1) * MOE_PER_GROUP)
    le = jnp.where(in_group, lt, neg)
    v1 = jnp.max(le, axis=0, keepdims=True)
    i1 = jnp.min(jnp.where(le == v1, row, LANES), axis=0, keepdims=True)
    le2 = jnp.where(row == i1, neg, le)
    v2 = jnp.max(le2, axis=0, keepdims=True)
    i2 = jnp.min(jnp.where(le2 == v2, row, LANES), axis=0, keepdims=True)
    e2 = jnp.exp(v2 - v1)
    w1 = p_g / (1.0 + e2)
    comb_t = jnp.where(row == i1, w1, 0.0) + jnp.where(row == i2, w1 * e2, 0.0)
    pad = jnp.zeros((LANES - MOE_ROUTE_ROWS, lt.shape[1]), F32)
    return jnp.transpose(jnp.concatenate([comb_t, pad], axis=0))


def _moe_kernel(x_ref, mods_ref, g_ref, wr_ref, br_ref, wg_ref, wu_ref, wd_ref, o_ref, h_scr, comb_scr,
                *, nt, nb, sub):
    i = pl.program_id(0)
    e = pl.program_id(1)

    def experts(rows):
        hb = h_scr[rows, :]
        comb = comb_scr[rows, :]
        lane = lax.broadcasted_iota(jnp.int32, comb.shape, 1)
        scaled = []
        for k in range(MOE_PER_GROUP):
            act = _silu(_dot(hb, wg_ref[0, k])) * _dot(hb, wu_ref[0, k])
            cw = jnp.sum(jnp.where(lane == e * MOE_PER_GROUP + k, comb, 0.0), axis=-1, keepdims=True)
            scaled.append((act * cw).astype(BF16))
        return _dot(jnp.concatenate(scaled, axis=1), wd_ref[0, 0])

    def gate_row(s):
        return mods_ref[_mod_row(i * sub + s, nt, nb)][5:6, :]

    @pl.when(e == 0)
    def _():
        w_hi, w_lo = _split_bf16(wr_ref[...])
        w_both = jnp.concatenate([w_hi, w_lo], axis=1)
        def prepare(s):
            rows = slice(s * TM, (s + 1) * TM)
            mod = mods_ref[_mod_row(i * sub + s, nt, nb)]
            h = _normed(x_ref[rows, :], g_ref[...], mod, 3, 4)
            h_hi, h_lo = _split_bf16(h)
            h_scr[rows, :] = h_hi
            both = _dot(h_hi, w_both)
            logits = both[:, :LANES] + both[:, LANES:] + _dot(h_lo, w_hi) + br_ref[...]
            comb_scr[rows, :] = _route(logits)

        for s in range(sub):
            prepare(s)
            rows = slice(s * TM, (s + 1) * TM)
            o_ref[rows, :] = x_ref[rows, :] + gate_row(s) * experts(rows)

    @pl.when(e > 0)
    def _():
        y = experts(slice(0, sub * TM))
        for s in range(sub):
            rows = slice(s * TM, (s + 1) * TM)
            o_ref[rows, :] += gate_row(s) * y[rows, :]


def _moe(x, mods, gain, w_router, b_router, w_gate, w_up, w_down, layer, nt, nb):
    t_tok, d = x.shape
    pg = MOE_PER_GROUP
    rows = MOE_ROWS if t_tok % MOE_ROWS == 0 else TM
    sub = rows // TM
    return pl.pallas_call(
        functools.partial(_moe_kernel, nt=nt, nb=nb, sub=sub),
        grid=(t_tok // rows, MOE_GROUPS),
        in_specs=[pl.BlockSpec((rows, d), lambda i, e: (i, 0)),
                  pl.BlockSpec(mods.shape, lambda i, e: (0, 0, 0)),
                  pl.BlockSpec((1, d), lambda i, e: (0, 0)),
                  pl.BlockSpec((d, LANES), lambda i, e: (0, 0)),
                  pl.BlockSpec((1, LANES), lambda i, e: (0, 0)),
                  pl.BlockSpec((1, pg, d, MOE_FF), lambda i, e: (layer, e, 0, 0)),
                  pl.BlockSpec((1, pg, d, MOE_FF), lambda i, e: (layer, e, 0, 0)),
                  pl.BlockSpec((1, 1, pg * MOE_FF, d), lambda i, e: (layer, e, 0, 0))],
        out_specs=pl.BlockSpec((rows, d), lambda i, e: (i, 0)),
        out_shape=jax.ShapeDtypeStruct((t_tok, d), F32),
        scratch_shapes=[pltpu.VMEM((rows, d), BF16), pltpu.VMEM((rows, LANES), F32)],
        input_output_aliases={0: 0},
        compiler_params=_params("parallel", "arbitrary"),
        name="moe",
    )(x, mods, gain.reshape(1, d), w_router, b_router, w_gate, w_up, w_down)


def _attn_kernel(sink_ref, q_ref, kp_ref, kc_ref, kn_ref, vp_ref, vc_ref, vn_ref, kx_ref, vx_ref, o_ref, *, n_lat):
    j = pl.program_id(1)
    jl = j - 1
    dh = ATT_HEAD_DIM
    n_loc = TM + 2 * CHUNK
    k_all = jnp.concatenate([kp_ref[...], kc_ref[...], kn_ref[...], kx_ref[...]], axis=0)
    q_t = jnp.transpose(q_ref[...].astype(F32)).astype(BF16)
    n_keys = k_all.shape[0]
    kj = lax.broadcasted_iota(jnp.int32, (n_keys, TM), 0)
    qi = lax.broadcasted_iota(jnp.int32, (n_keys, TM), 1)
    kpos = jl * TM - CHUNK + kj
    local_ok = (kj >= qi + CHUNK - WINDOW) & (kj <= qi + CHUNK + WINDOW) & (kpos >= 0) & (kpos < n_lat) & (jl >= 0)
    bias = jnp.where((kj >= n_loc) | local_ok, 0.0, -jnp.inf)
    uh = ATT_UNIT_HEADS
    bias = jnp.concatenate([bias] * uh, axis=1)
    lane_head = lax.broadcasted_iota(jnp.int32, (1, uh * TM), 1) // TM
    zeros_q = jnp.zeros((dh, uh * TM), BF16)
    units = [(h0 // ATT_GROUP, h0) for h0 in range(0, ATT_HEADS, uh)]

    def scores(u):
        g, h0 = units[u]
        half = g % 2
        k_pair = k_all[:, (g - half) * dh:(g - half + 2) * dh]
        qg = jnp.concatenate([q_t[(h0 + a) * dh:(h0 + a + 1) * dh, :] for a in range(uh)], axis=1)
        q_m = jnp.concatenate([zeros_q, qg] if half else [qg, zeros_q], axis=0)
        return _dot(k_pair, q_m) + bias

    outs = []
    s_next = scores(0)
    v_t = jnp.transpose(jnp.concatenate([vp_ref[...], vc_ref[...], vn_ref[...], vx_ref[...]],
                                        axis=0).astype(F32)).astype(BF16)
    for u, (g, h0) in enumerate(units):
        s = s_next
        if u + 1 < len(units):
            s_next = scores(u + 1)
        sink = jnp.zeros((1, uh * TM), F32)
        for a in range(uh):
            sink = jnp.where(lane_head == a, sink_ref[h0 + a] * LOG2_E, sink)
        m = jnp.maximum(jnp.max(s, axis=0, keepdims=True), sink)
        p = jnp.exp2(s - m)
        den = jnp.sum(p, axis=0, keepdims=True) + jnp.exp2(sink - m)
        og = _dot(v_t[g * dh:(g + 1) * dh, :], p.astype(BF16)) / den
        outs += [og[:, a * TM:(a + 1) * TM] for a in range(uh)]
    o_ref[...] = jnp.transpose(jnp.concatenate(outs, axis=0)).astype(o_ref.dtype)


def _windowed_attention(q, k, v, sink, nb, seg):
    t_tok = q.shape[0]
    nchunk = seg // CHUNK
    nt = seg // TM
    per_tile = TM // CHUNK
    kvw = ATT_KV_HEADS * ATT_HEAD_DIM

    def halo(off):
        return lambda b, j, *_: (b * nchunk + jnp.clip(j * per_tile + off, per_tile, nchunk - 1), 0)

    tile = lambda b, j, *_: (b * nt + j, 0)
    kv_spec = [pl.BlockSpec((CHUNK, kvw), halo(-1)), pl.BlockSpec((TM, kvw), tile),
               pl.BlockSpec((CHUNK, kvw), halo(per_tile))]
    ctx_spec = pl.BlockSpec((TM, kvw), lambda b, j, *_: (b * nt, 0))
    return pl.pallas_call(
        functools.partial(_attn_kernel, n_lat=seg - TM),
        grid_spec=pltpu.PrefetchScalarGridSpec(
            num_scalar_prefetch=1,
            grid=(nb, nt),
            in_specs=[pl.BlockSpec((TM, q.shape[1]), tile)] + kv_spec + kv_spec + [ctx_spec, ctx_spec],
            out_specs=pl.BlockSpec((TM, q.shape[1]), tile),
        ),
        out_shape=jax.ShapeDtypeStruct((t_tok, q.shape[1]), BF16),
        compiler_params=_params("parallel", "parallel"),
        name="windowed_attention",
    )(sink, q, k, k, k, v, v, v, k, v)


def _ssm_in_kernel(x_ref, xp_ref, xn_ref, mods_ref, g_ref, w_ref, cw_ref, cb_ref, dtb_ref, aneg_ref,
                   z_ref, xs_ref, bm_ref, cm_ref, dtv_ref, a_ref, *ext, nt, nb):
    t = pl.program_id(0)
    tl = t % nt
    halo = 8
    first = (tl == 0) | (tl == 1)
    last = (tl == 0) | (tl == nt - 1)
    gain = g_ref[...]

    def normed(x, tile):
        return _normed(x, gain, mods_ref[_mod_row(tile, nt, nb)], 0, 1)

    h = normed(x_ref[...], t)
    h_prev = jnp.where(first, 0.0, normed(xp_ref[...], jnp.maximum(t - 1, 0)))
    h_next = jnp.where(last, 0.0, normed(xn_ref[...], jnp.minimum(t + 1, pl.num_programs(0) - 1)))
    hb = h.astype(BF16)
    hb_ext = jnp.concatenate([h_prev, h, h_next], axis=0).astype(BF16)
    chunk = SSM_CONV_CHUNK
    pad = SSM_CONV // 2

    def project(c):
        proj = _dot(hb_ext, w_ref[:, SSM_D_INNER + c:SSM_D_INNER + c + chunk])
        ext[c // chunk][...] = proj

    project(0)
    for c in range(0, SSM_CONV_DIM, chunk):
        if c + chunk < SSM_CONV_DIM:
            project(c + chunk)
        if c < SSM_D_INNER:
            z_ref[:, c:c + chunk] = _dot(hb, w_ref[:, c:c + chunk]).astype(z_ref.dtype)
        acc = jnp.broadcast_to(cb_ref[:, c:c + chunk], (TM, chunk))
        for k in range(SSM_CONV):
            acc = acc + cw_ref[k:k + 1, c:c + chunk] * ext[c // chunk][halo - pad + k:halo - pad + k + TM, :]
        y = _silu(acc)
        if c < SSM_D_INNER:
            xs_ref[0, c:c + chunk, :] = jnp.transpose(y).astype(xs_ref.dtype)
        elif c < SSM_D_INNER + SSM_BC:
            bm_ref[:, c - SSM_D_INNER:c - SSM_D_INNER + chunk] = y.astype(bm_ref.dtype)
        else:
            off = c - SSM_D_INNER - SSM_BC
            cm_ref[0, off:off + chunk, :] = jnp.transpose(y).astype(cm_ref.dtype)
    dt = _dot(hb, w_ref[:, SSM_D_INNER + SSM_CONV_DIM:])
    dtv = _softplus(dt + dtb_ref[...])
    dtv_ref[0] = jnp.transpose(dtv)
    a_ref[0] = jnp.transpose(dtv * aneg_ref[...] * LOG2_E)


def _ssm_in(x, mods, gain, w, conv_w, conv_b, dt_bias, a_neg, nt, nb):
    t_tok, d = x.shape
    n8 = t_tok // 8
    seg = nt * TM
    row = lambda t: (t, 0)
    col = lambda t: (t // nt, 0, t % nt)
    fixed = lambda t: (0, 0)
    return pl.pallas_call(
        functools.partial(_ssm_in_kernel, nt=nt, nb=nb),
        grid=(t_tok // TM,),
        in_specs=[pl.BlockSpec((TM, d), row),
                  pl.BlockSpec((8, d), lambda t: (jnp.maximum(t * (TM // 8) - 1, 0), 0)),
                  pl.BlockSpec((8, d), lambda t: (jnp.minimum((t + 1) * (TM // 8), n8 - 1), 0)),
                  pl.BlockSpec(mods.shape, lambda t: (0, 0, 0)),
                  pl.BlockSpec((1, d), fixed),
                  pl.BlockSpec(w.shape, fixed),
                  pl.BlockSpec((8, SSM_CONV_DIM), fixed),
                  pl.BlockSpec((1, SSM_CONV_DIM), fixed),
                  pl.BlockSpec((1, LANES), fixed),
                  pl.BlockSpec((1, LANES), fixed)],
        out_specs=[pl.BlockSpec((TM, SSM_D_INNER), row),
                   pl.BlockSpec((1, SSM_D_INNER, TM), col), pl.BlockSpec((TM, SSM_BC), row),
                   pl.BlockSpec((1, SSM_BC, TM), col), pl.BlockSpec((1, LANES, TM), col),
                   pl.BlockSpec((1, LANES, TM), col)],
        out_shape=[jax.ShapeDtypeStruct((t_tok, SSM_D_INNER), BF16),
                   jax.ShapeDtypeStruct((nb, SSM_D_INNER, seg), BF16), jax.ShapeDtypeStruct((t_tok, SSM_BC), BF16),
                   jax.ShapeDtypeStruct((nb, SSM_BC, seg), BF16), jax.ShapeDtypeStruct((nb, LANES, seg), F32),
                   jax.ShapeDtypeStruct((nb, LANES, seg), F32)],
        scratch_shapes=[pltpu.VMEM((TM + 16, SSM_CONV_CHUNK), F32)] * (SSM_CONV_DIM // SSM_CONV_CHUNK),
        compiler_params=_params("parallel"),
        name="ssm_in",
    )(x, x, x, mods, gain.reshape(1, d), w, conv_w, conv_b, dt_bias, a_neg)


def _tri(lower, n):
    r = lax.broadcasted_iota(jnp.int32, (n, n), 0)
    c = lax.broadcasted_iota(jnp.int32, (n, n), 1)
    return (c <= r) if lower else (c >= r)


def _ssd_prepare(dt_ref, a_ref, d):
    a_t = a_ref[0]
    dt_t = dt_ref[0]
    upper = jnp.where(_tri(False, SSD_CHUNK), 1.0, 0.0).astype(BF16)
    a_hi, a_lo = _split_bf16(a_t)
    cum = _dot(a_hi, upper) + _dot(a_lo, upper)
    cum_end = cum[:, SSD_CHUNK - 1:SSD_CHUNK]
    if d == 0:
        lane_v, sub_v = cum, -cum
        inter = jnp.exp2(cum)
        w_upd = jnp.exp2(cum_end - cum) * dt_t
    else:
        ecum = cum - a_t
        lane_v, sub_v = -ecum, ecum
        inter = jnp.exp2(cum_end - ecum)
        w_upd = jnp.exp2(ecum) * dt_t
    sub_c = jnp.transpose(sub_v)
    decay_end = jnp.exp2(cum_end)
    return dt_t, lane_v, sub_c, inter, w_upd, decay_end


def _ssd_groups(prep, xs_ref, bm_ref, ct_ref, dsk_ref, st_ref, y_ref, d):
    dt_t, lane_v, sub_c, inter, w_upd, decay_end = prep
    mask = _tri(d == 1, SSD_CHUNK)
    p = SSM_HEAD_DIM
    for g in range(SSM_GROUPS):
        bg = bm_ref[:, g * SSM_STATE:(g + 1) * SSM_STATE]
        cg_t = ct_ref[0, g * SSM_STATE:(g + 1) * SSM_STATE, :]
        cb_t = _dot(bg, cg_t)
        state = st_ref[g]
        y_in = _dot(state.astype(BF16), cg_t)
        upd = []
        for e in range(SSM_HPG):
            h = g * SSM_HPG + e
            ln = d * SSM_HEADS + h
            rows = slice(e * p, (e + 1) * p)
            blk = SSD_DECAY_BLOCK
            block_rows = []
            for bs in range(0, SSD_CHUNK, blk):
                block_cols = []
                for bl in range(0, SSD_CHUNK, blk):
                    if (bs > bl) if d == 0 else (bs < bl):
                        block_cols.append(jnp.zeros((blk, blk), BF16))
                        continue
                    seg = sub_c[bs:bs + blk, ln:ln + 1] + lane_v[ln:ln + 1, bl:bl + blk]
                    if bs == bl:
                        seg = jnp.where(mask[bs:bs + blk, bl:bl + blk], seg, -jnp.inf)
                    block_cols.append((cb_t[bs:bs + blk, bl:bl + blk] * jnp.exp2(seg)).astype(BF16))
                block_rows.append(jnp.concatenate(block_cols, axis=1))
            m_t = jnp.concatenate(block_rows, axis=0)
            xf = xs_ref[0, h * p:(h + 1) * p, :].astype(F32)
            u = (xf * dt_t[ln:ln + 1, :]).astype(BF16)
            yh = _dot(u, m_t) + y_in[rows, :] * inter[ln:ln + 1, :] + dsk_ref[d, h] * xf
            y_ref[0, h * p:(h + 1) * p, :] = yh.astype(y_ref.dtype)
            upd.append((xf * w_upd[ln:ln + 1, :]).astype(BF16))
        new = _dot(jnp.concatenate(upd, axis=0), bg)
        for e in range(SSM_HPG):
            ln = d * SSM_HEADS + g * SSM_HPG + e
            rows = slice(e * p, (e + 1) * p)
            st_ref[g, rows, :] = state[rows, :] * decay_end[ln:ln + 1, :] + new[rows, :]


def _ssd_kernel(dsk_ref, xsf, bmf, ctf, dtf, af, xsb, bmb, ctb, dtb, ab, yf_ref, yb_ref, stf, stb):
    @pl.when(pl.program_id(1) == 0)
    def _():
        stf[...] = jnp.zeros_like(stf)
        stb[...] = jnp.zeros_like(stb)

    prep_f = _ssd_prepare(dtf, af, 0)
    prep_b = _ssd_prepare(dtb, ab, 1)
    _ssd_groups(prep_f, xsf, bmf, ctf, dsk_ref, stf, yf_ref, 0)
    _ssd_groups(prep_b, xsb, bmb, ctb, dsk_ref, stb, yb_ref, 1)


def _scan_maps(nchunk, chunk):
    ctx_chunks = TM // chunk
    fwd = lambda b, t: (b * nchunk + t, 0)
    bwd = lambda b, t: (b * nchunk + jnp.where(t < ctx_chunks, ctx_chunks - 1 - t, nchunk + ctx_chunks - 1 - t), 0)
    return fwd, bwd


def _ssd_scan(xs_t, bm, c_t, dt_t, a_t, d_skip, nb, seg):
    nchunk = seg // SSD_CHUNK
    fwd_rows, bwd_rows = _scan_maps(nchunk, SSD_CHUNK)

    def specs(rows_map):
        cols_map = lambda b, t, *_: (b, 0, rows_map(b, t)[0] - b * nchunk)
        return [pl.BlockSpec((1, SSM_D_INNER, SSD_CHUNK), cols_map),
                pl.BlockSpec((SSD_CHUNK, SSM_BC), lambda b, t, *_: rows_map(b, t)),
                pl.BlockSpec((1, SSM_BC, SSD_CHUNK), cols_map),
                pl.BlockSpec((1, LANES, SSD_CHUNK), cols_map),
                pl.BlockSpec((1, LANES, SSD_CHUNK), cols_map)]

    arrs = (xs_t, bm, c_t, dt_t, a_t)
    return pl.pallas_call(
        _ssd_kernel,
        grid_spec=pltpu.PrefetchScalarGridSpec(
            num_scalar_prefetch=1,
            grid=(nb, nchunk),
            in_specs=specs(fwd_rows) + specs(bwd_rows),
            out_specs=[specs(fwd_rows)[0], specs(bwd_rows)[0]],
            scratch_shapes=[pltpu.VMEM((SSM_GROUPS, SSM_HPG * SSM_HEAD_DIM, SSM_STATE), F32)] * 2,
        ),
        out_shape=[jax.ShapeDtypeStruct(xs_t.shape, BF16)] * 2,
        compiler_params=_params("parallel", "arbitrary"),
        name="ssd_scan",
    )(d_skip, *arrs, *arrs)


def _log_sigmoid(x):
    return jnp.minimum(x, 0.0) - jnp.log1p(jnp.exp(-jnp.abs(x)))


ML_STATE_ROWS = ML_V_DIM + 16


def _mlstm_prepare(qk_ref, v_ref, g_ref, gb_ref, m_st, d):
    nh, dk, dv = ML_HEADS, ML_QK_DIM, ML_V_DIM
    g_t = jnp.transpose(g_ref[...] + gb_ref[...])
    ig = g_t[16 * d:16 * d + nh, :]
    lf = _log_sigmoid(g_t[16 * d + nh:16 * d + 2 * nh, :])
    upper = jnp.where(_tri(False, ML_CHUNK), 1.0, 0.0).astype(BF16)
    lf_hi, lf_lo = _split_bf16(lf)
    fc = _dot(lf_hi, upper) + _dot(lf_lo, upper)
    tot = fc[:, ML_CHUNK - 1:ML_CHUNK]
    m_prev = m_st[...]
    if d == 0:
        lane_v, sub_v = fc, ig - fc
        inter = fc + m_prev
        logw = tot - fc + ig
    else:
        ec = fc - lf
        lane_v, sub_v = -ec, ec + ig
        inter = tot - ec + m_prev
        logw = ec + ig
    sub_c = jnp.transpose(jnp.concatenate([sub_v, jnp.zeros((LANES - nh, ML_CHUNK), F32)], axis=0))
    m_new = jnp.maximum(tot + m_prev, jnp.max(logw, axis=-1, keepdims=True))
    ws = jnp.exp(logw - m_new)
    cw = jnp.exp(tot + m_prev - m_new)
    q_t = jnp.transpose(qk_ref[:, :nh * dk].astype(F32))
    v_t = jnp.transpose(v_ref[...].astype(F32))
    return lane_v, sub_c, inter, ws, cw, m_new, q_t, v_t


def _mlstm_heads(prep, qk_ref, c_st, m_st, h_ref, d):
    nh, dk, dv = ML_HEADS, ML_QK_DIM, ML_V_DIM
    lane_v, sub_c, inter, ws, cw, m_new, q_t, v_t = prep
    mask = _tri(d == 1, ML_CHUNK)
    lane = lax.broadcasted_iota(jnp.int32, (1, LANES), 1)
    zeros_q = jnp.zeros((dk, ML_CHUNK), F32)
    for h in range(nh):
        half = h % 2
        k_pair = qk_ref[:, nh * dk + (h - half) * dk:nh * dk + (h - half + 2) * dk] * (dk ** -0.5)
        qh = q_t[h * dk:(h + 1) * dk, :]
        q_m = jnp.concatenate([zeros_q, qh] if half else [qh, zeros_q], axis=0).astype(BF16)
        logd = jnp.where(mask, sub_c[:, h:h + 1] + lane_v[h:h + 1, :], -jnp.inf)
        mt = jnp.maximum(inter[h:h + 1, :], jnp.max(logd, axis=0, keepdims=True))
        sc = _dot(k_pair, q_m) * jnp.exp(logd - mt)
        vh = v_t[h * dv:(h + 1) * dv, :]
        state = c_st[h]
        cq = _dot(state.astype(BF16), q_m)
        w_int = jnp.exp(inter[h:h + 1, :] - mt)
        num = _dot(vh.astype(BF16), sc.astype(BF16)) + w_int * cq[:dv, :]
        den = jnp.sum(sc, axis=0, keepdims=True) + w_int * cq[dv:dv + 1, :]
        h_ref[0, h * dv:(h + 1) * dv, :] = (num / jnp.maximum(jnp.abs(den), jnp.exp(-mt))).astype(h_ref.dtype)
        ws_h = ws[h:h + 1, :]
        lhs = jnp.concatenate([vh * ws_h, jnp.broadcast_to(ws_h, (ML_STATE_ROWS - dv, ML_CHUNK))], axis=0)
        own = (lane >= half * dk) & (lane < (half + 1) * dk)
        row = lax.broadcasted_iota(jnp.int32, (ML_STATE_ROWS, 1), 0)
        new = cw[h:h + 1, :LANES] * state + _dot(lhs.astype(BF16), k_pair)
        c_st[h] = jnp.where(own & (row <= dv), new, 0.0)
    m_st[...] = jnp.broadcast_to(m_new, m_st.shape)


def _mlstm_kernel(qkf, vf, gf, qkb, vb, gb, gbias, hf_ref, hb_ref, cf, mf, cb, mb):
    @pl.when(pl.program_id(1) == 0)
    def _():
        for r in (cf, mf, cb, mb):
            r[...] = jnp.zeros_like(r)

    prep_f = _mlstm_prepare(qkf, vf, gf, gbias, mf, 0)
    prep_b = _mlstm_prepare(qkb, vb, gb, gbias, mb, 1)
    _mlstm_heads(prep_f, qkf, cf, mf, hf_ref, 0)
    _mlstm_heads(prep_b, qkb, cb, mb, hb_ref, 1)


def _mlstm_scan(qk, v, g, gate_b, nb, seg):
    t_tok = qk.shape[0]
    nchunk = seg // ML_CHUNK
    fwd, bwd = _scan_maps(nchunk, ML_CHUNK)
    widths = (qk.shape[1], v.shape[1], LANES)
    in_specs = ([pl.BlockSpec((ML_CHUNK, w), fwd) for w in widths] + [pl.BlockSpec((ML_CHUNK, w), bwd) for w in widths]
                + [pl.BlockSpec((1, LANES), lambda b, t: (0, 0))])
    state = [pltpu.VMEM((ML_HEADS, ML_STATE_ROWS, LANES), F32), pltpu.VMEM((ML_HEADS, ML_CHUNK), F32)]
    out_map = lambda rows_map: (lambda b, t: (b, 0, rows_map(b, t)[0] - b * nchunk))
    return pl.pallas_call(
        _mlstm_kernel,
        grid=(nb, nchunk),
        in_specs=in_specs,
        out_specs=[pl.BlockSpec((1, v.shape[1], ML_CHUNK), out_map(fwd)),
                   pl.BlockSpec((1, v.shape[1], ML_CHUNK), out_map(bwd))],
        out_shape=[jax.ShapeDtypeStruct((nb, v.shape[1], seg), BF16)] * 2,
        scratch_shapes=state + state,
        compiler_params=_params("parallel", "arbitrary"),
        name="mlstm_scan",
    )(qk, v, g, qk, v, g, gate_b)


def _rms_rows(x, g_row):
    return x * lax.rsqrt(jnp.mean(x * x, axis=-1, keepdims=True) + EPS) * g_row


MLA_SCORE_SCALE = (MLA_NOPE + MLA_ROPE) ** -0.5 * LOG2_E


def _mla_queries(cq, g_ref, wt_ref, cos_ref, sin_ref, qt_ref):
    cq_t = jnp.transpose(_rms_rows(cq, g_ref[...])).astype(BF16)
    q_t = _dot(wt_ref[...], cq_t) * MLA_SCORE_SCALE
    qt_ref[0] = q_t.astype(qt_ref.dtype)
    cos, sin = cos_ref[...], sin_ref[...]
    half = MLA_ROPE // 2
    for h in range(MLA_HEADS):
        r0 = h * LANES + MLA_NOPE
        x1, x2 = q_t[r0:r0 + half, :], q_t[r0 + half:r0 + MLA_ROPE, :]
        qt_ref[0, r0:r0 + half, :] = (x1 * cos - x2 * sin).astype(qt_ref.dtype)
        qt_ref[0, r0 + half:r0 + MLA_ROPE, :] = (x2 * cos + x1 * sin).astype(qt_ref.dtype)


def _mla_keys_values(ckv, kr, g_ref, wk_ref, wvt_ref, cos_ref, sin_ref, k_ref, vt_ref):
    cn = _rms_rows(ckv, g_ref[...])
    vt_ref[0] = _dot(wvt_ref[...], jnp.transpose(cn).astype(BF16)).astype(vt_ref.dtype)
    kn = _dot(cn.astype(BF16), wk_ref[...])
    kr = pltpu.roll(kr, MLA_NOPE, axis=1)
    half = MLA_ROPE // 2
    lane = lax.broadcasted_iota(jnp.int32, (1, LANES), 1)
    partner = jnp.where(lane < MLA_NOPE + half, pltpu.roll(kr, LANES - half, axis=1), pltpu.roll(kr, half, axis=1))
    roped = kr * cos_ref[...] + partner * sin_ref[...]
    for h in range(MLA_HEADS):
        k_ref[:, h * LANES:(h + 1) * LANES] = (kn[:, h * LANES:(h + 1) * LANES] + roped).astype(k_ref.dtype)


def _mla_in_kernel(x_ref, mods_ref, g_ref, w_ref, gq_ref, wqt_ref, gkv_ref, wk_ref, wvt_ref,
                   cos_t_ref, sin_t_ref, cos_ref, sin_ref, qt_ref, k_ref, vt_ref, *, nt, nb):
    mod = mods_ref[_mod_row(pl.program_id(0), nt, nb)]
    hb = _normed(x_ref[...], g_ref[...], mod, 0, 1).astype(BF16)
    p = _dot(hb, w_ref[...])
    _mla_queries(p[:, :MLA_RANK], gq_ref, wqt_ref, cos_t_ref, sin_t_ref, qt_ref)
    _mla_keys_values(p[:, MLA_RANK:2 * MLA_RANK], p[:, 2 * MLA_RANK:], gkv_ref, wk_ref, wvt_ref,
                     cos_ref, sin_ref, k_ref, vt_ref)


def _mla_in(x, mods, gain, w_in, gain_q, wq_t, gain_kv, w_k, wv_t, tables_t, tables, nt, nb):
    t_tok, d = x.shape
    seg = nt * TM
    half = MLA_ROPE // 2
    fixed = lambda t: (0, 0)
    return pl.pallas_call(
        functools.partial(_mla_in_kernel, nt=nt, nb=nb),
        grid=(t_tok // TM,),
        in_specs=[pl.BlockSpec((TM, d), lambda t: (t, 0)),
                  pl.BlockSpec(mods.shape, lambda t: (0, 0, 0)),
                  pl.BlockSpec((1, d), fixed),
                  pl.BlockSpec(w_in.shape, fixed),
                  pl.BlockSpec((1, MLA_RANK), fixed),
                  pl.BlockSpec(wq_t.shape, fixed),
                  pl.BlockSpec((1, MLA_RANK), fixed),
                  pl.BlockSpec(w_k.shape, fixed),
                  pl.BlockSpec(wv_t.shape, fixed),
                  pl.BlockSpec((half, TM), lambda t: (0, t % nt)),
                  pl.BlockSpec((half, TM), lambda t: (0, t % nt)),
                  pl.BlockSpec((TM, LANES), lambda t: (t % nt, 0)),
                  pl.BlockSpec((TM, LANES), lambda t: (t % nt, 0))],
        out_specs=[pl.BlockSpec((1, MLA_HEADS * LANES, TM), lambda t: (t, 0, 0)),
                   pl.BlockSpec((TM, MLA_HEADS * LANES), lambda t: (t, 0)),
                   pl.BlockSpec((1, MLA_HEADS * MLA_V, TM), lambda t: (t // nt, 0, t % nt))],
        out_shape=[jax.ShapeDtypeStruct((t_tok // TM, MLA_HEADS * LANES, TM), BF16),
                   jax.ShapeDtypeStruct((t_tok, MLA_HEADS * LANES), BF16),
                   jax.ShapeDtypeStruct((nb, MLA_HEADS * MLA_V, seg), BF16)],
        compiler_params=_params("parallel"),
        name="mla_in",
    )(x, mods, gain.reshape(1, d), w_in, gain_q.reshape(1, MLA_RANK), wq_t, gain_kv.reshape(1, MLA_RANK), w_k, wv_t,
      *tables_t, *tables)


def _mla_attn_kernel(qt_ref, k_ref, vt_ref, o_ref):
    @pl.when(pl.program_id(2) == 0)
    def _():
        o_ref[...] = jnp.zeros_like(o_ref)

    @pl.when(pl.program_id(2) > 0)
    def _():
        heads = range(MLA_HEADS_PER_STEP)
        seg = k_ref.shape[0]
        tk = MLA_KEY_TILE if (seg - TM) % MLA_KEY_TILE == 0 else TM
        tiles = [(0, TM)] + [(a, a + tk) for a in range(TM, seg, tk)]

        def scores(hh, j):
            return _dot(k_ref[tiles[j][0]:tiles[j][1], hh * LANES:(hh + 1) * LANES],
                        qt_ref[0, hh * LANES:(hh + 1) * LANES, :])

        m = [jnp.full((1, TM), -jnp.inf, F32) for _ in heads]
        l = [jnp.zeros((1, TM), F32) for _ in heads]
        acc = [jnp.zeros((MLA_V, TM), F32) for _ in heads]
        ahead = min(MLA_LOOKAHEAD, len(tiles))
        pending = [[scores(hh, j) for hh in heads] for j in range(ahead)]
        for j in range(len(tiles)):
            s_cur = pending.pop(0)
            if j + ahead < len(tiles):
                pending.append([scores(hh, j + ahead) for hh in heads])
            for hh in heads:
                m_new = jnp.maximum(m[hh], jnp.max(s_cur[hh], axis=0, keepdims=True))
                alpha = jnp.exp2(m[hh] - m_new)
                p = jnp.exp2(s_cur[hh] - m_new)
                l[hh] = alpha * l[hh] + jnp.sum(p, axis=0, keepdims=True)
                v_t = vt_ref[0, hh * MLA_V:(hh + 1) * MLA_V, tiles[j][0]:tiles[j][1]]
                acc[hh] = alpha * acc[hh] + _dot(v_t, p.astype(BF16))
                m[hh] = m_new
        o_t = jnp.concatenate([acc[hh] / l[hh] for hh in heads], axis=0)
        o_ref[...] = jnp.transpose(o_t).astype(o_ref.dtype)


def _mla_attention(q_t, k, v_t, nb, seg):
    t_tok = k.shape[0]
    nt = seg // TM
    hps = MLA_HEADS_PER_STEP
    return pl.pallas_call(
        _mla_attn_kernel,
        grid=(nb, MLA_HEADS // hps, nt),
        in_specs=[pl.BlockSpec((1, hps * LANES, TM), lambda b, hp, j: (b * nt + j, hp, 0)),
                  pl.BlockSpec((seg, hps * LANES), lambda b, hp, j: (b, hp)),
                  pl.BlockSpec((1, hps * MLA_V, seg), lambda b, hp, j: (b, hp, 0))],
        out_specs=pl.BlockSpec((TM, hps * MLA_V), lambda b, hp, j: (b * nt + j, hp)),
        out_shape=jax.ShapeDtypeStruct((t_tok, MLA_HEADS * MLA_V), BF16),
        compiler_params=_params("parallel", "parallel", "arbitrary"),
        name="mla_attention",
    )(q_t, k, v_t)


def _final_kernel(*refs, per):
    g_ref, o_ref = refs[per], refs[per + 1]
    for s in range(per):
        o_ref[0, s * TM:(s + 1) * TM, :] = _rms_rows(refs[s][...], g_ref[...])


def _final_norm(x, gain, nb, seg):
    d = x.shape[1]
    nt = seg // TM
    per = next(k for k in (4, 2, 1) if (nt - 1) % k == 0)
    tile = lambda s: (lambda b, j: (b * nt + 1 + j * per + s, 0))
    return pl.pallas_call(
        functools.partial(_final_kernel, per=per),
        grid=(nb, (nt - 1) // per),
        in_specs=[pl.BlockSpec((TM, d), tile(s)) for s in range(per)] + [pl.BlockSpec((1, d), lambda b, j: (0, 0))],
        out_specs=pl.BlockSpec((1, per * TM, d), lambda b, j: (b, j, 0)),
        out_shape=jax.ShapeDtypeStruct((nb, seg - TM, d), F32),
        compiler_params=_params("parallel", "parallel"),
        name="final_norm",
    )(*([x] * per), gain.reshape(1, d))


def _rope_angles(n_lat, rot_dim):
    rows = n_lat // GRID_W
    row = jnp.repeat(jnp.arange(rows), GRID_W).astype(F32)
    col = jnp.tile(jnp.arange(GRID_W), rows).astype(F32)
    quarter = rot_dim // 4
    inv = ROPE_BASE ** (-jnp.arange(quarter, dtype=F32) / quarter)
    ang = jnp.concatenate([row[:, None] * inv, col[:, None] * inv], axis=-1)
    return jnp.cos(ang), jnp.sin(ang)


def _with_ctx_rows(tab, fill):
    return jnp.concatenate([jnp.full((TM, tab.shape[1]), fill, F32), tab], axis=0)


def _attn_rope_tables(n_lat):
    cos, sin = _rope_angles(n_lat, ATT_HEAD_DIM)
    cos_h = jnp.concatenate([cos, cos], axis=1)
    sin_h = jnp.concatenate([-sin, sin], axis=1)
    reps = LANES // ATT_HEAD_DIM
    return (_with_ctx_rows(jnp.tile(cos_h, (1, reps)), 1.0), _with_ctx_rows(jnp.tile(sin_h, (1, reps)), 0.0))


def _mla_k_tables(n_lat):
    cos, sin = _rope_angles(n_lat, MLA_ROPE)
    lo = jnp.zeros((n_lat, MLA_NOPE), F32)
    hi = jnp.zeros((n_lat, LANES - MLA_NOPE - MLA_ROPE), F32)
    cos_c = jnp.concatenate([lo, cos, cos, hi], axis=1)
    sin_c = jnp.concatenate([lo, -sin, sin, hi], axis=1)
    ctx_cos = jnp.concatenate([lo[:TM], jnp.ones((TM, MLA_ROPE), F32), hi[:TM]], axis=1)
    return (jnp.concatenate([ctx_cos, cos_c], axis=0), _with_ctx_rows(sin_c, 0.0))


def _pad_heads(w, real):
    r = w.shape[0]
    w3 = w.reshape(r, MLA_HEADS, real)
    return jnp.pad(w3, ((0, 0), (0, 0), (0, LANES - real))).reshape(r, MLA_HEADS * LANES)


def kernel(x, c, ctx, c_ctx, norm1_g, norm2_g, w_mod, b_mod, moe_w_group, moe_b_group, moe_w_expert, moe_b_expert, moe_w_gate, moe_w_up, moe_w_down, attn_w_in, attn_sink, attn_w_out, ssm_w_in, ssm_conv_w, ssm_conv_b, ssm_dt_bias, ssm_a_log, ssm_d, ssm_norm_g, ssm_w_out, mlstm_w_in, mlstm_gate_b, mlstm_norm_g, mlstm_w_out, mla_w_in, mla_q_norm_g, mla_w_q_up, mla_kv_norm_g, mla_w_kv_up, mla_w_out, final_norm_g):
    nb, n_lat, d = x.shape
    assert ctx.shape[1] == TM and d == D_MODEL and n_lat % TM == 0
    depth = w_mod.shape[0]
    seg = TM + n_lat
    nt = seg // TM
    t_tok = nb * seg

    xs = (x, ctx)

    rows = -(-(nb + 1) // 8) * 8
    cvec = jnp.concatenate([c, c_ctx[None, :], jnp.zeros((rows - nb - 1, d), F32)], axis=0)
    mods = _modulation(cvec, w_mod, b_mod).reshape(depth, rows, ADALN_CHUNKS, d)
    mods = jnp.pad(mods, ((0, 0), (0, 0), (0, MOD_ROWS - ADALN_CHUNKS), (0, 0)))

    w_router = jnp.concatenate([moe_w_expert, moe_w_group,
                                jnp.zeros((depth, d, LANES - MOE_EXPERTS - MOE_GROUPS), F32)], axis=-1)
    b_router = jnp.concatenate([moe_b_expert, moe_b_group,
                                jnp.zeros((depth, LANES - MOE_EXPERTS - MOE_GROUPS), F32)], axis=-1)
    w_gate, w_up = moe_w_gate.astype(BF16), moe_w_up.astype(BF16)
    w_down = moe_w_down.astype(BF16).reshape(depth, MOE_GROUPS, MOE_PER_GROUP * MOE_FF, d)

    for i in range(depth):
        kind = i % 4
        mod_i = mods[i]
        if kind == 0:
            nq, nk = ATT_HEADS * ATT_HEAD_DIM, ATT_KV_HEADS * ATT_HEAD_DIM
            half = ATT_HEAD_DIM // 2
            q, k, v = _normproj(xs, mod_i, norm1_g[i], attn_w_in[i // 4].astype(BF16),
                                [(0, nq, half, ATT_HEAD_DIM ** -0.5 * LOG2_E), (nq, nk, half), (nq + nk, nk, None)],
                                [BF16, BF16, BF16], nt, nb, tables=_attn_rope_tables(n_lat))
            o = _windowed_attention(q, k, v, attn_sink[i // 4], nb, seg)
            xs = _outproj([o], [nq], xs, mod_i, attn_w_out[i // 4].astype(BF16), _plain_prologue, nt, nb)
        elif kind == 1:
            j = i // 4
            w_in = jnp.pad(ssm_w_in[j], ((0, 0), (0, LANES - 2 * SSM_HEADS))).astype(BF16)
            lane_pad = LANES - 2 * SSM_HEADS
            dt_bias = jnp.pad(ssm_dt_bias[j].reshape(1, -1), ((0, 0), (0, lane_pad)))
            a_neg = jnp.pad(-jnp.exp(ssm_a_log[j].astype(F32)).reshape(1, -1), ((0, 0), (0, lane_pad)))
            conv_w = jnp.pad(ssm_conv_w[j], ((0, 8 - SSM_CONV), (0, 0)))
            z, xc, bm, cm, dtv, a = _ssm_in(xs, mod_i, norm1_g[i], w_in, conv_w, ssm_conv_b[j].reshape(1, -1),
                                            dt_bias, a_neg, nt, nb)
            yf, yb = _ssd_scan(xc, bm, cm, dtv, a, ssm_d[j].astype(F32), nb, seg)
            xs = _outproj([z], [SSM_D_INNER], xs, mod_i, ssm_w_out[j].astype(BF16), _ssm_prologue,
                          nt, nb, extra=(ssm_norm_g[j].reshape(1, -1),), ins_t=(yf, yb))
        elif kind == 2:
            j = i // 4
            nqk, nv = 2 * ML_HEADS * ML_QK_DIM, ML_HEADS * ML_V_DIM
            w_in = jnp.pad(mlstm_w_in[j], ((0, 0), (0, LANES - 4 * ML_HEADS))).astype(BF16)
            qk, v, o, g = _normproj(xs, mod_i, norm1_g[i], w_in,
                                    [(0, nqk, None), (nqk, nv, None), (nqk + nv, nv, None), (nqk + 2 * nv, LANES, None)],
                                    [BF16, BF16, BF16, F32], nt, nb)
            gate_b = jnp.pad(mlstm_gate_b[j].reshape(1, -1), ((0, 0), (0, LANES - 4 * ML_HEADS)))
            hf, hb = _mlstm_scan(qk, v, g, gate_b, nb, seg)
            xs = _outproj([o], [nv], xs, mod_i, mlstm_w_out[j].astype(BF16), _mlstm_prologue,
                          nt, nb, extra=(mlstm_norm_g[j].reshape(1, -1),), ins_t=(hf, hb))
        else:
            j = i // 4
            w_in = jnp.pad(mla_w_in[j], ((0, 0), (0, LANES - MLA_ROPE))).astype(BF16)
            cos, sin = _rope_angles(n_lat, MLA_ROPE)
            tables_t = (jnp.transpose(_with_ctx_rows(cos, 1.0)), jnp.transpose(_with_ctx_rows(sin, 0.0)))
            wq_t = jnp.transpose(_pad_heads(mla_w_q_up[j], MLA_NOPE + MLA_ROPE)).astype(BF16)
            w_kv = mla_w_kv_up[j].reshape(MLA_RANK, MLA_HEADS, MLA_NOPE + MLA_V)
            w_k = _pad_heads(w_kv[:, :, :MLA_NOPE].reshape(MLA_RANK, MLA_HEADS * MLA_NOPE), MLA_NOPE).astype(BF16)
            wv_t = jnp.transpose(w_kv[:, :, MLA_NOPE:].reshape(MLA_RANK, MLA_HEADS * MLA_V)).astype(BF16)
            q_t, k, v_t = _mla_in(xs, mod_i, norm1_g[i], w_in, mla_q_norm_g[j], wq_t, mla_kv_norm_g[j], w_k, wv_t,
                                  tables_t, _mla_k_tables(n_lat), nt, nb)
            o = _mla_attention(q_t, k, v_t, nb, seg)
            xs = _outproj([o], [MLA_HEADS * MLA_V], xs, mod_i, mla_w_out[j].astype(BF16), _plain_prologue, nt, nb)

        xs = _moe(xs, mod_i, norm2_g[i], w_router[i], b_router[i].reshape(1, -1), w_gate, w_up, w_down, i, nt, nb)

    return _final_norm(xs, final_norm_g, nb, seg)
```
